```python
import math
import jax, jax.numpy as jnp
from jax import lax
import numpy as np

D_MODEL = 1024
BATCH = 8
SEQ = 4096
DEPTH = 1
DEC_BATCH = 4
DEC_SEQ = 8192
PAST_LEN = 128

GRID_W = 64
N_HEADS = 8
N_KV_HEADS = 2
HEAD_DIM = 128
ATTN_WIDTH = N_HEADS * HEAD_DIM
KV_WIDTH = N_KV_HEADS * HEAD_DIM
ROPE_THETA = 10000.0
Q_BLOCK = 128
RNN_WIDTH = D_MODEL
RNN_BLOCKS = 8
RNN_BLOCK_W = RNN_WIDTH // RNN_BLOCKS
CONV_W = 4
CONV_PAD = (2, 1)
LRU_C = 8.0
N_BRANCH = 2
N_EXPERTS = 32
TOP_K = 4
D_FF = D_MODEL
SWIGLU_LIMIT = 7.0
SWIGLU_ALPHA = 1.702
MOE_BLOCK = 512
DN_ALPHA = (2 * DEPTH) ** 0.25
DN_BETA = (8 * DEPTH) ** -0.25
LN_EPS = 1e-5
RMS_EPS = 1e-6
N_MOD = 6
IN_SPLITS = (ATTN_WIDTH, ATTN_WIDTH + KV_WIDTH, ATTN_WIDTH + 2 * KV_WIDTH,
             ATTN_WIDTH + 2 * KV_WIDTH + RNN_WIDTH, ATTN_WIDTH + 2 * KV_WIDTH + 2 * RNN_WIDTH)
IN_WIDTH = ATTN_WIDTH + 2 * KV_WIDTH + 2 * RNN_WIDTH + N_BRANCH * D_MODEL

kernel_name = "hybrid_gqa_rglru_moe_encoder"


def layer_norm(x, gain=None, bias=None):
    xf = x.astype(jnp.float32)
    mu = xf.mean(-1, keepdims=True)
    var = jnp.square(xf - mu).mean(-1, keepdims=True)
    y = (xf - mu) * lax.rsqrt(var + LN_EPS)
    if gain is not None:
        y = y * gain.astype(jnp.float32) + bias.astype(jnp.float32)
    return y.astype(x.dtype)


def rms_norm(x, gain):
    xf = x.astype(jnp.float32)
    y = xf * lax.rsqrt(jnp.mean(xf * xf, axis=-1, keepdims=True) + RMS_EPS) * gain.astype(jnp.float32)
    return y.astype(x.dtype)


def axial_rope_tables(seq_len):
    n_rows = seq_len // GRID_W
    rows = jnp.repeat(jnp.arange(n_rows), GRID_W).astype(jnp.float32)
    cols = jnp.tile(jnp.arange(GRID_W), n_rows).astype(jnp.float32)
    axis_dim = HEAD_DIM // 2
    inv = ROPE_THETA ** (-jnp.arange(0, axis_dim, 2, dtype=jnp.float32) / axis_dim)
    ang_r = rows[:, None] * inv
    ang_c = cols[:, None] * inv
    return (jnp.cos(ang_r), jnp.sin(ang_r), jnp.cos(ang_c), jnp.sin(ang_c))


def rotate(x, cos, sin):
    m = x.shape[-1] // 2
    x1, x2 = x[..., :m], x[..., m:]
    c, s = cos[:, None, :], sin[:, None, :]
    return jnp.concatenate([x1 * c - x2 * s, x2 * c + x1 * s], axis=-1)


def apply_axial_rope(x, tables):
    cos_r, sin_r, cos_c, sin_c = tables
    xf = x.astype(jnp.float32)
    half = HEAD_DIM // 2
    y = jnp.concatenate([rotate(xf[..., :half], cos_r, sin_r), rotate(xf[..., half:], cos_c, sin_c)], axis=-1)
    return y.astype(x.dtype)


def block_attention(q, k, v):
    b, s = q.shape[:2]
    groups = N_HEADS // N_KV_HEADS
    nb = s // Q_BLOCK
    qb = q.reshape(b, nb, Q_BLOCK, N_KV_HEADS, groups, HEAD_DIM).transpose(1, 0, 2, 3, 4, 5)
    scale = HEAD_DIM ** -0.5

    def one_block(qblk):
        sc = jnp.einsum('bqkgd,bskd->bkgqs', qblk, k, preferred_element_type=jnp.float32) * scale
        p = jax.nn.softmax(sc, axis=-1)
        return jnp.einsum('bkgqs,bskd->bqkgd', p.astype(v.dtype), v)

    o = lax.map(one_block, qb)
    return o.transpose(1, 0, 2, 3, 4, 5).reshape(b, s, ATTN_WIDTH)


def block_diag(x, w, bias):
    xb = x.reshape(x.shape[:-1] + (RNN_BLOCKS, RNN_BLOCK_W))
    return jnp.einsum('bsni,nij->bsnj', xb, w).reshape(x.shape) + bias


def linear_scan(a, u, reverse):
    def combine(l, r):
        a_l, u_l = l
        a_r, u_r = r
        return a_r * a_l, a_r * u_l + u_r
    _, h = lax.associative_scan(combine, (a, u), reverse=reverse, axis=1)
    return h


def rg_lru(xc, w_a, b_a, w_x, b_x, lam, reverse):
    r = jax.nn.sigmoid(block_diag(xc, w_a, b_a).astype(jnp.float32))
    i = jax.nn.sigmoid(block_diag(xc, w_x, b_x).astype(jnp.float32))
    log_a = -LRU_C * r * jax.nn.softplus(-lam.astype(jnp.float32))
    a = jnp.exp(log_a)
    u = jnp.sqrt(-jnp.expm1(2.0 * log_a)) * (i * xc.astype(jnp.float32))
    return linear_scan(a, u, reverse)


def recurrent_branch(xr, gate, conv_w, conv_b, lru_wa, lru_ba, lru_wx, lru_bx, lru_lam):
    xc = lax.conv_general_dilated(xr, conv_w[:, None, :], window_strides=(1,), padding=[CONV_PAD],
                                  dimension_numbers=('NWC', 'WIO', 'NWC'),
                                  feature_group_count=RNN_WIDTH) + conv_b
    h = (rg_lru(xc, lru_wa[0], lru_ba[0], lru_wx[0], lru_bx[0], lru_lam[0], False)
         + rg_lru(xc, lru_wa[1], lru_ba[1], lru_wx[1], lru_bx[1], lru_lam[1], True))
    return h.astype(xr.dtype) * jax.nn.gelu(gate)


def mixer(h, rope, w_in, q_gain, k_gain, conv_w, conv_b, lru_wa, lru_ba, lru_wx, lru_bx, lru_lam,
          w_pa, w_pr, w_out):
    b, s, _ = h.shape
    z = h @ w_in
    q, k, v, xr, gr, gl = jnp.split(z, IN_SPLITS, axis=-1)
    q = apply_axial_rope(rms_norm(q.reshape(b, s, N_HEADS, HEAD_DIM), q_gain), rope)
    k = apply_axial_rope(rms_norm(k.reshape(b, s, N_KV_HEADS, HEAD_DIM), k_gain), rope)
    v = v.reshape(b, s, N_KV_HEADS, HEAD_DIM)
    o_att = block_attention(q, k, v) @ w_pa
    o_rec = recurrent_branch(xr, gr, conv_w, conv_b, lru_wa, lru_ba, lru_wx, lru_bx, lru_lam) @ w_pr
    g = jax.nn.sigmoid(gl.astype(jnp.float32)).reshape(b, s, N_BRANCH, D_MODEL)
    merged = (g[:, :, 0] * o_att + g[:, :, 1] * o_rec).astype(h.dtype)
    return merged @ w_out


def moe(h, w_router, b_router, w1, b1, w2, b2):
    b, s, d = h.shape
    xf = h.reshape(-1, d)
    n = xf.shape[0]
    logits = (xf @ w_router + b_router).astype(jnp.float32)
    top_logits, top_idx = lax.top_k(logits, TOP_K)
    top_w = jax.nn.softmax(top_logits, axis=-1)
    n_assign = n * TOP_K
    e_flat = top_idx.reshape(-1)
    order = jnp.argsort(e_flat, stable=True)
    e_sorted = e_flat[order]
    tok_sorted = (order // TOP_K).astype(jnp.int32)
    w_sorted = top_w.reshape(-1)[order]
    counts = jnp.bincount(e_flat, length=N_EXPERTS)
    padded = (counts + MOE_BLOCK - 1) // MOE_BLOCK * MOE_BLOCK
    pad_end = jnp.cumsum(padded)
    pad_start = pad_end - padded
    start = jnp.cumsum(counts) - counts
    dest = pad_start[e_sorted] + jnp.arange(n_assign) - start[e_sorted]
    n_blocks = -(-n_assign // MOE_BLOCK) + N_EXPERTS
    cap = n_blocks * MOE_BLOCK
    slot_tok = jnp.zeros((cap,), jnp.int32).at[dest].set(tok_sorted)
    slot_w = jnp.zeros((cap,), jnp.float32).at[dest].set(w_sorted)
    block_expert = jnp.minimum(
        jnp.searchsorted(pad_end, jnp.arange(n_blocks) * MOE_BLOCK, side='right'), N_EXPERTS - 1)
    xs = xf[slot_tok].reshape(n_blocks, MOE_BLOCK, d)

    def expert_block(args):
        xb, e = args
        gu = xb @ w1[e] + b1[e]
        glu = jnp.minimum(gu[..., :D_FF], SWIGLU_LIMIT)
        lin = jnp.clip(gu[..., D_FF:], -SWIGLU_LIMIT, SWIGLU_LIMIT)
        act = (lin + 1.0) * glu * jax.nn.sigmoid(SWIGLU_ALPHA * glu)
        return act @ w2[e] + b2[e]

    ys = lax.map(expert_block, (xs, block_expert))
    y = jax.ops.segment_sum(ys.reshape(cap, d) * slot_w[:, None].astype(ys.dtype), slot_tok,
                            num_segments=n)
    return y.reshape(b, s, d)


def encoder_layer(x, c, rope, w_ada, b_ada, w_in, q_gain, k_gain, conv_w, conv_b, lru_wa, lru_ba,
                  lru_wx, lru_bx, lru_lam, w_pa, w_pr, w_out, ln1_g, ln1_b, w_router, b_router,
                  w1, b1, w2, b2, ln2_g, ln2_b):
    b = x.shape[0]
    mod = (jax.nn.silu(c) @ w_ada + b_ada).reshape(b, N_MOD, 1, D_MODEL)
    sh1, sc1, g1, sh2, sc2, g2 = (mod[:, j] for j in range(N_MOD))
    h = layer_norm(x) * (1.0 + sc1) + sh1
    mix = mixer(h, rope, w_in, q_gain, k_gain, conv_w, conv_b, lru_wa, lru_ba, lru_wx, lru_bx,
                lru_lam, w_pa, w_pr, w_out)
    x = layer_norm(DN_ALPHA * x + (1.0 + g1) * mix, ln1_g, ln1_b)
    h = layer_norm(x) * (1.0 + sc2) + sh2
    ff = moe(h, w_router, b_router, w1, b1, w2, b2)
    x = layer_norm(DN_ALPHA * x + (1.0 + g2) * ff, ln2_g, ln2_b)
    return x


def run_trunk(x, c, params):
    rope = axial_rope_tables(x.shape[1])
    for l in range(DEPTH):
        x = encoder_layer(x, c, rope, *[p[l] for p in params])
    return x


def setup_inputs(seed: int = 0) -> dict:
    key = jax.random.key(seed)
    ks = jax.random.split(key, 40)
    f32 = jnp.float32

    def nrm(k, shape, scale):
        return jax.random.normal(k, shape, f32) * scale

    u = jax.random.uniform(ks[14], (DEPTH, 2, RNN_WIDTH), f32, minval=0.9, maxval=0.999)
    s_lam = u ** (1.0 / LRU_C)
    lru_lam = jnp.log(s_lam) - jnp.log1p(-s_lam)
    return {
        "x_prompt": nrm(ks[0], (BATCH, SEQ, D_MODEL), 1.0),
        "x_sample": nrm(ks[1], (DEC_BATCH, DEC_SEQ, D_MODEL), 1.0),
        "c_prompt": nrm(ks[2], (BATCH, D_MODEL), 1.0),
        "c_sample": nrm(ks[3], (DEC_BATCH, D_MODEL), 1.0),
        "w_ada": nrm(ks[4], (DEPTH, D_MODEL, N_MOD * D_MODEL), 0.2 * D_MODEL ** -0.5),
        "b_ada": nrm(ks[5], (DEPTH, N_MOD * D_MODEL), 0.01),
        "w_in": nrm(ks[6], (DEPTH, D_MODEL, IN_WIDTH), D_MODEL ** -0.5),
        "q_gain": 1.0 + nrm(ks[7], (DEPTH, HEAD_DIM), 0.02),
        "k_gain": 1.0 + nrm(ks[8], (DEPTH, HEAD_DIM), 0.02),
        "conv_w": nrm(ks[9], (DEPTH, CONV_W, RNN_WIDTH), CONV_W ** -0.5),
        "conv_b": nrm(ks[10], (DEPTH, RNN_WIDTH), 0.01),
        "lru_wa": nrm(ks[11], (DEPTH, 2, RNN_BLOCKS, RNN_BLOCK_W, RNN_BLOCK_W), RNN_BLOCK_W ** -0.5),
        "lru_ba": nrm(ks[12], (DEPTH, 2, RNN_WIDTH), 0.01),
        "lru_wx": nrm(ks[13], (DEPTH, 2, RNN_BLOCKS, RNN_BLOCK_W, RNN_BLOCK_W), RNN_BLOCK_W ** -0.5),
        "lru_bx": nrm(ks[15], (DEPTH, 2, RNN_WIDTH), 0.01),
        "lru_lam": lru_lam,
        "w_pa": nrm(ks[16], (DEPTH, ATTN_WIDTH, D_MODEL), DN_BETA * ATTN_WIDTH ** -0.5),
        "w_pr": nrm(ks[17], (DEPTH, RNN_WIDTH, D_MODEL), DN_BETA * RNN_WIDTH ** -0.5),
        "w_out": nrm(ks[18], (DEPTH, D_MODEL, D_MODEL), DN_BETA * D_MODEL ** -0.5),
        "ln1_g": 1.0 + nrm(ks[19], (DEPTH, D_MODEL), 0.02),
        "ln1_b": nrm(ks[20], (DEPTH, D_MODEL), 0.01),
        "w_router": nrm(ks[21], (DEPTH, D_MODEL, N_EXPERTS), D_MODEL ** -0.5),
        "b_router": nrm(ks[22], (DEPTH, N_EXPERTS), 0.01),
        "w1": nrm(ks[23], (DEPTH, N_EXPERTS, D_MODEL, 2 * D_FF), D_MODEL ** -0.5),
        "b1": nrm(ks[24], (DEPTH, N_EXPERTS, 2 * D_FF), 0.01),
        "w2": nrm(ks[25], (DEPTH, N_EXPERTS, D_FF, D_MODEL), DN_BETA * D_FF ** -0.5),
        "b2": nrm(ks[26], (DEPTH, N_EXPERTS, D_MODEL), 0.01),
        "ln2_g": 1.0 + nrm(ks[27], (DEPTH, D_MODEL), 0.02),
        "ln2_b": nrm(ks[28], (DEPTH, D_MODEL), 0.01),
    }


def reference(x_prompt, x_sample, c_prompt, c_sample, w_ada, b_ada, w_in, q_gain, k_gain, conv_w,
              conv_b, lru_wa, lru_ba, lru_wx, lru_bx, lru_lam, w_pa, w_pr, w_out, ln1_g, ln1_b,
              w_router, b_router, w1, b1, w2, b2, ln2_g, ln2_b):
    params = (w_ada, b_ada, w_in, q_gain, k_gain, conv_w, conv_b, lru_wa, lru_ba, lru_wx, lru_bx,
              lru_lam, w_pa, w_pr, w_out, ln1_g, ln1_b, w_router, b_router, w1, b1, w2, b2,
              ln2_g, ln2_b)
    y_prompt = run_trunk(x_prompt, c_prompt, params)
    y_sample = run_trunk(x_sample, c_sample, params)
    return (y_prompt, y_sample)
```

```python
import functools
import math

import jax
import jax.numpy as jnp
from jax import lax
from jax.experimental import pallas as pl
from jax.experimental.pallas import tpu as pltpu

F32 = jnp.float32
BF16 = jnp.bfloat16

D_MODEL = 1024
GRID_W = 64
N_HEADS = 8
N_KV_HEADS = 2
HEAD_DIM = 128
GROUPS = N_HEADS // N_KV_HEADS
KV_WIDTH = N_KV_HEADS * HEAD_DIM
ROPE_THETA = 10000.0
RNN_BLOCKS = 8
RNN_BLOCK_W = D_MODEL // RNN_BLOCKS
CONV_W = 4
LRU_C = 8.0
N_EXPERTS = 32
TOP_K = 4
D_FF = D_MODEL
SWIGLU_LIMIT = 7.0
SWIGLU_ALPHA = 1.702
DEPTH = 1
DN_ALPHA = (2 * DEPTH) ** 0.25
LN_EPS = 1e-5
RMS_EPS = 1e-6
N_MOD = 6
IN_WIDTH = D_MODEL + 2 * KV_WIDTH + 2 * D_MODEL + 2 * D_MODEL
_Q0, _K0, _V0, _XR0, _GR0, _GL0 = 0, 1024, 1280, 1536, 2560, 3584

V7X_VMEM_LIMIT_BYTES = 56 * 1024 * 1024
HALO = 16


def _tiles(n_tokens):
    big = n_tokens >= 4096
    return dict(
        tm_in=512 if big else 128,
        tq=256 if big else 128,
        tt=512 if big else 128,
        tm_mid=256 if big else 128,
        tp=512 if big else 128,
        tg=512 if big else 256,
        tb=512 if big else 128,
        tc=256,
    )


def _cparams(sem):
    return pltpu.CompilerParams(dimension_semantics=sem, vmem_limit_bytes=V7X_VMEM_LIMIT_BYTES)


def _const_spec(shape):
    nd = len(shape)
    return pl.BlockSpec(shape, lambda *_: (0,) * nd, pipeline_mode=pl.Buffered(1))


def _layer_norm(x):
    mu = jnp.mean(x, axis=-1, keepdims=True)
    xc = x - mu
    var = jnp.mean(xc * xc, axis=-1, keepdims=True)
    return xc * lax.rsqrt(var + LN_EPS)


def _sigmoid(x):
    return 1.0 / (1.0 + jnp.exp(-x))


def _ada_kernel(c_ref, w_ref, b_ref, o_ref):
    c = c_ref[...]
    s = c * _sigmoid(c)
    o_ref[...] = jnp.dot(s, w_ref[...], preferred_element_type=F32,
                         precision=lax.Precision.HIGHEST) + b_ref[...]


def _ada_mod(c_all, w_ada, b_ada):
    bp = c_all.shape[0]
    ncol = w_ada.shape[1]
    return pl.pallas_call(
        _ada_kernel,
        grid=(ncol // D_MODEL,),
        in_specs=[pl.BlockSpec((bp, D_MODEL), lambda j: (0, 0)),
                  pl.BlockSpec((D_MODEL, D_MODEL), lambda j: (0, j)),
                  pl.BlockSpec((1, D_MODEL), lambda j: (0, j))],
        out_specs=pl.BlockSpec((bp, D_MODEL), lambda j: (0, j)),
        out_shape=jax.ShapeDtypeStruct((bp, ncol), F32),
        compiler_params=_cparams(("arbitrary",)),
        name="ada_mod",
    )(c_all, w_ada, b_ada.reshape(1, ncol))


class _Layout:
    def __init__(self, trunks):
        self.trunks = tuple(trunks)
        self.n_tokens = sum(b * s for b, s in trunks)
        self.n_seqs = sum(b for b, _ in trunks)
        self.max_seq = max(s for _, s in trunks)

    def seq_and_pos(self, t0):
        seq = jnp.int32(0)
        pos = jnp.int32(0)
        tok_off, seq_off = 0, 0
        for b, s in self.trunks:
            inside = (t0 >= tok_off) & (t0 < tok_off + b * s)
            rel = jnp.maximum(t0 - tok_off, 0)
            seq = jnp.where(inside, seq_off + rel // s, seq)
            pos = jnp.where(inside, rel % s, pos)
            tok_off += b * s
            seq_off += b
        return seq, pos


def _rope_tables(max_seq):
    t = jnp.arange(max_seq)
    rows = (t // GRID_W).astype(F32)
    cols = (t % GRID_W).astype(F32)
    axis_dim = HEAD_DIM // 2
    inv = ROPE_THETA ** (-jnp.arange(0, axis_dim, 2, dtype=F32) / axis_dim)
    ar, ac = rows[:, None] * inv, cols[:, None] * inv
    cr, sr, cc, sc = jnp.cos(ar), jnp.sin(ar), jnp.cos(ac), jnp.sin(ac)
    z = jnp.zeros_like(sr)
    cos_t = jnp.concatenate([cr, cr, cc, cc], axis=-1)
    up_t = jnp.concatenate([-sr, z, -sc, z], axis=-1)
    dn_t = jnp.concatenate([z, sr, z, sc], axis=-1)
    return cos_t, up_t, dn_t


def _inproj_kernel(x_ref, mod_ref, w_ref, qg_ref, kg_ref, cos_ref, up_ref, dn_ref,
                   q_ref, k_ref, vt_ref, xr_ref, gg_ref, gs_ref):
    x = x_ref[...]
    sh1 = mod_ref[0, 0:1, :]
    sc1 = mod_ref[0, 1:2, :]
    h = (_layer_norm(x) * (1.0 + sc1) + sh1).astype(BF16)
    cos_t, up_t, dn_t = cos_ref[...], up_ref[...], dn_ref[...]

    def proj(c0, width):
        return jnp.dot(h, w_ref[:, c0:c0 + width], preferred_element_type=F32)

    def norm_rope(z, gain):
        ms = jnp.mean(z * z, axis=-1, keepdims=True)
        y = z * lax.rsqrt(ms + RMS_EPS) * gain
        return (y * cos_t + pltpu.roll(y, HEAD_DIM - 32, 1) * up_t
                + pltpu.roll(y, 32, 1) * dn_t)

    zq = proj(_Q0, D_MODEL)
    qg = qg_ref[...] * (HEAD_DIM ** -0.5)
    for hd in range(N_HEADS):
        sl = slice(hd * HEAD_DIM, (hd + 1) * HEAD_DIM)
        q_ref[:, sl] = norm_rope(zq[:, sl], qg).astype(BF16)
    zk = proj(_K0, KV_WIDTH)
    kg = kg_ref[...]
    for hd in range(N_KV_HEADS):
        sl = slice(hd * HEAD_DIM, (hd + 1) * HEAD_DIM)
        k_ref[:, sl] = norm_rope(zk[:, sl], kg).astype(BF16)
    zv = proj(_V0, KV_WIDTH)
    vt_ref[0] = zv.T.astype(BF16)
    xr_ref[...] = proj(_XR0, D_MODEL).astype(BF16)
    gg_ref[...] = jax.nn.gelu(proj(_GR0, D_MODEL), approximate=True).astype(BF16)
    gs_ref[...] = _sigmoid(proj(_GL0, 2 * D_MODEL)).astype(BF16)


def _inproj(x, mod, w_in, q_gain, k_gain, tables, lay, tm):
    nt = lay.n_tokens
    cos_t, up_t, dn_t = tables

    def seq_map(i):
        return (lay.seq_and_pos(i * tm)[0], 0, 0)

    def pos_map(i):
        return (lay.seq_and_pos(i * tm)[1] // tm, 0)

    tok = lambda w: pl.BlockSpec((tm, w), lambda i: (i, 0))
    rope = pl.BlockSpec((tm, HEAD_DIM), pos_map)
    return pl.pallas_call(
        _inproj_kernel,
        grid=(nt // tm,),
        in_specs=[tok(D_MODEL),
                  pl.BlockSpec((1, N_MOD, D_MODEL), seq_map),
                  _const_spec((D_MODEL, IN_WIDTH)),
                  _const_spec((1, HEAD_DIM)), _const_spec((1, HEAD_DIM)),
                  rope, rope, rope],
        out_specs=[tok(D_MODEL), tok(KV_WIDTH),
                   pl.BlockSpec((1, KV_WIDTH, tm), lambda i: (i, 0, 0)),
                   tok(D_MODEL), tok(D_MODEL), tok(2 * D_MODEL)],
        out_shape=[jax.ShapeDtypeStruct((nt, D_MODEL), BF16),
                   jax.ShapeDtypeStruct((nt, KV_WIDTH), BF16),
                   jax.ShapeDtypeStruct((nt // tm, KV_WIDTH, tm), BF16),
                   jax.ShapeDtypeStruct((nt, D_MODEL), BF16),
                   jax.ShapeDtypeStruct((nt, D_MODEL), BF16),
                   jax.ShapeDtypeStruct((nt, 2 * D_MODEL), BF16)],
        compiler_params=_cparams(("arbitrary",)),
        name="in_proj",
    )(x, mod, w_in, q_gain.reshape(1, HEAD_DIM), k_gain.reshape(1, HEAD_DIM), cos_t, up_t, dn_t)


def _attn_kernel(q_ref, k_ref, vt_ref, o_ref, qt_scr, m_scr, l_scr, acc_scr, *, tq, tk, n_kv):
    for g in range(GROUPS):
        qg = q_ref[:, g * HEAD_DIM:(g + 1) * HEAD_DIM].astype(F32)
        qt_scr[:, g * tq:(g + 1) * tq] = qg.T.astype(BF16)
    m_scr[...] = jnp.full(m_scr.shape, -jnp.inf, F32)
    l_scr[...] = jnp.zeros(l_scr.shape, F32)
    acc_scr[...] = jnp.zeros(acc_scr.shape, F32)

    def body(j, carry):
        kt = k_ref[pl.ds(pl.multiple_of(j * tk, tk), tk), :]
        s = jnp.dot(kt, qt_scr[...], preferred_element_type=F32)
        m_old = m_scr[...]
        m_new = jnp.maximum(m_old, jnp.max(s, axis=0, keepdims=True))
        alpha = jnp.exp(m_old - m_new)
        p = jnp.exp(s - m_new)
        l_scr[...] = alpha * l_scr[...] + jnp.sum(p, axis=0, keepdims=True)
        pv = jnp.dot(vt_ref[j], p.astype(BF16), preferred_element_type=F32)
        acc_scr[...] = alpha * acc_scr[...] + pv
        m_scr[...] = m_new
        return carry

    lax.fori_loop(0, n_kv, body, 0)
    out = acc_scr[...] / l_scr[...]
    for g in range(GROUPS):
        o_ref[:, g * HEAD_DIM:(g + 1) * HEAD_DIM] = out[:, g * tq:(g + 1) * tq].T.astype(BF16)


def _attention(q, k, vt3, tok_off, batch, seq, tq, tk):
    n_kv = seq // tk
    qrow0 = tok_off // tq
    srow0 = tok_off // seq
    gw = GROUPS * HEAD_DIM
    n_q = seq // tq
    return pl.pallas_call(
        functools.partial(_attn_kernel, tq=tq, tk=tk, n_kv=n_kv),
        grid=(batch, N_KV_HEADS, n_q),
        in_specs=[pl.BlockSpec((tq, gw), lambda b, h, i: (qrow0 + b * n_q + i, h)),
                  pl.BlockSpec((seq, HEAD_DIM), lambda b, h, i: (srow0 + b, h)),
                  pl.BlockSpec((n_kv, HEAD_DIM, tk), lambda b, h, i: (srow0 + b, h, 0))],
        out_specs=pl.BlockSpec((tq, gw), lambda b, h, i: (b * n_q + i, h)),
        out_shape=jax.ShapeDtypeStruct((batch * seq, D_MODEL), BF16),
        scratch_shapes=[pltpu.VMEM((HEAD_DIM, GROUPS * tq), BF16),
                        pltpu.VMEM((1, GROUPS * tq), F32),
                        pltpu.VMEM((1, GROUPS * tq), F32),
                        pltpu.VMEM((HEAD_DIM, GROUPS * tq), F32)],
        compiler_params=_cparams(("arbitrary", "arbitrary", "arbitrary")),
        name="attention",
    )(q, k, vt3)


def _scan_kernel(cur_ref, prev_ref, next_ref, cw_ref, cb_ref, w_ref, b_ref, lam_ref, o_ref,
                 ext_scr, a_scr, u_scr, h_scr, carry_scr, *, tt, n_chunks):
    d = pl.program_id(1)
    c = pl.program_id(2)
    chunk = jnp.where(d == 0, c, n_chunks - 1 - c)

    @pl.when(c == 0)
    def _():
        carry_scr[...] = jnp.zeros(carry_scr.shape, F32)

    keep_prev = jnp.where(chunk == 0, 0.0, 1.0)
    keep_next = jnp.where(chunk == n_chunks - 1, 0.0, 1.0)
    ext_scr[0:HALO, :] = prev_ref[...].astype(F32) * keep_prev
    ext_scr[HALO:HALO + tt, :] = cur_ref[...].astype(F32)
    ext_scr[HALO + tt:HALO + tt + HALO, :] = next_ref[...].astype(F32) * keep_next
    xc = cb_ref[...] + sum(cw_ref[j:j + 1, :] * ext_scr[HALO - 2 + j:HALO - 2 + j + tt, :]
                           for j in range(CONV_W))
    xcb = xc.astype(BF16)

    lam = lam_ref[0]
    y = jnp.exp(-jnp.abs(lam))
    w1p = 1.0 + y
    log1p_y = jnp.where(w1p == 1.0, y, jnp.log(w1p) * y / jnp.where(w1p == 1.0, 1.0, w1p - 1.0))
    neg_c_sp = -LRU_C * (jnp.maximum(-lam, 0.0) + log1p_y)

    for n in range(RNN_BLOCKS):
        sl = slice(n * RNN_BLOCK_W, (n + 1) * RNN_BLOCK_W)
        pre = jnp.dot(xcb[:, sl], w_ref[0, n], preferred_element_type=F32)
        r = _sigmoid(pre[:, :RNN_BLOCK_W] + b_ref[0, 0:1, sl])
        i = _sigmoid(pre[:, RNN_BLOCK_W:] + b_ref[0, 1:2, sl])
        a = jnp.exp(r * neg_c_sp[:, sl])
        a_scr[:, sl] = a
        u_scr[:, sl] = jnp.sqrt(1.0 - a * a) * (i * xc[:, sl])

    def step(t, h):
        row = jnp.where(d == 0, t, tt - 1 - t)
        h = a_scr[pl.ds(row, 1), :] * h + u_scr[pl.ds(row, 1), :]
        h_scr[pl.ds(row, 1), :] = h
        return h

    carry_scr[...] = lax.fori_loop(0, tt, step, carry_scr[...], unroll=8)
    o_ref[0] = h_scr[...].astype(BF16)


def _scan(xr, conv_w, conv_b, w_lru, b_lru, lam, tok_off, batch, seq, tt):
    nt = xr.shape[0]
    n_chunks = seq // tt
    row0 = tok_off // tt
    hrow0 = tok_off // HALO
    hpc = tt // HALO
    n_halo = nt // HALO

    def chunk_of(d, c):
        return jnp.where(d == 0, c, n_chunks - 1 - c)

    def prev_map(b, d, c):
        return (jnp.maximum(hrow0 + (b * n_chunks + chunk_of(d, c)) * hpc - 1, 0), 0)

    def next_map(b, d, c):
        return (jnp.minimum(hrow0 + (b * n_chunks + chunk_of(d, c) + 1) * hpc, n_halo - 1), 0)

    return pl.pallas_call(
        functools.partial(_scan_kernel, tt=tt, n_chunks=n_chunks),
        grid=(batch, 2, n_chunks),
        in_specs=[pl.BlockSpec((tt, D_MODEL), lambda b, d, c: (row0 + b * n_chunks + chunk_of(d, c), 0)),
                  pl.BlockSpec((HALO, D_MODEL), prev_map),
                  pl.BlockSpec((HALO, D_MODEL), next_map),
                  pl.BlockSpec((CONV_W, D_MODEL), lambda b, d, c: (0, 0)),
                  pl.BlockSpec((1, D_MODEL), lambda b, d, c: (0, 0)),
                  pl.BlockSpec((1, RNN_BLOCKS, RNN_BLOCK_W, 2 * RNN_BLOCK_W), lambda b, d, c: (d, 0, 0, 0)),
                  pl.BlockSpec((1, 2, D_MODEL), lambda b, d, c: (d, 0, 0)),
                  pl.BlockSpec((1, 1, D_MODEL), lambda b, d, c: (d, 0, 0))],
        out_specs=pl.BlockSpec((1, tt, D_MODEL), lambda b, d, c: (d, b * n_chunks + chunk_of(d, c), 0)),
        out_shape=jax.ShapeDtypeStruct((2, batch * seq, D_MODEL), BF16),
        scratch_shapes=[pltpu.VMEM((tt + 2 * HALO, D_MODEL), F32),
                        pltpu.VMEM((tt, D_MODEL), F32),
                        pltpu.VMEM((tt, D_MODEL), F32),
                        pltpu.VMEM((tt, D_MODEL), F32),
                        pltpu.VMEM((1, D_MODEL), F32)],
        compiler_params=_cparams(("arbitrary", "arbitrary", "arbitrary")),
        name="lru_scan",
    )(xr, xr, xr, conv_w, conv_b, w_lru, b_lru, lam)


def _mid_kernel(*refs, tile_ends):
    n_tr = len(tile_ends)
    oa_refs = refs[:n_tr]
    h_refs = refs[n_tr:3 * n_tr]
    (gg_ref, gs_ref, x_ref, mod_ref, wpa_ref, wpr_ref, wo_ref, g1_ref, b1_ref, wr_ref, br_ref,
     x1_ref, h2_ref, wf_ref, ms_ref) = refs[3 * n_tr:]
    i = pl.program_id(0)
    oa = oa_refs[-1][...]
    hsum = h_refs[-2][0].astype(F32) + h_refs[-1][0].astype(F32)
    for j in range(n_tr - 2, -1, -1):
        mine = i < tile_ends[j]
        oa = jnp.where(mine, oa_refs[j][...], oa)
        hsum = jnp.where(mine, h_refs[2 * j][0].astype(F32) + h_refs[2 * j + 1][0].astype(F32), hsum)
    o_att = jnp.dot(oa, wpa_ref[...], preferred_element_type=F32)
    rec = hsum.astype(BF16) * gg_ref[...]
    o_rec = jnp.dot(rec, wpr_ref[...], preferred_element_type=F32)
    gs = gs_ref[...].astype(F32)
    merged = gs[:, :D_MODEL] * o_att + gs[:, D_MODEL:] * o_rec
    mix = jnp.dot(merged.astype(BF16), wo_ref[...], preferred_element_type=F32)
    g1 = mod_ref[0, 2:3, :]
    sh2 = mod_ref[0, 3:4, :]
    sc2 = mod_ref[0, 4:5, :]
    x1 = _layer_norm(DN_ALPHA * x_ref[...] + (1.0 + g1) * mix) * g1_ref[...] + b1_ref[...]
    x1_ref[...] = x1
    h2 = _layer_norm(x1) * (1.0 + sc2) + sh2
    h2_ref[...] = h2
    logits = jnp.dot(h2, wr_ref[...], preferred_element_type=F32,
                     precision=lax.Precision.HIGHEST) + br_ref[...]
    lane = lax.broadcasted_iota(jnp.int32, logits.shape, 1).astype(F32)
    rem = logits
    sel = jnp.zeros(logits.shape, F32)
    top = None
    denom = None
    for kk in range(TOP_K):
        m = jnp.max(rem, axis=-1, keepdims=True)
        idx = jnp.min(jnp.where(rem == m, lane, float(N_EXPERTS)), axis=-1, keepdims=True)
        pick = lane == idx
        sel = jnp.where(pick, 1.0, sel)
        rem = jnp.where(pick, -jnp.inf, rem)
        if kk == 0:
            top = m
            denom = jnp.ones_like(m)
        else:
            denom = denom + jnp.exp(m - top)
    wf_ref[...] = jnp.where(sel > 0.0, jnp.exp(logits - top) / denom, 0.0)
    ms_ref[...] = sel


def _mid(o_atts, hfbs, gg, gs, x, mod, w_pa, w_pr, w_out, ln1_g, ln1_b, w_router, b_router, lay, tm):
    nt = lay.n_tokens

    def seq_map(i):
        return (lay.seq_and_pos(i * tm)[0], 0, 0)

    tok = lambda w: pl.BlockSpec((tm, w), lambda i: (i, 0))
    vec = lambda w: _const_spec((1, w))
    tile_ends, att_specs, h_specs, h_args = [], [], [], []
    t0 = 0
    for (b, s), hfb in zip(lay.trunks, hfbs):
        n_t = b * s // tm
        local = functools.partial(lambda i, t0, n_t: jnp.clip(i - t0, 0, n_t - 1), t0=t0, n_t=n_t)
        att_specs.append(pl.BlockSpec((tm, D_MODEL), lambda i, local=local: (local(i), 0)))
        for d in range(2):
            h_specs.append(pl.BlockSpec((1, tm, D_MODEL), lambda i, local=local, d=d: (d, local(i), 0)))
            h_args.append(hfb)
        t0 += n_t
        tile_ends.append(t0)
    return pl.pallas_call(
        functools.partial(_mid_kernel, tile_ends=tuple(tile_ends)),
        grid=(nt // tm,),
        in_specs=att_specs + h_specs + [
                  tok(D_MODEL), tok(2 * D_MODEL), tok(D_MODEL),
                  pl.BlockSpec((1, N_MOD, D_MODEL), seq_map),
                  _const_spec((D_MODEL, D_MODEL)), _const_spec((D_MODEL, D_MODEL)),
                  _const_spec((D_MODEL, D_MODEL)),
                  vec(D_MODEL), vec(D_MODEL),
                  _const_spec((D_MODEL, N_EXPERTS)), vec(N_EXPERTS)],
        out_specs=[tok(D_MODEL), tok(D_MODEL), tok(N_EXPERTS), tok(N_EXPERTS)],
        out_shape=[jax.ShapeDtypeStruct((nt, D_MODEL), F32),
                   jax.ShapeDtypeStruct((nt, D_MODEL), F32),
                   jax.ShapeDtypeStruct((nt, N_EXPERTS), F32),
                   jax.ShapeDtypeStruct((nt, N_EXPERTS), F32)],
        compiler_params=_cparams(("arbitrary",)),
        name="merge_router",
    )(*o_atts, *h_args, gg, gs, x, mod, w_pa, w_pr, w_out, ln1_g.reshape(1, -1), ln1_b.reshape(1, -1),
      w_router, b_router.reshape(1, -1))


def _rank_kernel(ms_ref, rank_ref, cnt_ref, tri_scr, run_scr, *, tp):
    @pl.when(pl.program_id(0) == 0)
    def _():
        r = lax.broadcasted_iota(jnp.int32, (tp, tp), 0)
        c = lax.broadcasted_iota(jnp.int32, (tp, tp), 1)
        tri_scr[...] = jnp.where(c < r, 1.0, 0.0).astype(BF16)
        run_scr[...] = jnp.zeros(run_scr.shape, F32)

    ms = ms_ref[...]
    before = jnp.dot(tri_scr[...], ms.astype(BF16), preferred_element_type=F32)
    rank_ref[...] = before + run_scr[...]
    run_scr[...] = run_scr[...] + jnp.sum(ms, axis=0, keepdims=True)
    cnt_ref[...] = jnp.broadcast_to(run_scr[...], cnt_ref.shape)


def _rank(msel, tp):
    nt = msel.shape[0]
    return pl.pallas_call(
        functools.partial(_rank_kernel, tp=tp),
        grid=(nt // tp,),
        in_specs=[pl.BlockSpec((tp, N_EXPERTS), lambda i: (i, 0))],
        out_specs=[pl.BlockSpec((tp, N_EXPERTS), lambda i: (i, 0)),
                   pl.BlockSpec((8, N_EXPERTS), lambda i: (0, 0))],
        out_shape=[jax.ShapeDtypeStruct((nt, N_EXPERTS), F32),
                   jax.ShapeDtypeStruct((8, N_EXPERTS), F32)],
        scratch_shapes=[pltpu.VMEM((tp, tp), BF16), pltpu.VMEM((1, N_EXPERTS), F32)],
        compiler_params=_cparams(("arbitrary",)),
        name="route_rank",
    )(msel)


def _slots_kernel(rank_ref, ms_ref, wf_ref, start_ref, slot_ref, wt_ref):
    ms = ms_ref[...]
    wf = wf_ref[...]
    slot_full = rank_ref[...] + start_ref[...]
    lane = lax.broadcasted_iota(jnp.int32, ms.shape, 1).astype(F32)
    out_lane = lax.broadcasted_iota(jnp.int32, slot_ref.shape, 1)
    slots = jnp.zeros(slot_ref.shape, F32)
    wts = jnp.zeros(wt_ref.shape, F32)
    rem = ms
    for kk in range(TOP_K):
        idx = jnp.min(jnp.where(rem > 0.0, lane, float(2 * N_EXPERTS)), axis=-1, keepdims=True)
        pick = lane == idx
        s_k = jnp.sum(jnp.where(pick, slot_full, 0.0), axis=-1, keepdims=True)
        w_k = jnp.sum(jnp.where(pick, wf, 0.0), axis=-1, keepdims=True)
        rem = jnp.where(pick, 0.0, rem)
        slots = jnp.where(out_lane == kk, s_k, slots)
        wts = jnp.where(out_lane == kk, w_k, wts)
    slot_ref[...] = slots.astype(jnp.int32)
    wt_ref[...] = wts


def _slots(rank, msel, wfull, pad_start, tp):
    nt = msel.shape[0]
    tok = pl.BlockSpec((tp, N_EXPERTS), lambda i: (i, 0))
    out = pl.BlockSpec((tp, 128), lambda i: (i, 0))
    return pl.pallas_call(
        _slots_kernel,
        grid=(nt // tp,),
        in_specs=[tok, tok, tok, pl.BlockSpec((1, N_EXPERTS), lambda i: (0, 0))],
        out_specs=[out, out],
        out_shape=[jax.ShapeDtypeStruct((nt, 128), jnp.int32),
                   jax.ShapeDtypeStruct((nt, 128), F32)],
        compiler_params=_cparams(("arbitrary",)),
        name="route_slots",
    )(rank, msel, wfull, pad_start)


def _row_copy(src_ref, src_row, dst_ref, dst_row, sem):
    return pltpu.make_async_copy(src_ref.at[pl.ds(src_row, 1), :], dst_ref.at[pl.ds(dst_row, 1), :], sem)


def _dispatch_kernel(slot_ref, tail_ref, h_ref, xs_ref, zero_scr, sem, zsem, *, tg, tb):
    @pl.when(pl.program_id(0) == 0)
    def _():
        zero_scr[...] = jnp.zeros(zero_scr.shape, F32)
        for e in range(N_EXPERTS):
            tail = pl.multiple_of(tail_ref[e], tb)
            pltpu.make_async_copy(zero_scr, xs_ref.at[pl.ds(tail, tb), :], zsem).start()
        for e in range(N_EXPERTS):
            pltpu.make_async_copy(zero_scr, xs_ref.at[pl.ds(0, tb), :], zsem).wait()

    def issue(t, carry):
        for kk in range(TOP_K):
            _row_copy(h_ref, t, xs_ref, slot_ref[t * TOP_K + kk], sem).start()
        return carry

    lax.fori_loop(0, tg, issue, 0)

    def drain(t, carry):
        for kk in range(TOP_K):
            _row_copy(h_ref, 0, xs_ref, 0, sem).wait()
        return carry

    lax.fori_loop(0, tg, drain, 0)


def _dispatch(slots_flat, tail_start, h2, cap, tg, tb):
    nt = h2.shape[0]
    return pl.pallas_call(
        functools.partial(_dispatch_kernel, tg=tg, tb=tb),
        grid=(nt // tg,),
        in_specs=[pl.BlockSpec((tg * TOP_K,), lambda i: (i,), memory_space=pltpu.SMEM),
                  pl.BlockSpec(memory_space=pltpu.SMEM),
                  pl.BlockSpec((tg, D_MODEL), lambda i: (i, 0))],
        out_specs=pl.BlockSpec(memory_space=pl.ANY),
        out_shape=jax.ShapeDtypeStruct((cap, D_MODEL), F32),
        scratch_shapes=[pltpu.VMEM((tb, D_MODEL), F32), pltpu.SemaphoreType.DMA(()),
                        pltpu.SemaphoreType.DMA(())],
        compiler_params=_cparams(("arbitrary",)),
        name="moe_dispatch",
    )(slots_flat, tail_start, h2)


def _expert_kernel(be_ref, bv_ref, xs_ref, w1_ref, b1_ref, w2_ref, b2_ref, ys_ref, *, tb):
    valid = bv_ref[pl.program_id(0)]

    @pl.when(valid > 0)
    def _():
        gu = jnp.dot(xs_ref[...].astype(BF16), w1_ref[0], preferred_element_type=F32) + b1_ref[0]
        glu = jnp.minimum(gu[:, :D_FF], SWIGLU_LIMIT)
        lin = jnp.clip(gu[:, D_FF:], -SWIGLU_LIMIT, SWIGLU_LIMIT)
        act = (lin + 1.0) * glu * _sigmoid(SWIGLU_ALPHA * glu)
        ys_ref[...] = jnp.dot(act.astype(BF16), w2_ref[0], preferred_element_type=F32) + b2_ref[0]

    @pl.when(valid <= 0)
    def _():
        ys_ref[...] = jnp.zeros(ys_ref.shape, F32)


def _experts(block_expert, block_valid, xs, w1, b1, w2, b2, tb):
    cap = xs.shape[0]
    grid_spec = pltpu.PrefetchScalarGridSpec(
        num_scalar_prefetch=2,
        grid=(cap // tb,),
        in_specs=[pl.BlockSpec((tb, D_MODEL), lambda i, be, bv: (i, 0)),
                  pl.BlockSpec((1, D_MODEL, 2 * D_FF), lambda i, be, bv: (be[i], 0, 0)),
                  pl.BlockSpec((1, 1, 2 * D_FF), lambda i, be, bv: (be[i], 0, 0)),
                  pl.BlockSpec((1, D_FF, D_MODEL), lambda i, be, bv: (be[i], 0, 0)),
                  pl.BlockSpec((1, 1, D_MODEL), lambda i, be, bv: (be[i], 0, 0))],
        out_specs=pl.BlockSpec((tb, D_MODEL), lambda i, be, bv: (i, 0)),
    )
    return pl.pallas_call(
        functools.partial(_expert_kernel, tb=tb),
        grid_spec=grid_spec,
        out_shape=jax.ShapeDtypeStruct((cap, D_MODEL), F32),
        compiler_params=_cparams(("arbitrary",)),
        name="moe_experts",
    )(block_expert, block_valid, xs, w1, b1.reshape(N_EXPERTS, 1, -1), w2, b2.reshape(N_EXPERTS, 1, -1))


def _combine_kernel(slot_ref, ys_ref, wt_ref, x1_ref, mod_ref, g_ref, b_ref, y_ref, buf, sem, *, tc):
    def issue(t, carry):
        for kk in range(TOP_K):
            pltpu.make_async_copy(ys_ref.at[pl.ds(slot_ref[t * TOP_K + kk], 1), :],
                                  buf.at[kk, pl.ds(t, 1), :], sem).start()
        return carry

    lax.fori_loop(0, tc, issue, 0)

    def drain(t, carry):
        for kk in range(TOP_K):
            pltpu.make_async_copy(ys_ref.at[pl.ds(0, 1), :], buf.at[kk, pl.ds(0, 1), :], sem).wait()
        return carry

    lax.fori_loop(0, tc, drain, 0)
    wt = wt_ref[...]
    ff = sum(wt[:, kk:kk + 1] * buf[kk] for kk in range(TOP_K))
    g2 = mod_ref[0, 5:6, :]
    y_ref[...] = _layer_norm(DN_ALPHA * x1_ref[...] + (1.0 + g2) * ff) * g_ref[...] + b_ref[...]


def _combine(slots_flat, ys, wts, x1, mod, ln2_g, ln2_b, lay, tc):
    nt = lay.n_tokens

    def seq_map(i):
        return (lay.seq_and_pos(i * tc)[0], 0, 0)

    return pl.pallas_call(
        functools.partial(_combine_kernel, tc=tc),
        grid=(nt // tc,),
        in_specs=[pl.BlockSpec((tc * TOP_K,), lambda i: (i,), memory_space=pltpu.SMEM),
                  pl.BlockSpec(memory_space=pl.ANY),
                  pl.BlockSpec((tc, 128), lambda i: (i, 0)),
                  pl.BlockSpec((tc, D_MODEL), lambda i: (i, 0)),
                  pl.BlockSpec((1, N_MOD, D_MODEL), seq_map),
                  _const_spec((1, D_MODEL)), _const_spec((1, D_MODEL))],
        out_specs=pl.BlockSpec((tc, D_MODEL), lambda i: (i, 0)),
        out_shape=jax.ShapeDtypeStruct((nt, D_MODEL), F32),
        scratch_shapes=[pltpu.VMEM((TOP_K, tc, D_MODEL), F32), pltpu.SemaphoreType.DMA(())],
        compiler_params=_cparams(("arbitrary",)),
        name="moe_combine",
    )(slots_flat, ys, wts, x1, mod, ln2_g.reshape(1, -1), ln2_b.reshape(1, -1))


def _encoder_layer(xs, cs, p):
    trunks = [(x.shape[0], x.shape[1]) for x in xs]
    lay = _Layout(trunks)
    nt = lay.n_tokens
    t = _tiles(nt)

    x = jnp.concatenate([xi.reshape(-1, D_MODEL) for xi in xs], axis=0)
    c = jnp.concatenate(cs, axis=0)
    bp = -(-lay.n_seqs // 8) * 8
    c = jnp.pad(c, ((0, bp - lay.n_seqs), (0, 0)))
    mod = _ada_mod(c, p["w_ada"], p["b_ada"]).reshape(bp, N_MOD, D_MODEL)

    q, k, vt3, xr, gg, gs = _inproj(x, mod, p["w_in"].astype(BF16), p["q_gain"], p["k_gain"],
                                    _rope_tables(lay.max_seq), lay, t["tm_in"])

    w_lru = jnp.concatenate([p["lru_wa"], p["lru_wx"]], axis=-1).astype(BF16)
    b_lru = jnp.stack([p["lru_ba"], p["lru_bx"]], axis=1)
    lam = p["lru_lam"].reshape(2, 1, D_MODEL)
    o_atts, hfbs = [], []
    tok_off = 0
    for b, s in trunks:
        o_atts.append(_attention(q, k, vt3, tok_off, b, s, t["tq"], t["tm_in"]))
        hfbs.append(_scan(xr, p["conv_w"], p["conv_b"].reshape(1, -1), w_lru, b_lru, lam,
                          tok_off, b, s, t["tt"]))
        tok_off += b * s

    x1, h2, wfull, msel = _mid(o_atts, hfbs, gg, gs, x, mod, p["w_pa"].astype(BF16),
                               p["w_pr"].astype(BF16), p["w_out"].astype(BF16), p["ln1_g"], p["ln1_b"],
                               p["w_router"], p["b_router"], lay, t["tm_mid"])

    tb = t["tb"]
    rank, cnt = _rank(msel, t["tp"])
    counts = cnt[0].astype(jnp.int32)
    padded = (counts + tb - 1) // tb * tb
    pad_end = jnp.cumsum(padded)
    pad_start = pad_end - padded
    n_blocks = nt * TOP_K // tb + N_EXPERTS
    blk0 = jnp.arange(n_blocks, dtype=jnp.int32) * tb
    block_expert = jnp.minimum(jnp.searchsorted(pad_end, blk0, side="right"), N_EXPERTS - 1).astype(jnp.int32)
    block_valid = jnp.clip(pad_start[block_expert] + counts[block_expert] - blk0, 0, tb).astype(jnp.int32)
    slots, wts = _slots(rank, msel, wfull, pad_start.astype(F32).reshape(1, N_EXPERTS), t["tp"])
    slots_flat = slots[:, :TOP_K].reshape(-1)

    tail_start = jnp.where(padded > 0, pad_end - tb, pad_end[-1] - tb).astype(jnp.int32)
    xs_rows = _dispatch(slots_flat, tail_start, h2, n_blocks * tb, t["tg"], tb)
    ys = _experts(block_expert, block_valid, xs_rows, p["w1"].astype(BF16), p["b1"],
                  p["w2"].astype(BF16), p["b2"], tb)
    y = _combine(slots_flat, ys, wts, x1, mod, p["ln2_g"], p["ln2_b"], lay, t["tc"])

    outs, tok_off = [], 0
    for b, s in trunks:
        outs.append(y[tok_off:tok_off + b * s].reshape(b, s, D_MODEL))
        tok_off += b * s
    return outs


_PARAM_NAMES = ("w_ada", "b_ada", "w_in", "q_gain", "k_gain", "conv_w", "conv_b", "lru_wa", "lru_ba",
                "lru_wx", "lru_bx", "lru_lam", "w_pa", "w_pr", "w_out", "ln1_g", "ln1_b", "w_router",
                "b_router", "w1", "b1", "w2", "b2", "ln2_g", "ln2_b")


def kernel(x_prompt, x_sample, c_prompt, c_sample, w_ada, b_ada, w_in, q_gain, k_gain, conv_w, conv_b, lru_wa, lru_ba, lru_wx, lru_bx, lru_lam, w_pa, w_pr, w_out, ln1_g, ln1_b, w_router, b_router, w1, b1, w2, b2, ln2_g, ln2_b):
    stacked = (w_ada, b_ada, w_in, q_gain, k_gain, conv_w, conv_b, lru_wa, lru_ba, lru_wx, lru_bx,
               lru_lam, w_pa, w_pr, w_out, ln1_g, ln1_b, w_router, b_router, w1, b1, w2, b2, ln2_g, ln2_b)
    xs, cs = [x_prompt, x_sample], [c_prompt, c_sample]
    for layer in range(DEPTH):
        p = {name: arr[layer] for name, arr in zip(_PARAM_NAMES, stacked)}
        xs = _encoder_layer(xs, cs, p)
    return (xs[0], xs[1])
```

```python
import functools
import math

import jax
import jax.numpy as jnp
from jax import lax
from jax.experimental import pallas as pl
from jax.experimental.pallas import tpu as pltpu

F32 = jnp.float32
BF16 = jnp.bfloat16

D_MODEL = 1024
GRID_W = 64
N_HEADS = 8
N_KV_HEADS = 2
HEAD_DIM = 128
GROUPS = N_HEADS // N_KV_HEADS
KV_WIDTH = N_KV_HEADS * HEAD_DIM
ROPE_THETA = 10000.0
RNN_BLOCKS = 8
RNN_BLOCK_W = D_MODEL // RNN_BLOCKS
CONV_W = 4
LRU_C = 8.0
N_EXPERTS = 32
TOP_K = 4
D_FF = D_MODEL
SWIGLU_LIMIT = 7.0
SWIGLU_ALPHA = 1.702
DEPTH = 1
DN_ALPHA = (2 * DEPTH) ** 0.25
LN_EPS = 1e-5
RMS_EPS = 1e-6
N_MOD = 6
IN_WIDTH = D_MODEL + 2 * KV_WIDTH + 2 * D_MODEL + 2 * D_MODEL
_Q0, _K0, _V0, _XR0, _GR0, _GL0 = 0, 1024, 1280, 1536, 2560, 3584

V7X_VMEM_LIMIT_BYTES = 56 * 1024 * 1024
HALO = 16


def _tiles(n_tokens):
    big = n_tokens >= 4096
    return dict(
        tm_in=512 if big else 128,
        tq=256 if big else 128,
        tt=512 if big else 128,
        tm_mid=256 if big else 128,
        tp=512 if big else 128,
        tg=512 if big else 256,
        tb=512 if big else 128,
        tc=256,
    )


def _cparams(sem):
    return pltpu.CompilerParams(dimension_semantics=sem, vmem_limit_bytes=V7X_VMEM_LIMIT_BYTES)


def _const_spec(shape):
    nd = len(shape)
    return pl.BlockSpec(shape, lambda *_: (0,) * nd, pipeline_mode=pl.Buffered(1))


def _layer_norm(x):
    mu = jnp.mean(x, axis=-1, keepdims=True)
    xc = x - mu
    var = jnp.mean(xc * xc, axis=-1, keepdims=True)
    return xc * lax.rsqrt(var + LN_EPS)


def _sigmoid(x):
    return 1.0 / (1.0 + jnp.exp(-x))


def _ada_kernel(c_ref, w_ref, b_ref, o_ref):
    c = c_ref[...]
    s = c * _sigmoid(c)
    o_ref[...] = jnp.dot(s, w_ref[...], preferred_element_type=F32,
                         precision=lax.Precision.HIGHEST) + b_ref[...]


def _ada_mod(c_all, w_ada, b_ada):
    bp = c_all.shape[0]
    ncol = w_ada.shape[1]
    return pl.pallas_call(
        _ada_kernel,
        grid=(ncol // D_MODEL,),
        in_specs=[pl.BlockSpec((bp, D_MODEL), lambda j: (0, 0)),
                  pl.BlockSpec((D_MODEL, D_MODEL), lambda j: (0, j)),
                  pl.BlockSpec((1, D_MODEL), lambda j: (0, j))],
        out_specs=pl.BlockSpec((bp, D_MODEL), lambda j: (0, j)),
        out_shape=jax.ShapeDtypeStruct((bp, ncol), F32),
        compiler_params=_cparams(("arbitrary",)),
        name="ada_mod",
    )(c_all, w_ada, b_ada.reshape(1, ncol))


class _Layout:
    def __init__(self, trunks):
        self.trunks = tuple(trunks)
        self.n_tokens = sum(b * s for b, s in trunks)
        self.n_seqs = sum(b for b, _ in trunks)
        self.max_seq = max(s for _, s in trunks)

    def seq_and_pos(self, t0):
        seq = jnp.int32(0)
        pos = jnp.int32(0)
        tok_off, seq_off = 0, 0
        for b, s in self.trunks:
            inside = (t0 >= tok_off) & (t0 < tok_off + b * s)
            rel = jnp.maximum(t0 - tok_off, 0)
            seq = jnp.where(inside, seq_off + rel // s, seq)
            pos = jnp.where(inside, rel % s, pos)
            tok_off += b * s
            seq_off += b
        return seq, pos

    def trunk_specs(self, tm, make_spec):
        specs, ends, t0 = [], [], 0
        for b, s in self.trunks:
            n_t = b * s // tm
            specs.append(make_spec(functools.partial(_clamped_local, t0=t0, n_t=n_t)))
            t0 += n_t
            ends.append(t0)
        return specs, tuple(ends)


def _clamped_local(i, *, t0, n_t):
    return jnp.clip(i - t0, 0, n_t - 1)


def _owner_value(loads, i, tile_ends):
    val = loads[-1]()
    for j in range(len(loads) - 2, -1, -1):
        val = jnp.where(i < tile_ends[j], loads[j](), val)
    return val


def _rope_tables(max_seq):
    t = jnp.arange(max_seq)
    rows = (t // GRID_W).astype(F32)
    cols = (t % GRID_W).astype(F32)
    axis_dim = HEAD_DIM // 2
    inv = ROPE_THETA ** (-jnp.arange(0, axis_dim, 2, dtype=F32) / axis_dim)
    ar, ac = rows[:, None] * inv, cols[:, None] * inv
    cr, sr, cc, sc = jnp.cos(ar), jnp.sin(ar), jnp.cos(ac), jnp.sin(ac)
    z = jnp.zeros_like(sr)
    cos_t = jnp.concatenate([cr, cr, cc, cc], axis=-1)
    up_t = jnp.concatenate([-sr, z, -sc, z], axis=-1)
    dn_t = jnp.concatenate([z, sr, z, sc], axis=-1)
    return cos_t, up_t, dn_t


def _inproj_kernel(*refs, tile_ends):
    n_tr = len(tile_ends)
    x_refs = refs[:n_tr]
    (mod_ref, w_ref, qg_ref, kg_ref, cos_ref, up_ref, dn_ref,
     q_ref, k_ref, vt_ref, xr_ref, gg_ref, gs_ref) = refs[n_tr:]
    x = _owner_value([lambda r=r: r[...] for r in x_refs], pl.program_id(0), tile_ends)
    sh1 = mod_ref[0, 0:1, :]
    sc1 = mod_ref[0, 1:2, :]
    h = (_layer_norm(x) * (1.0 + sc1) + sh1).astype(BF16)
    cos_t, up_t, dn_t = cos_ref[...], up_ref[...], dn_ref[...]

    def proj(c0, width):
        return jnp.dot(h, w_ref[:, c0:c0 + width], preferred_element_type=F32)

    def norm_rope(z, gain):
        ms = jnp.mean(z * z, axis=-1, keepdims=True)
        y = z * lax.rsqrt(ms + RMS_EPS) * gain
        return (y * cos_t + pltpu.roll(y, HEAD_DIM - 32, 1) * up_t
                + pltpu.roll(y, 32, 1) * dn_t)

    zq = proj(_Q0, D_MODEL)
    qg = qg_ref[...] * (HEAD_DIM ** -0.5 * math.log2(math.e))
    for hd in range(N_HEADS):
        sl = slice(hd * HEAD_DIM, (hd + 1) * HEAD_DIM)
        q_ref[:, sl] = norm_rope(zq[:, sl], qg).astype(BF16)
    zk = proj(_K0, KV_WIDTH)
    kg = kg_ref[...]
    for hd in range(N_KV_HEADS):
        sl = slice(hd * HEAD_DIM, (hd + 1) * HEAD_DIM)
        k_ref[:, sl] = norm_rope(zk[:, sl], kg).astype(BF16)
    zv = proj(_V0, KV_WIDTH)
    vt_ref[0] = zv.T.astype(BF16)
    xr_ref[...] = proj(_XR0, D_MODEL).astype(BF16)
    gg_ref[...] = jax.nn.gelu(proj(_GR0, D_MODEL), approximate=True).astype(BF16)
    gs_ref[...] = _sigmoid(proj(_GL0, 2 * D_MODEL)).astype(BF16)


def _inproj(xs, mod, w_in, q_gain, k_gain, tables, lay, tm):
    nt = lay.n_tokens
    cos_t, up_t, dn_t = tables

    def seq_map(i):
        return (lay.seq_and_pos(i * tm)[0], 0, 0)

    def pos_map(i):
        return (lay.seq_and_pos(i * tm)[1] // tm, 0)

    tok = lambda w: pl.BlockSpec((tm, w), lambda i: (i, 0))
    rope = pl.BlockSpec((tm, HEAD_DIM), pos_map)
    x_specs, tile_ends = lay.trunk_specs(
        tm, lambda local: pl.BlockSpec((tm, D_MODEL), lambda i: (local(i), 0)))
    return pl.pallas_call(
        functools.partial(_inproj_kernel, tile_ends=tile_ends),
        grid=(nt // tm,),
        in_specs=x_specs + [
                  pl.BlockSpec((1, N_MOD, D_MODEL), seq_map),
                  _const_spec((D_MODEL, IN_WIDTH)),
                  _const_spec((1, HEAD_DIM)), _const_spec((1, HEAD_DIM)),
                  rope, rope, rope],
        out_specs=[tok(D_MODEL), tok(KV_WIDTH),
                   pl.BlockSpec((1, KV_WIDTH, tm), lambda i: (i, 0, 0)),
                   tok(D_MODEL), tok(D_MODEL), tok(2 * D_MODEL)],
        out_shape=[jax.ShapeDtypeStruct((nt, D_MODEL), BF16),
                   jax.ShapeDtypeStruct((nt, KV_WIDTH), BF16),
                   jax.ShapeDtypeStruct((nt // tm, KV_WIDTH, tm), BF16),
                   jax.ShapeDtypeStruct((nt, D_MODEL), BF16),
                   jax.ShapeDtypeStruct((nt, D_MODEL), BF16),
                   jax.ShapeDtypeStruct((nt, 2 * D_MODEL), BF16)],
        compiler_params=_cparams(("arbitrary",)),
        name="in_proj",
    )(*xs, mod, w_in, q_gain.reshape(1, HEAD_DIM), k_gain.reshape(1, HEAD_DIM), cos_t, up_t, dn_t)


def _attn_kernel(q_ref, k_ref, vt_ref, o_ref, qt_scr, s_scr, m_scr, l_scr, acc_scr, *, tq, tk, n_kv):
    for g in range(GROUPS):
        qg = q_ref[:, g * HEAD_DIM:(g + 1) * HEAD_DIM].astype(F32)
        qt_scr[:, g * tq:(g + 1) * tq] = qg.T.astype(BF16)
    m_scr[...] = jnp.full(m_scr.shape, -jnp.inf, F32)
    l_scr[...] = jnp.zeros(l_scr.shape, F32)
    acc_scr[...] = jnp.zeros(acc_scr.shape, F32)

    def scores(j, slot):
        kt = k_ref[pl.ds(pl.multiple_of(j * tk, tk), tk), :]
        s_scr[slot] = jnp.dot(kt, qt_scr[...], preferred_element_type=F32)

    def accumulate(j, slot):
        s = s_scr[slot]
        m_old = m_scr[...]
        m_new = jnp.maximum(m_old, jnp.max(s, axis=0, keepdims=True))
        alpha = jnp.exp2(m_old - m_new)
        p = jnp.exp2(s - m_new)
        l_scr[...] = alpha * l_scr[...] + jnp.sum(p, axis=0, keepdims=True)
        pv = jnp.dot(vt_ref[j], p.astype(BF16), preferred_element_type=F32)
        acc_scr[...] = alpha * acc_scr[...] + pv
        m_scr[...] = m_new

    scores(0, 0)

    def body(i, carry):
        j = 2 * i
        scores(j + 1, 1)
        accumulate(j, 0)
        scores(jnp.minimum(j + 2, n_kv - 1), 0)
        accumulate(j + 1, 1)
        return carry

    lax.fori_loop(0, n_kv // 2, body, 0)
    out = acc_scr[...] / l_scr[...]
    for g in range(GROUPS):
        o_ref[:, g * HEAD_DIM:(g + 1) * HEAD_DIM] = out[:, g * tq:(g + 1) * tq].T.astype(BF16)


def _attention(q, k, vt3, tok_off, batch, seq, tq, tk):
    n_kv = seq // tk
    assert n_kv % 2 == 0, "the kernel consumes kv tiles in pairs"
    qrow0 = tok_off // tq
    srow0 = tok_off // seq
    gw = GROUPS * HEAD_DIM
    n_q = seq // tq
    return pl.pallas_call(
        functools.partial(_attn_kernel, tq=tq, tk=tk, n_kv=n_kv),
        grid=(batch, N_KV_HEADS, n_q),
        in_specs=[pl.BlockSpec((tq, gw), lambda b, h, i: (qrow0 + b * n_q + i, h)),
                  pl.BlockSpec((seq, HEAD_DIM), lambda b, h, i: (srow0 + b, h)),
                  pl.BlockSpec((n_kv, HEAD_DIM, tk), lambda b, h, i: (srow0 + b, h, 0))],
        out_specs=pl.BlockSpec((tq, gw), lambda b, h, i: (b * n_q + i, h)),
        out_shape=jax.ShapeDtypeStruct((batch * seq, D_MODEL), BF16),
        scratch_shapes=[pltpu.VMEM((HEAD_DIM, GROUPS * tq), BF16),
                        pltpu.VMEM((2, tk, GROUPS * tq), F32),
                        pltpu.VMEM((1, GROUPS * tq), F32),
                        pltpu.VMEM((1, GROUPS * tq), F32),
                        pltpu.VMEM((HEAD_DIM, GROUPS * tq), F32)],
        compiler_params=_cparams(("arbitrary", "arbitrary", "arbitrary")),
        name="attention",
    )(q, k, vt3)


def _scan_kernel(cur_ref, prev_ref, next_ref, cw_ref, cb_ref, w_ref, b_ref, lam_ref, o_ref,
                 ext_scr, a_scr, u_scr, h_scr, carry_scr, *, tt, n_chunks):
    d = pl.program_id(1)
    c = pl.program_id(2)
    chunk = jnp.where(d == 0, c, n_chunks - 1 - c)

    @pl.when(c == 0)
    def _():
        carry_scr[...] = jnp.zeros(carry_scr.shape, F32)

    keep_prev = jnp.where(chunk == 0, 0.0, 1.0)
    keep_next = jnp.where(chunk == n_chunks - 1, 0.0, 1.0)
    ext_scr[0:HALO, :] = prev_ref[...].astype(F32) * keep_prev
    ext_scr[HALO:HALO + tt, :] = cur_ref[...].astype(F32)
    ext_scr[HALO + tt:HALO + tt + HALO, :] = next_ref[...].astype(F32) * keep_next
    xc = cb_ref[...] + sum(cw_ref[j:j + 1, :] * ext_scr[HALO - 2 + j:HALO - 2 + j + tt, :]
                           for j in range(CONV_W))
    xcb = xc.astype(BF16)

    lam = lam_ref[0]
    y = jnp.exp(-jnp.abs(lam))
    w1p = 1.0 + y
    log1p_y = jnp.where(w1p == 1.0, y, jnp.log(w1p) * y / jnp.where(w1p == 1.0, 1.0, w1p - 1.0))
    neg_c_sp = -LRU_C * (jnp.maximum(-lam, 0.0) + log1p_y)

    for n in range(RNN_BLOCKS):
        sl = slice(n * RNN_BLOCK_W, (n + 1) * RNN_BLOCK_W)
        pre = jnp.dot(xcb[:, sl], w_ref[0, n], preferred_element_type=F32)
        r = _sigmoid(pre[:, :RNN_BLOCK_W] + b_ref[0, 0:1, sl])
        i = _sigmoid(pre[:, RNN_BLOCK_W:] + b_ref[0, 1:2, sl])
        a = jnp.exp(r * neg_c_sp[:, sl])
        a_scr[:, sl] = a
        u_scr[:, sl] = jnp.sqrt(1.0 - a * a) * (i * xc[:, sl])

    def step(t, h):
        row = jnp.where(d == 0, t, tt - 1 - t)
        h = a_scr[pl.ds(row, 1), :] * h + u_scr[pl.ds(row, 1), :]
        h_scr[pl.ds(row, 1), :] = h
        return h

    carry_scr[...] = lax.fori_loop(0, tt, step, carry_scr[...], unroll=8)
    o_ref[0] = h_scr[...].astype(BF16)


def _scan(xr, conv_w, conv_b, w_lru, b_lru, lam, tok_off, batch, seq, tt):
    nt = xr.shape[0]
    n_chunks = seq // tt
    row0 = tok_off // tt
    hrow0 = tok_off // HALO
    hpc = tt // HALO
    n_halo = nt // HALO

    def chunk_of(d, c):
        return jnp.where(d == 0, c, n_chunks - 1 - c)

    def prev_map(b, d, c):
        return (jnp.maximum(hrow0 + (b * n_chunks + chunk_of(d, c)) * hpc - 1, 0), 0)

    def next_map(b, d, c):
        return (jnp.minimum(hrow0 + (b * n_chunks + chunk_of(d, c) + 1) * hpc, n_halo - 1), 0)

    return pl.pallas_call(
        functools.partial(_scan_kernel, tt=tt, n_chunks=n_chunks),
        grid=(batch, 2, n_chunks),
        in_specs=[pl.BlockSpec((tt, D_MODEL), lambda b, d, c: (row0 + b * n_chunks + chunk_of(d, c), 0)),
                  pl.BlockSpec((HALO, D_MODEL), prev_map),
                  pl.BlockSpec((HALO, D_MODEL), next_map),
                  pl.BlockSpec((CONV_W, D_MODEL), lambda b, d, c: (0, 0)),
                  pl.BlockSpec((1, D_MODEL), lambda b, d, c: (0, 0)),
                  pl.BlockSpec((1, RNN_BLOCKS, RNN_BLOCK_W, 2 * RNN_BLOCK_W), lambda b, d, c: (d, 0, 0, 0)),
                  pl.BlockSpec((1, 2, D_MODEL), lambda b, d, c: (d, 0, 0)),
                  pl.BlockSpec((1, 1, D_MODEL), lambda b, d, c: (d, 0, 0))],
        out_specs=pl.BlockSpec((1, tt, D_MODEL), lambda b, d, c: (d, b * n_chunks + chunk_of(d, c), 0)),
        out_shape=jax.ShapeDtypeStruct((2, batch * seq, D_MODEL), BF16),
        scratch_shapes=[pltpu.VMEM((tt + 2 * HALO, D_MODEL), F32),
                        pltpu.VMEM((tt, D_MODEL), F32),
                        pltpu.VMEM((tt, D_MODEL), F32),
                        pltpu.VMEM((tt, D_MODEL), F32),
                        pltpu.VMEM((1, D_MODEL), F32)],
        compiler_params=_cparams(("arbitrary", "arbitrary", "arbitrary")),
        name="lru_scan",
    )(xr, xr, xr, conv_w, conv_b, w_lru, b_lru, lam)


def _mid_kernel(*refs, tile_ends):
    n_tr = len(tile_ends)
    oa_refs = refs[:n_tr]
    h_refs = refs[n_tr:3 * n_tr]
    x_refs = refs[3 * n_tr:4 * n_tr]
    (gg_ref, gs_ref, mod_ref, wpa_ref, wpr_ref, wo_ref, g1_ref, b1_ref, wr_ref, br_ref,
     x1_ref, h2_ref, wf_ref, ms_ref) = refs[4 * n_tr:]
    i = pl.program_id(0)
    oa = _owner_value([lambda r=r: r[...] for r in oa_refs], i, tile_ends)
    hsum = _owner_value([lambda f=h_refs[2 * j], b=h_refs[2 * j + 1]: f[0].astype(F32) + b[0].astype(F32)
                         for j in range(n_tr)], i, tile_ends)
    x_in = _owner_value([lambda r=r: r[...] for r in x_refs], i, tile_ends)
    o_att = jnp.dot(oa, wpa_ref[...], preferred_element_type=F32)
    rec = hsum.astype(BF16) * gg_ref[...]
    o_rec = jnp.dot(rec, wpr_ref[...], preferred_element_type=F32)
    gs = gs_ref[...].astype(F32)
    merged = gs[:, :D_MODEL] * o_att + gs[:, D_MODEL:] * o_rec
    mix = jnp.dot(merged.astype(BF16), wo_ref[...], preferred_element_type=F32)
    g1 = mod_ref[0, 2:3, :]
    sh2 = mod_ref[0, 3:4, :]
    sc2 = mod_ref[0, 4:5, :]
    x1 = _layer_norm(DN_ALPHA * x_in + (1.0 + g1) * mix) * g1_ref[...] + b1_ref[...]
    x1_ref[...] = x1
    h2 = _layer_norm(x1) * (1.0 + sc2) + sh2
    h2_ref[...] = h2
    tm = h2.shape[0]
    h2_hi = h2.astype(BF16)
    h2_lo = (h2 - h2_hi.astype(F32)).astype(BF16)
    cross = jnp.dot(jnp.concatenate([h2_hi, h2_lo], axis=0), wr_ref[...], preferred_element_type=F32)
    logits = ((cross[:tm, :N_EXPERTS] + cross[:tm, N_EXPERTS:])
              + (cross[tm:, :N_EXPERTS] + cross[tm:, N_EXPERTS:])) + br_ref[...]
    lane = lax.broadcasted_iota(jnp.int32, logits.shape, 1).astype(F32)
    rem = logits
    sel = jnp.zeros(logits.shape, F32)
    top = None
    denom = None
    for kk in range(TOP_K):
        m = jnp.max(rem, axis=-1, keepdims=True)
        idx = jnp.min(jnp.where(rem == m, lane, float(N_EXPERTS)), axis=-1, keepdims=True)
        pick = lane == idx
        sel = jnp.where(pick, 1.0, sel)
        rem = jnp.where(pick, -jnp.inf, rem)
        if kk == 0:
            top = m
            denom = jnp.ones_like(m)
        else:
            denom = denom + jnp.exp(m - top)
    wf_ref[...] = jnp.where(sel > 0.0, jnp.exp(logits - top) / denom, 0.0)
    ms_ref[...] = sel


def _mid(o_atts, hfbs, gg, gs, xs, mod, w_pa, w_pr, w_out, ln1_g, ln1_b, w_router, b_router, lay, tm):
    nt = lay.n_tokens

    def seq_map(i):
        return (lay.seq_and_pos(i * tm)[0], 0, 0)

    tok = lambda w: pl.BlockSpec((tm, w), lambda i: (i, 0))
    vec = lambda w: _const_spec((1, w))
    row_specs, tile_ends = lay.trunk_specs(
        tm, lambda local: pl.BlockSpec((tm, D_MODEL), lambda i: (local(i), 0)))
    h_pairs, _ = lay.trunk_specs(
        tm, lambda local: [pl.BlockSpec((1, tm, D_MODEL), lambda i, d=d: (d, local(i), 0)) for d in range(2)])
    h_specs = [spec for pair in h_pairs for spec in pair]
    h_args = [hfb for hfb in hfbs for _ in range(2)]
    return pl.pallas_call(
        functools.partial(_mid_kernel, tile_ends=tile_ends),
        grid=(nt // tm,),
        in_specs=row_specs + h_specs + row_specs + [
                  tok(D_MODEL), tok(2 * D_MODEL),
                  pl.BlockSpec((1, N_MOD, D_MODEL), seq_map),
                  _const_spec((D_MODEL, D_MODEL)), _const_spec((D_MODEL, D_MODEL)),
                  _const_spec((D_MODEL, D_MODEL)),
                  vec(D_MODEL), vec(D_MODEL),
                  _const_spec((D_MODEL, 2 * N_EXPERTS)), vec(N_EXPERTS)],
        out_specs=[tok(D_MODEL), tok(D_MODEL), tok(N_EXPERTS), tok(N_EXPERTS)],
        out_shape=[jax.ShapeDtypeStruct((nt, D_MODEL), F32),
                   jax.ShapeDtypeStruct((nt, D_MODEL), F32),
                   jax.ShapeDtypeStruct((nt, N_EXPERTS), F32),
                   jax.ShapeDtypeStruct((nt, N_EXPERTS), F32)],
        compiler_params=_cparams(("arbitrary",)),
        name="merge_router",
    )(*o_atts, *h_args, *xs, gg, gs, mod, w_pa, w_pr, w_out, ln1_g.reshape(1, -1), ln1_b.reshape(1, -1),
      w_router, b_router.reshape(1, -1))


def _rank_kernel(ms_ref, rank_ref, cnt_ref, tri_scr, run_scr, *, tp):
    @pl.when(pl.program_id(0) == 0)
    def _():
        r = lax.broadcasted_iota(jnp.int32, (tp, tp), 0)
        c = lax.broadcasted_iota(jnp.int32, (tp, tp), 1)
        tri_scr[...] = jnp.where(c < r, 1.0, 0.0).astype(BF16)
        run_scr[...] = jnp.zeros(run_scr.shape, F32)

    ms = ms_ref[...]
    before = jnp.dot(tri_scr[...], ms.astype(BF16), preferred_element_type=F32)
    rank_ref[...] = before + run_scr[...]
    run_scr[...] = run_scr[...] + jnp.sum(ms, axis=0, keepdims=True)
    cnt_ref[...] = jnp.broadcast_to(run_scr[...], cnt_ref.shape)


def _rank(msel, tp):
    nt = msel.shape[0]
    return pl.pallas_call(
        functools.partial(_rank_kernel, tp=tp),
        grid=(nt // tp,),
        in_specs=[pl.BlockSpec((tp, N_EXPERTS), lambda i: (i, 0))],
        out_specs=[pl.BlockSpec((tp, N_EXPERTS), lambda i: (i, 0)),
                   pl.BlockSpec((8, N_EXPERTS), lambda i: (0, 0))],
        out_shape=[jax.ShapeDtypeStruct((nt, N_EXPERTS), F32),
                   jax.ShapeDtypeStruct((8, N_EXPERTS), F32)],
        scratch_shapes=[pltpu.VMEM((tp, tp), BF16), pltpu.VMEM((1, N_EXPERTS), F32)],
        compiler_params=_cparams(("arbitrary",)),
        name="route_rank",
    )(msel)


def _slots_kernel(rank_ref, ms_ref, wf_ref, start_ref, slot_ref, wt_ref):
    ms = ms_ref[...]
    wf = wf_ref[...]
    slot_full = rank_ref[...] + start_ref[...]
    lane = lax.broadcasted_iota(jnp.int32, ms.shape, 1).astype(F32)
    out_lane = lax.broadcasted_iota(jnp.int32, slot_ref.shape, 1)
    slots = jnp.zeros(slot_ref.shape, F32)
    wts = jnp.zeros(wt_ref.shape, F32)
    rem = ms
    for kk in range(TOP_K):
        idx = jnp.min(jnp.where(rem > 0.0, lane, float(2 * N_EXPERTS)), axis=-1, keepdims=True)
        pick = lane == idx
        s_k = jnp.sum(jnp.where(pick, slot_full, 0.0), axis=-1, keepdims=True)
        w_k = jnp.sum(jnp.where(pick, wf, 0.0), axis=-1, keepdims=True)
        rem = jnp.where(pick, 0.0, rem)
        slots = jnp.where(out_lane == kk, s_k, slots)
        wts = jnp.where(out_lane == kk, w_k, wts)
    slot_ref[...] = slots.astype(jnp.int32)
    wt_ref[...] = wts


def _slots(rank, msel, wfull, pad_start, tp):
    nt = msel.shape[0]
    tok = pl.BlockSpec((tp, N_EXPERTS), lambda i: (i, 0))
    out = pl.BlockSpec((tp, 128), lambda i: (i, 0))
    return pl.pallas_call(
        _slots_kernel,
        grid=(nt // tp,),
        in_specs=[tok, tok, tok, pl.BlockSpec((1, N_EXPERTS), lambda i: (0, 0))],
        out_specs=[out, out],
        out_shape=[jax.ShapeDtypeStruct((nt, 128), jnp.int32),
                   jax.ShapeDtypeStruct((nt, 128), F32)],
        compiler_params=_cparams(("arbitrary",)),
        name="route_slots",
    )(rank, msel, wfull, pad_start)


def _row_copy(src_ref, src_row, dst_ref, dst_row, sem):
    return pltpu.make_async_copy(src_ref.at[pl.ds(src_row, 1), :], dst_ref.at[pl.ds(dst_row, 1), :], sem)


def _dispatch_kernel(slot_ref, tail_ref, h_ref, xs_ref, zero_scr, sem, zsem, *, tg, tb):
    @pl.when(pl.program_id(0) == 0)
    def _():
        zero_scr[...] = jnp.zeros(zero_scr.shape, F32)
        for e in range(N_EXPERTS):
            tail = pl.multiple_of(tail_ref[e], tb)
            pltpu.make_async_copy(zero_scr, xs_ref.at[pl.ds(tail, tb), :], zsem).start()
        for e in range(N_EXPERTS):
            pltpu.make_async_copy(zero_scr, xs_ref.at[pl.ds(0, tb), :], zsem).wait()

    def issue(t, carry):
        for kk in range(TOP_K):
            _row_copy(h_ref, t, xs_ref, slot_ref[t * TOP_K + kk], sem).start()
        return carry

    lax.fori_loop(0, tg, issue, 0)

    def drain(t, carry):
        for kk in range(TOP_K):
            _row_copy(h_ref, 0, xs_ref, 0, sem).wait()
        return carry

    lax.fori_loop(0, tg, drain, 0)


def _dispatch(slots_flat, tail_start, h2, cap, tg, tb):
    nt = h2.shape[0]
    return pl.pallas_call(
        functools.partial(_dispatch_kernel, tg=tg, tb=tb),
        grid=(nt // tg,),
        in_specs=[pl.BlockSpec((tg * TOP_K,), lambda i: (i,), memory_space=pltpu.SMEM),
                  pl.BlockSpec(memory_space=pltpu.SMEM),
                  pl.BlockSpec((tg, D_MODEL), lambda i: (i, 0))],
        out_specs=pl.BlockSpec(memory_space=pl.ANY),
        out_shape=jax.ShapeDtypeStruct((cap, D_MODEL), F32),
        scratch_shapes=[pltpu.VMEM((tb, D_MODEL), F32), pltpu.SemaphoreType.DMA(()),
                        pltpu.SemaphoreType.DMA(())],
        compiler_params=_cparams(("arbitrary",)),
        name="moe_dispatch",
    )(slots_flat, tail_start, h2)


def _expert_kernel(be_ref, bv_ref, xs_ref, w1_ref, b1_ref, w2_ref, b2_ref, ys_ref, *, tb):
    valid = bv_ref[pl.program_id(0)]

    @pl.when(valid > 0)
    def _():
        gu = jnp.dot(xs_ref[...].astype(BF16), w1_ref[0], preferred_element_type=F32) + b1_ref[0]
        glu = jnp.minimum(gu[:, :D_FF], SWIGLU_LIMIT)
        lin = jnp.clip(gu[:, D_FF:], -SWIGLU_LIMIT, SWIGLU_LIMIT)
        act = (lin + 1.0) * glu * _sigmoid(SWIGLU_ALPHA * glu)
        ys_ref[...] = jnp.dot(act.astype(BF16), w2_ref[0], preferred_element_type=F32) + b2_ref[0]

    @pl.when(valid <= 0)
    def _():
        ys_ref[...] = jnp.zeros(ys_ref.shape, F32)


def _experts(block_expert, block_valid, xs, w1, b1, w2, b2, tb):
    cap = xs.shape[0]
    grid_spec = pltpu.PrefetchScalarGridSpec(
        num_scalar_prefetch=2,
        grid=(cap // tb,),
        in_specs=[pl.BlockSpec((tb, D_MODEL), lambda i, be, bv: (i, 0)),
                  pl.BlockSpec((1, D_MODEL, 2 * D_FF), lambda i, be, bv: (be[i], 0, 0)),
                  pl.BlockSpec((1, 1, 2 * D_FF), lambda i, be, bv: (be[i], 0, 0)),
                  pl.BlockSpec((1, D_FF, D_MODEL), lambda i, be, bv: (be[i], 0, 0)),
                  pl.BlockSpec((1, 1, D_MODEL), lambda i, be, bv: (be[i], 0, 0))],
        out_specs=pl.BlockSpec((tb, D_MODEL), lambda i, be, bv: (i, 0)),
    )
    return pl.pallas_call(
        functools.partial(_expert_kernel, tb=tb),
        grid_spec=grid_spec,
        out_shape=jax.ShapeDtypeStruct((cap, D_MODEL), F32),
        compiler_params=_cparams(("arbitrary",)),
        name="moe_experts",
    )(block_expert, block_valid, xs, w1, b1.reshape(N_EXPERTS, 1, -1), w2, b2.reshape(N_EXPERTS, 1, -1))


def _combine_kernel(slot_ref, ys_ref, wt_ref, x1_ref, mod_ref, g_ref, b_ref, *rest, tc, tile_ends):
    y_refs = rest[:len(tile_ends)]
    buf, sem = rest[len(tile_ends):]

    def issue(t, carry):
        for kk in range(TOP_K):
            pltpu.make_async_copy(ys_ref.at[pl.ds(slot_ref[t * TOP_K + kk], 1), :],
                                  buf.at[kk, pl.ds(t, 1), :], sem).start()
        return carry

    lax.fori_loop(0, tc, issue, 0)

    def drain(t, carry):
        for kk in range(TOP_K):
            pltpu.make_async_copy(ys_ref.at[pl.ds(0, 1), :], buf.at[kk, pl.ds(0, 1), :], sem).wait()
        return carry

    lax.fori_loop(0, tc, drain, 0)
    wt = wt_ref[...]
    ff = sum(wt[:, kk:kk + 1] * buf[kk] for kk in range(TOP_K))
    g2 = mod_ref[0, 5:6, :]
    y = _layer_norm(DN_ALPHA * x1_ref[...] + (1.0 + g2) * ff) * g_ref[...] + b_ref[...]
    i = pl.program_id(0)
    start = 0
    for y_ref, end in zip(y_refs, tile_ends):
        @pl.when((i >= start) & (i < end))
        def _(y_ref=y_ref):
            y_ref[...] = y
        start = end


def _combine(slots_flat, ys, wts, x1, mod, ln2_g, ln2_b, lay, tc):
    nt = lay.n_tokens

    def seq_map(i):
        return (lay.seq_and_pos(i * tc)[0], 0, 0)

    out_specs, tile_ends = lay.trunk_specs(
        tc, lambda local: pl.BlockSpec((tc, D_MODEL), lambda i: (local(i), 0)))
    return pl.pallas_call(
        functools.partial(_combine_kernel, tc=tc, tile_ends=tile_ends),
        grid=(nt // tc,),
        in_specs=[pl.BlockSpec((tc * TOP_K,), lambda i: (i,), memory_space=pltpu.SMEM),
                  pl.BlockSpec(memory_space=pl.ANY),
                  pl.BlockSpec((tc, 128), lambda i: (i, 0)),
                  pl.BlockSpec((tc, D_MODEL), lambda i: (i, 0)),
                  pl.BlockSpec((1, N_MOD, D_MODEL), seq_map),
                  _const_spec((1, D_MODEL)), _const_spec((1, D_MODEL))],
        out_specs=out_specs,
        out_shape=[jax.ShapeDtypeStruct((b * s, D_MODEL), F32) for b, s in lay.trunks],
        scratch_shapes=[pltpu.VMEM((TOP_K, tc, D_MODEL), F32), pltpu.SemaphoreType.DMA(())],
        compiler_params=_cparams(("arbitrary",)),
        name="moe_combine",
    )(slots_flat, ys, wts, x1, mod, ln2_g.reshape(1, -1), ln2_b.reshape(1, -1))


def _encoder_layer(xs, cs, p):
    trunks = [(x.shape[0], x.shape[1]) for x in xs]
    lay = _Layout(trunks)
    nt = lay.n_tokens
    t = _tiles(nt)

    x2d = [xi.reshape(-1, D_MODEL) for xi in xs]
    c = jnp.concatenate(cs, axis=0)
    bp = -(-lay.n_seqs // 8) * 8
    c = jnp.pad(c, ((0, bp - lay.n_seqs), (0, 0)))
    mod = _ada_mod(c, p["w_ada"], p["b_ada"]).reshape(bp, N_MOD, D_MODEL)

    q, k, vt3, xr, gg, gs = _inproj(x2d, mod, p["w_in"].astype(BF16), p["q_gain"], p["k_gain"],
                                    _rope_tables(lay.max_seq), lay, t["tm_in"])

    w_lru = jnp.concatenate([p["lru_wa"], p["lru_wx"]], axis=-1).astype(BF16)
    b_lru = jnp.stack([p["lru_ba"], p["lru_bx"]], axis=1)
    lam = p["lru_lam"].reshape(2, 1, D_MODEL)
    o_atts, hfbs = [], []
    tok_off = 0
    for b, s in trunks:
        o_atts.append(_attention(q, k, vt3, tok_off, b, s, t["tq"], t["tm_in"]))
        hfbs.append(_scan(xr, p["conv_w"], p["conv_b"].reshape(1, -1), w_lru, b_lru, lam,
                          tok_off, b, s, t["tt"]))
        tok_off += b * s

    wr_hi = p["w_router"].astype(BF16)
    wr_lo = (p["w_router"] - wr_hi.astype(F32)).astype(BF16)
    x1, h2, wfull, msel = _mid(o_atts, hfbs, gg, gs, x2d, mod, p["w_pa"].astype(BF16),
                               p["w_pr"].astype(BF16), p["w_out"].astype(BF16), p["ln1_g"], p["ln1_b"],
                               jnp.concatenate([wr_hi, wr_lo], axis=1), p["b_router"], lay, t["tm_mid"])

    tb = t["tb"]
    rank, cnt = _rank(msel, t["tp"])
    counts = cnt[0].astype(jnp.int32)
    padded = (counts + tb - 1) // tb * tb
    pad_end = jnp.cumsum(padded)
    pad_start = pad_end - padded
    n_blocks = nt * TOP_K // tb + N_EXPERTS
    blk0 = jnp.arange(n_blocks, dtype=jnp.int32) * tb
    block_expert = jnp.minimum(jnp.sum(pad_end[None, :] <= blk0[:, None], axis=1), N_EXPERTS - 1).astype(jnp.int32)
    block_valid = jnp.clip(pad_start[block_expert] + counts[block_expert] - blk0, 0, tb).astype(jnp.int32)
    slots, wts = _slots(rank, msel, wfull, pad_start.astype(F32).reshape(1, N_EXPERTS), t["tp"])
    slots_flat = slots[:, :TOP_K].reshape(-1)

    tail_start = jnp.where(padded > 0, pad_end - tb, pad_end[-1] - tb).astype(jnp.int32)
    xs_rows = _dispatch(slots_flat, tail_start, h2, n_blocks * tb, t["tg"], tb)
    ys = _experts(block_expert, block_valid, xs_rows, p["w1"].astype(BF16), p["b1"],
                  p["w2"].astype(BF16), p["b2"], tb)
    ys_out = _combine(slots_flat, ys, wts, x1, mod, p["ln2_g"], p["ln2_b"], lay, t["tc"])
    return [y.reshape(b, s, D_MODEL) for y, (b, s) in zip(ys_out, trunks)]


_PARAM_NAMES = ("w_ada", "b_ada", "w_in", "q_gain", "k_gain", "conv_w", "conv_b", "lru_wa", "lru_ba",
                "lru_wx", "lru_bx", "lru_lam", "w_pa", "w_pr", "w_out", "ln1_g", "ln1_b", "w_router",
                "b_router", "w1", "b1", "w2", "b2", "ln2_g", "ln2_b")


def kernel(x_prompt, x_sample, c_prompt, c_sample, w_ada, b_ada, w_in, q_gain, k_gain, conv_w, conv_b, lru_wa, lru_ba, lru_wx, lru_bx, lru_lam, w_pa, w_pr, w_out, ln1_g, ln1_b, w_router, b_router, w1, b1, w2, b2, ln2_g, ln2_b):
    stacked = (w_ada, b_ada, w_in, q_gain, k_gain, conv_w, conv_b, lru_wa, lru_ba, lru_wx, lru_bx,
               lru_lam, w_pa, w_pr, w_out, ln1_g, ln1_b, w_router, b_router, w1, b1, w2, b2, ln2_g, ln2_b)
    xs, cs = [x_prompt, x_sample], [c_prompt, c_sample]
    for layer in range(DEPTH):
        p = {name: arr[layer] for name, arr in zip(_PARAM_NAMES, stacked)}
        xs = _encoder_layer(xs, cs, p)
    return (xs[0], xs[1])
```

```python
import functools
import math

import jax
import jax.numpy as jnp
from jax import lax
from jax.experimental import pallas as pl
from jax.experimental.pallas import tpu as pltpu

F32 = jnp.float32
BF16 = jnp.bfloat16

D_MODEL = 1024
GRID_W = 64
N_HEADS = 8
N_KV_HEADS = 2
HEAD_DIM = 128
GROUPS = N_HEADS // N_KV_HEADS
KV_WIDTH = N_KV_HEADS * HEAD_DIM
ROPE_THETA = 10000.0
RNN_BLOCKS = 8
RNN_BLOCK_W = D_MODEL // RNN_BLOCKS
CONV_W = 4
LRU_C = 8.0
N_EXPERTS = 32
TOP_K = 4
D_FF = D_MODEL
SWIGLU_LIMIT = 7.0
SWIGLU_ALPHA = 1.702
DEPTH = 1
DN_ALPHA = (2 * DEPTH) ** 0.25
LN_EPS = 1e-5
RMS_EPS = 1e-6
N_MOD = 6
IN_WIDTH = D_MODEL + 2 * KV_WIDTH + 2 * D_MODEL + 2 * D_MODEL
_Q0, _K0, _V0, _XR0, _GR0, _GL0 = 0, 1024, 1280, 1536, 2560, 3584

V7X_VMEM_LIMIT_BYTES = 56 * 1024 * 1024
HALO = 16


def _tiles(n_tokens):
    big = n_tokens >= 4096
    return dict(
        tm_in=512 if big else 128,
        tq=256 if big else 128,
        tt=512 if big else 128,
        tm_mid=256 if big else 128,
        tp=512 if big else 128,
        tg=512 if big else 256,
        tb=512 if big else 128,
        tc=256,
    )


def _cparams(sem):
    return pltpu.CompilerParams(dimension_semantics=sem, vmem_limit_bytes=V7X_VMEM_LIMIT_BYTES)


def _const_spec(shape):
    nd = len(shape)
    return pl.BlockSpec(shape, lambda *_: (0,) * nd, pipeline_mode=pl.Buffered(1))


def _layer_norm(x):
    mu = jnp.mean(x, axis=-1, keepdims=True)
    xc = x - mu
    var = jnp.mean(xc * xc, axis=-1, keepdims=True)
    return xc * lax.rsqrt(var + LN_EPS)


def _sigmoid(x):
    return 1.0 / (1.0 + jnp.exp(-x))


ROW_SUB = D_MODEL // 128


def _store_row_tiles(ref, val):
    n = val.shape[0]
    for j in range(ROW_SUB):
        ref[pl.ds(j, n, stride=ROW_SUB), :] = val[:, j * 128:(j + 1) * 128]


def _load_row_tiles(ref, n):
    return jnp.concatenate([ref[pl.ds(j, n, stride=ROW_SUB), :] for j in range(ROW_SUB)], axis=1)


def _ada_kernel(c_ref, w_ref, b_ref, o_ref):
    c = c_ref[...]
    s = c * _sigmoid(c)
    o_ref[...] = jnp.dot(s, w_ref[...], preferred_element_type=F32,
                         precision=lax.Precision.HIGHEST) + b_ref[...]


def _ada_mod(c_all, w_ada, b_ada):
    bp = c_all.shape[0]
    ncol = w_ada.shape[1]
    return pl.pallas_call(
        _ada_kernel,
        grid=(ncol // D_MODEL,),
        in_specs=[pl.BlockSpec((bp, D_MODEL), lambda j: (0, 0)),
                  pl.BlockSpec((D_MODEL, D_MODEL), lambda j: (0, j)),
                  pl.BlockSpec((1, D_MODEL), lambda j: (0, j))],
        out_specs=pl.BlockSpec((bp, D_MODEL), lambda j: (0, j)),
        out_shape=jax.ShapeDtypeStruct((bp, ncol), F32),
        compiler_params=_cparams(("arbitrary",)),
        name="ada_mod",
    )(c_all, w_ada, b_ada.reshape(1, ncol))


class _Layout:
    def __init__(self, trunks):
        self.trunks = tuple(trunks)
        self.n_tokens = sum(b * s for b, s in trunks)
        self.n_seqs = sum(b for b, _ in trunks)
        self.max_seq = max(s for _, s in trunks)

    def seq_and_pos(self, t0):
        seq = jnp.int32(0)
        pos = jnp.int32(0)
        tok_off, seq_off = 0, 0
        for b, s in self.trunks:
            inside = (t0 >= tok_off) & (t0 < tok_off + b * s)
            rel = jnp.maximum(t0 - tok_off, 0)
            seq = jnp.where(inside, seq_off + rel // s, seq)
            pos = jnp.where(inside, rel % s, pos)
            tok_off += b * s
            seq_off += b
        return seq, pos

    def trunk_specs(self, tm, make_spec):
        specs, ends, t0 = [], [], 0
        for b, s in self.trunks:
            n_t = b * s // tm
            specs.append(make_spec(functools.partial(_clamped_local, t0=t0, n_t=n_t)))
            t0 += n_t
            ends.append(t0)
        return specs, tuple(ends)


def _clamped_local(i, *, t0, n_t):
    return jnp.clip(i - t0, 0, n_t - 1)


def _owner_value(loads, i, tile_ends):
    val = loads[-1]()
    for j in range(len(loads) - 2, -1, -1):
        val = jnp.where(i < tile_ends[j], loads[j](), val)
    return val


def _rope_tables(max_seq):
    t = jnp.arange(max_seq)
    rows = (t // GRID_W).astype(F32)
    cols = (t % GRID_W).astype(F32)
    axis_dim = HEAD_DIM // 2
    inv = ROPE_THETA ** (-jnp.arange(0, axis_dim, 2, dtype=F32) / axis_dim)
    ar, ac = rows[:, None] * inv, cols[:, None] * inv
    cr, sr, cc, sc = jnp.cos(ar), jnp.sin(ar), jnp.cos(ac), jnp.sin(ac)
    z = jnp.zeros_like(sr)
    cos_t = jnp.concatenate([cr, cr, cc, cc], axis=-1)
    up_t = jnp.concatenate([-sr, z, -sc, z], axis=-1)
    dn_t = jnp.concatenate([z, sr, z, sc], axis=-1)
    return cos_t, up_t, dn_t


def _inproj_kernel(*refs, tile_ends):
    n_tr = len(tile_ends)
    x_refs = refs[:n_tr]
    (mod_ref, w_ref, qg_ref, kg_ref, cos_ref, up_ref, dn_ref,
     q_ref, k_ref, vt_ref, xr_ref, gg_ref, gs_ref) = refs[n_tr:]
    x = _owner_value([lambda r=r: r[...] for r in x_refs], pl.program_id(0), tile_ends)
    sh1 = mod_ref[0, 0:1, :]
    sc1 = mod_ref[0, 1:2, :]
    h = (_layer_norm(x) * (1.0 + sc1) + sh1).astype(BF16)
    cos_t, up_t, dn_t = cos_ref[...], up_ref[...], dn_ref[...]

    def proj(c0, width):
        return jnp.dot(h, w_ref[:, c0:c0 + width], preferred_element_type=F32)

    def norm_rope(z, gain):
        ms = jnp.mean(z * z, axis=-1, keepdims=True)
        y = z * lax.rsqrt(ms + RMS_EPS) * gain
        return (y * cos_t + pltpu.roll(y, HEAD_DIM - 32, 1) * up_t
                + pltpu.roll(y, 32, 1) * dn_t)

    zq = proj(_Q0, D_MODEL)
    qg = qg_ref[...] * (HEAD_DIM ** -0.5 * math.log2(math.e))
    for hd in range(N_HEADS):
        sl = slice(hd * HEAD_DIM, (hd + 1) * HEAD_DIM)
        q_ref[:, sl] = norm_rope(zq[:, sl], qg).astype(BF16)
    zk = proj(_K0, KV_WIDTH)
    kg = kg_ref[...]
    for hd in range(N_KV_HEADS):
        sl = slice(hd * HEAD_DIM, (hd + 1) * HEAD_DIM)
        k_ref[:, sl] = norm_rope(zk[:, sl], kg).astype(BF16)
    zv = proj(_V0, KV_WIDTH)
    vt_ref[0] = zv.T.astype(BF16)
    xr_ref[...] = proj(_XR0, D_MODEL).astype(BF16)
    gg_ref[...] = jax.nn.gelu(proj(_GR0, D_MODEL), approximate=True).astype(BF16)
    gs_ref[...] = _sigmoid(proj(_GL0, 2 * D_MODEL)).astype(BF16)


def _inproj(xs, mod, w_in, q_gain, k_gain, tables, lay, tm):
    nt = lay.n_tokens
    cos_t, up_t, dn_t = tables

    def seq_map(i):
        return (lay.seq_and_pos(i * tm)[0], 0, 0)

    def pos_map(i):
        return (lay.seq_and_pos(i * tm)[1] // tm, 0)

    tok = lambda w: pl.BlockSpec((tm, w), lambda i: (i, 0))
    rope = pl.BlockSpec((tm, HEAD_DIM), pos_map)
    x_specs, tile_ends = lay.trunk_specs(
        tm, lambda local: pl.BlockSpec((tm, D_MODEL), lambda i: (local(i), 0)))
    return pl.pallas_call(
        functools.partial(_inproj_kernel, tile_ends=tile_ends),
        grid=(nt // tm,),
        in_specs=x_specs + [
                  pl.BlockSpec((1, N_MOD, D_MODEL), seq_map),
                  _const_spec((D_MODEL, IN_WIDTH)),
                  _const_spec((1, HEAD_DIM)), _const_spec((1, HEAD_DIM)),
                  rope, rope, rope],
        out_specs=[tok(D_MODEL), tok(KV_WIDTH),
                   pl.BlockSpec((1, KV_WIDTH, tm), lambda i: (i, 0, 0)),
                   tok(D_MODEL), tok(D_MODEL), tok(2 * D_MODEL)],
        out_shape=[jax.ShapeDtypeStruct((nt, D_MODEL), BF16),
                   jax.ShapeDtypeStruct((nt, KV_WIDTH), BF16),
                   jax.ShapeDtypeStruct((nt // tm, KV_WIDTH, tm), BF16),
                   jax.ShapeDtypeStruct((nt, D_MODEL), BF16),
                   jax.ShapeDtypeStruct((nt, D_MODEL), BF16),
                   jax.ShapeDtypeStruct((nt, 2 * D_MODEL), BF16)],
        compiler_params=_cparams(("arbitrary",)),
        name="in_proj",
    )(*xs, mod, w_in, q_gain.reshape(1, HEAD_DIM), k_gain.reshape(1, HEAD_DIM), cos_t, up_t, dn_t)


def _attn_kernel(q_ref, k_ref, vt_ref, o_ref, qt_scr, s_scr, m_scr, l_scr, acc_scr, *, tq, tk, n_kv):
    for g in range(GROUPS):
        qg = q_ref[:, g * HEAD_DIM:(g + 1) * HEAD_DIM].astype(F32)
        qt_scr[:, g * tq:(g + 1) * tq] = qg.T.astype(BF16)
    m_scr[...] = jnp.full(m_scr.shape, -jnp.inf, F32)
    l_scr[...] = jnp.zeros(l_scr.shape, F32)
    acc_scr[...] = jnp.zeros(acc_scr.shape, F32)

    def scores(j, slot):
        kt = k_ref[pl.ds(pl.multiple_of(j * tk, tk), tk), :]
        s_scr[slot] = jnp.dot(kt, qt_scr[...], preferred_element_type=F32)

    def accumulate(j, slot):
        s = s_scr[slot]
        m_old = m_scr[...]
        m_new = jnp.maximum(m_old, jnp.max(s, axis=0, keepdims=True))
        alpha = jnp.exp2(m_old - m_new)
        p = jnp.exp2(s - m_new)
        l_scr[...] = alpha * l_scr[...] + jnp.sum(p, axis=0, keepdims=True)
        pv = jnp.dot(vt_ref[j], p.astype(BF16), preferred_element_type=F32)
        acc_scr[...] = alpha * acc_scr[...] + pv
        m_scr[...] = m_new

    scores(0, 0)

    def body(i, carry):
        j = 2 * i
        scores(j + 1, 1)
        accumulate(j, 0)
        scores(jnp.minimum(j + 2, n_kv - 1), 0)
        accumulate(j + 1, 1)
        return carry

    lax.fori_loop(0, n_kv // 2, body, 0)
    out = acc_scr[...] / l_scr[...]
    for g in range(GROUPS):
        o_ref[:, g * HEAD_DIM:(g + 1) * HEAD_DIM] = out[:, g * tq:(g + 1) * tq].T.astype(BF16)


def _attention(q, k, vt3, tok_off, batch, seq, tq, tk):
    n_kv = seq // tk
    assert n_kv % 2 == 0, "the kernel consumes kv tiles in pairs"
    qrow0 = tok_off // tq
    srow0 = tok_off // seq
    gw = GROUPS * HEAD_DIM
    n_q = seq // tq
    return pl.pallas_call(
        functools.partial(_attn_kernel, tq=tq, tk=tk, n_kv=n_kv),
        grid=(batch, N_KV_HEADS, n_q),
        in_specs=[pl.BlockSpec((tq, gw), lambda b, h, i: (qrow0 + b * n_q + i, h)),
                  pl.BlockSpec((seq, HEAD_DIM), lambda b, h, i: (srow0 + b, h)),
                  pl.BlockSpec((n_kv, HEAD_DIM, tk), lambda b, h, i: (srow0 + b, h, 0))],
        out_specs=pl.BlockSpec((tq, gw), lambda b, h, i: (b * n_q + i, h)),
        out_shape=jax.ShapeDtypeStruct((batch * seq, D_MODEL), BF16),
        scratch_shapes=[pltpu.VMEM((HEAD_DIM, GROUPS * tq), BF16),
                        pltpu.VMEM((2, tk, GROUPS * tq), F32),
                        pltpu.VMEM((1, GROUPS * tq), F32),
                        pltpu.VMEM((1, GROUPS * tq), F32),
                        pltpu.VMEM((HEAD_DIM, GROUPS * tq), F32)],
        compiler_params=_cparams(("arbitrary", "arbitrary", "arbitrary")),
        name="attention",
    )(q, k, vt3)


def _scan_kernel(cur_ref, prev_ref, next_ref, cw_ref, cb_ref, w_ref, b_ref, lam_ref, o_ref,
                 ext_scr, a_scr, u_scr, h_scr, carry_scr, *, tt, n_chunks):
    d = pl.program_id(1)
    c = pl.program_id(2)
    chunk = jnp.where(d == 0, c, n_chunks - 1 - c)

    @pl.when(c == 0)
    def _():
        carry_scr[...] = jnp.zeros(carry_scr.shape, F32)

    keep_prev = jnp.where(chunk == 0, 0.0, 1.0)
    keep_next = jnp.where(chunk == n_chunks - 1, 0.0, 1.0)
    ext_scr[0:HALO, :] = prev_ref[...].astype(F32) * keep_prev
    ext_scr[HALO:HALO + tt, :] = cur_ref[...].astype(F32)
    ext_scr[HALO + tt:HALO + tt + HALO, :] = next_ref[...].astype(F32) * keep_next
    xc = cb_ref[...] + sum(cw_ref[j:j + 1, :] * ext_scr[HALO - 2 + j:HALO - 2 + j + tt, :]
                           for j in range(CONV_W))
    xcb = xc.astype(BF16)

    lam = lam_ref[0]
    y = jnp.exp(-jnp.abs(lam))
    w1p = 1.0 + y
    log1p_y = jnp.where(w1p == 1.0, y, jnp.log(w1p) * y / jnp.where(w1p == 1.0, 1.0, w1p - 1.0))
    neg_c_sp = -LRU_C * (jnp.maximum(-lam, 0.0) + log1p_y)

    for n in range(RNN_BLOCKS):
        sl = slice(n * RNN_BLOCK_W, (n + 1) * RNN_BLOCK_W)
        pre = jnp.dot(xcb[:, sl], w_ref[0, n], preferred_element_type=F32)
        r = _sigmoid(pre[:, :RNN_BLOCK_W] + b_ref[0, 0:1, sl])
        i = _sigmoid(pre[:, RNN_BLOCK_W:] + b_ref[0, 1:2, sl])
        a = jnp.exp(r * neg_c_sp[:, sl])
        a_scr[:, sl] = a
        u_scr[:, sl] = jnp.sqrt(1.0 - a * a) * (i * xc[:, sl])

    def step(t, h):
        row = jnp.where(d == 0, t, tt - 1 - t)
        h = a_scr[pl.ds(row, 1), :] * h + u_scr[pl.ds(row, 1), :]
        h_scr[pl.ds(row, 1), :] = h
        return h

    carry_scr[...] = lax.fori_loop(0, tt, step, carry_scr[...], unroll=8)
    o_ref[0] = h_scr[...].astype(BF16)


def _scan(xr, conv_w, conv_b, w_lru, b_lru, lam, tok_off, batch, seq, tt):
    nt = xr.shape[0]
    n_chunks = seq // tt
    row0 = tok_off // tt
    hrow0 = tok_off // HALO
    hpc = tt // HALO
    n_halo = nt // HALO

    def chunk_of(d, c):
        return jnp.where(d == 0, c, n_chunks - 1 - c)

    def prev_map(b, d, c):
        return (jnp.maximum(hrow0 + (b * n_chunks + chunk_of(d, c)) * hpc - 1, 0), 0)

    def next_map(b, d, c):
        return (jnp.minimum(hrow0 + (b * n_chunks + chunk_of(d, c) + 1) * hpc, n_halo - 1), 0)

    return pl.pallas_call(
        functools.partial(_scan_kernel, tt=tt, n_chunks=n_chunks),
        grid=(batch, 2, n_chunks),
        in_specs=[pl.BlockSpec((tt, D_MODEL), lambda b, d, c: (row0 + b * n_chunks + chunk_of(d, c), 0)),
                  pl.BlockSpec((HALO, D_MODEL), prev_map),
                  pl.BlockSpec((HALO, D_MODEL), next_map),
                  pl.BlockSpec((CONV_W, D_MODEL), lambda b, d, c: (0, 0)),
                  pl.BlockSpec((1, D_MODEL), lambda b, d, c: (0, 0)),
                  pl.BlockSpec((1, RNN_BLOCKS, RNN_BLOCK_W, 2 * RNN_BLOCK_W), lambda b, d, c: (d, 0, 0, 0)),
                  pl.BlockSpec((1, 2, D_MODEL), lambda b, d, c: (d, 0, 0)),
                  pl.BlockSpec((1, 1, D_MODEL), lambda b, d, c: (d, 0, 0))],
        out_specs=pl.BlockSpec((1, tt, D_MODEL), lambda b, d, c: (d, b * n_chunks + chunk_of(d, c), 0)),
        out_shape=jax.ShapeDtypeStruct((2, batch * seq, D_MODEL), BF16),
        scratch_shapes=[pltpu.VMEM((tt + 2 * HALO, D_MODEL), F32),
                        pltpu.VMEM((tt, D_MODEL), F32),
                        pltpu.VMEM((tt, D_MODEL), F32),
                        pltpu.VMEM((tt, D_MODEL), F32),
                        pltpu.VMEM((1, D_MODEL), F32)],
        compiler_params=_cparams(("arbitrary", "arbitrary", "arbitrary")),
        name="lru_scan",
    )(xr, xr, xr, conv_w, conv_b, w_lru, b_lru, lam)


def _mid_kernel(*refs, tile_ends):
    n_tr = len(tile_ends)
    oa_refs = refs[:n_tr]
    h_refs = refs[n_tr:3 * n_tr]
    x_refs = refs[3 * n_tr:4 * n_tr]
    (gg_ref, gs_ref, mod_ref, wpa_ref, wpr_ref, wo_ref, g1_ref, b1_ref, wr_ref, br_ref,
     x1_ref, h2_ref, wf_ref, ms_ref) = refs[4 * n_tr:]
    i = pl.program_id(0)
    oa = _owner_value([lambda r=r: r[...] for r in oa_refs], i, tile_ends)
    hsum = _owner_value([lambda f=h_refs[2 * j], b=h_refs[2 * j + 1]: f[0].astype(F32) + b[0].astype(F32)
                         for j in range(n_tr)], i, tile_ends)
    x_in = _owner_value([lambda r=r: r[...] for r in x_refs], i, tile_ends)
    o_att = jnp.dot(oa, wpa_ref[...], preferred_element_type=F32)
    rec = hsum.astype(BF16) * gg_ref[...]
    o_rec = jnp.dot(rec, wpr_ref[...], preferred_element_type=F32)
    gs = gs_ref[...].astype(F32)
    merged = gs[:, :D_MODEL] * o_att + gs[:, D_MODEL:] * o_rec
    mix = jnp.dot(merged.astype(BF16), wo_ref[...], preferred_element_type=F32)
    g1 = mod_ref[0, 2:3, :]
    sh2 = mod_ref[0, 3:4, :]
    sc2 = mod_ref[0, 4:5, :]
    x1 = _layer_norm(DN_ALPHA * x_in + (1.0 + g1) * mix) * g1_ref[...] + b1_ref[...]
    x1_ref[...] = x1
    h2 = _layer_norm(x1) * (1.0 + sc2) + sh2
    _store_row_tiles(h2_ref, h2)
    tm = h2.shape[0]
    h2_hi = h2.astype(BF16)
    h2_lo = (h2 - h2_hi.astype(F32)).astype(BF16)
    cross = jnp.dot(jnp.concatenate([h2_hi, h2_lo], axis=0), wr_ref[...], preferred_element_type=F32)
    logits = ((cross[:tm, :N_EXPERTS] + cross[:tm, N_EXPERTS:])
              + (cross[tm:, :N_EXPERTS] + cross[tm:, N_EXPERTS:])) + br_ref[...]
    lane = lax.broadcasted_iota(jnp.int32, logits.shape, 1).astype(F32)
    rem = logits
    sel = jnp.zeros(logits.shape, F32)
    top = None
    denom = None
    for kk in range(TOP_K):
        m = jnp.max(rem, axis=-1, keepdims=True)
        idx = jnp.min(jnp.where(rem == m, lane, float(N_EXPERTS)), axis=-1, keepdims=True)
        pick = lane == idx
        sel = jnp.where(pick, 1.0, sel)
        rem = jnp.where(pick, -jnp.inf, rem)
        if kk == 0:
            top = m
            denom = jnp.ones_like(m)
        else:
            denom = denom + jnp.exp(m - top)
    wf_ref[...] = jnp.where(sel > 0.0, jnp.exp(logits - top) / denom, 0.0)
    ms_ref[...] = sel


def _mid(o_atts, hfbs, gg, gs, xs, mod, w_pa, w_pr, w_out, ln1_g, ln1_b, w_router, b_router, lay, tm):
    nt = lay.n_tokens

    def seq_map(i):
        return (lay.seq_and_pos(i * tm)[0], 0, 0)

    tok = lambda w: pl.BlockSpec((tm, w), lambda i: (i, 0))
    vec = lambda w: _const_spec((1, w))
    row_specs, tile_ends = lay.trunk_specs(
        tm, lambda local: pl.BlockSpec((tm, D_MODEL), lambda i: (local(i), 0)))
    h_pairs, _ = lay.trunk_specs(
        tm, lambda local: [pl.BlockSpec((1, tm, D_MODEL), lambda i, d=d: (d, local(i), 0)) for d in range(2)])
    h_specs = [spec for pair in h_pairs for spec in pair]
    h_args = [hfb for hfb in hfbs for _ in range(2)]
    return pl.pallas_call(
        functools.partial(_mid_kernel, tile_ends=tile_ends),
        grid=(nt // tm,),
        in_specs=row_specs + h_specs + row_specs + [
                  tok(D_MODEL), tok(2 * D_MODEL),
                  pl.BlockSpec((1, N_MOD, D_MODEL), seq_map),
                  _const_spec((D_MODEL, D_MODEL)), _const_spec((D_MODEL, D_MODEL)),
                  _const_spec((D_MODEL, D_MODEL)),
                  vec(D_MODEL), vec(D_MODEL),
                  _const_spec((D_MODEL, 2 * N_EXPERTS)), vec(N_EXPERTS)],
        out_specs=[tok(D_MODEL), pl.BlockSpec((ROW_SUB * tm, 128), lambda i: (i, 0)),
                   tok(N_EXPERTS), tok(N_EXPERTS)],
        out_shape=[jax.ShapeDtypeStruct((nt, D_MODEL), F32),
                   jax.ShapeDtypeStruct((ROW_SUB * nt, 128), F32),
                   jax.ShapeDtypeStruct((nt, N_EXPERTS), F32),
                   jax.ShapeDtypeStruct((nt, N_EXPERTS), F32)],
        compiler_params=_cparams(("arbitrary",)),
        name="merge_router",
    )(*o_atts, *h_args, *xs, gg, gs, mod, w_pa, w_pr, w_out, ln1_g.reshape(1, -1), ln1_b.reshape(1, -1),
      w_router, b_router.reshape(1, -1))


def _rank_kernel(ms_ref, rank_ref, cnt_ref, tri_scr, run_scr, *, tp):
    @pl.when(pl.program_id(0) == 0)
    def _():
        r = lax.broadcasted_iota(jnp.int32, (tp, tp), 0)
        c = lax.broadcasted_iota(jnp.int32, (tp, tp), 1)
        tri_scr[...] = jnp.where(c < r, 1.0, 0.0).astype(BF16)
        run_scr[...] = jnp.zeros(run_scr.shape, F32)

    ms = ms_ref[...]
    before = jnp.dot(tri_scr[...], ms.astype(BF16), preferred_element_type=F32)
    rank_ref[...] = before + run_scr[...]
    run_scr[...] = run_scr[...] + jnp.sum(ms, axis=0, keepdims=True)
    cnt_ref[...] = jnp.broadcast_to(run_scr[...], cnt_ref.shape)


def _rank(msel, tp):
    nt = msel.shape[0]
    return pl.pallas_call(
        functools.partial(_rank_kernel, tp=tp),
        grid=(nt // tp,),
        in_specs=[pl.BlockSpec((tp, N_EXPERTS), lambda i: (i, 0))],
        out_specs=[pl.BlockSpec((tp, N_EXPERTS), lambda i: (i, 0)),
                   pl.BlockSpec((8, N_EXPERTS), lambda i: (0, 0))],
        out_shape=[jax.ShapeDtypeStruct((nt, N_EXPERTS), F32),
                   jax.ShapeDtypeStruct((8, N_EXPERTS), F32)],
        scratch_shapes=[pltpu.VMEM((tp, tp), BF16), pltpu.VMEM((1, N_EXPERTS), F32)],
        compiler_params=_cparams(("arbitrary",)),
        name="route_rank",
    )(msel)


def _slots_kernel(rank_ref, ms_ref, wf_ref, start_ref, slot_ref, wt_ref):
    ms = ms_ref[...]
    wf = wf_ref[...]
    slot_full = rank_ref[...] + start_ref[...]
    lane = lax.broadcasted_iota(jnp.int32, ms.shape, 1).astype(F32)
    out_lane = lax.broadcasted_iota(jnp.int32, slot_ref.shape, 1)
    slots = jnp.zeros(slot_ref.shape, F32)
    wts = jnp.zeros(wt_ref.shape, F32)
    rem = ms
    for kk in range(TOP_K):
        idx = jnp.min(jnp.where(rem > 0.0, lane, float(2 * N_EXPERTS)), axis=-1, keepdims=True)
        pick = lane == idx
        s_k = jnp.sum(jnp.where(pick, slot_full, 0.0), axis=-1, keepdims=True)
        w_k = jnp.sum(jnp.where(pick, wf, 0.0), axis=-1, keepdims=True)
        rem = jnp.where(pick, 0.0, rem)
        slots = jnp.where(out_lane == kk, s_k, slots)
        wts = jnp.where(out_lane == kk, w_k, wts)
    slot_ref[...] = slots.astype(jnp.int32)
    wt_ref[...] = wts


def _slots(rank, msel, wfull, pad_start, tp):
    nt = msel.shape[0]
    tok = pl.BlockSpec((tp, N_EXPERTS), lambda i: (i, 0))
    out = pl.BlockSpec((tp, 128), lambda i: (i, 0))
    return pl.pallas_call(
        _slots_kernel,
        grid=(nt // tp,),
        in_specs=[tok, tok, tok, pl.BlockSpec((1, N_EXPERTS), lambda i: (0, 0))],
        out_specs=[out, out],
        out_shape=[jax.ShapeDtypeStruct((nt, 128), jnp.int32),
                   jax.ShapeDtypeStruct((nt, 128), F32)],
        compiler_params=_cparams(("arbitrary",)),
        name="route_slots",
    )(rank, msel, wfull, pad_start)


def _row_view(ref, row):
    return ref.at[pl.ds(pl.multiple_of(row * ROW_SUB, ROW_SUB), ROW_SUB), :]


def _dispatch_kernel(slot_ref, tail_ref, h_ref, xs_ref, zero_scr, sem, zsem, *, tg, tb):
    @pl.when(pl.program_id(0) == 0)
    def _():
        zero_scr[...] = jnp.zeros(zero_scr.shape, F32)
        for e in range(N_EXPERTS):
            tail = pl.multiple_of(tail_ref[e] * ROW_SUB, tb * ROW_SUB)
            pltpu.make_async_copy(zero_scr, xs_ref.at[pl.ds(tail, tb * ROW_SUB), :], zsem).start()
        for e in range(N_EXPERTS):
            pltpu.make_async_copy(zero_scr, xs_ref.at[pl.ds(0, tb * ROW_SUB), :], zsem).wait()

    def row_copy(t, slot, kk):
        return pltpu.make_async_copy(_row_view(h_ref, t), _row_view(xs_ref, slot), sem)

    def issue(t, carry):
        for kk in range(TOP_K):
            row_copy(t, slot_ref[t * TOP_K + kk], kk).start(priority=kk % 2)
        return carry

    lax.fori_loop(0, tg, issue, 0, unroll=2)

    def drain(t, carry):
        for kk in range(TOP_K):
            row_copy(0, 0, kk).wait()
        return carry

    lax.fori_loop(0, tg, drain, 0)


def _dispatch(slots_flat, tail_start, h2r, cap, tg, tb):
    nt = h2r.shape[0] // ROW_SUB
    return pl.pallas_call(
        functools.partial(_dispatch_kernel, tg=tg, tb=tb),
        grid=(nt // tg,),
        in_specs=[pl.BlockSpec((tg * TOP_K,), lambda i: (i,), memory_space=pltpu.SMEM),
                  pl.BlockSpec(memory_space=pltpu.SMEM),
                  pl.BlockSpec((tg * ROW_SUB, 128), lambda i: (i, 0))],
        out_specs=pl.BlockSpec(memory_space=pl.ANY),
        out_shape=jax.ShapeDtypeStruct((cap * ROW_SUB, 128), F32),
        scratch_shapes=[pltpu.VMEM((tb * ROW_SUB, 128), F32), pltpu.SemaphoreType.DMA(()),
                        pltpu.SemaphoreType.DMA(())],
        compiler_params=_cparams(("arbitrary",)),
        name="moe_dispatch",
    )(slots_flat, tail_start, h2r)


def _expert_kernel(be_ref, bv_ref, xs_ref, w1_ref, b1_ref, w2_ref, b2_ref, ys_ref, *, tb):
    valid = bv_ref[pl.program_id(0)]

    @pl.when(valid > 0)
    def _():
        xb = _load_row_tiles(xs_ref, tb).astype(BF16)
        gu = jnp.dot(xb, w1_ref[0], preferred_element_type=F32) + b1_ref[0]
        glu = jnp.minimum(gu[:, :D_FF], SWIGLU_LIMIT)
        lin = jnp.clip(gu[:, D_FF:], -SWIGLU_LIMIT, SWIGLU_LIMIT)
        act = (lin + 1.0) * glu * _sigmoid(SWIGLU_ALPHA * glu)
        y = jnp.dot(act.astype(BF16), w2_ref[0], preferred_element_type=F32) + b2_ref[0]
        _store_row_tiles(ys_ref, y)

    @pl.when(valid <= 0)
    def _():
        ys_ref[...] = jnp.zeros(ys_ref.shape, F32)


def _experts(block_expert, block_valid, xs, w1, b1, w2, b2, tb):
    cap = xs.shape[0] // ROW_SUB
    rows = pl.BlockSpec((tb * ROW_SUB, 128), lambda i, be, bv: (i, 0))
    grid_spec = pltpu.PrefetchScalarGridSpec(
        num_scalar_prefetch=2,
        grid=(cap // tb,),
        in_specs=[rows,
                  pl.BlockSpec((1, D_MODEL, 2 * D_FF), lambda i, be, bv: (be[i], 0, 0)),
                  pl.BlockSpec((1, 1, 2 * D_FF), lambda i, be, bv: (be[i], 0, 0)),
                  pl.BlockSpec((1, D_FF, D_MODEL), lambda i, be, bv: (be[i], 0, 0)),
                  pl.BlockSpec((1, 1, D_MODEL), lambda i, be, bv: (be[i], 0, 0))],
        out_specs=rows,
    )
    return pl.pallas_call(
        functools.partial(_expert_kernel, tb=tb),
        grid_spec=grid_spec,
        out_shape=jax.ShapeDtypeStruct((cap * ROW_SUB, 128), F32),
        compiler_params=_cparams(("arbitrary",)),
        name="moe_experts",
    )(block_expert, block_valid, xs, w1, b1.reshape(N_EXPERTS, 1, -1), w2, b2.reshape(N_EXPERTS, 1, -1))


def _combine_kernel(slot_ref, next_slot_ref, ys_ref, wt_ref, x1_ref, mod_ref, g_ref, b_ref, *rest,
                    tc, tile_ends):
    y_refs = rest[:len(tile_ends)]
    buf, sem = rest[len(tile_ends):]
    i = pl.program_id(0)
    cur = i % 2

    def row_copy(slots, t, kk, half):
        return pltpu.make_async_copy(_row_view(ys_ref, slots[t * TOP_K + kk]),
                                     _row_view(buf.at[half, kk], t), sem.at[half])

    def gather(slots, half):
        def issue(t, carry):
            for kk in range(TOP_K):
                row_copy(slots, t, kk, half).start(priority=kk % 2)
            return carry
        lax.fori_loop(0, tc, issue, 0, unroll=2)

    @pl.when(i == 0)
    def _():
        gather(slot_ref, 0)

    @pl.when(i + 1 < pl.num_programs(0))
    def _():
        gather(next_slot_ref, 1 - cur)

    def drain(t, carry):
        for kk in range(TOP_K):
            row_copy(slot_ref, 0, kk, cur).wait()
        return carry

    lax.fori_loop(0, tc, drain, 0)
    wt = wt_ref[...]
    ff = sum(wt[:, kk:kk + 1] * _load_row_tiles(buf.at[cur, kk], tc) for kk in range(TOP_K))
    g2 = mod_ref[0, 5:6, :]
    y = _layer_norm(DN_ALPHA * x1_ref[...] + (1.0 + g2) * ff) * g_ref[...] + b_ref[...]
    i = pl.program_id(0)
    start = 0
    for y_ref, end in zip(y_refs, tile_ends):
        @pl.when((i >= start) & (i < end))
        def _(y_ref=y_ref):
            y_ref[...] = y
        start = end


def _combine(slots_flat, ys, wts, x1, mod, ln2_g, ln2_b, lay, tc):
    nt = lay.n_tokens

    def seq_map(i):
        return (lay.seq_and_pos(i * tc)[0], 0, 0)

    out_specs, tile_ends = lay.trunk_specs(
        tc, lambda local: pl.BlockSpec((tc, D_MODEL), lambda i: (local(i), 0)))
    n_steps = nt // tc
    return pl.pallas_call(
        functools.partial(_combine_kernel, tc=tc, tile_ends=tile_ends),
        grid=(n_steps,),
        in_specs=[pl.BlockSpec((tc * TOP_K,), lambda i: (i,), memory_space=pltpu.SMEM),
                  pl.BlockSpec((tc * TOP_K,), lambda i: (jnp.minimum(i + 1, n_steps - 1),),
                               memory_space=pltpu.SMEM),
                  pl.BlockSpec(memory_space=pl.ANY),
                  pl.BlockSpec((tc, 128), lambda i: (i, 0)),
                  pl.BlockSpec((tc, D_MODEL), lambda i: (i, 0)),
                  pl.BlockSpec((1, N_MOD, D_MODEL), seq_map),
                  _const_spec((1, D_MODEL)), _const_spec((1, D_MODEL))],
        out_specs=out_specs,
        out_shape=[jax.ShapeDtypeStruct((b * s, D_MODEL), F32) for b, s in lay.trunks],
        scratch_shapes=[pltpu.VMEM((2, TOP_K, tc * ROW_SUB, 128), F32), pltpu.SemaphoreType.DMA((2,))],
        compiler_params=_cparams(("arbitrary",)),
        name="moe_combine",
    )(slots_flat, slots_flat, ys, wts, x1, mod, ln2_g.reshape(1, -1), ln2_b.reshape(1, -1))


def _encoder_layer(xs, cs, p):
    trunks = [(x.shape[0], x.shape[1]) for x in xs]
    lay = _Layout(trunks)
    nt = lay.n_tokens
    t = _tiles(nt)

    x2d = [xi.reshape(-1, D_MODEL) for xi in xs]
    c = jnp.concatenate(cs, axis=0)
    bp = -(-lay.n_seqs // 8) * 8
    c = jnp.pad(c, ((0, bp - lay.n_seqs), (0, 0)))
    mod = _ada_mod(c, p["w_ada"], p["b_ada"]).reshape(bp, N_MOD, D_MODEL)

    q, k, vt3, xr, gg, gs = _inproj(x2d, mod, p["w_in"].astype(BF16), p["q_gain"], p["k_gain"],
                                    _rope_tables(lay.max_seq), lay, t["tm_in"])

    w_lru = jnp.concatenate([p["lru_wa"], p["lru_wx"]], axis=-1).astype(BF16)
    b_lru = jnp.stack([p["lru_ba"], p["lru_bx"]], axis=1)
    lam = p["lru_lam"].reshape(2, 1, D_MODEL)
    o_atts, hfbs = [], []
    tok_off = 0
    for b, s in trunks:
        o_atts.append(_attention(q, k, vt3, tok_off, b, s, t["tq"], t["tm_in"]))
        hfbs.append(_scan(xr, p["conv_w"], p["conv_b"].reshape(1, -1), w_lru, b_lru, lam,
                          tok_off, b, s, t["tt"]))
        tok_off += b * s

    wr_hi = p["w_router"].astype(BF16)
    wr_lo = (p["w_router"] - wr_hi.astype(F32)).astype(BF16)
    x1, h2r, wfull, msel = _mid(o_atts, hfbs, gg, gs, x2d, mod, p["w_pa"].astype(BF16),
                               p["w_pr"].astype(BF16), p["w_out"].astype(BF16), p["ln1_g"], p["ln1_b"],
                               jnp.concatenate([wr_hi, wr_lo], axis=1), p["b_router"], lay, t["tm_mid"])

    tb = t["tb"]
    rank, cnt = _rank(msel, t["tp"])
    counts = cnt[0].astype(jnp.int32)
    padded = (counts + tb - 1) // tb * tb
    pad_end = jnp.cumsum(padded)
    pad_start = pad_end - padded
    n_blocks = nt * TOP_K // tb + N_EXPERTS
    blk0 = jnp.arange(n_blocks, dtype=jnp.int32) * tb
    block_expert = jnp.minimum(jnp.sum(pad_end[None, :] <= blk0[:, None], axis=1), N_EXPERTS - 1).astype(jnp.int32)
    block_valid = jnp.clip(pad_start[block_expert] + counts[block_expert] - blk0, 0, tb).astype(jnp.int32)
    slots, wts = _slots(rank, msel, wfull, pad_start.astype(F32).reshape(1, N_EXPERTS), t["tp"])
    slots_flat = slots[:, :TOP_K].reshape(-1)

    tail_start = jnp.where(padded > 0, pad_end - tb, pad_end[-1] - tb).astype(jnp.int32)
    xs_rows = _dispatch(slots_flat, tail_start, h2r, n_blocks * tb, t["tg"], tb)
    ys = _experts(block_expert, block_valid, xs_rows, p["w1"].astype(BF16), p["b1"],
                  p["w2"].astype(BF16), p["b2"], tb)
    ys_out = _combine(slots_flat, ys, wts, x1, mod, p["ln2_g"], p["ln2_b"], lay, t["tc"])
    return [y.reshape(b, s, D_MODEL) for y, (b, s) in zip(ys_out, trunks)]


_PARAM_NAMES = ("w_ada", "b_ada", "w_in", "q_gain", "k_gain", "conv_w", "conv_b", "lru_wa", "lru_ba",
                "lru_wx", "lru_bx", "lru_lam", "w_pa", "w_pr", "w_out", "ln1_g", "ln1_b", "w_router",
                "b_router", "w1", "b1", "w2", "b2", "ln2_g", "ln2_b")


def kernel(x_prompt, x_sample, c_prompt, c_sample, w_ada, b_ada, w_in, q_gain, k_gain, conv_w, conv_b, lru_wa, lru_ba, lru_wx, lru_bx, lru_lam, w_pa, w_pr, w_out, ln1_g, ln1_b, w_router, b_router, w1, b1, w2, b2, ln2_g, ln2_b):
    stacked = (w_ada, b_ada, w_in, q_gain, k_gain, conv_w, conv_b, lru_wa, lru_ba, lru_wx, lru_bx,
               lru_lam, w_pa, w_pr, w_out, ln1_g, ln1_b, w_router, b_router, w1, b1, w2, b2, ln2_g, ln2_b)
    xs, cs = [x_prompt, x_sample], [c_prompt, c_sample]
    for layer in range(DEPTH):
        p = {name: arr[layer] for name, arr in zip(_PARAM_NAMES, stacked)}
        xs = _encoder_layer(xs, cs, p)
    return (xs[0], xs[1])
```

```python
import functools
import math

import jax
import jax.numpy as jnp
from jax import lax
from jax.experimental import pallas as pl
from jax.experimental.pallas import tpu as pltpu

F32 = jnp.float32
BF16 = jnp.bfloat16

D_MODEL = 1024
GRID_W = 64
N_HEADS = 8
N_KV_HEADS = 2
HEAD_DIM = 128
GROUPS = N_HEADS // N_KV_HEADS
KV_WIDTH = N_KV_HEADS * HEAD_DIM
ROPE_THETA = 10000.0
RNN_BLOCKS = 8
RNN_BLOCK_W = D_MODEL // RNN_BLOCKS
CONV_W = 4
LRU_C = 8.0
N_EXPERTS = 32
TOP_K = 4
D_FF = D_MODEL
SWIGLU_LIMIT = 7.0
SWIGLU_ALPHA = 1.702
DEPTH = 1
DN_ALPHA = (2 * DEPTH) ** 0.25
LN_EPS = 1e-5
RMS_EPS = 1e-6
N_MOD = 6
IN_WIDTH = D_MODEL + 2 * KV_WIDTH + 2 * D_MODEL + 2 * D_MODEL
_Q0, _K0, _V0, _XR0, _GR0, _GL0 = 0, 1024, 1280, 1536, 2560, 3584

V7X_VMEM_LIMIT_BYTES = 56 * 1024 * 1024
HALO = 16


def _tiles(n_tokens):
    big = n_tokens >= 4096
    return dict(
        tm_in=512 if big else 128,
        tq=256 if big else 128,
        tt=512 if big else 128,
        tm_mid=256 if big else 128,
        tp=512 if big else 128,
        tg=512 if big else 256,
        tb=512 if big else 128,
        tc=256,
    )


def _cparams(sem, flags=None):
    return pltpu.CompilerParams(dimension_semantics=sem, vmem_limit_bytes=V7X_VMEM_LIMIT_BYTES, flags=flags)


def _const_spec(shape):
    nd = len(shape)
    return pl.BlockSpec(shape, lambda *_: (0,) * nd, pipeline_mode=pl.Buffered(1))


def _layer_norm(x):
    mu = jnp.mean(x, axis=-1, keepdims=True)
    xc = x - mu
    var = jnp.mean(xc * xc, axis=-1, keepdims=True)
    return xc * lax.rsqrt(var + LN_EPS)


def _sigmoid(x):
    return 1.0 / (1.0 + jnp.exp(-x))


ROW_SUB = D_MODEL // 128


def _store_row_tiles(ref, val):
    n = val.shape[0]
    for j in range(ROW_SUB):
        ref[pl.ds(j, n, stride=ROW_SUB), :] = val[:, j * 128:(j + 1) * 128]


def _load_row_tiles(ref, n):
    return jnp.concatenate([ref[pl.ds(j, n, stride=ROW_SUB), :] for j in range(ROW_SUB)], axis=1)


def _ada_kernel(c_ref, w_ref, b_ref, o_ref):
    c = c_ref[...]
    s = c * _sigmoid(c)
    o_ref[...] = jnp.dot(s, w_ref[...], preferred_element_type=F32,
                         precision=lax.Precision.HIGHEST) + b_ref[...]


def _ada_mod(c_all, w_ada, b_ada):
    bp = c_all.shape[0]
    ncol = w_ada.shape[1]
    return pl.pallas_call(
        _ada_kernel,
        grid=(ncol // D_MODEL,),
        in_specs=[pl.BlockSpec((bp, D_MODEL), lambda j: (0, 0)),
                  pl.BlockSpec((D_MODEL, D_MODEL), lambda j: (0, j)),
                  pl.BlockSpec((1, D_MODEL), lambda j: (0, j))],
        out_specs=pl.BlockSpec((bp, D_MODEL), lambda j: (0, j)),
        out_shape=jax.ShapeDtypeStruct((bp, ncol), F32),
        compiler_params=_cparams(("arbitrary",)),
        name="ada_mod",
    )(c_all, w_ada, b_ada.reshape(1, ncol))


class _Layout:
    def __init__(self, trunks):
        self.trunks = tuple(trunks)
        self.n_tokens = sum(b * s for b, s in trunks)
        self.n_seqs = sum(b for b, _ in trunks)
        self.max_seq = max(s for _, s in trunks)

    def seq_and_pos(self, t0):
        seq = jnp.int32(0)
        pos = jnp.int32(0)
        tok_off, seq_off = 0, 0
        for b, s in self.trunks:
            inside = (t0 >= tok_off) & (t0 < tok_off + b * s)
            rel = jnp.maximum(t0 - tok_off, 0)
            seq = jnp.where(inside, seq_off + rel // s, seq)
            pos = jnp.where(inside, rel % s, pos)
            tok_off += b * s
            seq_off += b
        return seq, pos

    def trunk_specs(self, tm, make_spec):
        specs, ends, t0 = [], [], 0
        for b, s in self.trunks:
            n_t = b * s // tm
            specs.append(make_spec(functools.partial(_clamped_local, t0=t0, n_t=n_t)))
            t0 += n_t
            ends.append(t0)
        return specs, tuple(ends)


def _clamped_local(i, *, t0, n_t):
    return jnp.clip(i - t0, 0, n_t - 1)


def _owner_value(loads, i, tile_ends):
    val = loads[-1]()
    for j in range(len(loads) - 2, -1, -1):
        val = jnp.where(i < tile_ends[j], loads[j](), val)
    return val


def _rope_tables(max_seq):
    t = jnp.arange(max_seq)
    rows = (t // GRID_W).astype(F32)
    cols = (t % GRID_W).astype(F32)
    axis_dim = HEAD_DIM // 2
    inv = ROPE_THETA ** (-jnp.arange(0, axis_dim, 2, dtype=F32) / axis_dim)
    ar, ac = rows[:, None] * inv, cols[:, None] * inv
    cr, sr, cc, sc = jnp.cos(ar), jnp.sin(ar), jnp.cos(ac), jnp.sin(ac)
    z = jnp.zeros_like(sr)
    cos_t = jnp.concatenate([cr, cr, cc, cc], axis=-1)
    up_t = jnp.concatenate([-sr, z, -sc, z], axis=-1)
    dn_t = jnp.concatenate([z, sr, z, sc], axis=-1)
    return cos_t, up_t, dn_t


def _inproj_kernel(*refs, tile_ends):
    n_tr = len(tile_ends)
    x_refs = refs[:n_tr]
    (mod_ref, w_ref, qg_ref, kg_ref, cos_ref, up_ref, dn_ref,
     q_ref, k_ref, vt_ref, xr_ref, gg_ref, gs_ref) = refs[n_tr:]
    x = _owner_value([lambda r=r: r[...] for r in x_refs], pl.program_id(0), tile_ends)
    sh1 = mod_ref[0, 0:1, :]
    sc1 = mod_ref[0, 1:2, :]
    h = (_layer_norm(x) * (1.0 + sc1) + sh1).astype(BF16)
    cos_t, up_t, dn_t = cos_ref[...], up_ref[...], dn_ref[...]

    def proj(c0, width):
        return jnp.dot(h, w_ref[:, c0:c0 + width], preferred_element_type=F32)

    def norm_rope(z, gain):
        ms = jnp.mean(z * z, axis=-1, keepdims=True)
        y = z * lax.rsqrt(ms + RMS_EPS) * gain
        return (y * cos_t + pltpu.roll(y, HEAD_DIM - 32, 1) * up_t
                + pltpu.roll(y, 32, 1) * dn_t)

    zq = proj(_Q0, D_MODEL)
    qg = qg_ref[...] * (HEAD_DIM ** -0.5 * math.log2(math.e))
    for hd in range(N_HEADS):
        sl = slice(hd * HEAD_DIM, (hd + 1) * HEAD_DIM)
        q_ref[:, sl] = norm_rope(zq[:, sl], qg).astype(BF16)
    zk = proj(_K0, KV_WIDTH)
    kg = kg_ref[...]
    for hd in range(N_KV_HEADS):
        sl = slice(hd * HEAD_DIM, (hd + 1) * HEAD_DIM)
        k_ref[:, sl] = norm_rope(zk[:, sl], kg).astype(BF16)
    zv = proj(_V0, KV_WIDTH)
    vt_ref[0] = zv.T.astype(BF16)
    xr_ref[...] = proj(_XR0, D_MODEL).astype(BF16)
    gg_ref[...] = jax.nn.gelu(proj(_GR0, D_MODEL), approximate=True).astype(BF16)
    gs_ref[...] = _sigmoid(proj(_GL0, 2 * D_MODEL)).astype(BF16)


def _inproj(xs, mod, w_in, q_gain, k_gain, tables, lay, tm):
    nt = lay.n_tokens
    cos_t, up_t, dn_t = tables

    def seq_map(i):
        return (lay.seq_and_pos(i * tm)[0], 0, 0)

    def pos_map(i):
        return (lay.seq_and_pos(i * tm)[1] // tm, 0)

    tok = lambda w: pl.BlockSpec((tm, w), lambda i: (i, 0))
    rope = pl.BlockSpec((tm, HEAD_DIM), pos_map)
    x_specs, tile_ends = lay.trunk_specs(
        tm, lambda local: pl.BlockSpec((tm, D_MODEL), lambda i: (local(i), 0)))
    return pl.pallas_call(
        functools.partial(_inproj_kernel, tile_ends=tile_ends),
        grid=(nt // tm,),
        in_specs=x_specs + [
                  pl.BlockSpec((1, N_MOD, D_MODEL), seq_map),
                  _const_spec((D_MODEL, IN_WIDTH)),
                  _const_spec((1, HEAD_DIM)), _const_spec((1, HEAD_DIM)),
                  rope, rope, rope],
        out_specs=[tok(D_MODEL), tok(KV_WIDTH),
                   pl.BlockSpec((1, KV_WIDTH, tm), lambda i: (i, 0, 0)),
                   tok(D_MODEL), tok(D_MODEL), tok(2 * D_MODEL)],
        out_shape=[jax.ShapeDtypeStruct((nt, D_MODEL), BF16),
                   jax.ShapeDtypeStruct((nt, KV_WIDTH), BF16),
                   jax.ShapeDtypeStruct((nt // tm, KV_WIDTH, tm), BF16),
                   jax.ShapeDtypeStruct((nt, D_MODEL), BF16),
                   jax.ShapeDtypeStruct((nt, D_MODEL), BF16),
                   jax.ShapeDtypeStruct((nt, 2 * D_MODEL), BF16)],
        compiler_params=_cparams(("arbitrary",)),
        name="in_proj",
    )(*xs, mod, w_in, q_gain.reshape(1, HEAD_DIM), k_gain.reshape(1, HEAD_DIM), cos_t, up_t, dn_t)


def _attn_kernel(q_ref, k_ref, vt_ref, o_ref, qt_scr, s_scr, m_scr, l_scr, acc_scr, *,
                 tq, tk, n_kv, group):
    for g in range(GROUPS):
        qg = q_ref[:, g * HEAD_DIM:(g + 1) * HEAD_DIM].astype(F32)
        qt_scr[:, g * tq:(g + 1) * tq] = qg.T.astype(BF16)
    m_scr[...] = jnp.full(m_scr.shape, -jnp.inf, F32)
    l_scr[...] = jnp.zeros(l_scr.shape, F32)
    acc_scr[...] = jnp.zeros(acc_scr.shape, F32)

    def scores(j, slot):
        kt = k_ref[pl.ds(pl.multiple_of(j * tk, tk), tk), :]
        s_scr[slot] = jnp.dot(kt, qt_scr[...], preferred_element_type=F32)

    def accumulate(j, slot):
        s = s_scr[slot]
        m_old = m_scr[...]
        m_new = jnp.maximum(m_old, jnp.max(s, axis=0, keepdims=True))
        alpha = jnp.exp2(m_old - m_new)
        p = jnp.exp2(s - m_new)
        l_scr[...] = alpha * l_scr[...] + jnp.sum(p, axis=0, keepdims=True)
        pv = jnp.dot(vt_ref[j], p.astype(BF16), preferred_element_type=F32)
        acc_scr[...] = alpha * acc_scr[...] + pv
        m_scr[...] = m_new

    scores(0, 0)

    def body(i, carry):
        j = group * i
        for u in range(group):
            scores(jnp.minimum(j + u + 1, n_kv - 1), (u + 1) % 2)
            accumulate(j + u, u % 2)
        return carry

    lax.fori_loop(0, n_kv // group, body, 0)
    out = acc_scr[...] / l_scr[...]
    for g in range(GROUPS):
        o_ref[:, g * HEAD_DIM:(g + 1) * HEAD_DIM] = out[:, g * tq:(g + 1) * tq].T.astype(BF16)


def _attention(q, k, vt3, tok_off, batch, seq, tq, tk):
    n_kv = seq // tk
    group = next((g for g in (8, 4) if n_kv % g == 0 and n_kv // g >= 2), 2)
    assert n_kv % group == 0
    qrow0 = tok_off // tq
    srow0 = tok_off // seq
    gw = GROUPS * HEAD_DIM
    n_q = seq // tq
    return pl.pallas_call(
        functools.partial(_attn_kernel, tq=tq, tk=tk, n_kv=n_kv, group=group),
        grid=(batch, N_KV_HEADS, n_q),
        in_specs=[pl.BlockSpec((tq, gw), lambda b, h, i: (qrow0 + b * n_q + i, h)),
                  pl.BlockSpec((seq, HEAD_DIM), lambda b, h, i: (srow0 + b, h)),
                  pl.BlockSpec((n_kv, HEAD_DIM, tk), lambda b, h, i: (srow0 + b, h, 0))],
        out_specs=pl.BlockSpec((tq, gw), lambda b, h, i: (b * n_q + i, h)),
        out_shape=jax.ShapeDtypeStruct((batch * seq, D_MODEL), BF16),
        scratch_shapes=[pltpu.VMEM((HEAD_DIM, GROUPS * tq), BF16),
                        pltpu.VMEM((2, tk, GROUPS * tq), F32),
                        pltpu.VMEM((1, GROUPS * tq), F32),
                        pltpu.VMEM((1, GROUPS * tq), F32),
                        pltpu.VMEM((HEAD_DIM, GROUPS * tq), F32)],
        compiler_params=_cparams(("arbitrary", "arbitrary", "arbitrary")),
        name="attention",
    )(q, k, vt3)


def _scan_kernel(cur_ref, prev_ref, next_ref, cw_ref, cb_ref, w_ref, b_ref, lam_ref, o_ref,
                 ext_scr, a_scr, u_scr, h_scr, carry_scr, *, tt, n_chunks):
    d = pl.program_id(1)
    c = pl.program_id(2)
    chunk = jnp.where(d == 0, c, n_chunks - 1 - c)

    @pl.when(c == 0)
    def _():
        carry_scr[...] = jnp.zeros(carry_scr.shape, F32)

    keep_prev = jnp.where(chunk == 0, 0.0, 1.0)
    keep_next = jnp.where(chunk == n_chunks - 1, 0.0, 1.0)
    ext_scr[0:HALO, :] = prev_ref[...].astype(F32) * keep_prev
    ext_scr[HALO:HALO + tt, :] = cur_ref[...].astype(F32)
    ext_scr[HALO + tt:HALO + tt + HALO, :] = next_ref[...].astype(F32) * keep_next
    xc = cb_ref[...] + sum(cw_ref[j:j + 1, :] * ext_scr[HALO - 2 + j:HALO - 2 + j + tt, :]
                           for j in range(CONV_W))
    xcb = xc.astype(BF16)

    lam = lam_ref[0]
    y = jnp.exp(-jnp.abs(lam))
    w1p = 1.0 + y
    log1p_y = jnp.where(w1p == 1.0, y, jnp.log(w1p) * y / jnp.where(w1p == 1.0, 1.0, w1p - 1.0))
    neg_c_sp = -LRU_C * (jnp.maximum(-lam, 0.0) + log1p_y)

    for n in range(RNN_BLOCKS):
        sl = slice(n * RNN_BLOCK_W, (n + 1) * RNN_BLOCK_W)
        pre = jnp.dot(xcb[:, sl], w_ref[0, n], preferred_element_type=F32)
        r = _sigmoid(pre[:, :RNN_BLOCK_W] + b_ref[0, 0:1, sl])
        i = _sigmoid(pre[:, RNN_BLOCK_W:] + b_ref[0, 1:2, sl])
        a = jnp.exp(r * neg_c_sp[:, sl])
        a_scr[:, sl] = a
        u_scr[:, sl] = jnp.sqrt(1.0 - a * a) * (i * xc[:, sl])

    def step(t, h):
        row = jnp.where(d == 0, t, tt - 1 - t)
        h = a_scr[pl.ds(row, 1), :] * h + u_scr[pl.ds(row, 1), :]
        h_scr[pl.ds(row, 1), :] = h
        return h

    carry_scr[...] = lax.fori_loop(0, tt, step, carry_scr[...], unroll=8)
    o_ref[0] = h_scr[...].astype(BF16)


def _scan(xr, conv_w, conv_b, w_lru, b_lru, lam, tok_off, batch, seq, tt):
    nt = xr.shape[0]
    n_chunks = seq // tt
    row0 = tok_off // tt
    hrow0 = tok_off // HALO
    hpc = tt // HALO
    n_halo = nt // HALO

    def chunk_of(d, c):
        return jnp.where(d == 0, c, n_chunks - 1 - c)

    def prev_map(b, d, c):
        return (jnp.maximum(hrow0 + (b * n_chunks + chunk_of(d, c)) * hpc - 1, 0), 0)

    def next_map(b, d, c):
        return (jnp.minimum(hrow0 + (b * n_chunks + chunk_of(d, c) + 1) * hpc, n_halo - 1), 0)

    return pl.pallas_call(
        functools.partial(_scan_kernel, tt=tt, n_chunks=n_chunks),
        grid=(batch, 2, n_chunks),
        in_specs=[pl.BlockSpec((tt, D_MODEL), lambda b, d, c: (row0 + b * n_chunks + chunk_of(d, c), 0)),
                  pl.BlockSpec((HALO, D_MODEL), prev_map),
                  pl.BlockSpec((HALO, D_MODEL), next_map),
                  pl.BlockSpec((CONV_W, D_MODEL), lambda b, d, c: (0, 0)),
                  pl.BlockSpec((1, D_MODEL), lambda b, d, c: (0, 0)),
                  pl.BlockSpec((1, RNN_BLOCKS, RNN_BLOCK_W, 2 * RNN_BLOCK_W), lambda b, d, c: (d, 0, 0, 0)),
                  pl.BlockSpec((1, 2, D_MODEL), lambda b, d, c: (d, 0, 0)),
                  pl.BlockSpec((1, 1, D_MODEL), lambda b, d, c: (d, 0, 0))],
        out_specs=pl.BlockSpec((1, tt, D_MODEL), lambda b, d, c: (d, b * n_chunks + chunk_of(d, c), 0)),
        out_shape=jax.ShapeDtypeStruct((2, batch * seq, D_MODEL), BF16),
        scratch_shapes=[pltpu.VMEM((tt + 2 * HALO, D_MODEL), F32),
                        pltpu.VMEM((tt, D_MODEL), F32),
                        pltpu.VMEM((tt, D_MODEL), F32),
                        pltpu.VMEM((tt, D_MODEL), F32),
                        pltpu.VMEM((1, D_MODEL), F32)],
        compiler_params=_cparams(("arbitrary", "arbitrary", "arbitrary")),
        name="lru_scan",
    )(xr, xr, xr, conv_w, conv_b, w_lru, b_lru, lam)


def _mid_kernel(*refs, tile_ends):
    n_tr = len(tile_ends)
    oa_refs = refs[:n_tr]
    h_refs = refs[n_tr:3 * n_tr]
    x_refs = refs[3 * n_tr:4 * n_tr]
    (gg_ref, gs_ref, mod_ref, wpa_ref, wpr_ref, wo_ref, g1_ref, b1_ref, wr_ref, br_ref,
     x1_ref, h2_ref, wf_ref, ms_ref) = refs[4 * n_tr:]
    i = pl.program_id(0)
    oa = _owner_value([lambda r=r: r[...] for r in oa_refs], i, tile_ends)
    hsum = _owner_value([lambda f=h_refs[2 * j], b=h_refs[2 * j + 1]: f[0].astype(F32) + b[0].astype(F32)
                         for j in range(n_tr)], i, tile_ends)
    x_in = _owner_value([lambda r=r: r[...] for r in x_refs], i, tile_ends)
    o_att = jnp.dot(oa, wpa_ref[...], preferred_element_type=F32)
    rec = hsum.astype(BF16) * gg_ref[...]
    o_rec = jnp.dot(rec, wpr_ref[...], preferred_element_type=F32)
    gs = gs_ref[...].astype(F32)
    merged = gs[:, :D_MODEL] * o_att + gs[:, D_MODEL:] * o_rec
    mix = jnp.dot(merged.astype(BF16), wo_ref[...], preferred_element_type=F32)
    g1 = mod_ref[0, 2:3, :]
    sh2 = mod_ref[0, 3:4, :]
    sc2 = mod_ref[0, 4:5, :]
    x1 = _layer_norm(DN_ALPHA * x_in + (1.0 + g1) * mix) * g1_ref[...] + b1_ref[...]
    x1_ref[...] = x1
    h2 = _layer_norm(x1) * (1.0 + sc2) + sh2
    _store_row_tiles(h2_ref, h2)
    tm = h2.shape[0]
    h2_hi = h2.astype(BF16)
    h2_lo = (h2 - h2_hi.astype(F32)).astype(BF16)
    cross = jnp.dot(jnp.concatenate([h2_hi, h2_lo], axis=0), wr_ref[...], preferred_element_type=F32)
    logits = ((cross[:tm, :N_EXPERTS] + cross[:tm, N_EXPERTS:])
              + (cross[tm:, :N_EXPERTS] + cross[tm:, N_EXPERTS:])) + br_ref[...]
    lane = lax.broadcasted_iota(jnp.int32, logits.shape, 1).astype(F32)
    rem = logits
    sel = jnp.zeros(logits.shape, F32)
    top = None
    denom = None
    for kk in range(TOP_K):
        m = jnp.max(rem, axis=-1, keepdims=True)
        idx = jnp.min(jnp.where(rem == m, lane, float(N_EXPERTS)), axis=-1, keepdims=True)
        pick = lane == idx
        sel = jnp.where(pick, 1.0, sel)
        rem = jnp.where(pick, -jnp.inf, rem)
        if kk == 0:
            top = m
            denom = jnp.ones_like(m)
        else:
            denom = denom + jnp.exp(m - top)
    wf_ref[...] = jnp.where(sel > 0.0, jnp.exp(logits - top) / denom, 0.0)
    ms_ref[...] = sel


def _mid(o_atts, hfbs, gg, gs, xs, mod, w_pa, w_pr, w_out, ln1_g, ln1_b, w_router, b_router, lay, tm):
    nt = lay.n_tokens

    def seq_map(i):
        return (lay.seq_and_pos(i * tm)[0], 0, 0)

    tok = lambda w: pl.BlockSpec((tm, w), lambda i: (i, 0))
    vec = lambda w: _const_spec((1, w))
    row_specs, tile_ends = lay.trunk_specs(
        tm, lambda local: pl.BlockSpec((tm, D_MODEL), lambda i: (local(i), 0)))
    h_pairs, _ = lay.trunk_specs(
        tm, lambda local: [pl.BlockSpec((1, tm, D_MODEL), lambda i, d=d: (d, local(i), 0)) for d in range(2)])
    h_specs = [spec for pair in h_pairs for spec in pair]
    h_args = [hfb for hfb in hfbs for _ in range(2)]
    return pl.pallas_call(
        functools.partial(_mid_kernel, tile_ends=tile_ends),
        grid=(nt // tm,),
        in_specs=row_specs + h_specs + row_specs + [
                  tok(D_MODEL), tok(2 * D_MODEL),
                  pl.BlockSpec((1, N_MOD, D_MODEL), seq_map),
                  _const_spec((D_MODEL, D_MODEL)), _const_spec((D_MODEL, D_MODEL)),
                  _const_spec((D_MODEL, D_MODEL)),
                  vec(D_MODEL), vec(D_MODEL),
                  _const_spec((D_MODEL, 2 * N_EXPERTS)), vec(N_EXPERTS)],
        out_specs=[tok(D_MODEL), pl.BlockSpec((ROW_SUB * tm, 128), lambda i: (i, 0)),
                   tok(N_EXPERTS), tok(N_EXPERTS)],
        out_shape=[jax.ShapeDtypeStruct((nt, D_MODEL), F32),
                   jax.ShapeDtypeStruct((ROW_SUB * nt, 128), F32),
                   jax.ShapeDtypeStruct((nt, N_EXPERTS), F32),
                   jax.ShapeDtypeStruct((nt, N_EXPERTS), F32)],
        compiler_params=_cparams(("arbitrary",)),
        name="merge_router",
    )(*o_atts, *h_args, *xs, gg, gs, mod, w_pa, w_pr, w_out, ln1_g.reshape(1, -1), ln1_b.reshape(1, -1),
      w_router, b_router.reshape(1, -1))


def _rank_kernel(ms_ref, rank_ref, cnt_ref, tri_scr, run_scr, *, tp):
    @pl.when(pl.program_id(0) == 0)
    def _():
        r = lax.broadcasted_iota(jnp.int32, (tp, tp), 0)
        c = lax.broadcasted_iota(jnp.int32, (tp, tp), 1)
        tri_scr[...] = jnp.where(c < r, 1.0, 0.0).astype(BF16)
        run_scr[...] = jnp.zeros(run_scr.shape, F32)

    ms = ms_ref[...]
    before = jnp.dot(tri_scr[...], ms.astype(BF16), preferred_element_type=F32)
    rank_ref[...] = before + run_scr[...]
    run_scr[...] = run_scr[...] + jnp.sum(ms, axis=0, keepdims=True)
    cnt_ref[...] = jnp.broadcast_to(run_scr[...], cnt_ref.shape)


def _rank(msel, tp):
    nt = msel.shape[0]
    return pl.pallas_call(
        functools.partial(_rank_kernel, tp=tp),
        grid=(nt // tp,),
        in_specs=[pl.BlockSpec((tp, N_EXPERTS), lambda i: (i, 0))],
        out_specs=[pl.BlockSpec((tp, N_EXPERTS), lambda i: (i, 0)),
                   pl.BlockSpec((8, N_EXPERTS), lambda i: (0, 0))],
        out_shape=[jax.ShapeDtypeStruct((nt, N_EXPERTS), F32),
                   jax.ShapeDtypeStruct((8, N_EXPERTS), F32)],
        scratch_shapes=[pltpu.VMEM((tp, tp), BF16), pltpu.VMEM((1, N_EXPERTS), F32)],
        compiler_params=_cparams(("arbitrary",)),
        name="route_rank",
    )(msel)


def _slots_kernel(rank_ref, ms_ref, wf_ref, start_ref, slot_ref, wt_ref):
    ms = ms_ref[...]
    wf = wf_ref[...]
    slot_full = rank_ref[...] + start_ref[...]
    lane = lax.broadcasted_iota(jnp.int32, ms.shape, 1).astype(F32)
    out_lane = lax.broadcasted_iota(jnp.int32, slot_ref.shape, 1)
    slots = jnp.zeros(slot_ref.shape, F32)
    wts = jnp.zeros(wt_ref.shape, F32)
    rem = ms
    for kk in range(TOP_K):
        idx = jnp.min(jnp.where(rem > 0.0, lane, float(2 * N_EXPERTS)), axis=-1, keepdims=True)
        pick = lane == idx
        s_k = jnp.sum(jnp.where(pick, slot_full, 0.0), axis=-1, keepdims=True)
        w_k = jnp.sum(jnp.where(pick, wf, 0.0), axis=-1, keepdims=True)
        rem = jnp.where(pick, 0.0, rem)
        slots = jnp.where(out_lane == kk, s_k, slots)
        wts = jnp.where(out_lane == kk, w_k, wts)
    slot_ref[...] = slots.astype(jnp.int32)
    wt_ref[...] = wts


def _slots(rank, msel, wfull, pad_start, tp):
    nt = msel.shape[0]
    tok = pl.BlockSpec((tp, N_EXPERTS), lambda i: (i, 0))
    out = pl.BlockSpec((tp, 128), lambda i: (i, 0))
    return pl.pallas_call(
        _slots_kernel,
        grid=(nt // tp,),
        in_specs=[tok, tok, tok, pl.BlockSpec((1, N_EXPERTS), lambda i: (0, 0))],
        out_specs=[out, out],
        out_shape=[jax.ShapeDtypeStruct((nt, 128), jnp.int32),
                   jax.ShapeDtypeStruct((nt, 128), F32)],
        compiler_params=_cparams(("arbitrary",)),
        name="route_slots",
    )(rank, msel, wfull, pad_start)


def _row_view(ref, row):
    return ref.at[pl.ds(pl.multiple_of(row * ROW_SUB, ROW_SUB), ROW_SUB), :]


def _dispatch_kernel(slot_ref, tail_ref, h_ref, xs_ref, zero_scr, sem, zsem, *, tg, tb):
    @pl.when(pl.program_id(0) == 0)
    def _():
        zero_scr[...] = jnp.zeros(zero_scr.shape, F32)
        for e in range(N_EXPERTS):
            tail = pl.multiple_of(tail_ref[e] * ROW_SUB, tb * ROW_SUB)
            pltpu.make_async_copy(zero_scr, xs_ref.at[pl.ds(tail, tb * ROW_SUB), :], zsem).start()
        for e in range(N_EXPERTS):
            pltpu.make_async_copy(zero_scr, xs_ref.at[pl.ds(0, tb * ROW_SUB), :], zsem).wait()

    def row_copy(t, slot, kk):
        return pltpu.make_async_copy(_row_view(h_ref, t), _row_view(xs_ref, slot), sem)

    def issue(t, carry):
        for kk in range(TOP_K):
            row_copy(t, slot_ref[t * TOP_K + kk], kk).start(priority=kk % 2)
        return carry

    lax.fori_loop(0, tg, issue, 0, unroll=2)

    for kk in range(TOP_K):
        pltpu.make_async_copy(h_ref, xs_ref.at[pl.ds(0, tg * ROW_SUB), :], sem).wait()


def _dispatch(slots_flat, tail_start, h2r, cap, tg, tb):
    nt = h2r.shape[0] // ROW_SUB
    return pl.pallas_call(
        functools.partial(_dispatch_kernel, tg=tg, tb=tb),
        grid=(nt // tg,),
        in_specs=[pl.BlockSpec((tg * TOP_K,), lambda i: (i,), memory_space=pltpu.SMEM),
                  pl.BlockSpec(memory_space=pltpu.SMEM),
                  pl.BlockSpec((tg * ROW_SUB, 128), lambda i: (i, 0))],
        out_specs=pl.BlockSpec(memory_space=pl.ANY),
        out_shape=jax.ShapeDtypeStruct((cap * ROW_SUB, 128), F32),
        scratch_shapes=[pltpu.VMEM((tb * ROW_SUB, 128), F32), pltpu.SemaphoreType.DMA(()),
                        pltpu.SemaphoreType.DMA(())],
        compiler_params=_cparams(("arbitrary",)),
        name="moe_dispatch",
    )(slots_flat, tail_start, h2r)


def _expert_kernel(be_ref, bv_ref, xs_ref, w1_ref, b1_ref, w2_ref, b2_ref, ys_ref, w1_scr, w2_scr, *, tb):
    i = pl.program_id(0)
    valid = bv_ref[i]

    @pl.when((i == 0) | (be_ref[i] != be_ref[jnp.maximum(i - 1, 0)]))
    def _():
        w1_scr[...] = w1_ref[0].astype(BF16)
        w2_scr[...] = w2_ref[0].astype(BF16)

    @pl.when(valid > 0)
    def _():
        xb = _load_row_tiles(xs_ref, tb).astype(BF16)
        gu = jnp.dot(xb, w1_scr[...], preferred_element_type=F32) + b1_ref[0]
        glu = jnp.minimum(gu[:, :D_FF], SWIGLU_LIMIT)
        lin = jnp.clip(gu[:, D_FF:], -SWIGLU_LIMIT, SWIGLU_LIMIT)
        act = (lin + 1.0) * glu * _sigmoid(SWIGLU_ALPHA * glu)
        y = jnp.dot(act.astype(BF16), w2_scr[...], preferred_element_type=F32) + b2_ref[0]
        _store_row_tiles(ys_ref, y)

    @pl.when(valid <= 0)
    def _():
        ys_ref[...] = jnp.zeros(ys_ref.shape, F32)


def _experts(block_expert, block_valid, xs, w1, b1, w2, b2, tb):
    cap = xs.shape[0] // ROW_SUB
    rows = pl.BlockSpec((tb * ROW_SUB, 128), lambda i, be, bv: (i, 0))
    grid_spec = pltpu.PrefetchScalarGridSpec(
        num_scalar_prefetch=2,
        grid=(cap // tb,),
        in_specs=[rows,
                  pl.BlockSpec((1, D_MODEL, 2 * D_FF), lambda i, be, bv: (be[i], 0, 0)),
                  pl.BlockSpec((1, 1, 2 * D_FF), lambda i, be, bv: (be[i], 0, 0)),
                  pl.BlockSpec((1, D_FF, D_MODEL), lambda i, be, bv: (be[i], 0, 0)),
                  pl.BlockSpec((1, 1, D_MODEL), lambda i, be, bv: (be[i], 0, 0))],
        out_specs=rows,
        scratch_shapes=[pltpu.VMEM((D_MODEL, 2 * D_FF), BF16), pltpu.VMEM((D_FF, D_MODEL), BF16)],
    )
    return pl.pallas_call(
        functools.partial(_expert_kernel, tb=tb),
        grid_spec=grid_spec,
        out_shape=jax.ShapeDtypeStruct((cap * ROW_SUB, 128), F32),
        compiler_params=_cparams(("arbitrary",)),
        name="moe_experts",
    )(block_expert, block_valid, xs, w1, b1.reshape(N_EXPERTS, 1, -1), w2, b2.reshape(N_EXPERTS, 1, -1))


def _combine_kernel(slot_ref, next_slot_ref, ys_ref, wt_ref, x1_ref, mod_ref, g_ref, b_ref, *rest,
                    tc, tile_ends):
    y_refs = rest[:len(tile_ends)]
    buf, sem = rest[len(tile_ends):]
    i = pl.program_id(0)
    cur = i % 2

    def row_copy(slots, t, kk, half):
        return pltpu.make_async_copy(_row_view(ys_ref, slots[t * TOP_K + kk]),
                                     _row_view(buf.at[half, kk], t), sem.at[half])

    def gather(slots, half):
        def issue(t, carry):
            for kk in range(TOP_K):
                row_copy(slots, t, kk, half).start(priority=kk % 2)
            return carry
        lax.fori_loop(0, tc, issue, 0, unroll=2)

    @pl.when(i == 0)
    def _():
        gather(slot_ref, 0)

    @pl.when(i + 1 < pl.num_programs(0))
    def _():
        gather(next_slot_ref, 1 - cur)

    for kk in range(TOP_K):
        pltpu.make_async_copy(ys_ref.at[pl.ds(0, tc * ROW_SUB), :], buf.at[cur, kk], sem.at[cur]).wait()
    wt = wt_ref[...]
    ff = sum(wt[:, kk:kk + 1] * _load_row_tiles(buf.at[cur, kk], tc) for kk in range(TOP_K))
    g2 = mod_ref[0, 5:6, :]
    y = _layer_norm(DN_ALPHA * x1_ref[...] + (1.0 + g2) * ff) * g_ref[...] + b_ref[...]
    i = pl.program_id(0)
    start = 0
    for y_ref, end in zip(y_refs, tile_ends):
        @pl.when((i >= start) & (i < end))
        def _(y_ref=y_ref):
            y_ref[...] = y
        start = end


def _combine(slots_flat, ys, wts, x1, mod, ln2_g, ln2_b, lay, tc):
    nt = lay.n_tokens

    def seq_map(i):
        return (lay.seq_and_pos(i * tc)[0], 0, 0)

    out_specs, tile_ends = lay.trunk_specs(
        tc, lambda local: pl.BlockSpec((tc, D_MODEL), lambda i: (local(i), 0)))
    n_steps = nt // tc
    return pl.pallas_call(
        functools.partial(_combine_kernel, tc=tc, tile_ends=tile_ends),
        grid=(n_steps,),
        in_specs=[pl.BlockSpec((tc * TOP_K,), lambda i: (i,), memory_space=pltpu.SMEM),
                  pl.BlockSpec((tc * TOP_K,), lambda i: (jnp.minimum(i + 1, n_steps - 1),),
                               memory_space=pltpu.SMEM),
                  pl.BlockSpec(memory_space=pl.ANY),
                  pl.BlockSpec((tc, 128), lambda i: (i, 0)),
                  pl.BlockSpec((tc, D_MODEL), lambda i: (i, 0)),
                  pl.BlockSpec((1, N_MOD, D_MODEL), seq_map),
                  _const_spec((1, D_MODEL)), _const_spec((1, D_MODEL))],
        out_specs=out_specs,
        out_shape=[jax.ShapeDtypeStruct((b * s, D_MODEL), F32) for b, s in lay.trunks],
        scratch_shapes=[pltpu.VMEM((2, TOP_K, tc * ROW_SUB, 128), F32), pltpu.SemaphoreType.DMA((2,))],
        compiler_params=_cparams(("arbitrary",)),
        name="moe_combine",
    )(slots_flat, slots_flat, ys, wts, x1, mod, ln2_g.reshape(1, -1), ln2_b.reshape(1, -1))


def _encoder_layer(xs, cs, p):
    trunks = [(x.shape[0], x.shape[1]) for x in xs]
    lay = _Layout(trunks)
    nt = lay.n_tokens
    t = _tiles(nt)

    x2d = [xi.reshape(-1, D_MODEL) for xi in xs]
    c = jnp.concatenate(cs, axis=0)
    bp = -(-lay.n_seqs // 8) * 8
    c = jnp.pad(c, ((0, bp - lay.n_seqs), (0, 0)))
    mod = _ada_mod(c, p["w_ada"], p["b_ada"]).reshape(bp, N_MOD, D_MODEL)

    q, k, vt3, xr, gg, gs = _inproj(x2d, mod, p["w_in"].astype(BF16), p["q_gain"], p["k_gain"],
                                    _rope_tables(lay.max_seq), lay, t["tm_in"])

    w_lru = jnp.concatenate([p["lru_wa"], p["lru_wx"]], axis=-1).astype(BF16)
    b_lru = jnp.stack([p["lru_ba"], p["lru_bx"]], axis=1)
    lam = p["lru_lam"].reshape(2, 1, D_MODEL)
    o_atts, hfbs = [], []
    tok_off = 0
    for b, s in trunks:
        o_atts.append(_attention(q, k, vt3, tok_off, b, s, t["tq"], t["tm_in"]))
        hfbs.append(_scan(xr, p["conv_w"], p["conv_b"].reshape(1, -1), w_lru, b_lru, lam,
                          tok_off, b, s, t["tt"]))
        tok_off += b * s

    wr_hi = p["w_router"].astype(BF16)
    wr_lo = (p["w_router"] - wr_hi.astype(F32)).astype(BF16)
    x1, h2r, wfull, msel = _mid(o_atts, hfbs, gg, gs, x2d, mod, p["w_pa"].astype(BF16),
                               p["w_pr"].astype(BF16), p["w_out"].astype(BF16), p["ln1_g"], p["ln1_b"],
                               jnp.concatenate([wr_hi, wr_lo], axis=1), p["b_router"], lay, t["tm_mid"])

    tb = t["tb"]
    rank, cnt = _rank(msel, t["tp"])
    counts = cnt[0].astype(jnp.int32)
    padded = (counts + tb - 1) // tb * tb
    pad_end = jnp.cumsum(padded)
    pad_start = pad_end - padded
    n_blocks = nt * TOP_K // tb + N_EXPERTS
    blk0 = jnp.arange(n_blocks, dtype=jnp.int32) * tb
    block_expert = jnp.minimum(jnp.sum(pad_end[None, :] <= blk0[:, None], axis=1), N_EXPERTS - 1).astype(jnp.int32)
    block_valid = jnp.clip(pad_start[block_expert] + counts[block_expert] - blk0, 0, tb).astype(jnp.int32)
    slots, wts = _slots(rank, msel, wfull, pad_start.astype(F32).reshape(1, N_EXPERTS), t["tp"])
    slots_flat = slots[:, :TOP_K].reshape(-1)

    tail_start = jnp.where(padded > 0, pad_end - tb, pad_end[-1] - tb).astype(jnp.int32)
    xs_rows = _dispatch(slots_flat, tail_start, h2r, n_blocks * tb, t["tg"], tb)
    ys = _experts(block_expert, block_valid, xs_rows, p["w1"], p["b1"], p["w2"], p["b2"], tb)
    ys_out = _combine(slots_flat, ys, wts, x1, mod, p["ln2_g"], p["ln2_b"], lay, t["tc"])
    return [y.reshape(b, s, D_MODEL) for y, (b, s) in zip(ys_out, trunks)]


_PARAM_NAMES = ("w_ada", "b_ada", "w_in", "q_gain", "k_gain", "conv_w", "conv_b", "lru_wa", "lru_ba",
                "lru_wx", "lru_bx", "lru_lam", "w_pa", "w_pr", "w_out", "ln1_g", "ln1_b", "w_router",
                "b_router", "w1", "b1", "w2", "b2", "ln2_g", "ln2_b")


def kernel(x_prompt, x_sample, c_prompt, c_sample, w_ada, b_ada, w_in, q_gain, k_gain, conv_w, conv_b, lru_wa, lru_ba, lru_wx, lru_bx, lru_lam, w_pa, w_pr, w_out, ln1_g, ln1_b, w_router, b_router, w1, b1, w2, b2, ln2_g, ln2_b):
    stacked = (w_ada, b_ada, w_in, q_gain, k_gain, conv_w, conv_b, lru_wa, lru_ba, lru_wx, lru_bx,
               lru_lam, w_pa, w_pr, w_out, ln1_g, ln1_b, w_router, b_router, w1, b1, w2, b2, ln2_g, ln2_b)
    xs, cs = [x_prompt, x_sample], [c_prompt, c_sample]
    for layer in range(DEPTH):
        p = {name: arr[layer] for name, arr in zip(_PARAM_NAMES, stacked)}
        xs = _encoder_layer(xs, cs, p)
    return (xs[0], xs[1])
```

```python
import functools
import math

import jax
import jax.numpy as jnp
from jax import lax
from jax.experimental import pallas as pl
from jax.experimental.pallas import tpu as pltpu

F32 = jnp.float32
BF16 = jnp.bfloat16

D_MODEL = 1024
GRID_W = 64
N_HEADS = 8
N_KV_HEADS = 2
HEAD_DIM = 128
GROUPS = N_HEADS // N_KV_HEADS
KV_WIDTH = N_KV_HEADS * HEAD_DIM
ROPE_THETA = 10000.0
RNN_BLOCKS = 8
RNN_BLOCK_W = D_MODEL // RNN_BLOCKS
CONV_W = 4
LRU_C = 8.0
N_EXPERTS = 32
TOP_K = 4
D_FF = D_MODEL
SWIGLU_LIMIT = 7.0
SWIGLU_ALPHA = 1.702
DEPTH = 1
DN_ALPHA = (2 * DEPTH) ** 0.25
LN_EPS = 1e-5
RMS_EPS = 1e-6
N_MOD = 6
IN_WIDTH = D_MODEL + 2 * KV_WIDTH + 2 * D_MODEL + 2 * D_MODEL
_Q0, _K0, _V0, _XR0, _GR0, _GL0 = 0, 1024, 1280, 1536, 2560, 3584

V7X_VMEM_LIMIT_BYTES = 56 * 1024 * 1024
HALO = 16


def _tiles(n_tokens):
    big = n_tokens >= 4096
    return dict(
        tm_in=512 if big else 128,
        tq=256 if big else 128,
        tt=512 if big else 128,
        tm_mid=512 if big else 128,
        tp=512 if big else 128,
        tg=512 if big else 256,
        tb=512 if big else 128,
        tc=256,
    )


def _cparams(sem):
    return pltpu.CompilerParams(dimension_semantics=sem, vmem_limit_bytes=V7X_VMEM_LIMIT_BYTES)


def _const_spec(shape):
    nd = len(shape)
    return pl.BlockSpec(shape, lambda *_: (0,) * nd, pipeline_mode=pl.Buffered(1))


def _layer_norm(x):
    mu = jnp.mean(x, axis=-1, keepdims=True)
    xc = x - mu
    var = jnp.mean(xc * xc, axis=-1, keepdims=True)
    return xc * lax.rsqrt(var + LN_EPS)


def _sigmoid(x):
    return 1.0 / (1.0 + jnp.exp(-x))


ROW_SUB = D_MODEL // 128


def _store_row_tiles(ref, val):
    n = val.shape[0]
    for j in range(ROW_SUB):
        ref[pl.ds(j, n, stride=ROW_SUB), :] = val[:, j * 128:(j + 1) * 128]


def _load_row_tiles(ref, n):
    return jnp.concatenate([ref[pl.ds(j, n, stride=ROW_SUB), :] for j in range(ROW_SUB)], axis=1)


def _ada_kernel(c_ref, w_ref, b_ref, o_ref):
    c = c_ref[...]
    s = c * _sigmoid(c)
    o_ref[...] = jnp.dot(s, w_ref[...], preferred_element_type=F32,
                         precision=lax.Precision.HIGHEST) + b_ref[...]


def _ada_mod(c_all, w_ada, b_ada):
    bp = c_all.shape[0]
    ncol = w_ada.shape[1]
    return pl.pallas_call(
        _ada_kernel,
        grid=(ncol // D_MODEL,),
        in_specs=[pl.BlockSpec((bp, D_MODEL), lambda j: (0, 0)),
                  pl.BlockSpec((D_MODEL, D_MODEL), lambda j: (0, j)),
                  pl.BlockSpec((1, D_MODEL), lambda j: (0, j))],
        out_specs=pl.BlockSpec((bp, D_MODEL), lambda j: (0, j)),
        out_shape=jax.ShapeDtypeStruct((bp, ncol), F32),
        compiler_params=_cparams(("arbitrary",)),
        name="ada_mod",
    )(c_all, w_ada, b_ada.reshape(1, ncol))


class _Layout:
    def __init__(self, trunks):
        self.trunks = tuple(trunks)
        self.n_tokens = sum(b * s for b, s in trunks)
        self.n_seqs = sum(b for b, _ in trunks)
        self.max_seq = max(s for _, s in trunks)

    def seq_and_pos(self, t0):
        seq = jnp.int32(0)
        pos = jnp.int32(0)
        tok_off, seq_off = 0, 0
        for b, s in self.trunks:
            inside = (t0 >= tok_off) & (t0 < tok_off + b * s)
            rel = jnp.maximum(t0 - tok_off, 0)
            seq = jnp.where(inside, seq_off + rel // s, seq)
            pos = jnp.where(inside, rel % s, pos)
            tok_off += b * s
            seq_off += b
        return seq, pos

    def trunk_specs(self, tm, make_spec):
        specs, ends, t0 = [], [], 0
        for b, s in self.trunks:
            n_t = b * s // tm
            specs.append(make_spec(functools.partial(_clamped_local, t0=t0, n_t=n_t)))
            t0 += n_t
            ends.append(t0)
        return specs, tuple(ends)


def _clamped_local(i, *, t0, n_t):
    return jnp.clip(i - t0, 0, n_t - 1)


def _owner_value(loads, i, tile_ends):
    val = loads[-1]()
    for j in range(len(loads) - 2, -1, -1):
        val = jnp.where(i < tile_ends[j], loads[j](), val)
    return val


def _rope_tables(max_seq):
    t = jnp.arange(max_seq)
    rows = (t // GRID_W).astype(F32)
    cols = (t % GRID_W).astype(F32)
    axis_dim = HEAD_DIM // 2
    inv = ROPE_THETA ** (-jnp.arange(0, axis_dim, 2, dtype=F32) / axis_dim)
    ar, ac = rows[:, None] * inv, cols[:, None] * inv
    cr, sr, cc, sc = jnp.cos(ar), jnp.sin(ar), jnp.cos(ac), jnp.sin(ac)
    z = jnp.zeros_like(sr)
    cos_t = jnp.concatenate([cr, cr, cc, cc], axis=-1)
    up_t = jnp.concatenate([-sr, z, -sc, z], axis=-1)
    dn_t = jnp.concatenate([z, sr, z, sc], axis=-1)
    return cos_t, up_t, dn_t


def _inproj_kernel(*refs, tile_ends):
    n_tr = len(tile_ends)
    x_refs = refs[:n_tr]
    (mod_ref, w_ref, qg_ref, kg_ref, cos_ref, up_ref, dn_ref,
     q_ref, k_ref, vt_ref, xr_ref, gg_ref, gs_ref) = refs[n_tr:]
    x = _owner_value([lambda r=r: r[...] for r in x_refs], pl.program_id(0), tile_ends)
    sh1 = mod_ref[0, 0:1, :]
    sc1 = mod_ref[0, 1:2, :]
    h = (_layer_norm(x) * (1.0 + sc1) + sh1).astype(BF16)
    cos_t, up_t, dn_t = cos_ref[...], up_ref[...], dn_ref[...]

    def proj(c0, width):
        return jnp.dot(h, w_ref[:, c0:c0 + width], preferred_element_type=F32)

    def norm_rope(z, gain):
        ms = jnp.mean(z * z, axis=-1, keepdims=True)
        y = z * lax.rsqrt(ms + RMS_EPS) * gain
        return (y * cos_t + pltpu.roll(y, HEAD_DIM - 32, 1) * up_t
                + pltpu.roll(y, 32, 1) * dn_t)

    zq = proj(_Q0, D_MODEL)
    qg = qg_ref[...] * (HEAD_DIM ** -0.5 * math.log2(math.e))
    for hd in range(N_HEADS):
        sl = slice(hd * HEAD_DIM, (hd + 1) * HEAD_DIM)
        q_ref[:, sl] = norm_rope(zq[:, sl], qg).astype(BF16)
    zk = proj(_K0, KV_WIDTH)
    kg = kg_ref[...]
    for hd in range(N_KV_HEADS):
        sl = slice(hd * HEAD_DIM, (hd + 1) * HEAD_DIM)
        k_ref[:, sl] = norm_rope(zk[:, sl], kg).astype(BF16)
    zv = proj(_V0, KV_WIDTH)
    vt_ref[0] = zv.T.astype(BF16)
    xr_ref[...] = proj(_XR0, D_MODEL).astype(BF16)
    gg_ref[...] = jax.nn.gelu(proj(_GR0, D_MODEL), approximate=True).astype(BF16)
    gs_ref[...] = _sigmoid(proj(_GL0, 2 * D_MODEL)).astype(BF16)


def _inproj(xs, mod, w_in, q_gain, k_gain, tables, lay, tm):
    nt = lay.n_tokens
    cos_t, up_t, dn_t = tables

    def seq_map(i):
        return (lay.seq_and_pos(i * tm)[0], 0, 0)

    def pos_map(i):
        return (lay.seq_and_pos(i * tm)[1] // tm, 0)

    tok = lambda w: pl.BlockSpec((tm, w), lambda i: (i, 0))
    rope = pl.BlockSpec((tm, HEAD_DIM), pos_map)
    x_specs, tile_ends = lay.trunk_specs(
        tm, lambda local: pl.BlockSpec((tm, D_MODEL), lambda i: (local(i), 0)))
    return pl.pallas_call(
        functools.partial(_inproj_kernel, tile_ends=tile_ends),
        grid=(nt // tm,),
        in_specs=x_specs + [
                  pl.BlockSpec((1, N_MOD, D_MODEL), seq_map),
                  _const_spec((D_MODEL, IN_WIDTH)),
                  _const_spec((1, HEAD_DIM)), _const_spec((1, HEAD_DIM)),
                  rope, rope, rope],
        out_specs=[tok(D_MODEL), tok(KV_WIDTH),
                   pl.BlockSpec((1, KV_WIDTH, tm), lambda i: (i, 0, 0)),
                   tok(D_MODEL), tok(D_MODEL), tok(2 * D_MODEL)],
        out_shape=[jax.ShapeDtypeStruct((nt, D_MODEL), BF16),
                   jax.ShapeDtypeStruct((nt, KV_WIDTH), BF16),
                   jax.ShapeDtypeStruct((nt // tm, KV_WIDTH, tm), BF16),
                   jax.ShapeDtypeStruct((nt, D_MODEL), BF16),
                   jax.ShapeDtypeStruct((nt, D_MODEL), BF16),
                   jax.ShapeDtypeStruct((nt, 2 * D_MODEL), BF16)],
        compiler_params=_cparams(("arbitrary",)),
        name="in_proj",
    )(*xs, mod, w_in, q_gain.reshape(1, HEAD_DIM), k_gain.reshape(1, HEAD_DIM), cos_t, up_t, dn_t)


def _attn_kernel(q_ref, k_ref, vt_ref, o_ref, qt_scr, s_scr, m_scr, l_scr, acc_scr, *,
                 tq, tk, n_kv, group):
    for g in range(GROUPS):
        qg = q_ref[:, g * HEAD_DIM:(g + 1) * HEAD_DIM].astype(F32)
        qt_scr[:, g * tq:(g + 1) * tq] = qg.T.astype(BF16)
    m_scr[...] = jnp.full(m_scr.shape, -jnp.inf, F32)
    l_scr[...] = jnp.zeros(l_scr.shape, F32)
    acc_scr[...] = jnp.zeros(acc_scr.shape, F32)

    def scores(j, slot):
        kt = k_ref[pl.ds(pl.multiple_of(j * tk, tk), tk), :]
        s_scr[slot] = jnp.dot(kt, qt_scr[...], preferred_element_type=F32)

    def accumulate(j, slot):
        s = s_scr[slot]
        m_old = m_scr[...]
        m_new = jnp.maximum(m_old, jnp.max(s, axis=0, keepdims=True))
        alpha = jnp.exp2(m_old - m_new)
        p = jnp.exp2(s - m_new)
        l_scr[...] = alpha * l_scr[...] + jnp.sum(p, axis=0, keepdims=True)
        pv = jnp.dot(vt_ref[j], p.astype(BF16), preferred_element_type=F32)
        acc_scr[...] = alpha * acc_scr[...] + pv
        m_scr[...] = m_new

    scores(0, 0)

    def body(i, carry):
        j = group * i
        for u in range(group):
            scores(jnp.minimum(j + u + 1, n_kv - 1), (u + 1) % 2)
            accumulate(j + u, u % 2)
        return carry

    lax.fori_loop(0, n_kv // group, body, 0)
    out = acc_scr[...] / l_scr[...]
    for g in range(GROUPS):
        o_ref[:, g * HEAD_DIM:(g + 1) * HEAD_DIM] = out[:, g * tq:(g + 1) * tq].T.astype(BF16)


def _attention(q, k, vt3, tok_off, batch, seq, tq, tk):
    n_kv = seq // tk
    group = next((g for g in (8, 4) if n_kv % g == 0 and n_kv // g >= 2), 2)
    assert n_kv % group == 0
    qrow0 = tok_off // tq
    srow0 = tok_off // seq
    gw = GROUPS * HEAD_DIM
    n_q = seq // tq
    return pl.pallas_call(
        functools.partial(_attn_kernel, tq=tq, tk=tk, n_kv=n_kv, group=group),
        grid=(batch, N_KV_HEADS, n_q),
        in_specs=[pl.BlockSpec((tq, gw), lambda b, h, i: (qrow0 + b * n_q + i, h)),
                  pl.BlockSpec((seq, HEAD_DIM), lambda b, h, i: (srow0 + b, h)),
                  pl.BlockSpec((n_kv, HEAD_DIM, tk), lambda b, h, i: (srow0 + b, h, 0))],
        out_specs=pl.BlockSpec((tq, gw), lambda b, h, i: (b * n_q + i, h)),
        out_shape=jax.ShapeDtypeStruct((batch * seq, D_MODEL), BF16),
        scratch_shapes=[pltpu.VMEM((HEAD_DIM, GROUPS * tq), BF16),
                        pltpu.VMEM((2, tk, GROUPS * tq), F32),
                        pltpu.VMEM((1, GROUPS * tq), F32),
                        pltpu.VMEM((1, GROUPS * tq), F32),
                        pltpu.VMEM((HEAD_DIM, GROUPS * tq), F32)],
        compiler_params=_cparams(("arbitrary", "arbitrary", "arbitrary")),
        name="attention",
    )(q, k, vt3)


def _scan_kernel(cur_ref, prev_ref, next_ref, cw_ref, cb_ref, w_ref, b_ref, lam_ref, o_ref,
                 xc_scr, a_scr, u_scr, h_scr, carry_scr, *, tt, n_chunks):
    d = pl.program_id(1)
    c = pl.program_id(2)
    chunk = jnp.where(d == 0, c, n_chunks - 1 - c)

    @pl.when(c == 0)
    def _():
        carry_scr[...] = jnp.zeros(carry_scr.shape, F32)

    keep_prev = jnp.where(chunk == 0, 0.0, 1.0)
    keep_next = jnp.where(chunk == n_chunks - 1, 0.0, 1.0)
    cur = cur_ref[...].astype(F32)
    taps = [cw_ref[j:j + 1, :] for j in range(CONV_W)]
    xc_scr[...] = (cb_ref[...] + taps[2] * cur + taps[1] * pltpu.roll(cur, 1, 0)
                   + taps[0] * pltpu.roll(cur, 2, 0) + taps[3] * pltpu.roll(cur, tt - 1, 0))
    head = jnp.concatenate([prev_ref[...].astype(F32)[HALO - 8:HALO] * keep_prev, cur[0:16]], axis=0)
    tail = jnp.concatenate([cur[tt - 16:tt], next_ref[...].astype(F32)[0:8] * keep_next], axis=0)
    xc_scr[0:8, :] = cb_ref[...] + sum(taps[j] * head[6 + j:14 + j] for j in range(CONV_W))
    xc_scr[tt - 8:tt, :] = cb_ref[...] + sum(taps[j] * tail[6 + j:14 + j] for j in range(CONV_W))
    xc = xc_scr[...]
    xcb = xc.astype(BF16)

    lam = lam_ref[0]
    y = jnp.exp(-jnp.abs(lam))
    w1p = 1.0 + y
    log1p_y = jnp.where(w1p == 1.0, y, jnp.log(w1p) * y / jnp.where(w1p == 1.0, 1.0, w1p - 1.0))
    neg_c_sp = (-LRU_C * math.log2(math.e)) * (jnp.maximum(-lam, 0.0) + log1p_y)

    for n in range(RNN_BLOCKS):
        sl = slice(n * RNN_BLOCK_W, (n + 1) * RNN_BLOCK_W)
        pre = jnp.dot(xcb[:, sl], w_ref[0, n], preferred_element_type=F32)
        r = _sigmoid(pre[:, :RNN_BLOCK_W] + b_ref[0, 0:1, sl])
        i = _sigmoid(pre[:, RNN_BLOCK_W:] + b_ref[0, 1:2, sl])
        a = jnp.exp2(r * neg_c_sp[:, sl])
        a_scr[:, sl] = a
        u_scr[:, sl] = jnp.sqrt(1.0 - a * a) * (i * xc[:, sl])

    def step(t, h):
        row = jnp.where(d == 0, t, tt - 1 - t)
        h = a_scr[pl.ds(row, 1), :] * h + u_scr[pl.ds(row, 1), :]
        h_scr[pl.ds(row, 1), :] = h
        return h

    carry_scr[...] = lax.fori_loop(0, tt, step, carry_scr[...], unroll=8)
    o_ref[0] = h_scr[...].astype(BF16)


def _scan(xr, conv_w, conv_b, w_lru, b_lru, lam, tok_off, batch, seq, tt):
    nt = xr.shape[0]
    n_chunks = seq // tt
    row0 = tok_off // tt
    hrow0 = tok_off // HALO
    hpc = tt // HALO
    n_halo = nt // HALO

    def chunk_of(d, c):
        return jnp.where(d == 0, c, n_chunks - 1 - c)

    def prev_map(b, d, c):
        return (jnp.maximum(hrow0 + (b * n_chunks + chunk_of(d, c)) * hpc - 1, 0), 0)

    def next_map(b, d, c):
        return (jnp.minimum(hrow0 + (b * n_chunks + chunk_of(d, c) + 1) * hpc, n_halo - 1), 0)

    return pl.pallas_call(
        functools.partial(_scan_kernel, tt=tt, n_chunks=n_chunks),
        grid=(batch, 2, n_chunks),
        in_specs=[pl.BlockSpec((tt, D_MODEL), lambda b, d, c: (row0 + b * n_chunks + chunk_of(d, c), 0)),
                  pl.BlockSpec((HALO, D_MODEL), prev_map),
                  pl.BlockSpec((HALO, D_MODEL), next_map),
                  pl.BlockSpec((CONV_W, D_MODEL), lambda b, d, c: (0, 0)),
                  pl.BlockSpec((1, D_MODEL), lambda b, d, c: (0, 0)),
                  pl.BlockSpec((1, RNN_BLOCKS, RNN_BLOCK_W, 2 * RNN_BLOCK_W), lambda b, d, c: (d, 0, 0, 0)),
                  pl.BlockSpec((1, 2, D_MODEL), lambda b, d, c: (d, 0, 0)),
                  pl.BlockSpec((1, 1, D_MODEL), lambda b, d, c: (d, 0, 0))],
        out_specs=pl.BlockSpec((1, tt, D_MODEL), lambda b, d, c: (d, b * n_chunks + chunk_of(d, c), 0)),
        out_shape=jax.ShapeDtypeStruct((2, batch * seq, D_MODEL), BF16),
        scratch_shapes=[pltpu.VMEM((tt, D_MODEL), F32),
                        pltpu.VMEM((tt, D_MODEL), F32),
                        pltpu.VMEM((tt, D_MODEL), F32),
                        pltpu.VMEM((tt, D_MODEL), F32),
                        pltpu.VMEM((1, D_MODEL), F32)],
        compiler_params=_cparams(("arbitrary", "arbitrary", "arbitrary")),
        name="lru_scan",
    )(xr, xr, xr, conv_w, conv_b, w_lru, b_lru, lam)


def _mid_kernel(*refs, tile_ends):
    n_tr = len(tile_ends)
    oa_refs = refs[:n_tr]
    h_refs = refs[n_tr:3 * n_tr]
    x_refs = refs[3 * n_tr:4 * n_tr]
    (gg_ref, gs_ref, mod_ref, wpa_ref, wpr_ref, wo_ref, g1_ref, b1_ref, wr_ref, br_ref,
     x1_ref, h2_ref, wf_ref, ms_ref) = refs[4 * n_tr:]
    i = pl.program_id(0)
    oa = _owner_value([lambda r=r: r[...] for r in oa_refs], i, tile_ends)
    hsum = _owner_value([lambda f=h_refs[2 * j], b=h_refs[2 * j + 1]: f[0].astype(F32) + b[0].astype(F32)
                         for j in range(n_tr)], i, tile_ends)
    x_in = _owner_value([lambda r=r: r[...] for r in x_refs], i, tile_ends)
    o_att = jnp.dot(oa, wpa_ref[...], preferred_element_type=F32)
    rec = hsum.astype(BF16) * gg_ref[...]
    o_rec = jnp.dot(rec, wpr_ref[...], preferred_element_type=F32)
    gs = gs_ref[...].astype(F32)
    merged = gs[:, :D_MODEL] * o_att + gs[:, D_MODEL:] * o_rec
    mix = jnp.dot(merged.astype(BF16), wo_ref[...], preferred_element_type=F32)
    g1 = mod_ref[0, 2:3, :]
    sh2 = mod_ref[0, 3:4, :]
    sc2 = mod_ref[0, 4:5, :]
    x1 = _layer_norm(DN_ALPHA * x_in + (1.0 + g1) * mix) * g1_ref[...] + b1_ref[...]
    x1_ref[...] = x1
    h2 = _layer_norm(x1) * (1.0 + sc2) + sh2
    _store_row_tiles(h2_ref, h2)
    tm = h2.shape[0]
    h2_hi = h2.astype(BF16)
    h2_lo = (h2 - h2_hi.astype(F32)).astype(BF16)
    cross = jnp.dot(jnp.concatenate([h2_hi, h2_lo], axis=0), wr_ref[...], preferred_element_type=F32)
    logits = ((cross[:tm, :N_EXPERTS] + cross[:tm, N_EXPERTS:])
              + (cross[tm:, :N_EXPERTS] + cross[tm:, N_EXPERTS:])) + br_ref[...]
    lane = lax.broadcasted_iota(jnp.int32, logits.shape, 1).astype(F32)
    rem = logits
    sel = jnp.zeros(logits.shape, F32)
    top = None
    denom = None
    for kk in range(TOP_K):
        m = jnp.max(rem, axis=-1, keepdims=True)
        idx = jnp.min(jnp.where(rem == m, lane, float(N_EXPERTS)), axis=-1, keepdims=True)
        pick = lane == idx
        sel = jnp.where(pick, 1.0, sel)
        rem = jnp.where(pick, -jnp.inf, rem)
        if kk == 0:
            top = m
            denom = jnp.ones_like(m)
        else:
            denom = denom + jnp.exp(m - top)
    wf_ref[...] = jnp.where(sel > 0.0, jnp.exp(logits - top) / denom, 0.0)
    ms_ref[...] = sel


def _mid(o_atts, hfbs, gg, gs, xs, mod, w_pa, w_pr, w_out, ln1_g, ln1_b, w_router, b_router, lay, tm):
    nt = lay.n_tokens

    def seq_map(i):
        return (lay.seq_and_pos(i * tm)[0], 0, 0)

    tok = lambda w: pl.BlockSpec((tm, w), lambda i: (i, 0))
    vec = lambda w: _const_spec((1, w))
    row_specs, tile_ends = lay.trunk_specs(
        tm, lambda local: pl.BlockSpec((tm, D_MODEL), lambda i: (local(i), 0)))
    h_pairs, _ = lay.trunk_specs(
        tm, lambda local: [pl.BlockSpec((1, tm, D_MODEL), lambda i, d=d: (d, local(i), 0)) for d in range(2)])
    h_specs = [spec for pair in h_pairs for spec in pair]
    h_args = [hfb for hfb in hfbs for _ in range(2)]
    return pl.pallas_call(
        functools.partial(_mid_kernel, tile_ends=tile_ends),
        grid=(nt // tm,),
        in_specs=row_specs + h_specs + row_specs + [
                  tok(D_MODEL), tok(2 * D_MODEL),
                  pl.BlockSpec((1, N_MOD, D_MODEL), seq_map),
                  _const_spec((D_MODEL, D_MODEL)), _const_spec((D_MODEL, D_MODEL)),
                  _const_spec((D_MODEL, D_MODEL)),
                  vec(D_MODEL), vec(D_MODEL),
                  _const_spec((D_MODEL, 2 * N_EXPERTS)), vec(N_EXPERTS)],
        out_specs=[tok(D_MODEL), pl.BlockSpec((ROW_SUB * tm, 128), lambda i: (i, 0)),
                   tok(N_EXPERTS), tok(N_EXPERTS)],
        out_shape=[jax.ShapeDtypeStruct((nt, D_MODEL), F32),
                   jax.ShapeDtypeStruct((ROW_SUB * nt, 128), F32),
                   jax.ShapeDtypeStruct((nt, N_EXPERTS), F32),
                   jax.ShapeDtypeStruct((nt, N_EXPERTS), F32)],
        compiler_params=_cparams(("arbitrary",)),
        name="merge_router",
    )(*o_atts, *h_args, *xs, gg, gs, mod, w_pa, w_pr, w_out, ln1_g.reshape(1, -1), ln1_b.reshape(1, -1),
      w_router, b_router.reshape(1, -1))


def _rank_kernel(ms_ref, rank_ref, cnt_ref, tri_scr, run_scr, *, tp):
    @pl.when(pl.program_id(0) == 0)
    def _():
        r = lax.broadcasted_iota(jnp.int32, (tp, tp), 0)
        c = lax.broadcasted_iota(jnp.int32, (tp, tp), 1)
        tri_scr[...] = jnp.where(c < r, 1.0, 0.0).astype(BF16)
        run_scr[...] = jnp.zeros(run_scr.shape, F32)

    ms = ms_ref[...]
    before = jnp.dot(tri_scr[...], ms.astype(BF16), preferred_element_type=F32)
    rank_ref[...] = before + run_scr[...]
    run_scr[...] = run_scr[...] + jnp.sum(ms, axis=0, keepdims=True)
    cnt_ref[...] = jnp.broadcast_to(run_scr[...], cnt_ref.shape)


def _rank(msel, tp):
    nt = msel.shape[0]
    return pl.pallas_call(
        functools.partial(_rank_kernel, tp=tp),
        grid=(nt // tp,),
        in_specs=[pl.BlockSpec((tp, N_EXPERTS), lambda i: (i, 0))],
        out_specs=[pl.BlockSpec((tp, N_EXPERTS), lambda i: (i, 0)),
                   pl.BlockSpec((8, N_EXPERTS), lambda i: (0, 0))],
        out_shape=[jax.ShapeDtypeStruct((nt, N_EXPERTS), F32),
                   jax.ShapeDtypeStruct((8, N_EXPERTS), F32)],
        scratch_shapes=[pltpu.VMEM((tp, tp), BF16), pltpu.VMEM((1, N_EXPERTS), F32)],
        compiler_params=_cparams(("arbitrary",)),
        name="route_rank",
    )(msel)


def _slots_kernel(rank_ref, ms_ref, wf_ref, start_ref, slot_ref, wt_ref):
    ms = ms_ref[...]
    wf = wf_ref[...]
    slot_full = rank_ref[...] + start_ref[...]
    lane = lax.broadcasted_iota(jnp.int32, ms.shape, 1).astype(F32)
    out_lane = lax.broadcasted_iota(jnp.int32, slot_ref.shape, 1)
    slots = jnp.zeros(slot_ref.shape, F32)
    wts = jnp.zeros(wt_ref.shape, F32)
    rem = ms
    for kk in range(TOP_K):
        idx = jnp.min(jnp.where(rem > 0.0, lane, float(2 * N_EXPERTS)), axis=-1, keepdims=True)
        pick = lane == idx
        s_k = jnp.sum(jnp.where(pick, slot_full, 0.0), axis=-1, keepdims=True)
        w_k = jnp.sum(jnp.where(pick, wf, 0.0), axis=-1, keepdims=True)
        rem = jnp.where(pick, 0.0, rem)
        slots = jnp.where(out_lane == kk, s_k, slots)
        wts = jnp.where(out_lane == kk, w_k, wts)
    slot_ref[...] = slots.astype(jnp.int32)
    wt_ref[...] = wts


def _slots(rank, msel, wfull, pad_start, tp):
    nt = msel.shape[0]
    tok = pl.BlockSpec((tp, N_EXPERTS), lambda i: (i, 0))
    out = pl.BlockSpec((tp, 128), lambda i: (i, 0))
    return pl.pallas_call(
        _slots_kernel,
        grid=(nt // tp,),
        in_specs=[tok, tok, tok, pl.BlockSpec((1, N_EXPERTS), lambda i: (0, 0))],
        out_specs=[out, out],
        out_shape=[jax.ShapeDtypeStruct((nt, 128), jnp.int32),
                   jax.ShapeDtypeStruct((nt, 128), F32)],
        compiler_params=_cparams(("arbitrary",)),
        name="route_slots",
    )(rank, msel, wfull, pad_start)


def _row_view(ref, row):
    return ref.at[pl.ds(pl.multiple_of(row * ROW_SUB, ROW_SUB), ROW_SUB), :]


def _dispatch_kernel(slot_ref, tail_ref, h_ref, xs_ref, zero_scr, sem, zsem, *, tg, tb):
    @pl.when(pl.program_id(0) == 0)
    def _():
        zero_scr[...] = jnp.zeros(zero_scr.shape, F32)
        for e in range(N_EXPERTS):
            tail = pl.multiple_of(tail_ref[e] * ROW_SUB, tb * ROW_SUB)
            pltpu.make_async_copy(zero_scr, xs_ref.at[pl.ds(tail, tb * ROW_SUB), :], zsem).start()
        for e in range(N_EXPERTS):
            pltpu.make_async_copy(zero_scr, xs_ref.at[pl.ds(0, tb * ROW_SUB), :], zsem).wait()

    def row_copy(t, slot, kk):
        return pltpu.make_async_copy(_row_view(h_ref, t), _row_view(xs_ref, slot), sem)

    def issue(t, carry):
        for kk in range(TOP_K):
            row_copy(t, slot_ref[t * TOP_K + kk], kk).start(priority=kk % 2)
        return carry

    lax.fori_loop(0, tg, issue, 0, unroll=2)

    for kk in range(TOP_K):
        pltpu.make_async_copy(h_ref, xs_ref.at[pl.ds(0, tg * ROW_SUB), :], sem).wait()


def _dispatch(slots_flat, tail_start, h2r, cap, tg, tb):
    nt = h2r.shape[0] // ROW_SUB
    return pl.pallas_call(
        functools.partial(_dispatch_kernel, tg=tg, tb=tb),
        grid=(nt // tg,),
        in_specs=[pl.BlockSpec((tg * TOP_K,), lambda i: (i,), memory_space=pltpu.SMEM),
                  pl.BlockSpec(memory_space=pltpu.SMEM),
                  pl.BlockSpec((tg * ROW_SUB, 128), lambda i: (i, 0))],
        out_specs=pl.BlockSpec(memory_space=pl.ANY),
        out_shape=jax.ShapeDtypeStruct((cap * ROW_SUB, 128), F32),
        scratch_shapes=[pltpu.VMEM((tb * ROW_SUB, 128), F32), pltpu.SemaphoreType.DMA(()),
                        pltpu.SemaphoreType.DMA(())],
        compiler_params=_cparams(("arbitrary",)),
        name="moe_dispatch",
    )(slots_flat, tail_start, h2r)


def _expert_kernel(be_ref, bv_ref, xs_ref, w1_ref, b1_ref, w2_ref, b2_ref, ys_ref, w1_scr, w2_scr, *, tb):
    i = pl.program_id(0)
    valid = bv_ref[i]

    @pl.when((i == 0) | (be_ref[i] != be_ref[jnp.maximum(i - 1, 0)]))
    def _():
        w1_scr[...] = w1_ref[0].astype(BF16)
        w2_scr[...] = w2_ref[0].astype(BF16)

    @pl.when(valid > 0)
    def _():
        xb = _load_row_tiles(xs_ref, tb).astype(BF16)
        gu = jnp.dot(xb, w1_scr[...], preferred_element_type=F32) + b1_ref[0]
        glu = jnp.minimum(gu[:, :D_FF], SWIGLU_LIMIT)
        lin = jnp.clip(gu[:, D_FF:], -SWIGLU_LIMIT, SWIGLU_LIMIT)
        act = (lin + 1.0) * glu * _sigmoid(SWIGLU_ALPHA * glu)
        y = jnp.dot(act.astype(BF16), w2_scr[...], preferred_element_type=F32) + b2_ref[0]
        _store_row_tiles(ys_ref, y)

    @pl.when(valid <= 0)
    def _():
        ys_ref[...] = jnp.zeros(ys_ref.shape, F32)


def _experts(block_expert, block_valid, xs, w1, b1, w2, b2, tb):
    cap = xs.shape[0] // ROW_SUB
    rows = pl.BlockSpec((tb * ROW_SUB, 128), lambda i, be, bv: (i, 0))
    grid_spec = pltpu.PrefetchScalarGridSpec(
        num_scalar_prefetch=2,
        grid=(cap // tb,),
        in_specs=[rows,
                  pl.BlockSpec((1, D_MODEL, 2 * D_FF), lambda i, be, bv: (be[i], 0, 0)),
                  pl.BlockSpec((1, 1, 2 * D_FF), lambda i, be, bv: (be[i], 0, 0)),
                  pl.BlockSpec((1, D_FF, D_MODEL), lambda i, be, bv: (be[i], 0, 0)),
                  pl.BlockSpec((1, 1, D_MODEL), lambda i, be, bv: (be[i], 0, 0))],
        out_specs=rows,
        scratch_shapes=[pltpu.VMEM((D_MODEL, 2 * D_FF), BF16), pltpu.VMEM((D_FF, D_MODEL), BF16)],
    )
    return pl.pallas_call(
        functools.partial(_expert_kernel, tb=tb),
        grid_spec=grid_spec,
        out_shape=jax.ShapeDtypeStruct((cap * ROW_SUB, 128), F32),
        compiler_params=_cparams(("arbitrary",)),
        name="moe_experts",
    )(block_expert, block_valid, xs, w1, b1.reshape(N_EXPERTS, 1, -1), w2, b2.reshape(N_EXPERTS, 1, -1))


def _combine_kernel(slot_ref, next_slot_ref, ys_ref, wt_ref, x1_ref, mod_ref, g_ref, b_ref, *rest,
                    tc, tile_ends):
    y_refs = rest[:len(tile_ends)]
    buf, sem = rest[len(tile_ends):]
    i = pl.program_id(0)
    cur = i % 2

    def row_copy(slots, t, kk, half):
        return pltpu.make_async_copy(_row_view(ys_ref, slots[t * TOP_K + kk]),
                                     _row_view(buf.at[half, kk], t), sem.at[half])

    def gather(slots, half):
        def issue(t, carry):
            for kk in range(TOP_K):
                row_copy(slots, t, kk, half).start(priority=kk % 2)
            return carry
        lax.fori_loop(0, tc, issue, 0, unroll=8)

    @pl.when(i == 0)
    def _():
        gather(slot_ref, 0)

    @pl.when(i + 1 < pl.num_programs(0))
    def _():
        gather(next_slot_ref, 1 - cur)

    for kk in range(TOP_K):
        pltpu.make_async_copy(ys_ref.at[pl.ds(0, tc * ROW_SUB), :], buf.at[cur, kk], sem.at[cur]).wait()
    wt = wt_ref[...]
    ff = sum(wt[:, kk:kk + 1] * _load_row_tiles(buf.at[cur, kk], tc) for kk in range(TOP_K))
    g2 = mod_ref[0, 5:6, :]
    y = _layer_norm(DN_ALPHA * x1_ref[...] + (1.0 + g2) * ff) * g_ref[...] + b_ref[...]
    i = pl.program_id(0)
    start = 0
    for y_ref, end in zip(y_refs, tile_ends):
        @pl.when((i >= start) & (i < end))
        def _(y_ref=y_ref):
            y_ref[...] = y
        start = end


def _combine(slots_flat, ys, wts, x1, mod, ln2_g, ln2_b, lay, tc):
    nt = lay.n_tokens

    def seq_map(i):
        return (lay.seq_and_pos(i * tc)[0], 0, 0)

    out_specs, tile_ends = lay.trunk_specs(
        tc, lambda local: pl.BlockSpec((tc, D_MODEL), lambda i: (local(i), 0)))
    n_steps = nt // tc
    return pl.pallas_call(
        functools.partial(_combine_kernel, tc=tc, tile_ends=tile_ends),
        grid=(n_steps,),
        in_specs=[pl.BlockSpec((tc * TOP_K,), lambda i: (i,), memory_space=pltpu.SMEM),
                  pl.BlockSpec((tc * TOP_K,), lambda i: (jnp.minimum(i + 1, n_steps - 1),),
                               memory_space=pltpu.SMEM),
                  pl.BlockSpec(memory_space=pl.ANY),
                  pl.BlockSpec((tc, 128), lambda i: (i, 0)),
                  pl.BlockSpec((tc, D_MODEL), lambda i: (i, 0)),
                  pl.BlockSpec((1, N_MOD, D_MODEL), seq_map),
                  _const_spec((1, D_MODEL)), _const_spec((1, D_MODEL))],
        out_specs=out_specs,
        out_shape=[jax.ShapeDtypeStruct((b * s, D_MODEL), F32) for b, s in lay.trunks],
        scratch_shapes=[pltpu.VMEM((2, TOP_K, tc * ROW_SUB, 128), F32), pltpu.SemaphoreType.DMA((2,))],
        compiler_params=_cparams(("arbitrary",)),
        name="moe_combine",
    )(slots_flat, slots_flat, ys, wts, x1, mod, ln2_g.reshape(1, -1), ln2_b.reshape(1, -1))


def _encoder_layer(xs, cs, p):
    trunks = [(x.shape[0], x.shape[1]) for x in xs]
    lay = _Layout(trunks)
    nt = lay.n_tokens
    t = _tiles(nt)

    x2d = [xi.reshape(-1, D_MODEL) for xi in xs]
    c = jnp.concatenate(cs, axis=0)
    bp = -(-lay.n_seqs // 8) * 8
    c = jnp.pad(c, ((0, bp - lay.n_seqs), (0, 0)))
    mod = _ada_mod(c, p["w_ada"], p["b_ada"]).reshape(bp, N_MOD, D_MODEL)

    q, k, vt3, xr, gg, gs = _inproj(x2d, mod, p["w_in"].astype(BF16), p["q_gain"], p["k_gain"],
                                    _rope_tables(lay.max_seq), lay, t["tm_in"])

    w_lru = jnp.concatenate([p["lru_wa"], p["lru_wx"]], axis=-1).astype(BF16)
    b_lru = jnp.stack([p["lru_ba"], p["lru_bx"]], axis=1)
    lam = p["lru_lam"].reshape(2, 1, D_MODEL)
    o_atts, hfbs = [], []
    tok_off = 0
    for b, s in trunks:
        o_atts.append(_attention(q, k, vt3, tok_off, b, s, t["tq"], t["tm_in"]))
        hfbs.append(_scan(xr, p["conv_w"], p["conv_b"].reshape(1, -1), w_lru, b_lru, lam,
                          tok_off, b, s, t["tt"]))
        tok_off += b * s

    wr_hi = p["w_router"].astype(BF16)
    wr_lo = (p["w_router"] - wr_hi.astype(F32)).astype(BF16)
    x1, h2r, wfull, msel = _mid(o_atts, hfbs, gg, gs, x2d, mod, p["w_pa"].astype(BF16),
                               p["w_pr"].astype(BF16), p["w_out"].astype(BF16), p["ln1_g"], p["ln1_b"],
                               jnp.concatenate([wr_hi, wr_lo], axis=1), p["b_router"], lay, t["tm_mid"])

    tb = t["tb"]
    rank, cnt = _rank(msel, t["tp"])
    counts = cnt[0].astype(jnp.int32)
    padded = (counts + tb - 1) // tb * tb
    pad_end = jnp.cumsum(padded)
    pad_start = pad_end - padded
    n_blocks = nt * TOP_K // tb + N_EXPERTS
    blk0 = jnp.arange(n_blocks, dtype=jnp.int32) * tb
    block_expert = jnp.minimum(jnp.sum(pad_end[None, :] <= blk0[:, None], axis=1), N_EXPERTS - 1).astype(jnp.int32)
    block_valid = jnp.clip(pad_start[block_expert] + counts[block_expert] - blk0, 0, tb).astype(jnp.int32)
    slots, wts = _slots(rank, msel, wfull, pad_start.astype(F32).reshape(1, N_EXPERTS), t["tp"])
    slots_flat = slots[:, :TOP_K].reshape(-1)

    tail_start = jnp.where(padded > 0, pad_end - tb, pad_end[-1] - tb).astype(jnp.int32)
    xs_rows = _dispatch(slots_flat, tail_start, h2r, n_blocks * tb, t["tg"], tb)
    ys = _experts(block_expert, block_valid, xs_rows, p["w1"], p["b1"], p["w2"], p["b2"], tb)
    ys_out = _combine(slots_flat, ys, wts, x1, mod, p["ln2_g"], p["ln2_b"], lay, t["tc"])
    return [y.reshape(b, s, D_MODEL) for y, (b, s) in zip(ys_out, trunks)]


_PARAM_NAMES = ("w_ada", "b_ada", "w_in", "q_gain", "k_gain", "conv_w", "conv_b", "lru_wa", "lru_ba",
                "lru_wx", "lru_bx", "lru_lam", "w_pa", "w_pr", "w_out", "ln1_g", "ln1_b", "w_router",
                "b_router", "w1", "b1", "w2", "b2", "ln2_g", "ln2_b")


def kernel(x_prompt, x_sample, c_prompt, c_sample, w_ada, b_ada, w_in, q_gain, k_gain, conv_w, conv_b, lru_wa, lru_ba, lru_wx, lru_bx, lru_lam, w_pa, w_pr, w_out, ln1_g, ln1_b, w_router, b_router, w1, b1, w2, b2, ln2_g, ln2_b):
    stacked = (w_ada, b_ada, w_in, q_gain, k_gain, conv_w, conv_b, lru_wa, lru_ba, lru_wx, lru_bx,
               lru_lam, w_pa, w_pr, w_out, ln1_g, ln1_b, w_router, b_router, w1, b1, w2, b2, ln2_g, ln2_b)
    xs, cs = [x_prompt, x_sample], [c_prompt, c_sample]
    for layer in range(DEPTH):
        p = {name: arr[layer] for name, arr in zip(_PARAM_NAMES, stacked)}
        xs = _encoder_layer(xs, cs, p)
    return (xs[0], xs[1])
```

```python
import functools
import math

import jax
import jax.numpy as jnp
from jax import lax
from jax.experimental import pallas as pl
from jax.experimental.pallas import tpu as pltpu

F32 = jnp.float32
BF16 = jnp.bfloat16

D_MODEL = 1024
GRID_W = 64
N_HEADS = 8
N_KV_HEADS = 2
HEAD_DIM = 128
GROUPS = N_HEADS // N_KV_HEADS
KV_WIDTH = N_KV_HEADS * HEAD_DIM
ROPE_THETA = 10000.0
RNN_BLOCKS = 8
RNN_BLOCK_W = D_MODEL // RNN_BLOCKS
CONV_W = 4
LRU_C = 8.0
N_EXPERTS = 32
TOP_K = 4
D_FF = D_MODEL
SWIGLU_LIMIT = 7.0
SWIGLU_ALPHA = 1.702
DEPTH = 1
DN_ALPHA = (2 * DEPTH) ** 0.25
LN_EPS = 1e-5
RMS_EPS = 1e-6
N_MOD = 6
IN_WIDTH = D_MODEL + 2 * KV_WIDTH + 2 * D_MODEL + 2 * D_MODEL
_Q0, _K0, _V0, _XR0, _GR0, _GL0 = 0, 1024, 1280, 1536, 2560, 3584

V7X_VMEM_LIMIT_BYTES = 56 * 1024 * 1024
HALO = 16


def _tiles(n_tokens):
    big = n_tokens >= 4096
    return dict(
        tm_in=512 if big else 128,
        tq=256 if big else 128,
        tt=512 if big else 128,
        tm_mid=512 if big else 128,
        tp=512 if big else 128,
        tg=512 if big else 256,
        tb=512 if big else 128,
        tc=256,
    )


def _cparams(sem):
    return pltpu.CompilerParams(dimension_semantics=sem, vmem_limit_bytes=V7X_VMEM_LIMIT_BYTES)


def _const_spec(shape):
    nd = len(shape)
    return pl.BlockSpec(shape, lambda *_: (0,) * nd, pipeline_mode=pl.Buffered(1))


def _layer_norm(x):
    mu = jnp.mean(x, axis=-1, keepdims=True)
    xc = x - mu
    var = jnp.mean(xc * xc, axis=-1, keepdims=True)
    return xc * lax.rsqrt(var + LN_EPS)


def _sigmoid(x):
    return 1.0 / (1.0 + jnp.exp(-x))


ROW_SUB = D_MODEL // 128


def _store_row_tiles(ref, val):
    n = val.shape[0]
    for j in range(ROW_SUB):
        ref[pl.ds(j, n, stride=ROW_SUB), :] = val[:, j * 128:(j + 1) * 128]


def _load_row_tiles(ref, n):
    return jnp.concatenate([ref[pl.ds(j, n, stride=ROW_SUB), :] for j in range(ROW_SUB)], axis=1)


def _ada_kernel(c_ref, w_ref, b_ref, o_ref):
    c = c_ref[...]
    s = c * _sigmoid(c)
    o_ref[...] = jnp.dot(s, w_ref[...], preferred_element_type=F32,
                         precision=lax.Precision.HIGHEST) + b_ref[...]


def _ada_mod(c_all, w_ada, b_ada):
    bp = c_all.shape[0]
    ncol = w_ada.shape[1]
    return pl.pallas_call(
        _ada_kernel,
        grid=(ncol // D_MODEL,),
        in_specs=[pl.BlockSpec((bp, D_MODEL), lambda j: (0, 0)),
                  pl.BlockSpec((D_MODEL, D_MODEL), lambda j: (0, j)),
                  pl.BlockSpec((1, D_MODEL), lambda j: (0, j))],
        out_specs=pl.BlockSpec((bp, D_MODEL), lambda j: (0, j)),
        out_shape=jax.ShapeDtypeStruct((bp, ncol), F32),
        compiler_params=_cparams(("arbitrary",)),
        name="ada_mod",
    )(c_all, w_ada, b_ada.reshape(1, ncol))


class _Layout:
    def __init__(self, trunks):
        self.trunks = tuple(trunks)
        self.n_tokens = sum(b * s for b, s in trunks)
        self.n_seqs = sum(b for b, _ in trunks)
        self.max_seq = max(s for _, s in trunks)

    def seq_and_pos(self, t0):
        seq = jnp.int32(0)
        pos = jnp.int32(0)
        tok_off, seq_off = 0, 0
        for b, s in self.trunks:
            inside = (t0 >= tok_off) & (t0 < tok_off + b * s)
            rel = jnp.maximum(t0 - tok_off, 0)
            seq = jnp.where(inside, seq_off + rel // s, seq)
            pos = jnp.where(inside, rel % s, pos)
            tok_off += b * s
            seq_off += b
        return seq, pos

    def trunk_specs(self, tm, make_spec):
        specs, ends, t0 = [], [], 0
        for b, s in self.trunks:
            n_t = b * s // tm
            specs.append(make_spec(functools.partial(_clamped_local, t0=t0, n_t=n_t)))
            t0 += n_t
            ends.append(t0)
        return specs, tuple(ends)


def _clamped_local(i, *, t0, n_t):
    return jnp.clip(i - t0, 0, n_t - 1)


def _owner_value(loads, i, tile_ends):
    val = loads[-1]()
    for j in range(len(loads) - 2, -1, -1):
        val = jnp.where(i < tile_ends[j], loads[j](), val)
    return val


def _rope_tables(max_seq):
    t = jnp.arange(max_seq)
    rows = (t // GRID_W).astype(F32)
    cols = (t % GRID_W).astype(F32)
    axis_dim = HEAD_DIM // 2
    inv = ROPE_THETA ** (-jnp.arange(0, axis_dim, 2, dtype=F32) / axis_dim)
    ar, ac = rows[:, None] * inv, cols[:, None] * inv
    cr, sr, cc, sc = jnp.cos(ar), jnp.sin(ar), jnp.cos(ac), jnp.sin(ac)
    z = jnp.zeros_like(sr)
    cos_t = jnp.concatenate([cr, cr, cc, cc], axis=-1)
    up_t = jnp.concatenate([-sr, z, -sc, z], axis=-1)
    dn_t = jnp.concatenate([z, sr, z, sc], axis=-1)
    return cos_t, up_t, dn_t


def _inproj_kernel(*refs, tile_ends):
    n_tr = len(tile_ends)
    x_refs = refs[:n_tr]
    (mod_ref, w_ref, qg_ref, kg_ref, cos_ref, up_ref, dn_ref,
     q_ref, k_ref, vt_ref, xr_ref, gg_ref, gs_ref) = refs[n_tr:]
    x = _owner_value([lambda r=r: r[...] for r in x_refs], pl.program_id(0), tile_ends)
    sh1 = mod_ref[0, 0:1, :]
    sc1 = mod_ref[0, 1:2, :]
    h = (_layer_norm(x) * (1.0 + sc1) + sh1).astype(BF16)
    cos_t, up_t, dn_t = cos_ref[...], up_ref[...], dn_ref[...]

    def proj(c0, width):
        return jnp.dot(h, w_ref[:, c0:c0 + width], preferred_element_type=F32)

    def norm_rope(z, gain):
        ms = jnp.mean(z * z, axis=-1, keepdims=True)
        y = z * lax.rsqrt(ms + RMS_EPS) * gain
        return (y * cos_t + pltpu.roll(y, HEAD_DIM - 32, 1) * up_t
                + pltpu.roll(y, 32, 1) * dn_t)

    zq = proj(_Q0, D_MODEL)
    qg = qg_ref[...] * (HEAD_DIM ** -0.5 * math.log2(math.e))
    for hd in range(N_HEADS):
        sl = slice(hd * HEAD_DIM, (hd + 1) * HEAD_DIM)
        q_ref[:, sl] = norm_rope(zq[:, sl], qg).astype(BF16)
    zk = proj(_K0, KV_WIDTH)
    kg = kg_ref[...]
    for hd in range(N_KV_HEADS):
        sl = slice(hd * HEAD_DIM, (hd + 1) * HEAD_DIM)
        k_ref[:, sl] = norm_rope(zk[:, sl], kg).astype(BF16)
    zv = proj(_V0, KV_WIDTH)
    vt_ref[0] = zv.T.astype(BF16)
    xr_ref[...] = proj(_XR0, D_MODEL).astype(BF16)
    gg_ref[...] = jax.nn.gelu(proj(_GR0, D_MODEL), approximate=True).astype(BF16)
    gs_ref[...] = _sigmoid(proj(_GL0, 2 * D_MODEL)).astype(BF16)


def _inproj(xs, mod, w_in, q_gain, k_gain, tables, lay, tm):
    nt = lay.n_tokens
    cos_t, up_t, dn_t = tables

    def seq_map(i):
        return (lay.seq_and_pos(i * tm)[0], 0, 0)

    def pos_map(i):
        return (lay.seq_and_pos(i * tm)[1] // tm, 0)

    tok = lambda w: pl.BlockSpec((tm, w), lambda i: (i, 0))
    rope = pl.BlockSpec((tm, HEAD_DIM), pos_map)
    x_specs, tile_ends = lay.trunk_specs(
        tm, lambda local: pl.BlockSpec((tm, D_MODEL), lambda i: (local(i), 0)))
    return pl.pallas_call(
        functools.partial(_inproj_kernel, tile_ends=tile_ends),
        grid=(nt // tm,),
        in_specs=x_specs + [
                  pl.BlockSpec((1, N_MOD, D_MODEL), seq_map),
                  _const_spec((D_MODEL, IN_WIDTH)),
                  _const_spec((1, HEAD_DIM)), _const_spec((1, HEAD_DIM)),
                  rope, rope, rope],
        out_specs=[tok(D_MODEL), tok(KV_WIDTH),
                   pl.BlockSpec((1, KV_WIDTH, tm), lambda i: (i, 0, 0)),
                   tok(D_MODEL), tok(D_MODEL), tok(2 * D_MODEL)],
        out_shape=[jax.ShapeDtypeStruct((nt, D_MODEL), BF16),
                   jax.ShapeDtypeStruct((nt, KV_WIDTH), BF16),
                   jax.ShapeDtypeStruct((nt // tm, KV_WIDTH, tm), BF16),
                   jax.ShapeDtypeStruct((nt, D_MODEL), BF16),
                   jax.ShapeDtypeStruct((nt, D_MODEL), BF16),
                   jax.ShapeDtypeStruct((nt, 2 * D_MODEL), BF16)],
        compiler_params=_cparams(("arbitrary",)),
        name="in_proj",
    )(*xs, mod, w_in, q_gain.reshape(1, HEAD_DIM), k_gain.reshape(1, HEAD_DIM), cos_t, up_t, dn_t)


def _attn_kernel(q_ref, k_ref, vt_ref, o_ref, qt_scr, s_scr, m_scr, l_scr, acc_scr, *,
                 tq, tk, n_kv, group):
    for g in range(GROUPS):
        qg = q_ref[:, g * HEAD_DIM:(g + 1) * HEAD_DIM].astype(F32)
        qt_scr[:, g * tq:(g + 1) * tq] = qg.T.astype(BF16)
    m_scr[...] = jnp.full(m_scr.shape, -jnp.inf, F32)
    l_scr[...] = jnp.zeros(l_scr.shape, F32)
    acc_scr[...] = jnp.zeros(acc_scr.shape, F32)

    def scores(j, slot):
        kt = k_ref[pl.ds(pl.multiple_of(j * tk, tk), tk), :]
        s_scr[slot] = jnp.dot(kt, qt_scr[...], preferred_element_type=F32)

    def accumulate(j, slot):
        s = s_scr[slot]
        m_old = m_scr[...]
        m_new = jnp.maximum(m_old, jnp.max(s, axis=0, keepdims=True))
        alpha = jnp.exp2(m_old - m_new)
        p = jnp.exp2(s - m_new)
        l_scr[...] = alpha * l_scr[...] + jnp.sum(p, axis=0, keepdims=True)
        pv = jnp.dot(vt_ref[j], p.astype(BF16), preferred_element_type=F32)
        acc_scr[...] = alpha * acc_scr[...] + pv
        m_scr[...] = m_new

    scores(0, 0)

    def body(i, carry):
        j = group * i
        for u in range(group):
            scores(jnp.minimum(j + u + 1, n_kv - 1), (u + 1) % 2)
            accumulate(j + u, u % 2)
        return carry

    lax.fori_loop(0, n_kv // group, body, 0)
    out = acc_scr[...] / l_scr[...]
    for g in range(GROUPS):
        o_ref[:, g * HEAD_DIM:(g + 1) * HEAD_DIM] = out[:, g * tq:(g + 1) * tq].T.astype(BF16)


def _attention(q, k, vt3, tok_off, batch, seq, tq, tk):
    n_kv = seq // tk
    group = next((g for g in (8, 4) if n_kv % g == 0 and n_kv // g >= 2), 2)
    assert n_kv % group == 0
    qrow0 = tok_off // tq
    srow0 = tok_off // seq
    gw = GROUPS * HEAD_DIM
    n_q = seq // tq
    return pl.pallas_call(
        functools.partial(_attn_kernel, tq=tq, tk=tk, n_kv=n_kv, group=group),
        grid=(batch, N_KV_HEADS, n_q),
        in_specs=[pl.BlockSpec((tq, gw), lambda b, h, i: (qrow0 + b * n_q + i, h)),
                  pl.BlockSpec((seq, HEAD_DIM), lambda b, h, i: (srow0 + b, h)),
                  pl.BlockSpec((n_kv, HEAD_DIM, tk), lambda b, h, i: (srow0 + b, h, 0))],
        out_specs=pl.BlockSpec((tq, gw), lambda b, h, i: (b * n_q + i, h)),
        out_shape=jax.ShapeDtypeStruct((batch * seq, D_MODEL), BF16),
        scratch_shapes=[pltpu.VMEM((HEAD_DIM, GROUPS * tq), BF16),
                        pltpu.VMEM((2, tk, GROUPS * tq), F32),
                        pltpu.VMEM((1, GROUPS * tq), F32),
                        pltpu.VMEM((1, GROUPS * tq), F32),
                        pltpu.VMEM((HEAD_DIM, GROUPS * tq), F32)],
        compiler_params=_cparams(("arbitrary", "arbitrary", "arbitrary")),
        name="attention",
    )(q, k, vt3)


def _scan_kernel(cur_ref, prev_ref, next_ref, cw_ref, cb_ref, w_ref, b_ref, lam_ref, o_ref,
                 xc_scr, a_scr, u_scr, h_scr, carry_scr, *, tt, n_chunks):
    d = pl.program_id(1)
    c = pl.program_id(2)
    chunk = jnp.where(d == 0, c, n_chunks - 1 - c)

    @pl.when(c == 0)
    def _():
        carry_scr[...] = jnp.zeros(carry_scr.shape, F32)

    keep_prev = jnp.where(chunk == 0, 0.0, 1.0)
    keep_next = jnp.where(chunk == n_chunks - 1, 0.0, 1.0)
    cur = cur_ref[...].astype(F32)
    taps = [cw_ref[j:j + 1, :] for j in range(CONV_W)]
    xc_scr[...] = (cb_ref[...] + taps[2] * cur + taps[1] * pltpu.roll(cur, 1, 0)
                   + taps[0] * pltpu.roll(cur, 2, 0) + taps[3] * pltpu.roll(cur, tt - 1, 0))
    head = jnp.concatenate([prev_ref[...].astype(F32)[HALO - 8:HALO] * keep_prev, cur[0:16]], axis=0)
    tail = jnp.concatenate([cur[tt - 16:tt], next_ref[...].astype(F32)[0:8] * keep_next], axis=0)
    xc_scr[0:8, :] = cb_ref[...] + sum(taps[j] * head[6 + j:14 + j] for j in range(CONV_W))
    xc_scr[tt - 8:tt, :] = cb_ref[...] + sum(taps[j] * tail[6 + j:14 + j] for j in range(CONV_W))
    xc = xc_scr[...]
    xcb = xc.astype(BF16)

    lam = lam_ref[0]
    y = jnp.exp(-jnp.abs(lam))
    w1p = 1.0 + y
    log1p_y = jnp.where(w1p == 1.0, y, jnp.log(w1p) * y / jnp.where(w1p == 1.0, 1.0, w1p - 1.0))
    neg_c_sp = (-LRU_C * math.log2(math.e)) * (jnp.maximum(-lam, 0.0) + log1p_y)

    for n in range(RNN_BLOCKS):
        sl = slice(n * RNN_BLOCK_W, (n + 1) * RNN_BLOCK_W)
        pre = jnp.dot(xcb[:, sl], w_ref[0, n], preferred_element_type=F32)
        r = _sigmoid(pre[:, :RNN_BLOCK_W] + b_ref[0, 0:1, sl])
        i = _sigmoid(pre[:, RNN_BLOCK_W:] + b_ref[0, 1:2, sl])
        a = jnp.exp2(r * neg_c_sp[:, sl])
        a_scr[:, sl] = a
        u_scr[:, sl] = jnp.sqrt(1.0 - a * a) * (i * xc[:, sl])

    def step(t, h):
        row = jnp.where(d == 0, t, tt - 1 - t)
        h = a_scr[pl.ds(row, 1), :] * h + u_scr[pl.ds(row, 1), :]
        h_scr[pl.ds(row, 1), :] = h
        return h

    carry_scr[...] = lax.fori_loop(0, tt, step, carry_scr[...], unroll=8)
    o_ref[0] = h_scr[...].astype(BF16)


def _scan(xr, conv_w, conv_b, w_lru, b_lru, lam, tok_off, batch, seq, tt):
    nt = xr.shape[0]
    n_chunks = seq // tt
    row0 = tok_off // tt
    hrow0 = tok_off // HALO
    hpc = tt // HALO
    n_halo = nt // HALO

    def chunk_of(d, c):
        return jnp.where(d == 0, c, n_chunks - 1 - c)

    def prev_map(b, d, c):
        return (jnp.maximum(hrow0 + (b * n_chunks + chunk_of(d, c)) * hpc - 1, 0), 0)

    def next_map(b, d, c):
        return (jnp.minimum(hrow0 + (b * n_chunks + chunk_of(d, c) + 1) * hpc, n_halo - 1), 0)

    return pl.pallas_call(
        functools.partial(_scan_kernel, tt=tt, n_chunks=n_chunks),
        grid=(batch, 2, n_chunks),
        in_specs=[pl.BlockSpec((tt, D_MODEL), lambda b, d, c: (row0 + b * n_chunks + chunk_of(d, c), 0)),
                  pl.BlockSpec((HALO, D_MODEL), prev_map),
                  pl.BlockSpec((HALO, D_MODEL), next_map),
                  pl.BlockSpec((CONV_W, D_MODEL), lambda b, d, c: (0, 0)),
                  pl.BlockSpec((1, D_MODEL), lambda b, d, c: (0, 0)),
                  pl.BlockSpec((1, RNN_BLOCKS, RNN_BLOCK_W, 2 * RNN_BLOCK_W), lambda b, d, c: (d, 0, 0, 0)),
                  pl.BlockSpec((1, 2, D_MODEL), lambda b, d, c: (d, 0, 0)),
                  pl.BlockSpec((1, 1, D_MODEL), lambda b, d, c: (d, 0, 0))],
        out_specs=pl.BlockSpec((1, tt, D_MODEL), lambda b, d, c: (d, b * n_chunks + chunk_of(d, c), 0)),
        out_shape=jax.ShapeDtypeStruct((2, batch * seq, D_MODEL), BF16),
        scratch_shapes=[pltpu.VMEM((tt, D_MODEL), F32),
                        pltpu.VMEM((tt, D_MODEL), F32),
                        pltpu.VMEM((tt, D_MODEL), F32),
                        pltpu.VMEM((tt, D_MODEL), F32),
                        pltpu.VMEM((1, D_MODEL), F32)],
        compiler_params=_cparams(("arbitrary", "arbitrary", "arbitrary")),
        name="lru_scan",
    )(xr, xr, xr, conv_w, conv_b, w_lru, b_lru, lam)


def _mid_kernel(*refs, tile_ends):
    n_tr = len(tile_ends)
    oa_refs = refs[:n_tr]
    h_refs = refs[n_tr:3 * n_tr]
    x_refs = refs[3 * n_tr:4 * n_tr]
    (gg_ref, gs_ref, mod_ref, wpa_ref, wpr_ref, wo_ref, g1_ref, b1_ref, wr_ref, br_ref,
     x1_ref, h2_ref, wf_ref, ms_ref) = refs[4 * n_tr:]
    i = pl.program_id(0)
    oa = _owner_value([lambda r=r: r[...] for r in oa_refs], i, tile_ends)
    hsum = _owner_value([lambda f=h_refs[2 * j], b=h_refs[2 * j + 1]: f[0].astype(F32) + b[0].astype(F32)
                         for j in range(n_tr)], i, tile_ends)
    x_in = _owner_value([lambda r=r: r[...] for r in x_refs], i, tile_ends)
    o_att = jnp.dot(oa, wpa_ref[...], preferred_element_type=F32)
    rec = hsum.astype(BF16) * gg_ref[...]
    o_rec = jnp.dot(rec, wpr_ref[...], preferred_element_type=F32)
    gs = gs_ref[...].astype(F32)
    merged = gs[:, :D_MODEL] * o_att + gs[:, D_MODEL:] * o_rec
    mix = jnp.dot(merged.astype(BF16), wo_ref[...], preferred_element_type=F32)
    g1 = mod_ref[0, 2:3, :]
    sh2 = mod_ref[0, 3:4, :]
    sc2 = mod_ref[0, 4:5, :]
    x1 = _layer_norm(DN_ALPHA * x_in + (1.0 + g1) * mix) * g1_ref[...] + b1_ref[...]
    x1_ref[...] = x1
    h2 = _layer_norm(x1) * (1.0 + sc2) + sh2
    _store_row_tiles(h2_ref, h2)
    tm = h2.shape[0]
    h2_hi = h2.astype(BF16)
    h2_lo = (h2 - h2_hi.astype(F32)).astype(BF16)
    cross = jnp.dot(jnp.concatenate([h2_hi, h2_lo], axis=0), wr_ref[...], preferred_element_type=F32)
    logits = ((cross[:tm, :N_EXPERTS] + cross[:tm, N_EXPERTS:])
              + (cross[tm:, :N_EXPERTS] + cross[tm:, N_EXPERTS:])) + br_ref[...]
    lane = lax.broadcasted_iota(jnp.int32, logits.shape, 1).astype(F32)
    rem = logits
    sel = jnp.zeros(logits.shape, F32)
    top = None
    denom = None
    for kk in range(TOP_K):
        m = jnp.max(rem, axis=-1, keepdims=True)
        idx = jnp.min(jnp.where(rem == m, lane, float(N_EXPERTS)), axis=-1, keepdims=True)
        pick = lane == idx
        sel = jnp.where(pick, 1.0, sel)
        rem = jnp.where(pick, -jnp.inf, rem)
        if kk == 0:
            top = m
            denom = jnp.ones_like(m)
        else:
            denom = denom + jnp.exp(m - top)
    wf_ref[...] = jnp.where(sel > 0.0, jnp.exp(logits - top) / denom, 0.0)
    ms_ref[...] = sel


def _mid(o_atts, hfbs, gg, gs, xs, mod, w_pa, w_pr, w_out, ln1_g, ln1_b, w_router, b_router, lay, tm):
    nt = lay.n_tokens

    def seq_map(i):
        return (lay.seq_and_pos(i * tm)[0], 0, 0)

    tok = lambda w: pl.BlockSpec((tm, w), lambda i: (i, 0))
    vec = lambda w: _const_spec((1, w))
    row_specs, tile_ends = lay.trunk_specs(
        tm, lambda local: pl.BlockSpec((tm, D_MODEL), lambda i: (local(i), 0)))
    h_pairs, _ = lay.trunk_specs(
        tm, lambda local: [pl.BlockSpec((1, tm, D_MODEL), lambda i, d=d: (d, local(i), 0)) for d in range(2)])
    h_specs = [spec for pair in h_pairs for spec in pair]
    h_args = [hfb for hfb in hfbs for _ in range(2)]
    return pl.pallas_call(
        functools.partial(_mid_kernel, tile_ends=tile_ends),
        grid=(nt // tm,),
        in_specs=row_specs + h_specs + row_specs + [
                  tok(D_MODEL), tok(2 * D_MODEL),
                  pl.BlockSpec((1, N_MOD, D_MODEL), seq_map),
                  _const_spec((D_MODEL, D_MODEL)), _const_spec((D_MODEL, D_MODEL)),
                  _const_spec((D_MODEL, D_MODEL)),
                  vec(D_MODEL), vec(D_MODEL),
                  _const_spec((D_MODEL, 2 * N_EXPERTS)), vec(N_EXPERTS)],
        out_specs=[tok(D_MODEL), pl.BlockSpec((ROW_SUB * tm, 128), lambda i: (i, 0)),
                   tok(N_EXPERTS), tok(N_EXPERTS)],
        out_shape=[jax.ShapeDtypeStruct((nt, D_MODEL), F32),
                   jax.ShapeDtypeStruct((ROW_SUB * nt, 128), F32),
                   jax.ShapeDtypeStruct((nt, N_EXPERTS), F32),
                   jax.ShapeDtypeStruct((nt, N_EXPERTS), F32)],
        compiler_params=_cparams(("arbitrary",)),
        name="merge_router",
    )(*o_atts, *h_args, *xs, gg, gs, mod, w_pa, w_pr, w_out, ln1_g.reshape(1, -1), ln1_b.reshape(1, -1),
      w_router, b_router.reshape(1, -1))


def _rank_kernel(ms_ref, rank_ref, cnt_ref, tri_scr, run_scr, *, tp):
    @pl.when(pl.program_id(0) == 0)
    def _():
        r = lax.broadcasted_iota(jnp.int32, (tp, tp), 0)
        c = lax.broadcasted_iota(jnp.int32, (tp, tp), 1)
        tri_scr[...] = jnp.where(c < r, 1.0, 0.0).astype(BF16)
        run_scr[...] = jnp.zeros(run_scr.shape, F32)

    ms = ms_ref[...]
    before = jnp.dot(tri_scr[...], ms.astype(BF16), preferred_element_type=F32)
    rank_ref[...] = before + run_scr[...]
    run_scr[...] = run_scr[...] + jnp.sum(ms, axis=0, keepdims=True)
    cnt_ref[...] = jnp.broadcast_to(run_scr[...], cnt_ref.shape)


def _rank(msel, tp):
    nt = msel.shape[0]
    return pl.pallas_call(
        functools.partial(_rank_kernel, tp=tp),
        grid=(nt // tp,),
        in_specs=[pl.BlockSpec((tp, N_EXPERTS), lambda i: (i, 0))],
        out_specs=[pl.BlockSpec((tp, N_EXPERTS), lambda i: (i, 0)),
                   pl.BlockSpec((8, N_EXPERTS), lambda i: (0, 0))],
        out_shape=[jax.ShapeDtypeStruct((nt, N_EXPERTS), F32),
                   jax.ShapeDtypeStruct((8, N_EXPERTS), F32)],
        scratch_shapes=[pltpu.VMEM((tp, tp), BF16), pltpu.VMEM((1, N_EXPERTS), F32)],
        compiler_params=_cparams(("arbitrary",)),
        name="route_rank",
    )(msel)


def _slots_kernel(rank_ref, ms_ref, wf_ref, start_ref, slot_ref, wt_ref):
    ms = ms_ref[...]
    wf = wf_ref[...]
    slot_full = rank_ref[...] + start_ref[...]
    lane = lax.broadcasted_iota(jnp.int32, ms.shape, 1).astype(F32)
    out_lane = lax.broadcasted_iota(jnp.int32, slot_ref.shape, 1)
    slots = jnp.zeros(slot_ref.shape, F32)
    wts = jnp.zeros(wt_ref.shape, F32)
    rem = ms
    for kk in range(TOP_K):
        idx = jnp.min(jnp.where(rem > 0.0, lane, float(2 * N_EXPERTS)), axis=-1, keepdims=True)
        pick = lane == idx
        s_k = jnp.sum(jnp.where(pick, slot_full, 0.0), axis=-1, keepdims=True)
        w_k = jnp.sum(jnp.where(pick, wf, 0.0), axis=-1, keepdims=True)
        rem = jnp.where(pick, 0.0, rem)
        slots = jnp.where(out_lane == kk, s_k, slots)
        wts = jnp.where(out_lane == kk, w_k, wts)
    slot_ref[...] = slots.astype(jnp.int32)
    wt_ref[...] = wts


def _slots(rank, msel, wfull, pad_start, tp):
    nt = msel.shape[0]
    tok = pl.BlockSpec((tp, N_EXPERTS), lambda i: (i, 0))
    out = pl.BlockSpec((tp, 128), lambda i: (i, 0))
    return pl.pallas_call(
        _slots_kernel,
        grid=(nt // tp,),
        in_specs=[tok, tok, tok, pl.BlockSpec((1, N_EXPERTS), lambda i: (0, 0))],
        out_specs=[out, out],
        out_shape=[jax.ShapeDtypeStruct((nt, 128), jnp.int32),
                   jax.ShapeDtypeStruct((nt, 128), F32)],
        compiler_params=_cparams(("arbitrary",)),
        name="route_slots",
    )(rank, msel, wfull, pad_start)


def _row_view(ref, row):
    return ref.at[pl.ds(pl.multiple_of(row * ROW_SUB, ROW_SUB), ROW_SUB), :]


def _dispatch_kernel(slot_ref, tail_ref, h_ref, h_hbm, xs_ref, zero_scr, sem, hsem, zsem, *, tg, tb):
    @pl.when(pl.program_id(0) == 0)
    def _():
        zero_scr[...] = jnp.zeros(zero_scr.shape, F32)
        for e in range(N_EXPERTS):
            tail = pl.multiple_of(tail_ref[e] * ROW_SUB, tb * ROW_SUB)
            pltpu.make_async_copy(zero_scr, xs_ref.at[pl.ds(tail, tb * ROW_SUB), :], zsem).start()
        for e in range(N_EXPERTS):
            pltpu.make_async_copy(zero_scr, xs_ref.at[pl.ds(0, tb * ROW_SUB), :], zsem).wait()

    base = pl.program_id(0) * tg
    n_vmem = TOP_K // 2

    def issue(t, carry):
        for kk in range(TOP_K):
            dst = _row_view(xs_ref, slot_ref[t * TOP_K + kk])
            if kk < n_vmem:
                pltpu.make_async_copy(_row_view(h_ref, t), dst, sem).start(priority=kk % 2)
            else:
                pltpu.make_async_copy(_row_view(h_hbm, base + t), dst, hsem).start()
        return carry

    lax.fori_loop(0, tg, issue, 0, unroll=2)

    block = pl.ds(0, tg * ROW_SUB)
    for kk in range(TOP_K):
        if kk < n_vmem:
            pltpu.make_async_copy(h_ref, xs_ref.at[block, :], sem).wait()
        else:
            pltpu.make_async_copy(h_hbm.at[block, :], xs_ref.at[block, :], hsem).wait()


def _dispatch(slots_flat, tail_start, h2r, cap, tg, tb):
    nt = h2r.shape[0] // ROW_SUB
    return pl.pallas_call(
        functools.partial(_dispatch_kernel, tg=tg, tb=tb),
        grid=(nt // tg,),
        in_specs=[pl.BlockSpec((tg * TOP_K,), lambda i: (i,), memory_space=pltpu.SMEM),
                  pl.BlockSpec(memory_space=pltpu.SMEM),
                  pl.BlockSpec((tg * ROW_SUB, 128), lambda i: (i, 0)),
                  pl.BlockSpec(memory_space=pl.ANY)],
        out_specs=pl.BlockSpec(memory_space=pl.ANY),
        out_shape=jax.ShapeDtypeStruct((cap * ROW_SUB, 128), F32),
        scratch_shapes=[pltpu.VMEM((tb * ROW_SUB, 128), F32), pltpu.SemaphoreType.DMA(()),
                        pltpu.SemaphoreType.DMA(()), pltpu.SemaphoreType.DMA(())],
        compiler_params=_cparams(("arbitrary",)),
        name="moe_dispatch",
    )(slots_flat, tail_start, h2r, h2r)


def _expert_kernel(be_ref, bv_ref, xs_ref, w1_ref, b1_ref, w2_ref, b2_ref, ys_ref, w1_scr, w2_scr, *, tb):
    i = pl.program_id(0)
    valid = bv_ref[i]

    @pl.when((i == 0) | (be_ref[i] != be_ref[jnp.maximum(i - 1, 0)]))
    def _():
        w1_scr[...] = w1_ref[0].astype(BF16)
        w2_scr[...] = w2_ref[0].astype(BF16)

    @pl.when(valid > 0)
    def _():
        xb = _load_row_tiles(xs_ref, tb).astype(BF16)
        gu = jnp.dot(xb, w1_scr[...], preferred_element_type=F32) + b1_ref[0]
        glu = jnp.minimum(gu[:, :D_FF], SWIGLU_LIMIT)
        lin = jnp.clip(gu[:, D_FF:], -SWIGLU_LIMIT, SWIGLU_LIMIT)
        act = (lin + 1.0) * glu * _sigmoid(SWIGLU_ALPHA * glu)
        y = jnp.dot(act.astype(BF16), w2_scr[...], preferred_element_type=F32) + b2_ref[0]
        _store_row_tiles(ys_ref, y)

    @pl.when(valid <= 0)
    def _():
        ys_ref[...] = jnp.zeros(ys_ref.shape, F32)


def _experts(block_expert, block_valid, xs, w1, b1, w2, b2, tb):
    cap = xs.shape[0] // ROW_SUB
    rows = pl.BlockSpec((tb * ROW_SUB, 128), lambda i, be, bv: (i, 0))
    grid_spec = pltpu.PrefetchScalarGridSpec(
        num_scalar_prefetch=2,
        grid=(cap // tb,),
        in_specs=[rows,
                  pl.BlockSpec((1, D_MODEL, 2 * D_FF), lambda i, be, bv: (be[i], 0, 0)),
                  pl.BlockSpec((1, 1, 2 * D_FF), lambda i, be, bv: (be[i], 0, 0)),
                  pl.BlockSpec((1, D_FF, D_MODEL), lambda i, be, bv: (be[i], 0, 0)),
                  pl.BlockSpec((1, 1, D_MODEL), lambda i, be, bv: (be[i], 0, 0))],
        out_specs=rows,
        scratch_shapes=[pltpu.VMEM((D_MODEL, 2 * D_FF), BF16), pltpu.VMEM((D_FF, D_MODEL), BF16)],
    )
    return pl.pallas_call(
        functools.partial(_expert_kernel, tb=tb),
        grid_spec=grid_spec,
        out_shape=jax.ShapeDtypeStruct((cap * ROW_SUB, 128), F32),
        compiler_params=_cparams(("arbitrary",)),
        name="moe_experts",
    )(block_expert, block_valid, xs, w1, b1.reshape(N_EXPERTS, 1, -1), w2, b2.reshape(N_EXPERTS, 1, -1))


def _combine_kernel(slot_ref, next_slot_ref, ys_ref, wt_ref, x1_ref, mod_ref, g_ref, b_ref, *rest,
                    tc, tile_ends):
    y_refs = rest[:len(tile_ends)]
    buf, sem = rest[len(tile_ends):]
    i = pl.program_id(0)
    cur = i % 2

    def row_copy(slots, t, kk, half):
        return pltpu.make_async_copy(_row_view(ys_ref, slots[t * TOP_K + kk]),
                                     _row_view(buf.at[half, kk], t), sem.at[half])

    def gather(slots, half):
        def issue(t, carry):
            for kk in range(TOP_K):
                row_copy(slots, t, kk, half).start(priority=kk % 2)
            return carry
        lax.fori_loop(0, tc, issue, 0, unroll=8)

    @pl.when(i == 0)
    def _():
        gather(slot_ref, 0)

    @pl.when(i + 1 < pl.num_programs(0))
    def _():
        gather(next_slot_ref, 1 - cur)

    for kk in range(TOP_K):
        pltpu.make_async_copy(ys_ref.at[pl.ds(0, tc * ROW_SUB), :], buf.at[cur, kk], sem.at[cur]).wait()
    wt = wt_ref[...]
    ff = sum(wt[:, kk:kk + 1] * _load_row_tiles(buf.at[cur, kk], tc) for kk in range(TOP_K))
    g2 = mod_ref[0, 5:6, :]
    y = _layer_norm(DN_ALPHA * x1_ref[...] + (1.0 + g2) * ff) * g_ref[...] + b_ref[...]
    i = pl.program_id(0)
    start = 0
    for y_ref, end in zip(y_refs, tile_ends):
        @pl.when((i >= start) & (i < end))
        def _(y_ref=y_ref):
            y_ref[...] = y
        start = end


def _combine(slots_flat, ys, wts, x1, mod, ln2_g, ln2_b, lay, tc):
    nt = lay.n_tokens

    def seq_map(i):
        return (lay.seq_and_pos(i * tc)[0], 0, 0)

    out_specs, tile_ends = lay.trunk_specs(
        tc, lambda local: pl.BlockSpec((tc, D_MODEL), lambda i: (local(i), 0)))
    n_steps = nt // tc
    return pl.pallas_call(
        functools.partial(_combine_kernel, tc=tc, tile_ends=tile_ends),
        grid=(n_steps,),
        in_specs=[pl.BlockSpec((tc * TOP_K,), lambda i: (i,), memory_space=pltpu.SMEM),
                  pl.BlockSpec((tc * TOP_K,), lambda i: (jnp.minimum(i + 1, n_steps - 1),),
                               memory_space=pltpu.SMEM),
                  pl.BlockSpec(memory_space=pl.ANY),
                  pl.BlockSpec((tc, 128), lambda i: (i, 0)),
                  pl.BlockSpec((tc, D_MODEL), lambda i: (i, 0)),
                  pl.BlockSpec((1, N_MOD, D_MODEL), seq_map),
                  _const_spec((1, D_MODEL)), _const_spec((1, D_MODEL))],
        out_specs=out_specs,
        out_shape=[jax.ShapeDtypeStruct((b * s, D_MODEL), F32) for b, s in lay.trunks],
        scratch_shapes=[pltpu.VMEM((2, TOP_K, tc * ROW_SUB, 128), F32), pltpu.SemaphoreType.DMA((2,))],
        compiler_params=_cparams(("arbitrary",)),
        name="moe_combine",
    )(slots_flat, slots_flat, ys, wts, x1, mod, ln2_g.reshape(1, -1), ln2_b.reshape(1, -1))


def _encoder_layer(xs, cs, p):
    trunks = [(x.shape[0], x.shape[1]) for x in xs]
    lay = _Layout(trunks)
    nt = lay.n_tokens
    t = _tiles(nt)

    x2d = [xi.reshape(-1, D_MODEL) for xi in xs]
    c = jnp.concatenate(cs, axis=0)
    bp = -(-lay.n_seqs // 8) * 8
    c = jnp.pad(c, ((0, bp - lay.n_seqs), (0, 0)))
    mod = _ada_mod(c, p["w_ada"], p["b_ada"]).reshape(bp, N_MOD, D_MODEL)

    q, k, vt3, xr, gg, gs = _inproj(x2d, mod, p["w_in"].astype(BF16), p["q_gain"], p["k_gain"],
                                    _rope_tables(lay.max_seq), lay, t["tm_in"])

    w_lru = jnp.concatenate([p["lru_wa"], p["lru_wx"]], axis=-1).astype(BF16)
    b_lru = jnp.stack([p["lru_ba"], p["lru_bx"]], axis=1)
    lam = p["lru_lam"].reshape(2, 1, D_MODEL)
    o_atts, hfbs = [], []
    tok_off = 0
    for b, s in trunks:
        o_atts.append(_attention(q, k, vt3, tok_off, b, s, t["tq"], t["tm_in"]))
        hfbs.append(_scan(xr, p["conv_w"], p["conv_b"].reshape(1, -1), w_lru, b_lru, lam,
                          tok_off, b, s, t["tt"]))
        tok_off += b * s

    wr_hi = p["w_router"].astype(BF16)
    wr_lo = (p["w_router"] - wr_hi.astype(F32)).astype(BF16)
    x1, h2r, wfull, msel = _mid(o_atts, hfbs, gg, gs, x2d, mod, p["w_pa"].astype(BF16),
                               p["w_pr"].astype(BF16), p["w_out"].astype(BF16), p["ln1_g"], p["ln1_b"],
                               jnp.concatenate([wr_hi, wr_lo], axis=1), p["b_router"], lay, t["tm_mid"])

    tb = t["tb"]
    rank, cnt = _rank(msel, t["tp"])
    counts = cnt[0].astype(jnp.int32)
    padded = (counts + tb - 1) // tb * tb
    pad_end = jnp.cumsum(padded)
    pad_start = pad_end - padded
    n_blocks = nt * TOP_K // tb + N_EXPERTS
    blk0 = jnp.arange(n_blocks, dtype=jnp.int32) * tb
    block_expert = jnp.minimum(jnp.sum(pad_end[None, :] <= blk0[:, None], axis=1), N_EXPERTS - 1).astype(jnp.int32)
    block_valid = jnp.clip(pad_start[block_expert] + counts[block_expert] - blk0, 0, tb).astype(jnp.int32)
    slots, wts = _slots(rank, msel, wfull, pad_start.astype(F32).reshape(1, N_EXPERTS), t["tp"])
    slots_flat = slots[:, :TOP_K].reshape(-1)

    tail_start = jnp.where(padded > 0, pad_end - tb, pad_end[-1] - tb).astype(jnp.int32)
    xs_rows = _dispatch(slots_flat, tail_start, h2r, n_blocks * tb, t["tg"], tb)
    ys = _experts(block_expert, block_valid, xs_rows, p["w1"], p["b1"], p["w2"], p["b2"], tb)
    ys_out = _combine(slots_flat, ys, wts, x1, mod, p["ln2_g"], p["ln2_b"], lay, t["tc"])
    return [y.reshape(b, s, D_MODEL) for y, (b, s) in zip(ys_out, trunks)]


_PARAM_NAMES = ("w_ada", "b_ada", "w_in", "q_gain", "k_gain", "conv_w", "conv_b", "lru_wa", "lru_ba",
                "lru_wx", "lru_bx", "lru_lam", "w_pa", "w_pr", "w_out", "ln1_g", "ln1_b", "w_router",
                "b_router", "w1", "b1", "w2", "b2", "ln2_g", "ln2_b")


def kernel(x_prompt, x_sample, c_prompt, c_sample, w_ada, b_ada, w_in, q_gain, k_gain, conv_w, conv_b, lru_wa, lru_ba, lru_wx, lru_bx, lru_lam, w_pa, w_pr, w_out, ln1_g, ln1_b, w_router, b_router, w1, b1, w2, b2, ln2_g, ln2_b):
    stacked = (w_ada, b_ada, w_in, q_gain, k_gain, conv_w, conv_b, lru_wa, lru_ba, lru_wx, lru_bx,
               lru_lam, w_pa, w_pr, w_out, ln1_g, ln1_b, w_router, b_router, w1, b1, w2, b2, ln2_g, ln2_b)
    xs, cs = [x_prompt, x_sample], [c_prompt, c_sample]
    for layer in range(DEPTH):
        p = {name: arr[layer] for name, arr in zip(_PARAM_NAMES, stacked)}
        xs = _encoder_layer(xs, cs, p)
    return (xs[0], xs[1])
```

```python
import functools
import math

import jax
import jax.numpy as jnp
from jax import lax
from jax.experimental import pallas as pl
from jax.experimental.pallas import tpu as pltpu

F32 = jnp.float32
BF16 = jnp.bfloat16

D_MODEL = 1024
GRID_W = 64
N_HEADS = 8
N_KV_HEADS = 2
HEAD_DIM = 128
GROUPS = N_HEADS // N_KV_HEADS
KV_WIDTH = N_KV_HEADS * HEAD_DIM
ROPE_THETA = 10000.0
RNN_BLOCKS = 8
RNN_BLOCK_W = D_MODEL // RNN_BLOCKS
CONV_W = 4
LRU_C = 8.0
N_EXPERTS = 32
TOP_K = 4
D_FF = D_MODEL
SWIGLU_LIMIT = 7.0
SWIGLU_ALPHA = 1.702
DEPTH = 1
DN_ALPHA = (2 * DEPTH) ** 0.25
LN_EPS = 1e-5
RMS_EPS = 1e-6
N_MOD = 6
IN_WIDTH = D_MODEL + 2 * KV_WIDTH + 2 * D_MODEL + 2 * D_MODEL
_Q0, _K0, _V0, _XR0, _GR0, _GL0 = 0, 1024, 1280, 1536, 2560, 3584

V7X_VMEM_LIMIT_BYTES = 56 * 1024 * 1024
HALO = 16


def _tiles(n_tokens):
    big = n_tokens >= 4096
    return dict(
        tm_in=512 if big else 128,
        tq=256 if big else 128,
        tt=512 if big else 128,
        tm_mid=512 if big else 128,
        tp=512 if big else 128,
        tg=1024 if big else 256,
        tb=512 if big else 128,
        tc=512 if big else 256,
    )


def _cparams(sem):
    return pltpu.CompilerParams(dimension_semantics=sem, vmem_limit_bytes=V7X_VMEM_LIMIT_BYTES)


def _const_spec(shape):
    nd = len(shape)
    return pl.BlockSpec(shape, lambda *_: (0,) * nd, pipeline_mode=pl.Buffered(1))


def _layer_norm(x):
    mu = jnp.mean(x, axis=-1, keepdims=True)
    xc = x - mu
    var = jnp.mean(xc * xc, axis=-1, keepdims=True)
    return xc * lax.rsqrt(var + LN_EPS)


def _sigmoid(x):
    return 1.0 / (1.0 + jnp.exp(-x))


ROW_SUB = D_MODEL // 128


def _store_row_tiles(ref, val):
    n = val.shape[0]
    for j in range(ROW_SUB):
        ref[pl.ds(j, n, stride=ROW_SUB), :] = val[:, j * 128:(j + 1) * 128]


def _load_row_tiles(ref, n):
    return jnp.concatenate([ref[pl.ds(j, n, stride=ROW_SUB), :] for j in range(ROW_SUB)], axis=1)


def _ada_kernel(c_ref, w_ref, b_ref, o_ref):
    c = c_ref[...]
    s = c * _sigmoid(c)
    o_ref[...] = jnp.dot(s, w_ref[...], preferred_element_type=F32,
                         precision=lax.Precision.HIGHEST) + b_ref[...]


def _ada_mod(c_all, w_ada, b_ada):
    bp = c_all.shape[0]
    ncol = w_ada.shape[1]
    return pl.pallas_call(
        _ada_kernel,
        grid=(ncol // D_MODEL,),
        in_specs=[pl.BlockSpec((bp, D_MODEL), lambda j: (0, 0)),
                  pl.BlockSpec((D_MODEL, D_MODEL), lambda j: (0, j)),
                  pl.BlockSpec((1, D_MODEL), lambda j: (0, j))],
        out_specs=pl.BlockSpec((bp, D_MODEL), lambda j: (0, j)),
        out_shape=jax.ShapeDtypeStruct((bp, ncol), F32),
        compiler_params=_cparams(("arbitrary",)),
        name="ada_mod",
    )(c_all, w_ada, b_ada.reshape(1, ncol))


class _Layout:
    def __init__(self, trunks):
        self.trunks = tuple(trunks)
        self.n_tokens = sum(b * s for b, s in trunks)
        self.n_seqs = sum(b for b, _ in trunks)
        self.max_seq = max(s for _, s in trunks)

    def seq_and_pos(self, t0):
        seq = jnp.int32(0)
        pos = jnp.int32(0)
        tok_off, seq_off = 0, 0
        for b, s in self.trunks:
            inside = (t0 >= tok_off) & (t0 < tok_off + b * s)
            rel = jnp.maximum(t0 - tok_off, 0)
            seq = jnp.where(inside, seq_off + rel // s, seq)
            pos = jnp.where(inside, rel % s, pos)
            tok_off += b * s
            seq_off += b
        return seq, pos

    def trunk_specs(self, tm, make_spec):
        specs, ends, t0 = [], [], 0
        for b, s in self.trunks:
            n_t = b * s // tm
            specs.append(make_spec(functools.partial(_clamped_local, t0=t0, n_t=n_t)))
            t0 += n_t
            ends.append(t0)
        return specs, tuple(ends)


def _clamped_local(i, *, t0, n_t):
    return jnp.clip(i - t0, 0, n_t - 1)


def _owner_value(loads, i, tile_ends):
    val = loads[-1]()
    for j in range(len(loads) - 2, -1, -1):
        val = jnp.where(i < tile_ends[j], loads[j](), val)
    return val


def _rope_tables(max_seq):
    t = jnp.arange(max_seq)
    rows = (t // GRID_W).astype(F32)
    cols = (t % GRID_W).astype(F32)
    axis_dim = HEAD_DIM // 2
    inv = ROPE_THETA ** (-jnp.arange(0, axis_dim, 2, dtype=F32) / axis_dim)
    ar, ac = rows[:, None] * inv, cols[:, None] * inv
    cr, sr, cc, sc = jnp.cos(ar), jnp.sin(ar), jnp.cos(ac), jnp.sin(ac)
    z = jnp.zeros_like(sr)
    cos_t = jnp.concatenate([cr, cr, cc, cc], axis=-1)
    up_t = jnp.concatenate([-sr, z, -sc, z], axis=-1)
    dn_t = jnp.concatenate([z, sr, z, sc], axis=-1)
    return cos_t, up_t, dn_t


def _inproj_kernel(*refs, tile_ends):
    n_tr = len(tile_ends)
    x_refs = refs[:n_tr]
    (mod_ref, w_ref, qg_ref, kg_ref, cos_ref, up_ref, dn_ref,
     q_ref, k_ref, vt_ref, xr_ref, gg_ref, gs_ref) = refs[n_tr:]
    x = _owner_value([lambda r=r: r[...] for r in x_refs], pl.program_id(0), tile_ends)
    sh1 = mod_ref[0, 0:1, :]
    sc1 = mod_ref[0, 1:2, :]
    h = (_layer_norm(x) * (1.0 + sc1) + sh1).astype(BF16)
    cos_t, up_t, dn_t = cos_ref[...], up_ref[...], dn_ref[...]

    def proj(c0, width):
        return jnp.dot(h, w_ref[:, c0:c0 + width], preferred_element_type=F32)

    def norm_rope(z, gain):
        ms = jnp.mean(z * z, axis=-1, keepdims=True)
        y = z * lax.rsqrt(ms + RMS_EPS) * gain
        return (y * cos_t + pltpu.roll(y, HEAD_DIM - 32, 1) * up_t
                + pltpu.roll(y, 32, 1) * dn_t)

    zq = proj(_Q0, D_MODEL)
    qg = qg_ref[...] * (HEAD_DIM ** -0.5 * math.log2(math.e))
    for hd in range(N_HEADS):
        sl = slice(hd * HEAD_DIM, (hd + 1) * HEAD_DIM)
        q_ref[:, sl] = norm_rope(zq[:, sl], qg).astype(BF16)
    zk = proj(_K0, KV_WIDTH)
    kg = kg_ref[...]
    for hd in range(N_KV_HEADS):
        sl = slice(hd * HEAD_DIM, (hd + 1) * HEAD_DIM)
        k_ref[:, sl] = norm_rope(zk[:, sl], kg).astype(BF16)
    zv = proj(_V0, KV_WIDTH)
    vt_ref[0] = zv.T.astype(BF16)
    xr_ref[...] = proj(_XR0, D_MODEL).astype(BF16)
    gg_ref[...] = jax.nn.gelu(proj(_GR0, D_MODEL), approximate=True).astype(BF16)
    gs_ref[...] = _sigmoid(proj(_GL0, 2 * D_MODEL)).astype(BF16)


def _inproj(xs, mod, w_in, q_gain, k_gain, tables, lay, tm):
    nt = lay.n_tokens
    cos_t, up_t, dn_t = tables

    def seq_map(i):
        return (lay.seq_and_pos(i * tm)[0], 0, 0)

    def pos_map(i):
        return (lay.seq_and_pos(i * tm)[1] // tm, 0)

    tok = lambda w: pl.BlockSpec((tm, w), lambda i: (i, 0))
    rope = pl.BlockSpec((tm, HEAD_DIM), pos_map)
    x_specs, tile_ends = lay.trunk_specs(
        tm, lambda local: pl.BlockSpec((tm, D_MODEL), lambda i: (local(i), 0)))
    return pl.pallas_call(
        functools.partial(_inproj_kernel, tile_ends=tile_ends),
        grid=(nt // tm,),
        in_specs=x_specs + [
                  pl.BlockSpec((1, N_MOD, D_MODEL), seq_map),
                  _const_spec((D_MODEL, IN_WIDTH)),
                  _const_spec((1, HEAD_DIM)), _const_spec((1, HEAD_DIM)),
                  rope, rope, rope],
        out_specs=[tok(D_MODEL), tok(KV_WIDTH),
                   pl.BlockSpec((1, KV_WIDTH, tm), lambda i: (i, 0, 0)),
                   tok(D_MODEL), tok(D_MODEL), tok(2 * D_MODEL)],
        out_shape=[jax.ShapeDtypeStruct((nt, D_MODEL), BF16),
                   jax.ShapeDtypeStruct((nt, KV_WIDTH), BF16),
                   jax.ShapeDtypeStruct((nt // tm, KV_WIDTH, tm), BF16),
                   jax.ShapeDtypeStruct((nt, D_MODEL), BF16),
                   jax.ShapeDtypeStruct((nt, D_MODEL), BF16),
                   jax.ShapeDtypeStruct((nt, 2 * D_MODEL), BF16)],
        compiler_params=_cparams(("arbitrary",)),
        name="in_proj",
    )(*xs, mod, w_in, q_gain.reshape(1, HEAD_DIM), k_gain.reshape(1, HEAD_DIM), cos_t, up_t, dn_t)


def _attn_kernel(q_ref, k_ref, vt_ref, o_ref, qt_scr, s_scr, m_scr, l_scr, acc_scr, *,
                 tq, tk, n_kv, group):
    for g in range(GROUPS):
        qg = q_ref[:, g * HEAD_DIM:(g + 1) * HEAD_DIM].astype(F32)
        qt_scr[:, g * tq:(g + 1) * tq] = qg.T.astype(BF16)
    m_scr[...] = jnp.full(m_scr.shape, -jnp.inf, F32)
    l_scr[...] = jnp.zeros(l_scr.shape, F32)
    acc_scr[...] = jnp.zeros(acc_scr.shape, F32)

    def scores(j, slot):
        kt = k_ref[pl.ds(pl.multiple_of(j * tk, tk), tk), :]
        s_scr[slot] = jnp.dot(kt, qt_scr[...], preferred_element_type=F32)

    def accumulate(j, slot):
        s = s_scr[slot]
        m_old = m_scr[...]
        m_new = jnp.maximum(m_old, jnp.max(s, axis=0, keepdims=True))
        alpha = jnp.exp2(m_old - m_new)
        p = jnp.exp2(s - m_new)
        l_scr[...] = alpha * l_scr[...] + jnp.sum(p, axis=0, keepdims=True)
        pv = jnp.dot(vt_ref[j], p.astype(BF16), preferred_element_type=F32)
        acc_scr[...] = alpha * acc_scr[...] + pv
        m_scr[...] = m_new

    scores(0, 0)

    def body(i, carry):
        j = group * i
        for u in range(group):
            scores(jnp.minimum(j + u + 1, n_kv - 1), (u + 1) % 2)
            accumulate(j + u, u % 2)
        return carry

    lax.fori_loop(0, n_kv // group, body, 0)
    out = acc_scr[...] / l_scr[...]
    for g in range(GROUPS):
        o_ref[:, g * HEAD_DIM:(g + 1) * HEAD_DIM] = out[:, g * tq:(g + 1) * tq].T.astype(BF16)


def _attention(q, k, vt3, tok_off, batch, seq, tq, tk):
    n_kv = seq // tk
    group = next((g for g in (8, 4) if n_kv % g == 0 and n_kv // g >= 2), 2)
    assert n_kv % group == 0
    qrow0 = tok_off // tq
    srow0 = tok_off // seq
    gw = GROUPS * HEAD_DIM
    n_q = seq // tq
    return pl.pallas_call(
        functools.partial(_attn_kernel, tq=tq, tk=tk, n_kv=n_kv, group=group),
        grid=(batch, N_KV_HEADS, n_q),
        in_specs=[pl.BlockSpec((tq, gw), lambda b, h, i: (qrow0 + b * n_q + i, h)),
                  pl.BlockSpec((seq, HEAD_DIM), lambda b, h, i: (srow0 + b, h)),
                  pl.BlockSpec((n_kv, HEAD_DIM, tk), lambda b, h, i: (srow0 + b, h, 0))],
        out_specs=pl.BlockSpec((tq, gw), lambda b, h, i: (b * n_q + i, h)),
        out_shape=jax.ShapeDtypeStruct((batch * seq, D_MODEL), BF16),
        scratch_shapes=[pltpu.VMEM((HEAD_DIM, GROUPS * tq), BF16),
                        pltpu.VMEM((2, tk, GROUPS * tq), F32),
                        pltpu.VMEM((1, GROUPS * tq), F32),
                        pltpu.VMEM((1, GROUPS * tq), F32),
                        pltpu.VMEM((HEAD_DIM, GROUPS * tq), F32)],
        compiler_params=_cparams(("arbitrary", "arbitrary", "arbitrary")),
        name="attention",
    )(q, k, vt3)


def _scan_kernel(cur_ref, prev_ref, next_ref, cw_ref, cb_ref, w_ref, b_ref, lam_ref, o_ref,
                 xc_scr, a_scr, u_scr, h_scr, carry_scr, *, tt, n_chunks):
    d = pl.program_id(1)
    c = pl.program_id(2)
    chunk = jnp.where(d == 0, c, n_chunks - 1 - c)

    @pl.when(c == 0)
    def _():
        carry_scr[...] = jnp.zeros(carry_scr.shape, F32)

    keep_prev = jnp.where(chunk == 0, 0.0, 1.0)
    keep_next = jnp.where(chunk == n_chunks - 1, 0.0, 1.0)
    cur = cur_ref[...].astype(F32)
    taps = [cw_ref[j:j + 1, :] for j in range(CONV_W)]
    xc_scr[...] = (cb_ref[...] + taps[2] * cur + taps[1] * pltpu.roll(cur, 1, 0)
                   + taps[0] * pltpu.roll(cur, 2, 0) + taps[3] * pltpu.roll(cur, tt - 1, 0))
    head = jnp.concatenate([prev_ref[...].astype(F32)[HALO - 8:HALO] * keep_prev, cur[0:16]], axis=0)
    tail = jnp.concatenate([cur[tt - 16:tt], next_ref[...].astype(F32)[0:8] * keep_next], axis=0)
    xc_scr[0:8, :] = cb_ref[...] + sum(taps[j] * head[6 + j:14 + j] for j in range(CONV_W))
    xc_scr[tt - 8:tt, :] = cb_ref[...] + sum(taps[j] * tail[6 + j:14 + j] for j in range(CONV_W))
    xc = xc_scr[...]
    xcb = xc.astype(BF16)

    lam = lam_ref[0]
    y = jnp.exp(-jnp.abs(lam))
    w1p = 1.0 + y
    log1p_y = jnp.where(w1p == 1.0, y, jnp.log(w1p) * y / jnp.where(w1p == 1.0, 1.0, w1p - 1.0))
    neg_c_sp = (-LRU_C * math.log2(math.e)) * (jnp.maximum(-lam, 0.0) + log1p_y)

    for n in range(RNN_BLOCKS):
        sl = slice(n * RNN_BLOCK_W, (n + 1) * RNN_BLOCK_W)
        pre = jnp.dot(xcb[:, sl], w_ref[0, n], preferred_element_type=F32)
        r = _sigmoid(pre[:, :RNN_BLOCK_W] + b_ref[0, 0:1, sl])
        i = _sigmoid(pre[:, RNN_BLOCK_W:] + b_ref[0, 1:2, sl])
        a = jnp.exp2(r * neg_c_sp[:, sl])
        a_scr[:, sl] = a
        u_scr[:, sl] = jnp.sqrt(1.0 - a * a) * (i * xc[:, sl])

    def step(t, h):
        row = jnp.where(d == 0, t, tt - 1 - t)
        h = a_scr[pl.ds(row, 1), :] * h + u_scr[pl.ds(row, 1), :]
        h_scr[pl.ds(row, 1), :] = h
        return h

    carry_scr[...] = lax.fori_loop(0, tt, step, carry_scr[...], unroll=8)
    o_ref[0] = h_scr[...].astype(BF16)


def _scan(xr, conv_w, conv_b, w_lru, b_lru, lam, tok_off, batch, seq, tt):
    nt = xr.shape[0]
    n_chunks = seq // tt
    row0 = tok_off // tt
    hrow0 = tok_off // HALO
    hpc = tt // HALO
    n_halo = nt // HALO

    def chunk_of(d, c):
        return jnp.where(d == 0, c, n_chunks - 1 - c)

    def prev_map(b, d, c):
        return (jnp.maximum(hrow0 + (b * n_chunks + chunk_of(d, c)) * hpc - 1, 0), 0)

    def next_map(b, d, c):
        return (jnp.minimum(hrow0 + (b * n_chunks + chunk_of(d, c) + 1) * hpc, n_halo - 1), 0)

    return pl.pallas_call(
        functools.partial(_scan_kernel, tt=tt, n_chunks=n_chunks),
        grid=(batch, 2, n_chunks),
        in_specs=[pl.BlockSpec((tt, D_MODEL), lambda b, d, c: (row0 + b * n_chunks + chunk_of(d, c), 0)),
                  pl.BlockSpec((HALO, D_MODEL), prev_map),
                  pl.BlockSpec((HALO, D_MODEL), next_map),
                  pl.BlockSpec((CONV_W, D_MODEL), lambda b, d, c: (0, 0)),
                  pl.BlockSpec((1, D_MODEL), lambda b, d, c: (0, 0)),
                  pl.BlockSpec((1, RNN_BLOCKS, RNN_BLOCK_W, 2 * RNN_BLOCK_W), lambda b, d, c: (d, 0, 0, 0)),
                  pl.BlockSpec((1, 2, D_MODEL), lambda b, d, c: (d, 0, 0)),
                  pl.BlockSpec((1, 1, D_MODEL), lambda b, d, c: (d, 0, 0))],
        out_specs=pl.BlockSpec((1, tt, D_MODEL), lambda b, d, c: (d, b * n_chunks + chunk_of(d, c), 0)),
        out_shape=jax.ShapeDtypeStruct((2, batch * seq, D_MODEL), BF16),
        scratch_shapes=[pltpu.VMEM((tt, D_MODEL), F32),
                        pltpu.VMEM((tt, D_MODEL), F32),
                        pltpu.VMEM((tt, D_MODEL), F32),
                        pltpu.VMEM((tt, D_MODEL), F32),
                        pltpu.VMEM((1, D_MODEL), F32)],
        compiler_params=_cparams(("arbitrary", "arbitrary", "arbitrary")),
        name="lru_scan",
    )(xr, xr, xr, conv_w, conv_b, w_lru, b_lru, lam)


def _mid_kernel(*refs, tile_ends):
    n_tr = len(tile_ends)
    oa_refs = refs[:n_tr]
    h_refs = refs[n_tr:3 * n_tr]
    x_refs = refs[3 * n_tr:4 * n_tr]
    (gg_ref, gs_ref, mod_ref, wpa_ref, wpr_ref, wo_ref, g1_ref, b1_ref, wr_ref, br_ref,
     x1_ref, h2_ref, wf_ref, ms_ref) = refs[4 * n_tr:]
    i = pl.program_id(0)
    oa = _owner_value([lambda r=r: r[...] for r in oa_refs], i, tile_ends)
    h_fwd = _owner_value([lambda r=h_refs[2 * j]: r[0] for j in range(n_tr)], i, tile_ends)
    h_bwd = _owner_value([lambda r=h_refs[2 * j + 1]: r[0] for j in range(n_tr)], i, tile_ends)
    hsum = h_fwd.astype(F32) + h_bwd.astype(F32)
    x_in = _owner_value([lambda r=r: r[...] for r in x_refs], i, tile_ends)
    o_att = jnp.dot(oa, wpa_ref[...], preferred_element_type=F32)
    rec = hsum.astype(BF16) * gg_ref[...]
    o_rec = jnp.dot(rec, wpr_ref[...], preferred_element_type=F32)
    gs = gs_ref[...].astype(F32)
    merged = gs[:, :D_MODEL] * o_att + gs[:, D_MODEL:] * o_rec
    mix = jnp.dot(merged.astype(BF16), wo_ref[...], preferred_element_type=F32)
    g1 = mod_ref[0, 2:3, :]
    sh2 = mod_ref[0, 3:4, :]
    sc2 = mod_ref[0, 4:5, :]
    x1 = _layer_norm(DN_ALPHA * x_in + (1.0 + g1) * mix) * g1_ref[...] + b1_ref[...]
    x1_ref[...] = x1
    h2 = _layer_norm(x1) * (1.0 + sc2) + sh2
    _store_row_tiles(h2_ref, h2)
    tm = h2.shape[0]
    h2_hi = h2.astype(BF16)
    h2_lo = (h2 - h2_hi.astype(F32)).astype(BF16)
    cross = jnp.dot(jnp.concatenate([h2_hi, h2_lo], axis=0), wr_ref[...], preferred_element_type=F32)
    logits = ((cross[:tm, :N_EXPERTS] + cross[:tm, N_EXPERTS:])
              + (cross[tm:, :N_EXPERTS] + cross[tm:, N_EXPERTS:])) + br_ref[...]
    lane = lax.broadcasted_iota(jnp.int32, logits.shape, 1).astype(F32)
    rem = logits
    sel = jnp.zeros(logits.shape, F32)
    top = None
    denom = None
    for kk in range(TOP_K):
        m = jnp.max(rem, axis=-1, keepdims=True)
        idx = jnp.min(jnp.where(rem == m, lane, float(N_EXPERTS)), axis=-1, keepdims=True)
        pick = lane == idx
        sel = jnp.where(pick, 1.0, sel)
        rem = jnp.where(pick, -jnp.inf, rem)
        if kk == 0:
            top = m
            denom = jnp.ones_like(m)
        else:
            denom = denom + jnp.exp(m - top)
    wf_ref[...] = jnp.where(sel > 0.0, jnp.exp(logits - top) / denom, 0.0)
    ms_ref[...] = sel


def _mid(o_atts, hfbs, gg, gs, xs, mod, w_pa, w_pr, w_out, ln1_g, ln1_b, w_router, b_router, lay, tm):
    nt = lay.n_tokens

    def seq_map(i):
        return (lay.seq_and_pos(i * tm)[0], 0, 0)

    tok = lambda w: pl.BlockSpec((tm, w), lambda i: (i, 0))
    vec = lambda w: _const_spec((1, w))
    row_specs, tile_ends = lay.trunk_specs(
        tm, lambda local: pl.BlockSpec((tm, D_MODEL), lambda i: (local(i), 0)))
    h_pairs, _ = lay.trunk_specs(
        tm, lambda local: [pl.BlockSpec((1, tm, D_MODEL), lambda i, d=d: (d, local(i), 0)) for d in range(2)])
    h_specs = [spec for pair in h_pairs for spec in pair]
    h_args = [hfb for hfb in hfbs for _ in range(2)]
    return pl.pallas_call(
        functools.partial(_mid_kernel, tile_ends=tile_ends),
        grid=(nt // tm,),
        in_specs=row_specs + h_specs + row_specs + [
                  tok(D_MODEL), tok(2 * D_MODEL),
                  pl.BlockSpec((1, N_MOD, D_MODEL), seq_map),
                  _const_spec((D_MODEL, D_MODEL)), _const_spec((D_MODEL, D_MODEL)),
                  _const_spec((D_MODEL, D_MODEL)),
                  vec(D_MODEL), vec(D_MODEL),
                  _const_spec((D_MODEL, 2 * N_EXPERTS)), vec(N_EXPERTS)],
        out_specs=[tok(D_MODEL), pl.BlockSpec((ROW_SUB * tm, 128), lambda i: (i, 0)),
                   tok(N_EXPERTS), tok(N_EXPERTS)],
        out_shape=[jax.ShapeDtypeStruct((nt, D_MODEL), F32),
                   jax.ShapeDtypeStruct((ROW_SUB * nt, 128), F32),
                   jax.ShapeDtypeStruct((nt, N_EXPERTS), F32),
                   jax.ShapeDtypeStruct((nt, N_EXPERTS), F32)],
        compiler_params=_cparams(("arbitrary",)),
        name="merge_router",
    )(*o_atts, *h_args, *xs, gg, gs, mod, w_pa, w_pr, w_out, ln1_g.reshape(1, -1), ln1_b.reshape(1, -1),
      w_router, b_router.reshape(1, -1))


def _rank_kernel(ms_ref, rank_ref, cnt_ref, tri_scr, run_scr, *, tp):
    @pl.when(pl.program_id(0) == 0)
    def _():
        r = lax.broadcasted_iota(jnp.int32, (tp, tp), 0)
        c = lax.broadcasted_iota(jnp.int32, (tp, tp), 1)
        tri_scr[...] = jnp.where(c < r, 1.0, 0.0).astype(BF16)
        run_scr[...] = jnp.zeros(run_scr.shape, F32)

    ms = ms_ref[...]
    before = jnp.dot(tri_scr[...], ms.astype(BF16), preferred_element_type=F32)
    rank_ref[...] = before + run_scr[...]
    run_scr[...] = run_scr[...] + jnp.sum(ms, axis=0, keepdims=True)
    cnt_ref[...] = jnp.broadcast_to(run_scr[...], cnt_ref.shape)


def _rank(msel, tp):
    nt = msel.shape[0]
    return pl.pallas_call(
        functools.partial(_rank_kernel, tp=tp),
        grid=(nt // tp,),
        in_specs=[pl.BlockSpec((tp, N_EXPERTS), lambda i: (i, 0))],
        out_specs=[pl.BlockSpec((tp, N_EXPERTS), lambda i: (i, 0)),
                   pl.BlockSpec((8, N_EXPERTS), lambda i: (0, 0))],
        out_shape=[jax.ShapeDtypeStruct((nt, N_EXPERTS), F32),
                   jax.ShapeDtypeStruct((8, N_EXPERTS), F32)],
        scratch_shapes=[pltpu.VMEM((tp, tp), BF16), pltpu.VMEM((1, N_EXPERTS), F32)],
        compiler_params=_cparams(("arbitrary",)),
        name="route_rank",
    )(msel)


def _slots_kernel(rank_ref, ms_ref, wf_ref, start_ref, slot_ref, wt_ref):
    ms = ms_ref[...]
    wf = wf_ref[...]
    slot_full = rank_ref[...] + start_ref[...]
    lane = lax.broadcasted_iota(jnp.int32, ms.shape, 1).astype(F32)
    out_lane = lax.broadcasted_iota(jnp.int32, slot_ref.shape, 1)
    slots = jnp.zeros(slot_ref.shape, F32)
    wts = jnp.zeros(wt_ref.shape, F32)
    rem = ms
    for kk in range(TOP_K):
        idx = jnp.min(jnp.where(rem > 0.0, lane, float(2 * N_EXPERTS)), axis=-1, keepdims=True)
        pick = lane == idx
        s_k = jnp.sum(jnp.where(pick, slot_full, 0.0), axis=-1, keepdims=True)
        w_k = jnp.sum(jnp.where(pick, wf, 0.0), axis=-1, keepdims=True)
        rem = jnp.where(pick, 0.0, rem)
        slots = jnp.where(out_lane == kk, s_k, slots)
        wts = jnp.where(out_lane == kk, w_k, wts)
    slot_ref[...] = slots.astype(jnp.int32)
    wt_ref[...] = wts


def _slots(rank, msel, wfull, pad_start, tp):
    nt = msel.shape[0]
    tok = pl.BlockSpec((tp, N_EXPERTS), lambda i: (i, 0))
    out = pl.BlockSpec((tp, 128), lambda i: (i, 0))
    return pl.pallas_call(
        _slots_kernel,
        grid=(nt // tp,),
        in_specs=[tok, tok, tok, pl.BlockSpec((1, N_EXPERTS), lambda i: (0, 0))],
        out_specs=[out, out],
        out_shape=[jax.ShapeDtypeStruct((nt, 128), jnp.int32),
                   jax.ShapeDtypeStruct((nt, 128), F32)],
        compiler_params=_cparams(("arbitrary",)),
        name="route_slots",
    )(rank, msel, wfull, pad_start)


def _row_view(ref, row):
    return ref.at[pl.ds(pl.multiple_of(row * ROW_SUB, ROW_SUB), ROW_SUB), :]


def _dispatch_kernel(slot_ref, tail_ref, h_ref, xs_ref, zero_scr, sem, zsem, *, tg, tb):
    @pl.when(pl.program_id(0) == 0)
    def _():
        zero_scr[...] = jnp.zeros(zero_scr.shape, F32)
        for e in range(N_EXPERTS):
            tail = pl.multiple_of(tail_ref[e] * ROW_SUB, tb * ROW_SUB)
            pltpu.make_async_copy(zero_scr, xs_ref.at[pl.ds(tail, tb * ROW_SUB), :], zsem).start()
        for e in range(N_EXPERTS):
            pltpu.make_async_copy(zero_scr, xs_ref.at[pl.ds(0, tb * ROW_SUB), :], zsem).wait()

    def issue(t, carry):
        for kk in range(TOP_K):
            pltpu.make_async_copy(_row_view(h_ref, t), _row_view(xs_ref, slot_ref[t * TOP_K + kk]),
                                  sem).start(priority=kk % 2)
        return carry

    lax.fori_loop(0, tg, issue, 0, unroll=2)

    for kk in range(TOP_K):
        pltpu.make_async_copy(h_ref, xs_ref.at[pl.ds(0, tg * ROW_SUB), :], sem).wait()


def _dispatch(slots_flat, tail_start, h2r, cap, tg, tb):
    nt = h2r.shape[0] // ROW_SUB
    return pl.pallas_call(
        functools.partial(_dispatch_kernel, tg=tg, tb=tb),
        grid=(nt // tg,),
        in_specs=[pl.BlockSpec((tg * TOP_K,), lambda i: (i,), memory_space=pltpu.SMEM),
                  pl.BlockSpec(memory_space=pltpu.SMEM),
                  pl.BlockSpec((tg * ROW_SUB, 128), lambda i: (i, 0))],
        out_specs=pl.BlockSpec(memory_space=pl.ANY),
        out_shape=jax.ShapeDtypeStruct((cap * ROW_SUB, 128), F32),
        scratch_shapes=[pltpu.VMEM((tb * ROW_SUB, 128), F32), pltpu.SemaphoreType.DMA(()),
                        pltpu.SemaphoreType.DMA(())],
        compiler_params=_cparams(("arbitrary",)),
        name="moe_dispatch",
    )(slots_flat, tail_start, h2r)


def _expert_kernel(be_ref, bv_ref, xs_ref, w1_ref, b1_ref, w2_ref, b2_ref, ys_ref, w1_scr, w2_scr, *, tb):
    i = pl.program_id(0)
    valid = bv_ref[i]

    @pl.when((i == 0) | (be_ref[i] != be_ref[jnp.maximum(i - 1, 0)]))
    def _():
        w1_scr[...] = w1_ref[0].astype(BF16)
        w2_scr[...] = w2_ref[0].astype(BF16)

    @pl.when(valid > 0)
    def _():
        xb = _load_row_tiles(xs_ref, tb).astype(BF16)
        gu = jnp.dot(xb, w1_scr[...], preferred_element_type=F32) + b1_ref[0]
        glu = jnp.minimum(gu[:, :D_FF], SWIGLU_LIMIT)
        lin = jnp.clip(gu[:, D_FF:], -SWIGLU_LIMIT, SWIGLU_LIMIT)
        act = (lin + 1.0) * glu * _sigmoid(SWIGLU_ALPHA * glu)
        y = jnp.dot(act.astype(BF16), w2_scr[...], preferred_element_type=F32) + b2_ref[0]
        _store_row_tiles(ys_ref, y)

    @pl.when(valid <= 0)
    def _():
        ys_ref[...] = jnp.zeros(ys_ref.shape, F32)


def _experts(block_expert, block_valid, xs, w1, b1, w2, b2, tb):
    cap = xs.shape[0] // ROW_SUB
    rows = pl.BlockSpec((tb * ROW_SUB, 128), lambda i, be, bv: (i, 0))
    grid_spec = pltpu.PrefetchScalarGridSpec(
        num_scalar_prefetch=2,
        grid=(cap // tb,),
        in_specs=[rows,
                  pl.BlockSpec((1, D_MODEL, 2 * D_FF), lambda i, be, bv: (be[i], 0, 0)),
                  pl.BlockSpec((1, 1, 2 * D_FF), lambda i, be, bv: (be[i], 0, 0)),
                  pl.BlockSpec((1, D_FF, D_MODEL), lambda i, be, bv: (be[i], 0, 0)),
                  pl.BlockSpec((1, 1, D_MODEL), lambda i, be, bv: (be[i], 0, 0))],
        out_specs=rows,
        scratch_shapes=[pltpu.VMEM((D_MODEL, 2 * D_FF), BF16), pltpu.VMEM((D_FF, D_MODEL), BF16)],
    )
    return pl.pallas_call(
        functools.partial(_expert_kernel, tb=tb),
        grid_spec=grid_spec,
        out_shape=jax.ShapeDtypeStruct((cap * ROW_SUB, 128), F32),
        compiler_params=_cparams(("arbitrary",)),
        name="moe_experts",
    )(block_expert, block_valid, xs, w1, b1.reshape(N_EXPERTS, 1, -1), w2, b2.reshape(N_EXPERTS, 1, -1))


def _combine_kernel(slot_ref, next_slot_ref, ys_ref, wt_ref, x1_ref, mod_ref, g_ref, b_ref, *rest,
                    tc, tile_ends):
    y_refs = rest[:len(tile_ends)]
    buf, sem = rest[len(tile_ends):]
    i = pl.program_id(0)
    cur = i % 2

    def row_copy(slots, t, kk, half):
        return pltpu.make_async_copy(_row_view(ys_ref, slots[t * TOP_K + kk]),
                                     _row_view(buf.at[half, kk], t), sem.at[half])

    def gather(slots, half):
        def issue(t, carry):
            for kk in range(TOP_K):
                row_copy(slots, t, kk, half).start(priority=kk % 2)
            return carry
        lax.fori_loop(0, tc, issue, 0, unroll=8)

    @pl.when(i == 0)
    def _():
        gather(slot_ref, 0)

    @pl.when(i + 1 < pl.num_programs(0))
    def _():
        gather(next_slot_ref, 1 - cur)

    for kk in range(TOP_K):
        pltpu.make_async_copy(ys_ref.at[pl.ds(0, tc * ROW_SUB), :], buf.at[cur, kk], sem.at[cur]).wait()
    wt = wt_ref[...]
    ff = sum(wt[:, kk:kk + 1] * _load_row_tiles(buf.at[cur, kk], tc) for kk in range(TOP_K))
    g2 = mod_ref[0, 5:6, :]
    y = _layer_norm(DN_ALPHA * x1_ref[...] + (1.0 + g2) * ff) * g_ref[...] + b_ref[...]
    i = pl.program_id(0)
    start = 0
    for y_ref, end in zip(y_refs, tile_ends):
        @pl.when((i >= start) & (i < end))
        def _(y_ref=y_ref):
            y_ref[...] = y
        start = end


def _combine(slots_flat, ys, wts, x1, mod, ln2_g, ln2_b, lay, tc):
    nt = lay.n_tokens

    def seq_map(i):
        return (lay.seq_and_pos(i * tc)[0], 0, 0)

    out_specs, tile_ends = lay.trunk_specs(
        tc, lambda local: pl.BlockSpec((tc, D_MODEL), lambda i: (local(i), 0)))
    n_steps = nt // tc
    return pl.pallas_call(
        functools.partial(_combine_kernel, tc=tc, tile_ends=tile_ends),
        grid=(n_steps,),
        in_specs=[pl.BlockSpec((tc * TOP_K,), lambda i: (i,), memory_space=pltpu.SMEM),
                  pl.BlockSpec((tc * TOP_K,), lambda i: (jnp.minimum(i + 1, n_steps - 1),),
                               memory_space=pltpu.SMEM),
                  pl.BlockSpec(memory_space=pl.ANY),
                  pl.BlockSpec((tc, 128), lambda i: (i, 0)),
                  pl.BlockSpec((tc, D_MODEL), lambda i: (i, 0)),
                  pl.BlockSpec((1, N_MOD, D_MODEL), seq_map),
                  _const_spec((1, D_MODEL)), _const_spec((1, D_MODEL))],
        out_specs=out_specs,
        out_shape=[jax.ShapeDtypeStruct((b * s, D_MODEL), F32) for b, s in lay.trunks],
        scratch_shapes=[pltpu.VMEM((2, TOP_K, tc * ROW_SUB, 128), F32), pltpu.SemaphoreType.DMA((2,))],
        compiler_params=_cparams(("arbitrary",)),
        name="moe_combine",
    )(slots_flat, slots_flat, ys, wts, x1, mod, ln2_g.reshape(1, -1), ln2_b.reshape(1, -1))


def _encoder_layer(xs, cs, p):
    trunks = [(x.shape[0], x.shape[1]) for x in xs]
    lay = _Layout(trunks)
    nt = lay.n_tokens
    t = _tiles(nt)

    x2d = [xi.reshape(-1, D_MODEL) for xi in xs]
    c = jnp.concatenate(cs, axis=0)
    bp = -(-lay.n_seqs // 8) * 8
    c = jnp.pad(c, ((0, bp - lay.n_seqs), (0, 0)))
    mod = _ada_mod(c, p["w_ada"], p["b_ada"]).reshape(bp, N_MOD, D_MODEL)

    q, k, vt3, xr, gg, gs = _inproj(x2d, mod, p["w_in"].astype(BF16), p["q_gain"], p["k_gain"],
                                    _rope_tables(lay.max_seq), lay, t["tm_in"])

    w_lru = jnp.concatenate([p["lru_wa"], p["lru_wx"]], axis=-1).astype(BF16)
    b_lru = jnp.stack([p["lru_ba"], p["lru_bx"]], axis=1)
    lam = p["lru_lam"].reshape(2, 1, D_MODEL)
    o_atts, hfbs = [], []
    tok_off = 0
    for b, s in trunks:
        o_atts.append(_attention(q, k, vt3, tok_off, b, s, t["tq"], t["tm_in"]))
        hfbs.append(_scan(xr, p["conv_w"], p["conv_b"].reshape(1, -1), w_lru, b_lru, lam,
                          tok_off, b, s, t["tt"]))
        tok_off += b * s

    wr_hi = p["w_router"].astype(BF16)
    wr_lo = (p["w_router"] - wr_hi.astype(F32)).astype(BF16)
    x1, h2r, wfull, msel = _mid(o_atts, hfbs, gg, gs, x2d, mod, p["w_pa"].astype(BF16),
                               p["w_pr"].astype(BF16), p["w_out"].astype(BF16), p["ln1_g"], p["ln1_b"],
                               jnp.concatenate([wr_hi, wr_lo], axis=1), p["b_router"], lay, t["tm_mid"])

    tb = t["tb"]
    rank, cnt = _rank(msel, t["tp"])
    counts = cnt[0].astype(jnp.int32)
    padded = (counts + tb - 1) // tb * tb
    pad_end = jnp.cumsum(padded)
    pad_start = pad_end - padded
    n_blocks = nt * TOP_K // tb + N_EXPERTS
    blk0 = jnp.arange(n_blocks, dtype=jnp.int32) * tb
    block_expert = jnp.minimum(jnp.sum(pad_end[None, :] <= blk0[:, None], axis=1), N_EXPERTS - 1).astype(jnp.int32)
    block_valid = jnp.clip(pad_start[block_expert] + counts[block_expert] - blk0, 0, tb).astype(jnp.int32)
    slots, wts = _slots(rank, msel, wfull, pad_start.astype(F32).reshape(1, N_EXPERTS), t["tp"])
    slots_flat = slots[:, :TOP_K].reshape(-1)

    tail_start = jnp.where(padded > 0, pad_end - tb, pad_end[-1] - tb).astype(jnp.int32)
    xs_rows = _dispatch(slots_flat, tail_start, h2r, n_blocks * tb, t["tg"], tb)
    ys = _experts(block_expert, block_valid, xs_rows, p["w1"], p["b1"], p["w2"], p["b2"], tb)
    ys_out = _combine(slots_flat, ys, wts, x1, mod, p["ln2_g"], p["ln2_b"], lay, t["tc"])
    return [y.reshape(b, s, D_MODEL) for y, (b, s) in zip(ys_out, trunks)]


_PARAM_NAMES = ("w_ada", "b_ada", "w_in", "q_gain", "k_gain", "conv_w", "conv_b", "lru_wa", "lru_ba",
                "lru_wx", "lru_bx", "lru_lam", "w_pa", "w_pr", "w_out", "ln1_g", "ln1_b", "w_router",
                "b_router", "w1", "b1", "w2", "b2", "ln2_g", "ln2_b")


def kernel(x_prompt, x_sample, c_prompt, c_sample, w_ada, b_ada, w_in, q_gain, k_gain, conv_w, conv_b, lru_wa, lru_ba, lru_wx, lru_bx, lru_lam, w_pa, w_pr, w_out, ln1_g, ln1_b, w_router, b_router, w1, b1, w2, b2, ln2_g, ln2_b):
    stacked = (w_ada, b_ada, w_in, q_gain, k_gain, conv_w, conv_b, lru_wa, lru_ba, lru_wx, lru_bx,
               lru_lam, w_pa, w_pr, w_out, ln1_g, ln1_b, w_router, b_router, w1, b1, w2, b2, ln2_g, ln2_b)
    xs, cs = [x_prompt, x_sample], [c_prompt, c_sample]
    for layer in range(DEPTH):
        p = {name: arr[layer] for name, arr in zip(_PARAM_NAMES, stacked)}
        xs = _encoder_layer(xs, cs, p)
    return (xs[0], xs[1])
```

```python
import functools
import math

import jax
import jax.numpy as jnp
from jax import lax
from jax.experimental import pallas as pl
from jax.experimental.pallas import tpu as pltpu

F32 = jnp.float32
BF16 = jnp.bfloat16

D_MODEL = 1024
GRID_W = 64
N_HEADS = 8
N_KV_HEADS = 2
HEAD_DIM = 128
GROUPS = N_HEADS // N_KV_HEADS
KV_WIDTH = N_KV_HEADS * HEAD_DIM
ROPE_THETA = 10000.0
RNN_BLOCKS = 8
RNN_BLOCK_W = D_MODEL // RNN_BLOCKS
CONV_W = 4
LRU_C = 8.0
N_EXPERTS = 32
TOP_K = 4
D_FF = D_MODEL
SWIGLU_LIMIT = 7.0
SWIGLU_ALPHA = 1.702
DEPTH = 1
DN_ALPHA = (2 * DEPTH) ** 0.25
LN_EPS = 1e-5
RMS_EPS = 1e-6
N_MOD = 6
IN_WIDTH = D_MODEL + 2 * KV_WIDTH + 2 * D_MODEL + 2 * D_MODEL
_Q0, _K0, _V0, _XR0, _GR0, _GL0 = 0, 1024, 1280, 1536, 2560, 3584

V7X_VMEM_LIMIT_BYTES = 56 * 1024 * 1024
HALO = 16
VT_ROWS = HEAD_DIM + 16


def _tiles(n_tokens):
    big = n_tokens >= 4096
    return dict(
        tm_in=512 if big else 128,
        tq=256 if big else 128,
        tt=512 if big else 128,
        tm_mid=512 if big else 128,
        tp=512 if big else 128,
        tg=1024 if big else 256,
        tb=512 if big else 128,
        tc=512 if big else 256,
    )


def _cparams(sem):
    return pltpu.CompilerParams(dimension_semantics=sem, vmem_limit_bytes=V7X_VMEM_LIMIT_BYTES)


def _const_spec(shape):
    nd = len(shape)
    return pl.BlockSpec(shape, lambda *_: (0,) * nd, pipeline_mode=pl.Buffered(1))


def _layer_norm(x):
    mu = jnp.mean(x, axis=-1, keepdims=True)
    xc = x - mu
    var = jnp.mean(xc * xc, axis=-1, keepdims=True)
    return xc * lax.rsqrt(var + LN_EPS)


def _sigmoid(x):
    return 1.0 / (1.0 + jnp.exp(-x))


ROW_SUB = D_MODEL // 128


def _store_row_tiles(ref, val):
    n = val.shape[0]
    for j in range(ROW_SUB):
        ref[pl.ds(j, n, stride=ROW_SUB), :] = val[:, j * 128:(j + 1) * 128]


def _load_row_tiles(ref, n):
    return jnp.concatenate([ref[pl.ds(j, n, stride=ROW_SUB), :] for j in range(ROW_SUB)], axis=1)


def _ada_kernel(c_ref, w_ref, b_ref, o_ref):
    c = c_ref[...]
    s = c * _sigmoid(c)
    o_ref[...] = jnp.dot(s, w_ref[...], preferred_element_type=F32,
                         precision=lax.Precision.HIGHEST) + b_ref[...]


def _ada_mod(c_all, w_ada, b_ada):
    bp = c_all.shape[0]
    ncol = w_ada.shape[1]
    return pl.pallas_call(
        _ada_kernel,
        grid=(ncol // D_MODEL,),
        in_specs=[pl.BlockSpec((bp, D_MODEL), lambda j: (0, 0)),
                  pl.BlockSpec((D_MODEL, D_MODEL), lambda j: (0, j)),
                  pl.BlockSpec((1, D_MODEL), lambda j: (0, j))],
        out_specs=pl.BlockSpec((bp, D_MODEL), lambda j: (0, j)),
        out_shape=jax.ShapeDtypeStruct((bp, ncol), F32),
        compiler_params=_cparams(("arbitrary",)),
        name="ada_mod",
    )(c_all, w_ada, b_ada.reshape(1, ncol))


class _Layout:
    def __init__(self, trunks):
        self.trunks = tuple(trunks)
        self.n_tokens = sum(b * s for b, s in trunks)
        self.n_seqs = sum(b for b, _ in trunks)
        self.max_seq = max(s for _, s in trunks)

    def seq_and_pos(self, t0):
        seq = jnp.int32(0)
        pos = jnp.int32(0)
        tok_off, seq_off = 0, 0
        for b, s in self.trunks:
            inside = (t0 >= tok_off) & (t0 < tok_off + b * s)
            rel = jnp.maximum(t0 - tok_off, 0)
            seq = jnp.where(inside, seq_off + rel // s, seq)
            pos = jnp.where(inside, rel % s, pos)
            tok_off += b * s
            seq_off += b
        return seq, pos

    def trunk_specs(self, tm, make_spec):
        specs, ends, t0 = [], [], 0
        for b, s in self.trunks:
            n_t = b * s // tm
            specs.append(make_spec(functools.partial(_clamped_local, t0=t0, n_t=n_t)))
            t0 += n_t
            ends.append(t0)
        return specs, tuple(ends)


def _clamped_local(i, *, t0, n_t):
    return jnp.clip(i - t0, 0, n_t - 1)


def _owner_value(loads, i, tile_ends):
    val = loads[-1]()
    for j in range(len(loads) - 2, -1, -1):
        val = jnp.where(i < tile_ends[j], loads[j](), val)
    return val


def _rope_tables(max_seq):
    t = jnp.arange(max_seq)
    rows = (t // GRID_W).astype(F32)
    cols = (t % GRID_W).astype(F32)
    axis_dim = HEAD_DIM // 2
    inv = ROPE_THETA ** (-jnp.arange(0, axis_dim, 2, dtype=F32) / axis_dim)
    ar, ac = rows[:, None] * inv, cols[:, None] * inv
    cr, sr, cc, sc = jnp.cos(ar), jnp.sin(ar), jnp.cos(ac), jnp.sin(ac)
    z = jnp.zeros_like(sr)
    cos_t = jnp.concatenate([cr, cr, cc, cc], axis=-1)
    up_t = jnp.concatenate([-sr, z, -sc, z], axis=-1)
    dn_t = jnp.concatenate([z, sr, z, sc], axis=-1)
    return cos_t, up_t, dn_t


def _inproj_kernel(*refs, tile_ends):
    n_tr = len(tile_ends)
    x_refs = refs[:n_tr]
    (mod_ref, w_ref, qg_ref, kg_ref, cos_ref, up_ref, dn_ref,
     q_ref, k_ref, vt_ref, xr_ref, gg_ref, gs_ref) = refs[n_tr:]
    x = _owner_value([lambda r=r: r[...] for r in x_refs], pl.program_id(0), tile_ends)
    sh1 = mod_ref[0, 0:1, :]
    sc1 = mod_ref[0, 1:2, :]
    h = (_layer_norm(x) * (1.0 + sc1) + sh1).astype(BF16)
    cos_t, up_t, dn_t = cos_ref[...], up_ref[...], dn_ref[...]

    def proj(c0, width):
        return jnp.dot(h, w_ref[:, c0:c0 + width], preferred_element_type=F32)

    def norm_rope(z, gain):
        ms = jnp.mean(z * z, axis=-1, keepdims=True)
        y = z * lax.rsqrt(ms + RMS_EPS) * gain
        return (y * cos_t + pltpu.roll(y, HEAD_DIM - 32, 1) * up_t
                + pltpu.roll(y, 32, 1) * dn_t)

    zq = proj(_Q0, D_MODEL)
    qg = qg_ref[...] * (HEAD_DIM ** -0.5 * math.log2(math.e))
    for hd in range(N_HEADS):
        sl = slice(hd * HEAD_DIM, (hd + 1) * HEAD_DIM)
        q_ref[:, sl] = norm_rope(zq[:, sl], qg).astype(BF16)
    zk = proj(_K0, KV_WIDTH)
    kg = kg_ref[...]
    for hd in range(N_KV_HEADS):
        sl = slice(hd * HEAD_DIM, (hd + 1) * HEAD_DIM)
        k_ref[:, sl] = norm_rope(zk[:, sl], kg).astype(BF16)
    zvt = proj(_V0, KV_WIDTH).T.astype(BF16)
    pad_row = lax.broadcasted_iota(jnp.int32, (VT_ROWS - HEAD_DIM, zvt.shape[1]), 0)
    ones_pad = jnp.where(pad_row == 0, 1.0, 0.0).astype(BF16)
    for hd in range(N_KV_HEADS):
        vt_ref[0, hd * VT_ROWS:hd * VT_ROWS + HEAD_DIM, :] = zvt[hd * HEAD_DIM:(hd + 1) * HEAD_DIM]
        vt_ref[0, hd * VT_ROWS + HEAD_DIM:(hd + 1) * VT_ROWS, :] = ones_pad
    xr_ref[...] = proj(_XR0, D_MODEL).astype(BF16)
    gg_ref[...] = jax.nn.gelu(proj(_GR0, D_MODEL), approximate=True).astype(BF16)
    gs_ref[...] = _sigmoid(proj(_GL0, 2 * D_MODEL)).astype(BF16)


def _inproj(xs, mod, w_in, q_gain, k_gain, tables, lay, tm):
    nt = lay.n_tokens
    cos_t, up_t, dn_t = tables

    def seq_map(i):
        return (lay.seq_and_pos(i * tm)[0], 0, 0)

    def pos_map(i):
        return (lay.seq_and_pos(i * tm)[1] // tm, 0)

    tok = lambda w: pl.BlockSpec((tm, w), lambda i: (i, 0))
    rope = pl.BlockSpec((tm, HEAD_DIM), pos_map)
    x_specs, tile_ends = lay.trunk_specs(
        tm, lambda local: pl.BlockSpec((tm, D_MODEL), lambda i: (local(i), 0)))
    return pl.pallas_call(
        functools.partial(_inproj_kernel, tile_ends=tile_ends),
        grid=(nt // tm,),
        in_specs=x_specs + [
                  pl.BlockSpec((1, N_MOD, D_MODEL), seq_map),
                  _const_spec((D_MODEL, IN_WIDTH)),
                  _const_spec((1, HEAD_DIM)), _const_spec((1, HEAD_DIM)),
                  rope, rope, rope],
        out_specs=[tok(D_MODEL), tok(KV_WIDTH),
                   pl.BlockSpec((1, N_KV_HEADS * VT_ROWS, tm), lambda i: (i, 0, 0)),
                   tok(D_MODEL), tok(D_MODEL), tok(2 * D_MODEL)],
        out_shape=[jax.ShapeDtypeStruct((nt, D_MODEL), BF16),
                   jax.ShapeDtypeStruct((nt, KV_WIDTH), BF16),
                   jax.ShapeDtypeStruct((nt // tm, N_KV_HEADS * VT_ROWS, tm), BF16),
                   jax.ShapeDtypeStruct((nt, D_MODEL), BF16),
                   jax.ShapeDtypeStruct((nt, D_MODEL), BF16),
                   jax.ShapeDtypeStruct((nt, 2 * D_MODEL), BF16)],
        compiler_params=_cparams(("arbitrary",)),
        name="in_proj",
    )(*xs, mod, w_in, q_gain.reshape(1, HEAD_DIM), k_gain.reshape(1, HEAD_DIM), cos_t, up_t, dn_t)


def _attn_kernel(q_ref, k_ref, vt_ref, o_ref, qt_scr, s_scr, m_scr, acc_scr, *,
                 tq, tk, n_kv, group):
    for g in range(GROUPS):
        qg = q_ref[:, g * HEAD_DIM:(g + 1) * HEAD_DIM].astype(F32)
        qt_scr[:, g * tq:(g + 1) * tq] = qg.T.astype(BF16)
    m_scr[...] = jnp.full(m_scr.shape, -jnp.inf, F32)
    acc_scr[...] = jnp.zeros(acc_scr.shape, F32)

    def scores(j, slot):
        kt = k_ref[pl.ds(pl.multiple_of(j * tk, tk), tk), :]
        s_scr[slot] = jnp.dot(kt, qt_scr[...], preferred_element_type=F32)

    def accumulate(j, slot):
        s = s_scr[slot]
        m_old = m_scr[...]
        m_new = jnp.maximum(m_old, jnp.max(s, axis=0, keepdims=True))
        alpha = jnp.exp2(m_old - m_new)
        p = jnp.exp2(s - m_new).astype(BF16)
        pv = jnp.dot(vt_ref[j], p, preferred_element_type=F32)
        acc_scr[...] = alpha * acc_scr[...] + pv
        m_scr[...] = m_new

    scores(0, 0)

    def body(i, carry):
        j = group * i
        for u in range(group):
            scores(jnp.minimum(j + u + 1, n_kv - 1), (u + 1) % 2)
            accumulate(j + u, u % 2)
        return carry

    lax.fori_loop(0, n_kv // group, body, 0)
    out = acc_scr[0:HEAD_DIM, :] / acc_scr[HEAD_DIM:HEAD_DIM + 1, :]
    for g in range(GROUPS):
        o_ref[:, g * HEAD_DIM:(g + 1) * HEAD_DIM] = out[:, g * tq:(g + 1) * tq].T.astype(BF16)


def _attention(q, k, vt3, tok_off, batch, seq, tq, tk):
    n_kv = seq // tk
    group = next((g for g in (8, 4) if n_kv % g == 0 and n_kv // g >= 2), 2)
    assert n_kv % group == 0
    qrow0 = tok_off // tq
    srow0 = tok_off // seq
    gw = GROUPS * HEAD_DIM
    n_q = seq // tq
    return pl.pallas_call(
        functools.partial(_attn_kernel, tq=tq, tk=tk, n_kv=n_kv, group=group),
        grid=(batch, N_KV_HEADS, n_q),
        in_specs=[pl.BlockSpec((tq, gw), lambda b, h, i: (qrow0 + b * n_q + i, h)),
                  pl.BlockSpec((seq, HEAD_DIM), lambda b, h, i: (srow0 + b, h)),
                  pl.BlockSpec((n_kv, VT_ROWS, tk), lambda b, h, i: (srow0 + b, h, 0))],
        out_specs=pl.BlockSpec((tq, gw), lambda b, h, i: (b * n_q + i, h)),
        out_shape=jax.ShapeDtypeStruct((batch * seq, D_MODEL), BF16),
        scratch_shapes=[pltpu.VMEM((HEAD_DIM, GROUPS * tq), BF16),
                        pltpu.VMEM((2, tk, GROUPS * tq), F32),
                        pltpu.VMEM((1, GROUPS * tq), F32),
                        pltpu.VMEM((VT_ROWS, GROUPS * tq), F32)],
        compiler_params=_cparams(("arbitrary", "arbitrary", "arbitrary")),
        name="attention",
    )(q, k, vt3)


def _scan_kernel(cur_ref, prev_ref, next_ref, cw_ref, cb_ref, w_ref, b_ref, lam_ref, o_ref,
                 xc_scr, a_scr, u_scr, h_scr, carry_scr, *, tt, n_chunks):
    d = pl.program_id(1)
    c = pl.program_id(2)
    chunk = jnp.where(d == 0, c, n_chunks - 1 - c)

    @pl.when(c == 0)
    def _():
        carry_scr[...] = jnp.zeros(carry_scr.shape, F32)

    keep_prev = jnp.where(chunk == 0, 0.0, 1.0)
    keep_next = jnp.where(chunk == n_chunks - 1, 0.0, 1.0)
    cur = cur_ref[...].astype(F32)
    taps = [cw_ref[j:j + 1, :] for j in range(CONV_W)]
    xc_scr[...] = (cb_ref[...] + taps[2] * cur + taps[1] * pltpu.roll(cur, 1, 0)
                   + taps[0] * pltpu.roll(cur, 2, 0) + taps[3] * pltpu.roll(cur, tt - 1, 0))
    head = jnp.concatenate([prev_ref[...].astype(F32)[HALO - 8:HALO] * keep_prev, cur[0:16]], axis=0)
    tail = jnp.concatenate([cur[tt - 16:tt], next_ref[...].astype(F32)[0:8] * keep_next], axis=0)
    xc_scr[0:8, :] = cb_ref[...] + sum(taps[j] * head[6 + j:14 + j] for j in range(CONV_W))
    xc_scr[tt - 8:tt, :] = cb_ref[...] + sum(taps[j] * tail[6 + j:14 + j] for j in range(CONV_W))
    xc = xc_scr[...]
    xcb = xc.astype(BF16)

    lam = lam_ref[0]
    y = jnp.exp(-jnp.abs(lam))
    w1p = 1.0 + y
    log1p_y = jnp.where(w1p == 1.0, y, jnp.log(w1p) * y / jnp.where(w1p == 1.0, 1.0, w1p - 1.0))
    neg_c_sp = (-LRU_C * math.log2(math.e)) * (jnp.maximum(-lam, 0.0) + log1p_y)

    for n in range(RNN_BLOCKS):
        sl = slice(n * RNN_BLOCK_W, (n + 1) * RNN_BLOCK_W)
        pre = jnp.dot(xcb[:, sl], w_ref[0, n], preferred_element_type=F32)
        r = _sigmoid(pre[:, :RNN_BLOCK_W] + b_ref[0, 0:1, sl])
        i = _sigmoid(pre[:, RNN_BLOCK_W:] + b_ref[0, 1:2, sl])
        a = jnp.exp2(r * neg_c_sp[:, sl])
        a_scr[:, sl] = a
        u_scr[:, sl] = jnp.sqrt(1.0 - a * a) * (i * xc[:, sl])

    def step(t, h):
        row = jnp.where(d == 0, t, tt - 1 - t)
        h = a_scr[pl.ds(row, 1), :] * h + u_scr[pl.ds(row, 1), :]
        h_scr[pl.ds(row, 1), :] = h
        return h

    carry_scr[...] = lax.fori_loop(0, tt, step, carry_scr[...], unroll=8)
    o_ref[0] = h_scr[...].astype(BF16)


def _scan(xr, conv_w, conv_b, w_lru, b_lru, lam, tok_off, batch, seq, tt):
    nt = xr.shape[0]
    n_chunks = seq // tt
    row0 = tok_off // tt
    hrow0 = tok_off // HALO
    hpc = tt // HALO
    n_halo = nt // HALO

    def chunk_of(d, c):
        return jnp.where(d == 0, c, n_chunks - 1 - c)

    def prev_map(b, d, c):
        return (jnp.maximum(hrow0 + (b * n_chunks + chunk_of(d, c)) * hpc - 1, 0), 0)

    def next_map(b, d, c):
        return (jnp.minimum(hrow0 + (b * n_chunks + chunk_of(d, c) + 1) * hpc, n_halo - 1), 0)

    return pl.pallas_call(
        functools.partial(_scan_kernel, tt=tt, n_chunks=n_chunks),
        grid=(batch, 2, n_chunks),
        in_specs=[pl.BlockSpec((tt, D_MODEL), lambda b, d, c: (row0 + b * n_chunks + chunk_of(d, c), 0)),
                  pl.BlockSpec((HALO, D_MODEL), prev_map),
                  pl.BlockSpec((HALO, D_MODEL), next_map),
                  pl.BlockSpec((CONV_W, D_MODEL), lambda b, d, c: (0, 0)),
                  pl.BlockSpec((1, D_MODEL), lambda b, d, c: (0, 0)),
                  pl.BlockSpec((1, RNN_BLOCKS, RNN_BLOCK_W, 2 * RNN_BLOCK_W), lambda b, d, c: (d, 0, 0, 0)),
                  pl.BlockSpec((1, 2, D_MODEL), lambda b, d, c: (d, 0, 0)),
                  pl.BlockSpec((1, 1, D_MODEL), lambda b, d, c: (d, 0, 0))],
        out_specs=pl.BlockSpec((1, tt, D_MODEL), lambda b, d, c: (d, b * n_chunks + chunk_of(d, c), 0)),
        out_shape=jax.ShapeDtypeStruct((2, batch * seq, D_MODEL), BF16),
        scratch_shapes=[pltpu.VMEM((tt, D_MODEL), F32),
                        pltpu.VMEM((tt, D_MODEL), F32),
                        pltpu.VMEM((tt, D_MODEL), F32),
                        pltpu.VMEM((tt, D_MODEL), F32),
                        pltpu.VMEM((1, D_MODEL), F32)],
        compiler_params=_cparams(("arbitrary", "arbitrary", "arbitrary")),
        name="lru_scan",
    )(xr, xr, xr, conv_w, conv_b, w_lru, b_lru, lam)


def _mid_kernel(*refs, tile_ends):
    n_tr = len(tile_ends)
    oa_refs = refs[:n_tr]
    h_refs = refs[n_tr:3 * n_tr]
    x_refs = refs[3 * n_tr:4 * n_tr]
    (gg_ref, gs_ref, mod_ref, wpa_ref, wpr_ref, wo_ref, g1_ref, b1_ref, wr_ref, br_ref,
     x1_ref, h2_ref, wf_ref, ms_ref) = refs[4 * n_tr:]
    i = pl.program_id(0)
    oa = _owner_value([lambda r=r: r[...] for r in oa_refs], i, tile_ends)
    h_fwd = _owner_value([lambda r=h_refs[2 * j]: r[0] for j in range(n_tr)], i, tile_ends)
    h_bwd = _owner_value([lambda r=h_refs[2 * j + 1]: r[0] for j in range(n_tr)], i, tile_ends)
    hsum = h_fwd.astype(F32) + h_bwd.astype(F32)
    x_in = _owner_value([lambda r=r: r[...] for r in x_refs], i, tile_ends)
    o_att = jnp.dot(oa, wpa_ref[...], preferred_element_type=F32)
    rec = hsum.astype(BF16) * gg_ref[...]
    o_rec = jnp.dot(rec, wpr_ref[...], preferred_element_type=F32)
    gs = gs_ref[...].astype(F32)
    merged = gs[:, :D_MODEL] * o_att + gs[:, D_MODEL:] * o_rec
    mix = jnp.dot(merged.astype(BF16), wo_ref[...], preferred_element_type=F32)
    g1 = mod_ref[0, 2:3, :]
    sh2 = mod_ref[0, 3:4, :]
    sc2 = mod_ref[0, 4:5, :]
    x1 = _layer_norm(DN_ALPHA * x_in + (1.0 + g1) * mix) * g1_ref[...] + b1_ref[...]
    x1_ref[...] = x1
    h2 = _layer_norm(x1) * (1.0 + sc2) + sh2
    _store_row_tiles(h2_ref, h2)
    tm = h2.shape[0]
    h2_hi = h2.astype(BF16)
    h2_lo = (h2 - h2_hi.astype(F32)).astype(BF16)
    cross = jnp.dot(jnp.concatenate([h2_hi, h2_lo], axis=0), wr_ref[...], preferred_element_type=F32)
    logits = ((cross[:tm, :N_EXPERTS] + cross[:tm, N_EXPERTS:])
              + (cross[tm:, :N_EXPERTS] + cross[tm:, N_EXPERTS:])) + br_ref[...]
    lane = lax.broadcasted_iota(jnp.int32, logits.shape, 1).astype(F32)
    rem = logits
    sel = jnp.zeros(logits.shape, F32)
    top = None
    denom = None
    for kk in range(TOP_K):
        m = jnp.max(rem, axis=-1, keepdims=True)
        idx = jnp.min(jnp.where(rem == m, lane, float(N_EXPERTS)), axis=-1, keepdims=True)
        pick = lane == idx
        sel = jnp.where(pick, 1.0, sel)
        rem = jnp.where(pick, -jnp.inf, rem)
        if kk == 0:
            top = m
            denom = jnp.ones_like(m)
        else:
            denom = denom + jnp.exp(m - top)
    wf_ref[...] = jnp.where(sel > 0.0, jnp.exp(logits - top) / denom, 0.0)
    ms_ref[...] = sel


def _mid(o_atts, hfbs, gg, gs, xs, mod, w_pa, w_pr, w_out, ln1_g, ln1_b, w_router, b_router, lay, tm):
    nt = lay.n_tokens

    def seq_map(i):
        return (lay.seq_and_pos(i * tm)[0], 0, 0)

    tok = lambda w: pl.BlockSpec((tm, w), lambda i: (i, 0))
    vec = lambda w: _const_spec((1, w))
    row_specs, tile_ends = lay.trunk_specs(
        tm, lambda local: pl.BlockSpec((tm, D_MODEL), lambda i: (local(i), 0)))
    h_pairs, _ = lay.trunk_specs(
        tm, lambda local: [pl.BlockSpec((1, tm, D_MODEL), lambda i, d=d: (d, local(i), 0)) for d in range(2)])
    h_specs = [spec for pair in h_pairs for spec in pair]
    h_args = [hfb for hfb in hfbs for _ in range(2)]
    return pl.pallas_call(
        functools.partial(_mid_kernel, tile_ends=tile_ends),
        grid=(nt // tm,),
        in_specs=row_specs + h_specs + row_specs + [
                  tok(D_MODEL), tok(2 * D_MODEL),
                  pl.BlockSpec((1, N_MOD, D_MODEL), seq_map),
                  _const_spec((D_MODEL, D_MODEL)), _const_spec((D_MODEL, D_MODEL)),
                  _const_spec((D_MODEL, D_MODEL)),
                  vec(D_MODEL), vec(D_MODEL),
                  _const_spec((D_MODEL, 2 * N_EXPERTS)), vec(N_EXPERTS)],
        out_specs=[tok(D_MODEL), pl.BlockSpec((ROW_SUB * tm, 128), lambda i: (i, 0)),
                   tok(N_EXPERTS), tok(N_EXPERTS)],
        out_shape=[jax.ShapeDtypeStruct((nt, D_MODEL), F32),
                   jax.ShapeDtypeStruct((ROW_SUB * nt, 128), F32),
                   jax.ShapeDtypeStruct((nt, N_EXPERTS), F32),
                   jax.ShapeDtypeStruct((nt, N_EXPERTS), F32)],
        compiler_params=_cparams(("arbitrary",)),
        name="merge_router",
    )(*o_atts, *h_args, *xs, gg, gs, mod, w_pa, w_pr, w_out, ln1_g.reshape(1, -1), ln1_b.reshape(1, -1),
      w_router, b_router.reshape(1, -1))


def _rank_kernel(ms_ref, rank_ref, cnt_ref, tri_scr, run_scr, *, tp):
    @pl.when(pl.program_id(0) == 0)
    def _():
        r = lax.broadcasted_iota(jnp.int32, (tp, tp), 0)
        c = lax.broadcasted_iota(jnp.int32, (tp, tp), 1)
        tri_scr[...] = jnp.where(c < r, 1.0, 0.0).astype(BF16)
        run_scr[...] = jnp.zeros(run_scr.shape, F32)

    ms = ms_ref[...]
    before = jnp.dot(tri_scr[...], ms.astype(BF16), preferred_element_type=F32)
    rank_ref[...] = before + run_scr[...]
    run_scr[...] = run_scr[...] + jnp.sum(ms, axis=0, keepdims=True)
    cnt_ref[...] = jnp.broadcast_to(run_scr[...], cnt_ref.shape)


def _rank(msel, tp):
    nt = msel.shape[0]
    return pl.pallas_call(
        functools.partial(_rank_kernel, tp=tp),
        grid=(nt // tp,),
        in_specs=[pl.BlockSpec((tp, N_EXPERTS), lambda i: (i, 0))],
        out_specs=[pl.BlockSpec((tp, N_EXPERTS), lambda i: (i, 0)),
                   pl.BlockSpec((8, N_EXPERTS), lambda i: (0, 0))],
        out_shape=[jax.ShapeDtypeStruct((nt, N_EXPERTS), F32),
                   jax.ShapeDtypeStruct((8, N_EXPERTS), F32)],
        scratch_shapes=[pltpu.VMEM((tp, tp), BF16), pltpu.VMEM((1, N_EXPERTS), F32)],
        compiler_params=_cparams(("arbitrary",)),
        name="route_rank",
    )(msel)


def _slots_kernel(rank_ref, ms_ref, wf_ref, start_ref, slot_ref, wt_ref):
    ms = ms_ref[...]
    wf = wf_ref[...]
    slot_full = rank_ref[...] + start_ref[...]
    lane = lax.broadcasted_iota(jnp.int32, ms.shape, 1).astype(F32)
    out_lane = lax.broadcasted_iota(jnp.int32, slot_ref.shape, 1)
    slots = jnp.zeros(slot_ref.shape, F32)
    wts = jnp.zeros(wt_ref.shape, F32)
    rem = ms
    for kk in range(TOP_K):
        idx = jnp.min(jnp.where(rem > 0.0, lane, float(2 * N_EXPERTS)), axis=-1, keepdims=True)
        pick = lane == idx
        s_k = jnp.sum(jnp.where(pick, slot_full, 0.0), axis=-1, keepdims=True)
        w_k = jnp.sum(jnp.where(pick, wf, 0.0), axis=-1, keepdims=True)
        rem = jnp.where(pick, 0.0, rem)
        slots = jnp.where(out_lane == kk, s_k, slots)
        wts = jnp.where(out_lane == kk, w_k, wts)
    slot_ref[...] = slots.astype(jnp.int32)
    wt_ref[...] = wts


def _slots(rank, msel, wfull, pad_start, tp):
    nt = msel.shape[0]
    tok = pl.BlockSpec((tp, N_EXPERTS), lambda i: (i, 0))
    out = pl.BlockSpec((tp, 128), lambda i: (i, 0))
    return pl.pallas_call(
        _slots_kernel,
        grid=(nt // tp,),
        in_specs=[tok, tok, tok, pl.BlockSpec((1, N_EXPERTS), lambda i: (0, 0))],
        out_specs=[out, out],
        out_shape=[jax.ShapeDtypeStruct((nt, 128), jnp.int32),
                   jax.ShapeDtypeStruct((nt, 128), F32)],
        compiler_params=_cparams(("arbitrary",)),
        name="route_slots",
    )(rank, msel, wfull, pad_start)


def _row_view(ref, row):
    return ref.at[pl.ds(pl.multiple_of(row * ROW_SUB, ROW_SUB), ROW_SUB), :]


def _dispatch_kernel(slot_ref, tail_ref, h_ref, xs_ref, zero_scr, sem, zsem, *, tg, tb):
    @pl.when(pl.program_id(0) == 0)
    def _():
        zero_scr[...] = jnp.zeros(zero_scr.shape, F32)
        for e in range(N_EXPERTS):
            tail = pl.multiple_of(tail_ref[e] * ROW_SUB, tb * ROW_SUB)
            pltpu.make_async_copy(zero_scr, xs_ref.at[pl.ds(tail, tb * ROW_SUB), :], zsem).start()
        for e in range(N_EXPERTS):
            pltpu.make_async_copy(zero_scr, xs_ref.at[pl.ds(0, tb * ROW_SUB), :], zsem).wait()

    def issue(t, carry):
        for kk in range(TOP_K):
            pltpu.make_async_copy(_row_view(h_ref, t), _row_view(xs_ref, slot_ref[t * TOP_K + kk]),
                                  sem).start(priority=kk % 2)
        return carry

    lax.fori_loop(0, tg, issue, 0, unroll=2)

    for kk in range(TOP_K):
        pltpu.make_async_copy(h_ref, xs_ref.at[pl.ds(0, tg * ROW_SUB), :], sem).wait()


def _dispatch(slots_flat, tail_start, h2r, cap, tg, tb):
    nt = h2r.shape[0] // ROW_SUB
    return pl.pallas_call(
        functools.partial(_dispatch_kernel, tg=tg, tb=tb),
        grid=(nt // tg,),
        in_specs=[pl.BlockSpec((tg * TOP_K,), lambda i: (i,), memory_space=pltpu.SMEM),
                  pl.BlockSpec(memory_space=pltpu.SMEM),
                  pl.BlockSpec((tg * ROW_SUB, 128), lambda i: (i, 0))],
        out_specs=pl.BlockSpec(memory_space=pl.ANY),
        out_shape=jax.ShapeDtypeStruct((cap * ROW_SUB, 128), F32),
        scratch_shapes=[pltpu.VMEM((tb * ROW_SUB, 128), F32), pltpu.SemaphoreType.DMA(()),
                        pltpu.SemaphoreType.DMA(())],
        compiler_params=_cparams(("arbitrary",)),
        name="moe_dispatch",
    )(slots_flat, tail_start, h2r)


def _expert_kernel(be_ref, bv_ref, xs_ref, w1_ref, b1_ref, w2_ref, b2_ref, ys_ref, w1_scr, w2_scr, *, tb):
    i = pl.program_id(0)
    valid = bv_ref[i]

    @pl.when((i == 0) | (be_ref[i] != be_ref[jnp.maximum(i - 1, 0)]))
    def _():
        w1_scr[...] = w1_ref[0].astype(BF16)
        w2_scr[...] = w2_ref[0].astype(BF16)

    @pl.when(valid > 0)
    def _():
        xb = _load_row_tiles(xs_ref, tb).astype(BF16)
        gu = jnp.dot(xb, w1_scr[...], preferred_element_type=F32) + b1_ref[0]
        glu = jnp.minimum(gu[:, :D_FF], SWIGLU_LIMIT)
        lin = jnp.clip(gu[:, D_FF:], -SWIGLU_LIMIT, SWIGLU_LIMIT)
        act = (lin + 1.0) * glu * _sigmoid(SWIGLU_ALPHA * glu)
        y = jnp.dot(act.astype(BF16), w2_scr[...], preferred_element_type=F32) + b2_ref[0]
        _store_row_tiles(ys_ref, y)

    @pl.when(valid <= 0)
    def _():
        ys_ref[...] = jnp.zeros(ys_ref.shape, F32)


def _experts(block_expert, block_valid, xs, w1, b1, w2, b2, tb):
    cap = xs.shape[0] // ROW_SUB
    rows = pl.BlockSpec((tb * ROW_SUB, 128), lambda i, be, bv: (i, 0))
    grid_spec = pltpu.PrefetchScalarGridSpec(
        num_scalar_prefetch=2,
        grid=(cap // tb,),
        in_specs=[rows,
                  pl.BlockSpec((1, D_MODEL, 2 * D_FF), lambda i, be, bv: (be[i], 0, 0)),
                  pl.BlockSpec((1, 1, 2 * D_FF), lambda i, be, bv: (be[i], 0, 0)),
                  pl.BlockSpec((1, D_FF, D_MODEL), lambda i, be, bv: (be[i], 0, 0)),
                  pl.BlockSpec((1, 1, D_MODEL), lambda i, be, bv: (be[i], 0, 0))],
        out_specs=rows,
        scratch_shapes=[pltpu.VMEM((D_MODEL, 2 * D_FF), BF16), pltpu.VMEM((D_FF, D_MODEL), BF16)],
    )
    return pl.pallas_call(
        functools.partial(_expert_kernel, tb=tb),
        grid_spec=grid_spec,
        out_shape=jax.ShapeDtypeStruct((cap * ROW_SUB, 128), F32),
        compiler_params=_cparams(("arbitrary",)),
        name="moe_experts",
    )(block_expert, block_valid, xs, w1, b1.reshape(N_EXPERTS, 1, -1), w2, b2.reshape(N_EXPERTS, 1, -1))


def _combine_kernel(slot_ref, next_slot_ref, ys_ref, wt_ref, x1_ref, mod_ref, g_ref, b_ref, *rest,
                    tc, tile_ends):
    y_refs = rest[:len(tile_ends)]
    buf, sem = rest[len(tile_ends):]
    i = pl.program_id(0)
    cur = i % 2

    def row_copy(slots, t, kk, half):
        return pltpu.make_async_copy(_row_view(ys_ref, slots[t * TOP_K + kk]),
                                     _row_view(buf.at[half, kk], t), sem.at[half])

    def gather(slots, half):
        def issue(t, carry):
            for kk in range(TOP_K):
                row_copy(slots, t, kk, half).start(priority=kk % 2)
            return carry
        lax.fori_loop(0, tc, issue, 0, unroll=8)

    @pl.when(i == 0)
    def _():
        gather(slot_ref, 0)

    @pl.when(i + 1 < pl.num_programs(0))
    def _():
        gather(next_slot_ref, 1 - cur)

    for kk in range(TOP_K):
        pltpu.make_async_copy(ys_ref.at[pl.ds(0, tc * ROW_SUB), :], buf.at[cur, kk], sem.at[cur]).wait()
    wt = wt_ref[...]
    ff = sum(wt[:, kk:kk + 1] * _load_row_tiles(buf.at[cur, kk], tc) for kk in range(TOP_K))
    g2 = mod_ref[0, 5:6, :]
    y = _layer_norm(DN_ALPHA * x1_ref[...] + (1.0 + g2) * ff) * g_ref[...] + b_ref[...]
    i = pl.program_id(0)
    start = 0
    for y_ref, end in zip(y_refs, tile_ends):
        @pl.when((i >= start) & (i < end))
        def _(y_ref=y_ref):
            y_ref[...] = y
        start = end


def _combine(slots_flat, ys, wts, x1, mod, ln2_g, ln2_b, lay, tc):
    nt = lay.n_tokens

    def seq_map(i):
        return (lay.seq_and_pos(i * tc)[0], 0, 0)

    out_specs, tile_ends = lay.trunk_specs(
        tc, lambda local: pl.BlockSpec((tc, D_MODEL), lambda i: (local(i), 0)))
    n_steps = nt // tc
    return pl.pallas_call(
        functools.partial(_combine_kernel, tc=tc, tile_ends=tile_ends),
        grid=(n_steps,),
        in_specs=[pl.BlockSpec((tc * TOP_K,), lambda i: (i,), memory_space=pltpu.SMEM),
                  pl.BlockSpec((tc * TOP_K,), lambda i: (jnp.minimum(i + 1, n_steps - 1),),
                               memory_space=pltpu.SMEM),
                  pl.BlockSpec(memory_space=pl.ANY),
                  pl.BlockSpec((tc, 128), lambda i: (i, 0)),
                  pl.BlockSpec((tc, D_MODEL), lambda i: (i, 0)),
                  pl.BlockSpec((1, N_MOD, D_MODEL), seq_map),
                  _const_spec((1, D_MODEL)), _const_spec((1, D_MODEL))],
        out_specs=out_specs,
        out_shape=[jax.ShapeDtypeStruct((b * s, D_MODEL), F32) for b, s in lay.trunks],
        scratch_shapes=[pltpu.VMEM((2, TOP_K, tc * ROW_SUB, 128), F32), pltpu.SemaphoreType.DMA((2,))],
        compiler_params=_cparams(("arbitrary",)),
        name="moe_combine",
    )(slots_flat, slots_flat, ys, wts, x1, mod, ln2_g.reshape(1, -1), ln2_b.reshape(1, -1))


def _encoder_layer(xs, cs, p):
    trunks = [(x.shape[0], x.shape[1]) for x in xs]
    lay = _Layout(trunks)
    nt = lay.n_tokens
    t = _tiles(nt)

    x2d = [xi.reshape(-1, D_MODEL) for xi in xs]
    c = jnp.concatenate(cs, axis=0)
    bp = -(-lay.n_seqs // 8) * 8
    c = jnp.pad(c, ((0, bp - lay.n_seqs), (0, 0)))
    mod = _ada_mod(c, p["w_ada"], p["b_ada"]).reshape(bp, N_MOD, D_MODEL)

    q, k, vt3, xr, gg, gs = _inproj(x2d, mod, p["w_in"].astype(BF16), p["q_gain"], p["k_gain"],
                                    _rope_tables(lay.max_seq), lay, t["tm_in"])

    w_lru = jnp.concatenate([p["lru_wa"], p["lru_wx"]], axis=-1).astype(BF16)
    b_lru = jnp.stack([p["lru_ba"], p["lru_bx"]], axis=1)
    lam = p["lru_lam"].reshape(2, 1, D_MODEL)
    o_atts, hfbs = [], []
    tok_off = 0
    for b, s in trunks:
        o_atts.append(_attention(q, k, vt3, tok_off, b, s, t["tq"], t["tm_in"]))
        hfbs.append(_scan(xr, p["conv_w"], p["conv_b"].reshape(1, -1), w_lru, b_lru, lam,
                          tok_off, b, s, t["tt"]))
        tok_off += b * s

    wr_hi = p["w_router"].astype(BF16)
    wr_lo = (p["w_router"] - wr_hi.astype(F32)).astype(BF16)
    x1, h2r, wfull, msel = _mid(o_atts, hfbs, gg, gs, x2d, mod, p["w_pa"].astype(BF16),
                               p["w_pr"].astype(BF16), p["w_out"].astype(BF16), p["ln1_g"], p["ln1_b"],
                               jnp.concatenate([wr_hi, wr_lo], axis=1), p["b_router"], lay, t["tm_mid"])

    tb = t["tb"]
    rank, cnt = _rank(msel, t["tp"])
    counts = cnt[0].astype(jnp.int32)
    padded = (counts + tb - 1) // tb * tb
    pad_end = jnp.cumsum(padded)
    pad_start = pad_end - padded
    n_blocks = nt * TOP_K // tb + N_EXPERTS
    blk0 = jnp.arange(n_blocks, dtype=jnp.int32) * tb
    block_expert = jnp.minimum(jnp.sum(pad_end[None, :] <= blk0[:, None], axis=1), N_EXPERTS - 1).astype(jnp.int32)
    block_valid = jnp.clip(pad_start[block_expert] + counts[block_expert] - blk0, 0, tb).astype(jnp.int32)
    slots, wts = _slots(rank, msel, wfull, pad_start.astype(F32).reshape(1, N_EXPERTS), t["tp"])
    slots_flat = slots[:, :TOP_K].reshape(-1)

    tail_start = jnp.where(padded > 0, pad_end - tb, pad_end[-1] - tb).astype(jnp.int32)
    xs_rows = _dispatch(slots_flat, tail_start, h2r, n_blocks * tb, t["tg"], tb)
    ys = _experts(block_expert, block_valid, xs_rows, p["w1"], p["b1"], p["w2"], p["b2"], tb)
    ys_out = _combine(slots_flat, ys, wts, x1, mod, p["ln2_g"], p["ln2_b"], lay, t["tc"])
    return [y.reshape(b, s, D_MODEL) for y, (b, s) in zip(ys_out, trunks)]


_PARAM_NAMES = ("w_ada", "b_ada", "w_in", "q_gain", "k_gain", "conv_w", "conv_b", "lru_wa", "lru_ba",
                "lru_wx", "lru_bx", "lru_lam", "w_pa", "w_pr", "w_out", "ln1_g", "ln1_b", "w_router",
                "b_router", "w1", "b1", "w2", "b2", "ln2_g", "ln2_b")


def kernel(x_prompt, x_sample, c_prompt, c_sample, w_ada, b_ada, w_in, q_gain, k_gain, conv_w, conv_b, lru_wa, lru_ba, lru_wx, lru_bx, lru_lam, w_pa, w_pr, w_out, ln1_g, ln1_b, w_router, b_router, w1, b1, w2, b2, ln2_g, ln2_b):
    stacked = (w_ada, b_ada, w_in, q_gain, k_gain, conv_w, conv_b, lru_wa, lru_ba, lru_wx, lru_bx,
               lru_lam, w_pa, w_pr, w_out, ln1_g, ln1_b, w_router, b_router, w1, b1, w2, b2, ln2_g, ln2_b)
    xs, cs = [x_prompt, x_sample], [c_prompt, c_sample]
    for layer in range(DEPTH):
        p = {name: arr[layer] for name, arr in zip(_PARAM_NAMES, stacked)}
        xs = _encoder_layer(xs, cs, p)
    return (xs[0], xs[1])
```

```python
import functools
import math

import jax
import jax.numpy as jnp
from jax import lax
from jax.experimental import pallas as pl
from jax.experimental.pallas import tpu as pltpu

F32 = jnp.float32
BF16 = jnp.bfloat16

D_MODEL = 1024
GRID_W = 64
N_HEADS = 8
N_KV_HEADS = 2
HEAD_DIM = 128
GROUPS = N_HEADS // N_KV_HEADS
KV_WIDTH = N_KV_HEADS * HEAD_DIM
ROPE_THETA = 10000.0
RNN_BLOCKS = 8
RNN_BLOCK_W = D_MODEL // RNN_BLOCKS
CONV_W = 4
LRU_C = 8.0
N_EXPERTS = 32
TOP_K = 4
D_FF = D_MODEL
SWIGLU_LIMIT = 7.0
SWIGLU_ALPHA = 1.702
DEPTH = 1
DN_ALPHA = (2 * DEPTH) ** 0.25
LN_EPS = 1e-5
RMS_EPS = 1e-6
N_MOD = 6
_Q0 = 0
_K0 = _Q0 + D_MODEL
_V0 = _K0 + KV_WIDTH
_XR0 = _V0 + KV_WIDTH
_GR0 = _XR0 + D_MODEL
_GL0 = _GR0 + D_MODEL
IN_WIDTH = _GL0 + 2 * D_MODEL

V7X_LANES = 128
V7X_SUBLANES = 8
V7X_BF16_SUBLANES = 16
V7X_VMEM_LIMIT_BYTES = 56 * 1024 * 1024
HALO = V7X_BF16_SUBLANES
VT_ROWS = HEAD_DIM + V7X_BF16_SUBLANES


def _tiles(n_tokens):
    big = n_tokens >= 4096
    return dict(
        tm_in=512 if big else 128,
        tq=256 if big else 128,
        tt=1024 if big else 128,
        tm_mid=512 if big else 128,
        tp=512 if big else 128,
        tg=1024 if big else 256,
        tb=512 if big else 128,
        tc=512 if big else 256,
    )


def _cparams(sem):
    return pltpu.CompilerParams(dimension_semantics=sem, vmem_limit_bytes=V7X_VMEM_LIMIT_BYTES)


def _const_spec(shape):
    nd = len(shape)
    return pl.BlockSpec(shape, lambda *_: (0,) * nd, pipeline_mode=pl.Buffered(1))


def _layer_norm(x):
    mu = jnp.mean(x, axis=-1, keepdims=True)
    xc = x - mu
    var = jnp.mean(xc * xc, axis=-1, keepdims=True)
    return xc * lax.rsqrt(var + LN_EPS)


def _sigmoid(x):
    return 1.0 / (1.0 + jnp.exp(-x))


ROW_SUB = D_MODEL // V7X_LANES
assert ROW_SUB == V7X_SUBLANES, "one token row must fill exactly one f32 tile"


def _store_row_tiles(ref, val):
    n = val.shape[0]
    for j in range(ROW_SUB):
        ref[pl.ds(j, n, stride=ROW_SUB), :] = val[:, j * V7X_LANES:(j + 1) * V7X_LANES]


def _load_row_tiles(ref, n):
    return jnp.concatenate([ref[pl.ds(j, n, stride=ROW_SUB), :] for j in range(ROW_SUB)], axis=1)


def _ada_kernel(c_ref, w_ref, b_ref, o_ref):
    c = c_ref[...]
    s = c * _sigmoid(c)
    o_ref[...] = jnp.dot(s, w_ref[...], preferred_element_type=F32,
                         precision=lax.Precision.HIGHEST) + b_ref[...]


def _ada_mod(c_all, w_ada, b_ada):
    bp = c_all.shape[0]
    ncol = w_ada.shape[1]
    return pl.pallas_call(
        _ada_kernel,
        grid=(ncol // D_MODEL,),
        in_specs=[pl.BlockSpec((bp, D_MODEL), lambda j: (0, 0)),
                  pl.BlockSpec((D_MODEL, D_MODEL), lambda j: (0, j)),
                  pl.BlockSpec((1, D_MODEL), lambda j: (0, j))],
        out_specs=pl.BlockSpec((bp, D_MODEL), lambda j: (0, j)),
        out_shape=jax.ShapeDtypeStruct((bp, ncol), F32),
        compiler_params=_cparams(("arbitrary",)),
        name="ada_mod",
    )(c_all, w_ada, b_ada.reshape(1, ncol))


class _Layout:
    def __init__(self, trunks):
        self.trunks = tuple(trunks)
        self.n_tokens = sum(b * s for b, s in trunks)
        self.n_seqs = sum(b for b, _ in trunks)
        self.max_seq = max(s for _, s in trunks)

    def seq_and_pos(self, t0):
        seq = jnp.int32(0)
        pos = jnp.int32(0)
        tok_off, seq_off = 0, 0
        for b, s in self.trunks:
            inside = (t0 >= tok_off) & (t0 < tok_off + b * s)
            rel = jnp.maximum(t0 - tok_off, 0)
            seq = jnp.where(inside, seq_off + rel // s, seq)
            pos = jnp.where(inside, rel % s, pos)
            tok_off += b * s
            seq_off += b
        return seq, pos

    def trunk_specs(self, tm, make_spec):
        specs, ends, t0 = [], [], 0
        for b, s in self.trunks:
            n_t = b * s // tm
            specs.append(make_spec(functools.partial(_clamped_local, t0=t0, n_t=n_t)))
            t0 += n_t
            ends.append(t0)
        return specs, tuple(ends)


def _clamped_local(i, *, t0, n_t):
    return jnp.clip(i - t0, 0, n_t - 1)


def _owner_value(loads, i, tile_ends):
    val = loads[-1]()
    for j in range(len(loads) - 2, -1, -1):
        val = jnp.where(i < tile_ends[j], loads[j](), val)
    return val


def _rope_tables(max_seq):
    n_rows = max_seq // GRID_W
    axis_dim = HEAD_DIM // 2
    inv = ROPE_THETA ** (-jnp.arange(0, axis_dim, 2, dtype=F32) / axis_dim)
    ar = jnp.arange(n_rows, dtype=F32)[:, None] * inv
    ac = jnp.arange(GRID_W, dtype=F32)[:, None] * inv
    cr, sr = (jnp.repeat(f(ar), GRID_W, axis=0) for f in (jnp.cos, jnp.sin))
    cc, sc = (jnp.tile(f(ac), (n_rows, 1)) for f in (jnp.cos, jnp.sin))
    z = jnp.zeros_like(sr)
    cos_t = jnp.concatenate([cr, cr, cc, cc], axis=-1)
    up_t = jnp.concatenate([-sr, z, -sc, z], axis=-1)
    dn_t = jnp.concatenate([z, sr, z, sc], axis=-1)
    return cos_t, up_t, dn_t


def _inproj_kernel(*refs, tile_ends):
    n_tr = len(tile_ends)
    x_refs = refs[:n_tr]
    (mod_ref, w_ref, qg_ref, kg_ref, cos_ref, up_ref, dn_ref,
     q_ref, k_ref, vt_ref, xr_ref, gg_ref, gs_ref) = refs[n_tr:]
    x = _owner_value([lambda r=r: r[...] for r in x_refs], pl.program_id(0), tile_ends)
    sh1 = mod_ref[0, 0:1, :]
    sc1 = mod_ref[0, 1:2, :]
    h = (_layer_norm(x) * (1.0 + sc1) + sh1).astype(BF16)
    cos_t, up_t, dn_t = cos_ref[...], up_ref[...], dn_ref[...]

    def proj(c0, width):
        return jnp.dot(h, w_ref[:, c0:c0 + width], preferred_element_type=F32)

    def norm_rope(z, gain):
        ms = jnp.mean(z * z, axis=-1, keepdims=True)
        y = z * lax.rsqrt(ms + RMS_EPS) * gain
        return (y * cos_t + pltpu.roll(y, HEAD_DIM - 32, 1) * up_t
                + pltpu.roll(y, 32, 1) * dn_t)

    zq = proj(_Q0, D_MODEL)
    qg = qg_ref[...] * (HEAD_DIM ** -0.5 * math.log2(math.e))
    for hd in range(N_HEADS):
        sl = slice(hd * HEAD_DIM, (hd + 1) * HEAD_DIM)
        q_ref[:, sl] = norm_rope(zq[:, sl], qg).astype(BF16)
    zk = proj(_K0, KV_WIDTH)
    kg = kg_ref[...]
    for hd in range(N_KV_HEADS):
        sl = slice(hd * HEAD_DIM, (hd + 1) * HEAD_DIM)
        k_ref[:, sl] = norm_rope(zk[:, sl], kg).astype(BF16)
    zvt = proj(_V0, KV_WIDTH).T.astype(BF16)
    pad_row = lax.broadcasted_iota(jnp.int32, (VT_ROWS - HEAD_DIM, zvt.shape[1]), 0)
    ones_pad = jnp.where(pad_row == 0, 1.0, 0.0).astype(BF16)
    for hd in range(N_KV_HEADS):
        vt_ref[0, hd * VT_ROWS:hd * VT_ROWS + HEAD_DIM, :] = zvt[hd * HEAD_DIM:(hd + 1) * HEAD_DIM]
        vt_ref[0, hd * VT_ROWS + HEAD_DIM:(hd + 1) * VT_ROWS, :] = ones_pad
    xr_ref[...] = proj(_XR0, D_MODEL).astype(BF16)
    gg_ref[...] = jax.nn.gelu(proj(_GR0, D_MODEL), approximate=True).astype(BF16)
    gs_ref[...] = _sigmoid(proj(_GL0, 2 * D_MODEL)).astype(BF16)


def _inproj(xs, mod, w_in, q_gain, k_gain, tables, lay, tm):
    nt = lay.n_tokens
    cos_t, up_t, dn_t = tables

    def seq_map(i):
        return (lay.seq_and_pos(i * tm)[0], 0, 0)

    def pos_map(i):
        return (lay.seq_and_pos(i * tm)[1] // tm, 0)

    tok = lambda w: pl.BlockSpec((tm, w), lambda i: (i, 0))
    rope = pl.BlockSpec((tm, HEAD_DIM), pos_map)
    x_specs, tile_ends = lay.trunk_specs(
        tm, lambda local: pl.BlockSpec((tm, D_MODEL), lambda i: (local(i), 0)))
    return pl.pallas_call(
        functools.partial(_inproj_kernel, tile_ends=tile_ends),
        grid=(nt // tm,),
        in_specs=x_specs + [
                  pl.BlockSpec((1, N_MOD, D_MODEL), seq_map),
                  _const_spec((D_MODEL, IN_WIDTH)),
                  _const_spec((1, HEAD_DIM)), _const_spec((1, HEAD_DIM)),
                  rope, rope, rope],
        out_specs=[tok(D_MODEL), tok(KV_WIDTH),
                   pl.BlockSpec((1, N_KV_HEADS * VT_ROWS, tm), lambda i: (i, 0, 0)),
                   tok(D_MODEL), tok(D_MODEL), tok(2 * D_MODEL)],
        out_shape=[jax.ShapeDtypeStruct((nt, D_MODEL), BF16),
                   jax.ShapeDtypeStruct((nt, KV_WIDTH), BF16),
                   jax.ShapeDtypeStruct((nt // tm, N_KV_HEADS * VT_ROWS, tm), BF16),
                   jax.ShapeDtypeStruct((nt, D_MODEL), BF16),
                   jax.ShapeDtypeStruct((nt, D_MODEL), BF16),
                   jax.ShapeDtypeStruct((nt, 2 * D_MODEL), BF16)],
        compiler_params=_cparams(("arbitrary",)),
        name="in_proj",
    )(*xs, mod, w_in, q_gain.reshape(1, HEAD_DIM), k_gain.reshape(1, HEAD_DIM), cos_t, up_t, dn_t)


def _attn_kernel(q_ref, k_ref, vt_ref, o_ref, qt_scr, s_scr, m_scr, acc_scr, *,
                 tq, tk, n_kv, group):
    for g in range(GROUPS):
        qg = q_ref[:, g * HEAD_DIM:(g + 1) * HEAD_DIM].astype(F32)
        qt_scr[:, g * tq:(g + 1) * tq] = qg.T.astype(BF16)
    m_scr[...] = jnp.full(m_scr.shape, -jnp.inf, F32)
    acc_scr[...] = jnp.zeros(acc_scr.shape, F32)

    def scores(j, slot):
        kt = k_ref[pl.ds(pl.multiple_of(j * tk, tk), tk), :]
        s_scr[slot] = jnp.dot(kt, qt_scr[...], preferred_element_type=F32)

    def accumulate(j, slot):
        s = s_scr[slot]
        m_old = m_scr[...]
        m_new = jnp.maximum(m_old, jnp.max(s, axis=0, keepdims=True))
        alpha = jnp.exp2(m_old - m_new)
        p = jnp.exp2(s - m_new).astype(BF16)
        pv = jnp.dot(vt_ref[j], p, preferred_element_type=F32)
        acc_scr[...] = alpha * acc_scr[...] + pv
        m_scr[...] = m_new

    scores(0, 0)

    def body(i, carry):
        j = group * i
        for u in range(group):
            scores(jnp.minimum(j + u + 1, n_kv - 1), (u + 1) % 2)
            accumulate(j + u, u % 2)
        return carry

    lax.fori_loop(0, n_kv // group, body, 0)
    out = acc_scr[0:HEAD_DIM, :] / acc_scr[HEAD_DIM:HEAD_DIM + 1, :]
    for g in range(GROUPS):
        o_ref[:, g * HEAD_DIM:(g + 1) * HEAD_DIM] = out[:, g * tq:(g + 1) * tq].T.astype(BF16)


def _attention(q, k, vt3, tok_off, batch, seq, tq, tk):
    n_kv = seq // tk
    group = next((g for g in (8, 4) if n_kv % g == 0 and n_kv // g >= 2), 2)
    assert n_kv % group == 0
    qrow0 = tok_off // tq
    srow0 = tok_off // seq
    gw = GROUPS * HEAD_DIM
    n_q = seq // tq
    return pl.pallas_call(
        functools.partial(_attn_kernel, tq=tq, tk=tk, n_kv=n_kv, group=group),
        grid=(batch, N_KV_HEADS, n_q),
        in_specs=[pl.BlockSpec((tq, gw), lambda b, h, i: (qrow0 + b * n_q + i, h)),
                  pl.BlockSpec((seq, HEAD_DIM), lambda b, h, i: (srow0 + b, h)),
                  pl.BlockSpec((n_kv, VT_ROWS, tk), lambda b, h, i: (srow0 + b, h, 0))],
        out_specs=pl.BlockSpec((tq, gw), lambda b, h, i: (b * n_q + i, h)),
        out_shape=jax.ShapeDtypeStruct((batch * seq, D_MODEL), BF16),
        scratch_shapes=[pltpu.VMEM((HEAD_DIM, GROUPS * tq), BF16),
                        pltpu.VMEM((2, tk, GROUPS * tq), F32),
                        pltpu.VMEM((1, GROUPS * tq), F32),
                        pltpu.VMEM((VT_ROWS, GROUPS * tq), F32)],
        compiler_params=_cparams(("arbitrary", "arbitrary", "arbitrary")),
        name="attention",
    )(q, k, vt3)


def _scan_kernel(cur_ref, prev_ref, next_ref, cw_ref, cb_ref, w_ref, b_ref, lam_ref, o_ref,
                 xc_scr, a_scr, u_scr, h_scr, carry_scr, *, tt, n_chunks):
    d = pl.program_id(1)
    c = pl.program_id(2)
    chunk = jnp.where(d == 0, c, n_chunks - 1 - c)

    @pl.when(c == 0)
    def _():
        carry_scr[...] = jnp.zeros(carry_scr.shape, F32)

    keep_prev = jnp.where(chunk == 0, 0.0, 1.0)
    keep_next = jnp.where(chunk == n_chunks - 1, 0.0, 1.0)
    cur = cur_ref[...].astype(F32)
    taps = [cw_ref[j:j + 1, :] for j in range(CONV_W)]
    xc_scr[...] = (cb_ref[...] + taps[2] * cur + taps[1] * pltpu.roll(cur, 1, 0)
                   + taps[0] * pltpu.roll(cur, 2, 0) + taps[3] * pltpu.roll(cur, tt - 1, 0))
    sub = V7X_SUBLANES
    head = jnp.concatenate([prev_ref[...].astype(F32)[HALO - sub:HALO] * keep_prev, cur[0:2 * sub]], axis=0)
    tail = jnp.concatenate([cur[tt - 2 * sub:tt], next_ref[...].astype(F32)[0:sub] * keep_next], axis=0)
    first = sum(taps[j] * head[sub - 2 + j:2 * sub - 2 + j] for j in range(CONV_W))
    last = sum(taps[j] * tail[sub - 2 + j:2 * sub - 2 + j] for j in range(CONV_W))
    xc_scr[0:sub, :] = cb_ref[...] + first
    xc_scr[tt - sub:tt, :] = cb_ref[...] + last
    xc = xc_scr[...]
    xcb = xc.astype(BF16)

    lam = lam_ref[0]
    y = jnp.exp(-jnp.abs(lam))
    w1p = 1.0 + y
    log1p_y = jnp.where(w1p == 1.0, y, jnp.log(w1p) * y / jnp.where(w1p == 1.0, 1.0, w1p - 1.0))
    neg_c_sp = (-LRU_C * math.log2(math.e)) * (jnp.maximum(-lam, 0.0) + log1p_y)

    for n in range(RNN_BLOCKS):
        sl = slice(n * RNN_BLOCK_W, (n + 1) * RNN_BLOCK_W)
        pre = jnp.dot(xcb[:, sl], w_ref[0, n], preferred_element_type=F32)
        r = _sigmoid(pre[:, :RNN_BLOCK_W] + b_ref[0, 0:1, sl])
        i = _sigmoid(pre[:, RNN_BLOCK_W:] + b_ref[0, 1:2, sl])
        a = jnp.exp2(r * neg_c_sp[:, sl])
        a_scr[:, sl] = a
        u_scr[:, sl] = jnp.sqrt(1.0 - a * a) * (i * xc[:, sl])

    def step(t, h):
        row = jnp.where(d == 0, t, tt - 1 - t)
        h = a_scr[pl.ds(row, 1), :] * h + u_scr[pl.ds(row, 1), :]
        h_scr[pl.ds(row, 1), :] = h
        return h

    carry_scr[...] = lax.fori_loop(0, tt, step, carry_scr[...], unroll=8)
    o_ref[0] = h_scr[...].astype(BF16)


def _scan(xr, conv_w, conv_b, w_lru, b_lru, lam, tok_off, batch, seq, tt):
    nt = xr.shape[0]
    n_chunks = seq // tt
    row0 = tok_off // tt
    hrow0 = tok_off // HALO
    hpc = tt // HALO
    n_halo = nt // HALO

    def chunk_of(d, c):
        return jnp.where(d == 0, c, n_chunks - 1 - c)

    def prev_map(b, d, c):
        return (jnp.maximum(hrow0 + (b * n_chunks + chunk_of(d, c)) * hpc - 1, 0), 0)

    def next_map(b, d, c):
        return (jnp.minimum(hrow0 + (b * n_chunks + chunk_of(d, c) + 1) * hpc, n_halo - 1), 0)

    return pl.pallas_call(
        functools.partial(_scan_kernel, tt=tt, n_chunks=n_chunks),
        grid=(batch, 2, n_chunks),
        in_specs=[pl.BlockSpec((tt, D_MODEL), lambda b, d, c: (row0 + b * n_chunks + chunk_of(d, c), 0)),
                  pl.BlockSpec((HALO, D_MODEL), prev_map),
                  pl.BlockSpec((HALO, D_MODEL), next_map),
                  pl.BlockSpec((CONV_W, D_MODEL), lambda b, d, c: (0, 0)),
                  pl.BlockSpec((1, D_MODEL), lambda b, d, c: (0, 0)),
                  pl.BlockSpec((1, RNN_BLOCKS, RNN_BLOCK_W, 2 * RNN_BLOCK_W), lambda b, d, c: (d, 0, 0, 0)),
                  pl.BlockSpec((1, 2, D_MODEL), lambda b, d, c: (d, 0, 0)),
                  pl.BlockSpec((1, 1, D_MODEL), lambda b, d, c: (d, 0, 0))],
        out_specs=pl.BlockSpec((1, tt, D_MODEL), lambda b, d, c: (d, b * n_chunks + chunk_of(d, c), 0)),
        out_shape=jax.ShapeDtypeStruct((2, batch * seq, D_MODEL), BF16),
        scratch_shapes=[pltpu.VMEM((tt, D_MODEL), F32),
                        pltpu.VMEM((tt, D_MODEL), F32),
                        pltpu.VMEM((tt, D_MODEL), F32),
                        pltpu.VMEM((tt, D_MODEL), F32),
                        pltpu.VMEM((1, D_MODEL), F32)],
        compiler_params=_cparams(("arbitrary", "arbitrary", "arbitrary")),
        name="lru_scan",
    )(xr, xr, xr, conv_w, conv_b, w_lru, b_lru, lam)


def _mid_kernel(*refs, tile_ends):
    n_tr = len(tile_ends)
    oa_refs = refs[:n_tr]
    h_refs = refs[n_tr:3 * n_tr]
    x_refs = refs[3 * n_tr:4 * n_tr]
    (gg_ref, gs_ref, mod_ref, wpa_ref, wpr_ref, wo_ref, g1_ref, b1_ref, wr_ref, br_ref,
     x1_ref, h2_ref, wf_ref, ms_ref) = refs[4 * n_tr:]
    i = pl.program_id(0)
    oa = _owner_value([lambda r=r: r[...] for r in oa_refs], i, tile_ends)
    h_fwd = _owner_value([lambda r=h_refs[2 * j]: r[0] for j in range(n_tr)], i, tile_ends)
    h_bwd = _owner_value([lambda r=h_refs[2 * j + 1]: r[0] for j in range(n_tr)], i, tile_ends)
    hsum = h_fwd.astype(F32) + h_bwd.astype(F32)
    x_in = _owner_value([lambda r=r: r[...] for r in x_refs], i, tile_ends)
    o_att = jnp.dot(oa, wpa_ref[...], preferred_element_type=F32)
    rec = hsum.astype(BF16) * gg_ref[...]
    o_rec = jnp.dot(rec, wpr_ref[...], preferred_element_type=F32)
    gs = gs_ref[...].astype(F32)
    merged = gs[:, :D_MODEL] * o_att + gs[:, D_MODEL:] * o_rec
    mix = jnp.dot(merged.astype(BF16), wo_ref[...], preferred_element_type=F32)
    g1 = mod_ref[0, 2:3, :]
    sh2 = mod_ref[0, 3:4, :]
    sc2 = mod_ref[0, 4:5, :]
    x1 = _layer_norm(DN_ALPHA * x_in + (1.0 + g1) * mix) * g1_ref[...] + b1_ref[...]
    x1_ref[...] = x1
    h2 = _layer_norm(x1) * (1.0 + sc2) + sh2
    _store_row_tiles(h2_ref, h2)
    tm = h2.shape[0]
    h2_hi = h2.astype(BF16)
    h2_lo = (h2 - h2_hi.astype(F32)).astype(BF16)
    cross = jnp.dot(jnp.concatenate([h2_hi, h2_lo], axis=0), wr_ref[...], preferred_element_type=F32)
    logits = ((cross[:tm, :N_EXPERTS] + cross[:tm, N_EXPERTS:])
              + (cross[tm:, :N_EXPERTS] + cross[tm:, N_EXPERTS:])) + br_ref[...]
    lane = lax.broadcasted_iota(jnp.int32, logits.shape, 1).astype(F32)
    rem = logits
    sel = jnp.zeros(logits.shape, F32)
    top = None
    denom = None
    for kk in range(TOP_K):
        m = jnp.max(rem, axis=-1, keepdims=True)
        idx = jnp.min(jnp.where(rem == m, lane, float(N_EXPERTS)), axis=-1, keepdims=True)
        pick = lane == idx
        sel = jnp.where(pick, 1.0, sel)
        rem = jnp.where(pick, -jnp.inf, rem)
        if kk == 0:
            top = m
            denom = jnp.ones_like(m)
        else:
            denom = denom + jnp.exp(m - top)
    wf_ref[...] = jnp.where(sel > 0.0, jnp.exp(logits - top) / denom, 0.0)
    ms_ref[...] = sel


def _mid(o_atts, hfbs, gg, gs, xs, mod, w_pa, w_pr, w_out, ln1_g, ln1_b, w_router, b_router, lay, tm):
    nt = lay.n_tokens

    def seq_map(i):
        return (lay.seq_and_pos(i * tm)[0], 0, 0)

    tok = lambda w: pl.BlockSpec((tm, w), lambda i: (i, 0))
    vec = lambda w: _const_spec((1, w))
    row_specs, tile_ends = lay.trunk_specs(
        tm, lambda local: pl.BlockSpec((tm, D_MODEL), lambda i: (local(i), 0)))
    h_pairs, _ = lay.trunk_specs(
        tm, lambda local: [pl.BlockSpec((1, tm, D_MODEL), lambda i, d=d: (d, local(i), 0)) for d in range(2)])
    h_specs = [spec for pair in h_pairs for spec in pair]
    h_args = [hfb for hfb in hfbs for _ in range(2)]
    return pl.pallas_call(
        functools.partial(_mid_kernel, tile_ends=tile_ends),
        grid=(nt // tm,),
        in_specs=row_specs + h_specs + row_specs + [
                  tok(D_MODEL), tok(2 * D_MODEL),
                  pl.BlockSpec((1, N_MOD, D_MODEL), seq_map),
                  _const_spec((D_MODEL, D_MODEL)), _const_spec((D_MODEL, D_MODEL)),
                  _const_spec((D_MODEL, D_MODEL)),
                  vec(D_MODEL), vec(D_MODEL),
                  _const_spec((D_MODEL, 2 * N_EXPERTS)), vec(N_EXPERTS)],
        out_specs=[tok(D_MODEL), pl.BlockSpec((ROW_SUB * tm, V7X_LANES), lambda i: (i, 0)),
                   tok(N_EXPERTS), tok(N_EXPERTS)],
        out_shape=[jax.ShapeDtypeStruct((nt, D_MODEL), F32),
                   jax.ShapeDtypeStruct((ROW_SUB * nt, V7X_LANES), F32),
                   jax.ShapeDtypeStruct((nt, N_EXPERTS), F32),
                   jax.ShapeDtypeStruct((nt, N_EXPERTS), F32)],
        compiler_params=_cparams(("arbitrary",)),
        name="merge_router",
    )(*o_atts, *h_args, *xs, gg, gs, mod, w_pa, w_pr, w_out, ln1_g.reshape(1, -1), ln1_b.reshape(1, -1),
      w_router, b_router.reshape(1, -1))


def _rank_kernel(ms_ref, rank_ref, cnt_ref, tri_scr, run_scr, *, tp):
    @pl.when(pl.program_id(0) == 0)
    def _():
        r = lax.broadcasted_iota(jnp.int32, (tp, tp), 0)
        c = lax.broadcasted_iota(jnp.int32, (tp, tp), 1)
        tri_scr[...] = jnp.where(c < r, 1.0, 0.0).astype(BF16)
        run_scr[...] = jnp.zeros(run_scr.shape, F32)

    ms = ms_ref[...]
    before = jnp.dot(tri_scr[...], ms.astype(BF16), preferred_element_type=F32)
    rank_ref[...] = before + run_scr[...]
    run_scr[...] = run_scr[...] + jnp.sum(ms, axis=0, keepdims=True)
    cnt_ref[...] = jnp.broadcast_to(run_scr[...], cnt_ref.shape)


def _rank(msel, tp):
    nt = msel.shape[0]
    return pl.pallas_call(
        functools.partial(_rank_kernel, tp=tp),
        grid=(nt // tp,),
        in_specs=[pl.BlockSpec((tp, N_EXPERTS), lambda i: (i, 0))],
        out_specs=[pl.BlockSpec((tp, N_EXPERTS), lambda i: (i, 0)),
                   pl.BlockSpec((V7X_SUBLANES, N_EXPERTS), lambda i: (0, 0))],
        out_shape=[jax.ShapeDtypeStruct((nt, N_EXPERTS), F32),
                   jax.ShapeDtypeStruct((V7X_SUBLANES, N_EXPERTS), F32)],
        scratch_shapes=[pltpu.VMEM((tp, tp), BF16), pltpu.VMEM((1, N_EXPERTS), F32)],
        compiler_params=_cparams(("arbitrary",)),
        name="route_rank",
    )(msel)


def _slots_kernel(rank_ref, ms_ref, wf_ref, start_ref, slot_ref, wt_ref):
    ms = ms_ref[...]
    wf = wf_ref[...]
    slot_full = rank_ref[...] + start_ref[...]
    lane = lax.broadcasted_iota(jnp.int32, ms.shape, 1).astype(F32)
    out_lane = lax.broadcasted_iota(jnp.int32, slot_ref.shape, 1)
    slots = jnp.zeros(slot_ref.shape, F32)
    wts = jnp.zeros(wt_ref.shape, F32)
    rem = ms
    for kk in range(TOP_K):
        idx = jnp.min(jnp.where(rem > 0.0, lane, float(2 * N_EXPERTS)), axis=-1, keepdims=True)
        pick = lane == idx
        s_k = jnp.sum(jnp.where(pick, slot_full, 0.0), axis=-1, keepdims=True)
        w_k = jnp.sum(jnp.where(pick, wf, 0.0), axis=-1, keepdims=True)
        rem = jnp.where(pick, 0.0, rem)
        slots = jnp.where(out_lane == kk, s_k, slots)
        wts = jnp.where(out_lane == kk, w_k, wts)
    slot_ref[...] = slots.astype(jnp.int32)
    wt_ref[...] = wts


def _slots(rank, msel, wfull, pad_start, tp):
    nt = msel.shape[0]
    tok = pl.BlockSpec((tp, N_EXPERTS), lambda i: (i, 0))
    out = pl.BlockSpec((tp, V7X_LANES), lambda i: (i, 0))
    return pl.pallas_call(
        _slots_kernel,
        grid=(nt // tp,),
        in_specs=[tok, tok, tok, pl.BlockSpec((1, N_EXPERTS), lambda i: (0, 0))],
        out_specs=[out, out],
        out_shape=[jax.ShapeDtypeStruct((nt, V7X_LANES), jnp.int32),
                   jax.ShapeDtypeStruct((nt, V7X_LANES), F32)],
        compiler_params=_cparams(("arbitrary",)),
        name="route_slots",
    )(rank, msel, wfull, pad_start)


def _row_view(ref, row):
    return ref.at[pl.ds(pl.multiple_of(row * ROW_SUB, ROW_SUB), ROW_SUB), :]


def _dispatch_kernel(slot_ref, tail_ref, h_ref, xs_ref, zero_scr, sem, zsem, *, tg, tb):
    @pl.when(pl.program_id(0) == 0)
    def _():
        zero_scr[...] = jnp.zeros(zero_scr.shape, F32)
        for e in range(N_EXPERTS):
            tail = pl.multiple_of(tail_ref[e] * ROW_SUB, tb * ROW_SUB)
            pltpu.make_async_copy(zero_scr, xs_ref.at[pl.ds(tail, tb * ROW_SUB), :], zsem).start()
        for e in range(N_EXPERTS):
            pltpu.make_async_copy(zero_scr, xs_ref.at[pl.ds(0, tb * ROW_SUB), :], zsem).wait()

    def issue(t, carry):
        for kk in range(TOP_K):
            pltpu.make_async_copy(_row_view(h_ref, t), _row_view(xs_ref, slot_ref[t * TOP_K + kk]),
                                  sem).start(priority=kk % 2)
        return carry

    lax.fori_loop(0, tg, issue, 0, unroll=2)

    for kk in range(TOP_K):
        pltpu.make_async_copy(h_ref, xs_ref.at[pl.ds(0, tg * ROW_SUB), :], sem).wait()


def _dispatch(slots_flat, tail_start, h2r, cap, tg, tb):
    nt = h2r.shape[0] // ROW_SUB
    return pl.pallas_call(
        functools.partial(_dispatch_kernel, tg=tg, tb=tb),
        grid=(nt // tg,),
        in_specs=[pl.BlockSpec((tg * TOP_K,), lambda i: (i,), memory_space=pltpu.SMEM),
                  pl.BlockSpec(memory_space=pltpu.SMEM),
                  pl.BlockSpec((tg * ROW_SUB, V7X_LANES), lambda i: (i, 0))],
        out_specs=pl.BlockSpec(memory_space=pl.ANY),
        out_shape=jax.ShapeDtypeStruct((cap * ROW_SUB, V7X_LANES), F32),
        scratch_shapes=[pltpu.VMEM((tb * ROW_SUB, V7X_LANES), F32), pltpu.SemaphoreType.DMA(()),
                        pltpu.SemaphoreType.DMA(())],
        compiler_params=_cparams(("arbitrary",)),
        name="moe_dispatch",
    )(slots_flat, tail_start, h2r)


def _expert_kernel(be_ref, bv_ref, xs_ref, w1_ref, b1_ref, w2_ref, b2_ref, ys_ref, w1_scr, w2_scr, *, tb):
    i = pl.program_id(0)
    valid = bv_ref[i]

    @pl.when((i == 0) | (be_ref[i] != be_ref[jnp.maximum(i - 1, 0)]))
    def _():
        w1_scr[...] = w1_ref[0].astype(BF16)
        w2_scr[...] = w2_ref[0].astype(BF16)

    @pl.when(valid > 0)
    def _():
        xb = _load_row_tiles(xs_ref, tb).astype(BF16)
        gu = jnp.dot(xb, w1_scr[...], preferred_element_type=F32) + b1_ref[0]
        glu = jnp.minimum(gu[:, :D_FF], SWIGLU_LIMIT)
        lin = jnp.clip(gu[:, D_FF:], -SWIGLU_LIMIT, SWIGLU_LIMIT)
        act = (lin + 1.0) * glu * _sigmoid(SWIGLU_ALPHA * glu)
        y = jnp.dot(act.astype(BF16), w2_scr[...], preferred_element_type=F32) + b2_ref[0]
        _store_row_tiles(ys_ref, y)

    @pl.when(valid <= 0)
    def _():
        ys_ref[...] = jnp.zeros(ys_ref.shape, F32)


def _experts(block_expert, block_valid, xs, w1, b1, w2, b2, tb):
    cap = xs.shape[0] // ROW_SUB
    rows = pl.BlockSpec((tb * ROW_SUB, V7X_LANES), lambda i, be, bv: (i, 0))
    grid_spec = pltpu.PrefetchScalarGridSpec(
        num_scalar_prefetch=2,
        grid=(cap // tb,),
        in_specs=[rows,
                  pl.BlockSpec((1, D_MODEL, 2 * D_FF), lambda i, be, bv: (be[i], 0, 0)),
                  pl.BlockSpec((1, 1, 2 * D_FF), lambda i, be, bv: (be[i], 0, 0)),
                  pl.BlockSpec((1, D_FF, D_MODEL), lambda i, be, bv: (be[i], 0, 0)),
                  pl.BlockSpec((1, 1, D_MODEL), lambda i, be, bv: (be[i], 0, 0))],
        out_specs=rows,
        scratch_shapes=[pltpu.VMEM((D_MODEL, 2 * D_FF), BF16), pltpu.VMEM((D_FF, D_MODEL), BF16)],
    )
    return pl.pallas_call(
        functools.partial(_expert_kernel, tb=tb),
        grid_spec=grid_spec,
        out_shape=jax.ShapeDtypeStruct((cap * ROW_SUB, V7X_LANES), F32),
        compiler_params=_cparams(("arbitrary",)),
        name="moe_experts",
    )(block_expert, block_valid, xs, w1, b1.reshape(N_EXPERTS, 1, -1), w2, b2.reshape(N_EXPERTS, 1, -1))


def _combine_kernel(slot_ref, next_slot_ref, ys_ref, wt_ref, x1_ref, mod_ref, g_ref, b_ref, *rest,
                    tc, tile_ends):
    y_refs = rest[:len(tile_ends)]
    buf, sem = rest[len(tile_ends):]
    i = pl.program_id(0)
    cur = i % 2

    def row_copy(slots, t, kk, half):
        return pltpu.make_async_copy(_row_view(ys_ref, slots[t * TOP_K + kk]),
                                     _row_view(buf.at[half, kk], t), sem.at[half])

    def gather(slots, half):
        def issue(t, carry):
            for kk in range(TOP_K):
                row_copy(slots, t, kk, half).start(priority=kk % 2)
            return carry
        lax.fori_loop(0, tc, issue, 0, unroll=8)

    @pl.when(i == 0)
    def _():
        gather(slot_ref, 0)

    @pl.when(i + 1 < pl.num_programs(0))
    def _():
        gather(next_slot_ref, 1 - cur)

    for kk in range(TOP_K):
        pltpu.make_async_copy(ys_ref.at[pl.ds(0, tc * ROW_SUB), :], buf.at[cur, kk], sem.at[cur]).wait()
    wt = wt_ref[...]
    ff = sum(wt[:, kk:kk + 1] * _load_row_tiles(buf.at[cur, kk], tc) for kk in range(TOP_K))
    g2 = mod_ref[0, 5:6, :]
    y = _layer_norm(DN_ALPHA * x1_ref[...] + (1.0 + g2) * ff) * g_ref[...] + b_ref[...]
    i = pl.program_id(0)
    start = 0
    for y_ref, end in zip(y_refs, tile_ends):
        @pl.when((i >= start) & (i < end))
        def _(y_ref=y_ref):
            y_ref[...] = y
        start = end


def _combine(slots_flat, ys, wts, x1, mod, ln2_g, ln2_b, lay, tc):
    nt = lay.n_tokens

    def seq_map(i):
        return (lay.seq_and_pos(i * tc)[0], 0, 0)

    out_specs, tile_ends = lay.trunk_specs(
        tc, lambda local: pl.BlockSpec((tc, D_MODEL), lambda i: (local(i), 0)))
    n_steps = nt // tc
    return pl.pallas_call(
        functools.partial(_combine_kernel, tc=tc, tile_ends=tile_ends),
        grid=(n_steps,),
        in_specs=[pl.BlockSpec((tc * TOP_K,), lambda i: (i,), memory_space=pltpu.SMEM),
                  pl.BlockSpec((tc * TOP_K,), lambda i: (jnp.minimum(i + 1, n_steps - 1),),
                               memory_space=pltpu.SMEM),
                  pl.BlockSpec(memory_space=pl.ANY),
                  pl.BlockSpec((tc, V7X_LANES), lambda i: (i, 0)),
                  pl.BlockSpec((tc, D_MODEL), lambda i: (i, 0)),
                  pl.BlockSpec((1, N_MOD, D_MODEL), seq_map),
                  _const_spec((1, D_MODEL)), _const_spec((1, D_MODEL))],
        out_specs=out_specs,
        out_shape=[jax.ShapeDtypeStruct((b * s, D_MODEL), F32) for b, s in lay.trunks],
        scratch_shapes=[pltpu.VMEM((2, TOP_K, tc * ROW_SUB, V7X_LANES), F32), pltpu.SemaphoreType.DMA((2,))],
        compiler_params=_cparams(("arbitrary",)),
        name="moe_combine",
    )(slots_flat, slots_flat, ys, wts, x1, mod, ln2_g.reshape(1, -1), ln2_b.reshape(1, -1))


def _encoder_layer(xs, cs, p):
    trunks = [(x.shape[0], x.shape[1]) for x in xs]
    lay = _Layout(trunks)
    nt = lay.n_tokens
    t = _tiles(nt)

    x2d = [xi.reshape(-1, D_MODEL) for xi in xs]
    c = jnp.concatenate(cs, axis=0)
    bp = -(-lay.n_seqs // 8) * 8
    c = jnp.pad(c, ((0, bp - lay.n_seqs), (0, 0)))
    mod = _ada_mod(c, p["w_ada"], p["b_ada"]).reshape(bp, N_MOD, D_MODEL)

    q, k, vt3, xr, gg, gs = _inproj(x2d, mod, p["w_in"].astype(BF16), p["q_gain"], p["k_gain"],
                                    _rope_tables(lay.max_seq), lay, t["tm_in"])

    w_lru = jnp.concatenate([p["lru_wa"], p["lru_wx"]], axis=-1).astype(BF16)
    b_lru = jnp.stack([p["lru_ba"], p["lru_bx"]], axis=1)
    lam = p["lru_lam"].reshape(2, 1, D_MODEL)
    o_atts, hfbs = [], []
    tok_off = 0
    for b, s in trunks:
        o_atts.append(_attention(q, k, vt3, tok_off, b, s, t["tq"], t["tm_in"]))
        hfbs.append(_scan(xr, p["conv_w"], p["conv_b"].reshape(1, -1), w_lru, b_lru, lam,
                          tok_off, b, s, t["tt"]))
        tok_off += b * s

    wr_hi = p["w_router"].astype(BF16)
    wr_lo = (p["w_router"] - wr_hi.astype(F32)).astype(BF16)
    x1, h2r, wfull, msel = _mid(o_atts, hfbs, gg, gs, x2d, mod, p["w_pa"].astype(BF16),
                               p["w_pr"].astype(BF16), p["w_out"].astype(BF16), p["ln1_g"], p["ln1_b"],
                               jnp.concatenate([wr_hi, wr_lo], axis=1), p["b_router"], lay, t["tm_mid"])

    tb = t["tb"]
    rank, cnt = _rank(msel, t["tp"])
    counts = cnt[0].astype(jnp.int32)
    padded = (counts + tb - 1) // tb * tb
    pad_end = jnp.cumsum(padded)
    pad_start = pad_end - padded
    n_blocks = nt * TOP_K // tb + N_EXPERTS
    blk0 = jnp.arange(n_blocks, dtype=jnp.int32) * tb
    block_expert = jnp.minimum(jnp.sum(pad_end[None, :] <= blk0[:, None], axis=1), N_EXPERTS - 1).astype(jnp.int32)
    block_valid = jnp.clip(pad_start[block_expert] + counts[block_expert] - blk0, 0, tb).astype(jnp.int32)
    slots, wts = _slots(rank, msel, wfull, pad_start.astype(F32).reshape(1, N_EXPERTS), t["tp"])
    slots_flat = slots[:, :TOP_K].reshape(-1)

    tail_start = jnp.where(padded > 0, pad_end - tb, pad_end[-1] - tb).astype(jnp.int32)
    xs_rows = _dispatch(slots_flat, tail_start, h2r, n_blocks * tb, t["tg"], tb)
    ys = _experts(block_expert, block_valid, xs_rows, p["w1"], p["b1"], p["w2"], p["b2"], tb)
    ys_out = _combine(slots_flat, ys, wts, x1, mod, p["ln2_g"], p["ln2_b"], lay, t["tc"])
    return [y.reshape(b, s, D_MODEL) for y, (b, s) in zip(ys_out, trunks)]


_PARAM_NAMES = ("w_ada", "b_ada", "w_in", "q_gain", "k_gain", "conv_w", "conv_b", "lru_wa", "lru_ba",
                "lru_wx", "lru_bx", "lru_lam", "w_pa", "w_pr", "w_out", "ln1_g", "ln1_b", "w_router",
                "b_router", "w1", "b1", "w2", "b2", "ln2_g", "ln2_b")


def kernel(x_prompt, x_sample, c_prompt, c_sample, w_ada, b_ada, w_in, q_gain, k_gain, conv_w, conv_b, lru_wa, lru_ba, lru_wx, lru_bx, lru_lam, w_pa, w_pr, w_out, ln1_g, ln1_b, w_router, b_router, w1, b1, w2, b2, ln2_g, ln2_b):
    stacked = (w_ada, b_ada, w_in, q_gain, k_gain, conv_w, conv_b, lru_wa, lru_ba, lru_wx, lru_bx,
               lru_lam, w_pa, w_pr, w_out, ln1_g, ln1_b, w_router, b_router, w1, b1, w2, b2, ln2_g, ln2_b)
    xs, cs = [x_prompt, x_sample], [c_prompt, c_sample]
    for layer in range(DEPTH):
        p = {name: arr[layer] for name, arr in zip(_PARAM_NAMES, stacked)}
        xs = _encoder_layer(xs, cs, p)
    return (xs[0], xs[1])
```

```python
import functools
import math

import jax
import jax.numpy as jnp
from jax import lax
from jax.experimental import pallas as pl
from jax.experimental.pallas import tpu as pltpu

F32 = jnp.float32
BF16 = jnp.bfloat16

D_MODEL = 1024
GRID_W = 64
N_HEADS = 8
N_KV_HEADS = 2
HEAD_DIM = 128
GROUPS = N_HEADS // N_KV_HEADS
KV_WIDTH = N_KV_HEADS * HEAD_DIM
ROPE_THETA = 10000.0
RNN_BLOCKS = 8
RNN_BLOCK_W = D_MODEL // RNN_BLOCKS
CONV_W = 4
LRU_C = 8.0
N_EXPERTS = 32
TOP_K = 4
D_FF = D_MODEL
SWIGLU_LIMIT = 7.0
SWIGLU_ALPHA = 1.702
DEPTH = 1
DN_ALPHA = (2 * DEPTH) ** 0.25
LN_EPS = 1e-5
RMS_EPS = 1e-6
N_MOD = 6
_Q0 = 0
_K0 = _Q0 + D_MODEL
_V0 = _K0 + KV_WIDTH
_XR0 = _V0 + KV_WIDTH
_GR0 = _XR0 + D_MODEL
_GL0 = _GR0 + D_MODEL
IN_WIDTH = _GL0 + 2 * D_MODEL

V7X_LANES = 128
V7X_SUBLANES = 8
V7X_BF16_SUBLANES = 16
V7X_VMEM_LIMIT_BYTES = 56 * 1024 * 1024
HALO = V7X_BF16_SUBLANES
VT_ROWS = HEAD_DIM + V7X_BF16_SUBLANES


def _tiles(n_tokens):
    big = n_tokens >= 4096
    return dict(
        tm_in=512 if big else 128,
        tq=256 if big else 128,
        tt=1024 if big else 128,
        tm_mid=512 if big else 128,
        tp=512 if big else 128,
        tg=1024 if big else 256,
        tb=512 if big else 128,
        tc=512 if big else 256,
    )


def _cparams(sem):
    return pltpu.CompilerParams(dimension_semantics=sem, vmem_limit_bytes=V7X_VMEM_LIMIT_BYTES)


def _const_spec(shape):
    nd = len(shape)
    return pl.BlockSpec(shape, lambda *_: (0,) * nd, pipeline_mode=pl.Buffered(1))


def _layer_norm(x):
    mu = jnp.mean(x, axis=-1, keepdims=True)
    xc = x - mu
    var = jnp.mean(xc * xc, axis=-1, keepdims=True)
    return xc * lax.rsqrt(var + LN_EPS)


def _sigmoid(x):
    return 1.0 / (1.0 + jnp.exp(-x))


ROW_SUB = D_MODEL // V7X_LANES
assert ROW_SUB == V7X_SUBLANES, "one token row must fill exactly one f32 tile"


def _store_row_tiles(ref, val):
    n = val.shape[0]
    for j in range(ROW_SUB):
        ref[pl.ds(j, n, stride=ROW_SUB), :] = val[:, j * V7X_LANES:(j + 1) * V7X_LANES]


def _load_row_tiles(ref, n):
    return jnp.concatenate([ref[pl.ds(j, n, stride=ROW_SUB), :] for j in range(ROW_SUB)], axis=1)


def _ada_kernel(c_ref, w_ref, b_ref, o_ref):
    c = c_ref[...]
    s = c * _sigmoid(c)
    o_ref[...] = jnp.dot(s, w_ref[...], preferred_element_type=F32,
                         precision=lax.Precision.HIGHEST) + b_ref[...]


def _ada_mod(c_all, w_ada, b_ada):
    bp = c_all.shape[0]
    ncol = w_ada.shape[1]
    return pl.pallas_call(
        _ada_kernel,
        grid=(ncol // D_MODEL,),
        in_specs=[pl.BlockSpec((bp, D_MODEL), lambda j: (0, 0)),
                  pl.BlockSpec((D_MODEL, D_MODEL), lambda j: (0, j)),
                  pl.BlockSpec((1, D_MODEL), lambda j: (0, j))],
        out_specs=pl.BlockSpec((bp, D_MODEL), lambda j: (0, j)),
        out_shape=jax.ShapeDtypeStruct((bp, ncol), F32),
        compiler_params=_cparams(("arbitrary",)),
        name="ada_mod",
    )(c_all, w_ada, b_ada.reshape(1, ncol))


class _Layout:
    def __init__(self, trunks):
        self.trunks = tuple(trunks)
        self.n_tokens = sum(b * s for b, s in trunks)
        self.n_seqs = sum(b for b, _ in trunks)
        self.max_seq = max(s for _, s in trunks)

    def seq_and_pos(self, t0):
        seq = jnp.int32(0)
        pos = jnp.int32(0)
        tok_off, seq_off = 0, 0
        for b, s in self.trunks:
            inside = (t0 >= tok_off) & (t0 < tok_off + b * s)
            rel = jnp.maximum(t0 - tok_off, 0)
            seq = jnp.where(inside, seq_off + rel // s, seq)
            pos = jnp.where(inside, rel % s, pos)
            tok_off += b * s
            seq_off += b
        return seq, pos

    def trunk_specs(self, tm, make_spec):
        specs, ends, t0 = [], [], 0
        for b, s in self.trunks:
            n_t = b * s // tm
            specs.append(make_spec(functools.partial(_clamped_local, t0=t0, n_t=n_t)))
            t0 += n_t
            ends.append(t0)
        return specs, tuple(ends)


def _clamped_local(i, *, t0, n_t):
    return jnp.clip(i - t0, 0, n_t - 1)


def _owner_value(loads, i, tile_ends):
    val = loads[-1]()
    for j in range(len(loads) - 2, -1, -1):
        val = jnp.where(i < tile_ends[j], loads[j](), val)
    return val


def _rope_tables(max_seq):
    n_rows = max_seq // GRID_W
    axis_dim = HEAD_DIM // 2
    inv = ROPE_THETA ** (-jnp.arange(0, axis_dim, 2, dtype=F32) / axis_dim)
    ar = jnp.arange(n_rows, dtype=F32)[:, None] * inv
    ac = jnp.arange(GRID_W, dtype=F32)[:, None] * inv
    cr, sr = (jnp.repeat(f(ar), GRID_W, axis=0) for f in (jnp.cos, jnp.sin))
    cc, sc = (jnp.tile(f(ac), (n_rows, 1)) for f in (jnp.cos, jnp.sin))
    z = jnp.zeros_like(sr)
    cos_t = jnp.concatenate([cr, cr, cc, cc], axis=-1)
    up_t = jnp.concatenate([-sr, z, -sc, z], axis=-1)
    dn_t = jnp.concatenate([z, sr, z, sc], axis=-1)
    return cos_t, up_t, dn_t


def _inproj_kernel(*refs, tile_ends):
    n_tr = len(tile_ends)
    x_refs = refs[:n_tr]
    (mod_ref, w_ref, qg_ref, kg_ref, cos_ref, up_ref, dn_ref,
     q_ref, k_ref, vt_ref, xr_ref, gg_ref, gs_ref) = refs[n_tr:]
    x = _owner_value([lambda r=r: r[...] for r in x_refs], pl.program_id(0), tile_ends)
    sh1 = mod_ref[0, 0:1, :]
    sc1 = mod_ref[0, 1:2, :]
    h = (_layer_norm(x) * (1.0 + sc1) + sh1).astype(BF16)
    cos_t, up_t, dn_t = cos_ref[...], up_ref[...], dn_ref[...]

    def proj(c0, width):
        return jnp.dot(h, w_ref[:, c0:c0 + width], preferred_element_type=F32)

    def norm_rope(z, gain):
        ms = jnp.mean(z * z, axis=-1, keepdims=True)
        y = z * lax.rsqrt(ms + RMS_EPS) * gain
        return (y * cos_t + pltpu.roll(y, HEAD_DIM - 32, 1) * up_t
                + pltpu.roll(y, 32, 1) * dn_t)

    xr_ref[...] = proj(_XR0, D_MODEL).astype(BF16)
    zq = proj(_Q0, D_MODEL)
    qg = qg_ref[...] * (HEAD_DIM ** -0.5 * math.log2(math.e))
    for hd in range(N_HEADS):
        sl = slice(hd * HEAD_DIM, (hd + 1) * HEAD_DIM)
        q_ref[:, sl] = norm_rope(zq[:, sl], qg).astype(BF16)
    zk = proj(_K0, KV_WIDTH)
    kg = kg_ref[...]
    for hd in range(N_KV_HEADS):
        sl = slice(hd * HEAD_DIM, (hd + 1) * HEAD_DIM)
        k_ref[:, sl] = norm_rope(zk[:, sl], kg).astype(BF16)
    zvt = proj(_V0, KV_WIDTH).T.astype(BF16)
    pad_row = lax.broadcasted_iota(jnp.int32, (VT_ROWS - HEAD_DIM, zvt.shape[1]), 0)
    ones_pad = jnp.where(pad_row == 0, 1.0, 0.0).astype(BF16)
    for hd in range(N_KV_HEADS):
        vt_ref[0, hd * VT_ROWS:hd * VT_ROWS + HEAD_DIM, :] = zvt[hd * HEAD_DIM:(hd + 1) * HEAD_DIM]
        vt_ref[0, hd * VT_ROWS + HEAD_DIM:(hd + 1) * VT_ROWS, :] = ones_pad
    gg_ref[...] = jax.nn.gelu(proj(_GR0, D_MODEL), approximate=True).astype(BF16)
    gs_ref[...] = _sigmoid(proj(_GL0, 2 * D_MODEL)).astype(BF16)


def _inproj(xs, mod, w_in, q_gain, k_gain, tables, lay, tm):
    nt = lay.n_tokens
    cos_t, up_t, dn_t = tables

    def seq_map(i):
        return (lay.seq_and_pos(i * tm)[0], 0, 0)

    def pos_map(i):
        return (lay.seq_and_pos(i * tm)[1] // tm, 0)

    tok = lambda w: pl.BlockSpec((tm, w), lambda i: (i, 0))
    rope = pl.BlockSpec((tm, HEAD_DIM), pos_map)
    x_specs, tile_ends = lay.trunk_specs(
        tm, lambda local: pl.BlockSpec((tm, D_MODEL), lambda i: (local(i), 0)))
    return pl.pallas_call(
        functools.partial(_inproj_kernel, tile_ends=tile_ends),
        grid=(nt // tm,),
        in_specs=x_specs + [
                  pl.BlockSpec((1, N_MOD, D_MODEL), seq_map),
                  _const_spec((D_MODEL, IN_WIDTH)),
                  _const_spec((1, HEAD_DIM)), _const_spec((1, HEAD_DIM)),
                  rope, rope, rope],
        out_specs=[tok(D_MODEL), tok(KV_WIDTH),
                   pl.BlockSpec((1, N_KV_HEADS * VT_ROWS, tm), lambda i: (i, 0, 0)),
                   tok(D_MODEL), tok(D_MODEL), tok(2 * D_MODEL)],
        out_shape=[jax.ShapeDtypeStruct((nt, D_MODEL), BF16),
                   jax.ShapeDtypeStruct((nt, KV_WIDTH), BF16),
                   jax.ShapeDtypeStruct((nt // tm, N_KV_HEADS * VT_ROWS, tm), BF16),
                   jax.ShapeDtypeStruct((nt, D_MODEL), BF16),
                   jax.ShapeDtypeStruct((nt, D_MODEL), BF16),
                   jax.ShapeDtypeStruct((nt, 2 * D_MODEL), BF16)],
        compiler_params=_cparams(("arbitrary",)),
        name="in_proj",
    )(*xs, mod, w_in, q_gain.reshape(1, HEAD_DIM), k_gain.reshape(1, HEAD_DIM), cos_t, up_t, dn_t)


def _attn_kernel(q_ref, k_ref, vt_ref, o_ref, qt_all, s_all, m_all, acc_all, *,
                 tq, tk, n_kv, group, n_sub):
    for sub in range(n_sub):
        rows = slice(sub * tq, (sub + 1) * tq)
        _attn_tile(q_ref.at[rows, :], k_ref, vt_ref, o_ref.at[rows, :], qt_all.at[sub], s_all.at[sub],
                   m_all.at[sub], acc_all.at[sub], tq=tq, tk=tk, n_kv=n_kv, group=group)


def _attn_tile(q_ref, k_ref, vt_ref, o_ref, qt_scr, s_scr, m_scr, acc_scr, *, tq, tk, n_kv, group):
    for g in range(GROUPS):
        qg = q_ref[:, g * HEAD_DIM:(g + 1) * HEAD_DIM].astype(F32)
        qt_scr[:, g * tq:(g + 1) * tq] = qg.T.astype(BF16)
    m_scr[...] = jnp.full(m_scr.shape, -jnp.inf, F32)
    acc_scr[...] = jnp.zeros(acc_scr.shape, F32)

    def scores(j, slot):
        kt = k_ref[pl.ds(pl.multiple_of(j * tk, tk), tk), :]
        s_scr[slot] = jnp.dot(kt, qt_scr[...], preferred_element_type=F32)

    def accumulate(j, slot):
        s = s_scr[slot]
        m_old = m_scr[...]
        m_new = jnp.maximum(m_old, jnp.max(s, axis=0, keepdims=True))
        alpha = jnp.exp2(m_old - m_new)
        p = jnp.exp2(s - m_new).astype(BF16)
        pv = jnp.dot(vt_ref[j], p, preferred_element_type=F32)
        acc_scr[...] = alpha * acc_scr[...] + pv
        m_scr[...] = m_new

    scores(0, 0)

    def body(i, carry):
        j = group * i
        for u in range(group):
            scores(jnp.minimum(j + u + 1, n_kv - 1), (u + 1) % 2)
            accumulate(j + u, u % 2)
        return carry

    lax.fori_loop(0, n_kv // group, body, 0)
    out = acc_scr[0:HEAD_DIM, :] / acc_scr[HEAD_DIM:HEAD_DIM + 1, :]
    for g in range(GROUPS):
        o_ref[:, g * HEAD_DIM:(g + 1) * HEAD_DIM] = out[:, g * tq:(g + 1) * tq].T.astype(BF16)


def _attention(q, k, vt3, tok_off, batch, seq, tq, tk):
    n_kv = seq // tk
    group = next((g for g in (8, 4) if n_kv % g == 0 and n_kv // g >= 2), 2)
    assert n_kv % group == 0
    n_sub = 2 if (seq // tq) % 2 == 0 else 1
    tqs = n_sub * tq
    qrow0 = tok_off // tqs
    srow0 = tok_off // seq
    gw = GROUPS * HEAD_DIM
    n_q = seq // tqs
    return pl.pallas_call(
        functools.partial(_attn_kernel, tq=tq, tk=tk, n_kv=n_kv, group=group, n_sub=n_sub),
        grid=(batch, N_KV_HEADS, n_q),
        in_specs=[pl.BlockSpec((tqs, gw), lambda b, h, i: (qrow0 + b * n_q + i, h)),
                  pl.BlockSpec((seq, HEAD_DIM), lambda b, h, i: (srow0 + b, h)),
                  pl.BlockSpec((n_kv, VT_ROWS, tk), lambda b, h, i: (srow0 + b, h, 0))],
        out_specs=pl.BlockSpec((tqs, gw), lambda b, h, i: (b * n_q + i, h)),
        out_shape=jax.ShapeDtypeStruct((batch * seq, D_MODEL), BF16),
        scratch_shapes=[pltpu.VMEM((n_sub, HEAD_DIM, GROUPS * tq), BF16),
                        pltpu.VMEM((n_sub, 2, tk, GROUPS * tq), F32),
                        pltpu.VMEM((n_sub, 1, GROUPS * tq), F32),
                        pltpu.VMEM((n_sub, VT_ROWS, GROUPS * tq), F32)],
        compiler_params=_cparams(("arbitrary", "arbitrary", "arbitrary")),
        name="attention",
    )(q, k, vt3)


def _scan_kernel(cur_ref, prev_ref, next_ref, cw_ref, cb_ref, w_ref, b_ref, lam_ref, o_ref,
                 xc_scr, a_scr, u_scr, h_scr, carry_scr, *, tt, n_chunks):
    d = pl.program_id(1)
    c = pl.program_id(2)
    chunk = jnp.where(d == 0, c, n_chunks - 1 - c)

    @pl.when(c == 0)
    def _():
        carry_scr[...] = jnp.zeros(carry_scr.shape, F32)

    keep_prev = jnp.where(chunk == 0, 0.0, 1.0)
    keep_next = jnp.where(chunk == n_chunks - 1, 0.0, 1.0)
    cur = cur_ref[...].astype(F32)
    taps = [cw_ref[j:j + 1, :] for j in range(CONV_W)]
    xc_scr[...] = (cb_ref[...] + taps[2] * cur + taps[1] * pltpu.roll(cur, 1, 0)
                   + taps[0] * pltpu.roll(cur, 2, 0) + taps[3] * pltpu.roll(cur, tt - 1, 0))
    sub = V7X_SUBLANES
    head = jnp.concatenate([prev_ref[...].astype(F32)[HALO - sub:HALO] * keep_prev, cur[0:2 * sub]], axis=0)
    tail = jnp.concatenate([cur[tt - 2 * sub:tt], next_ref[...].astype(F32)[0:sub] * keep_next], axis=0)
    first = sum(taps[j] * head[sub - 2 + j:2 * sub - 2 + j] for j in range(CONV_W))
    last = sum(taps[j] * tail[sub - 2 + j:2 * sub - 2 + j] for j in range(CONV_W))
    xc_scr[0:sub, :] = cb_ref[...] + first
    xc_scr[tt - sub:tt, :] = cb_ref[...] + last
    xc = xc_scr[...]
    xcb = xc.astype(BF16)

    lam = lam_ref[0]
    y = jnp.exp(-jnp.abs(lam))
    w1p = 1.0 + y
    log1p_y = jnp.where(w1p == 1.0, y, jnp.log(w1p) * y / jnp.where(w1p == 1.0, 1.0, w1p - 1.0))
    neg_c_sp = (-LRU_C * math.log2(math.e)) * (jnp.maximum(-lam, 0.0) + log1p_y)

    for n in range(RNN_BLOCKS):
        sl = slice(n * RNN_BLOCK_W, (n + 1) * RNN_BLOCK_W)
        pre = jnp.dot(xcb[:, sl], w_ref[0, n], preferred_element_type=F32)
        r = _sigmoid(pre[:, :RNN_BLOCK_W] + b_ref[0, 0:1, sl])
        i = _sigmoid(pre[:, RNN_BLOCK_W:] + b_ref[0, 1:2, sl])
        a = jnp.exp2(r * neg_c_sp[:, sl])
        a_scr[:, sl] = a
        u_scr[:, sl] = jnp.sqrt(1.0 - a * a) * (i * xc[:, sl])

    def step(t, h):
        row = jnp.where(d == 0, t, tt - 1 - t)
        h = a_scr[pl.ds(row, 1), :] * h + u_scr[pl.ds(row, 1), :]
        h_scr[pl.ds(row, 1), :] = h
        return h

    carry_scr[...] = lax.fori_loop(0, tt, step, carry_scr[...], unroll=8)
    o_ref[0] = h_scr[...].astype(BF16)


def _scan(xr, conv_w, conv_b, w_lru, b_lru, lam, tok_off, batch, seq, tt):
    nt = xr.shape[0]
    n_chunks = seq // tt
    row0 = tok_off // tt
    hrow0 = tok_off // HALO
    hpc = tt // HALO
    n_halo = nt // HALO

    def chunk_of(d, c):
        return jnp.where(d == 0, c, n_chunks - 1 - c)

    def prev_map(b, d, c):
        return (jnp.maximum(hrow0 + (b * n_chunks + chunk_of(d, c)) * hpc - 1, 0), 0)

    def next_map(b, d, c):
        return (jnp.minimum(hrow0 + (b * n_chunks + chunk_of(d, c) + 1) * hpc, n_halo - 1), 0)

    return pl.pallas_call(
        functools.partial(_scan_kernel, tt=tt, n_chunks=n_chunks),
        grid=(batch, 2, n_chunks),
        in_specs=[pl.BlockSpec((tt, D_MODEL), lambda b, d, c: (row0 + b * n_chunks + chunk_of(d, c), 0)),
                  pl.BlockSpec((HALO, D_MODEL), prev_map),
                  pl.BlockSpec((HALO, D_MODEL), next_map),
                  pl.BlockSpec((CONV_W, D_MODEL), lambda b, d, c: (0, 0)),
                  pl.BlockSpec((1, D_MODEL), lambda b, d, c: (0, 0)),
                  pl.BlockSpec((1, RNN_BLOCKS, RNN_BLOCK_W, 2 * RNN_BLOCK_W), lambda b, d, c: (d, 0, 0, 0)),
                  pl.BlockSpec((1, 2, D_MODEL), lambda b, d, c: (d, 0, 0)),
                  pl.BlockSpec((1, 1, D_MODEL), lambda b, d, c: (d, 0, 0))],
        out_specs=pl.BlockSpec((1, tt, D_MODEL), lambda b, d, c: (d, b * n_chunks + chunk_of(d, c), 0)),
        out_shape=jax.ShapeDtypeStruct((2, batch * seq, D_MODEL), BF16),
        scratch_shapes=[pltpu.VMEM((tt, D_MODEL), F32),
                        pltpu.VMEM((tt, D_MODEL), F32),
                        pltpu.VMEM((tt, D_MODEL), F32),
                        pltpu.VMEM((tt, D_MODEL), F32),
                        pltpu.VMEM((1, D_MODEL), F32)],
        compiler_params=_cparams(("arbitrary", "arbitrary", "arbitrary")),
        name="lru_scan",
    )(xr, xr, xr, conv_w, conv_b, w_lru, b_lru, lam)


def _mid_kernel(*refs, tile_ends):
    n_tr = len(tile_ends)
    oa_refs = refs[:n_tr]
    h_refs = refs[n_tr:3 * n_tr]
    x_refs = refs[3 * n_tr:4 * n_tr]
    (gg_ref, gs_ref, mod_ref, wpa_ref, wpr_ref, wo_ref, g1_ref, b1_ref, wr_ref, br_ref,
     x1_ref, h2_ref, wf_ref, ms_ref) = refs[4 * n_tr:]
    i = pl.program_id(0)
    oa = _owner_value([lambda r=r: r[...] for r in oa_refs], i, tile_ends)
    h_fwd = _owner_value([lambda r=h_refs[2 * j]: r[0] for j in range(n_tr)], i, tile_ends)
    h_bwd = _owner_value([lambda r=h_refs[2 * j + 1]: r[0] for j in range(n_tr)], i, tile_ends)
    hsum = h_fwd.astype(F32) + h_bwd.astype(F32)
    x_in = _owner_value([lambda r=r: r[...] for r in x_refs], i, tile_ends)
    o_att = jnp.dot(oa, wpa_ref[...], preferred_element_type=F32)
    rec = hsum.astype(BF16) * gg_ref[...]
    o_rec = jnp.dot(rec, wpr_ref[...], preferred_element_type=F32)
    gs = gs_ref[...].astype(F32)
    merged = gs[:, :D_MODEL] * o_att + gs[:, D_MODEL:] * o_rec
    mix = jnp.dot(merged.astype(BF16), wo_ref[...], preferred_element_type=F32)
    g1 = mod_ref[0, 2:3, :]
    sh2 = mod_ref[0, 3:4, :]
    sc2 = mod_ref[0, 4:5, :]
    x1 = _layer_norm(DN_ALPHA * x_in + (1.0 + g1) * mix) * g1_ref[...] + b1_ref[...]
    x1_ref[...] = x1
    h2 = _layer_norm(x1) * (1.0 + sc2) + sh2
    _store_row_tiles(h2_ref, h2)
    tm = h2.shape[0]
    h2_hi = h2.astype(BF16)
    h2_lo = (h2 - h2_hi.astype(F32)).astype(BF16)
    cross = jnp.dot(jnp.concatenate([h2_hi, h2_lo], axis=0), wr_ref[...], preferred_element_type=F32)
    logits = ((cross[:tm, :N_EXPERTS] + cross[:tm, N_EXPERTS:])
              + (cross[tm:, :N_EXPERTS] + cross[tm:, N_EXPERTS:])) + br_ref[...]
    lane = lax.broadcasted_iota(jnp.int32, logits.shape, 1).astype(F32)
    rem = logits
    sel = jnp.zeros(logits.shape, F32)
    top = None
    denom = None
    for kk in range(TOP_K):
        m = jnp.max(rem, axis=-1, keepdims=True)
        idx = jnp.min(jnp.where(rem == m, lane, float(N_EXPERTS)), axis=-1, keepdims=True)
        pick = lane == idx
        sel = jnp.where(pick, 1.0, sel)
        rem = jnp.where(pick, -jnp.inf, rem)
        if kk == 0:
            top = m
            denom = jnp.ones_like(m)
        else:
            denom = denom + jnp.exp(m - top)
    wf_ref[...] = jnp.where(sel > 0.0, jnp.exp(logits - top) / denom, 0.0)
    ms_ref[...] = sel


def _mid(o_atts, hfbs, gg, gs, xs, mod, w_pa, w_pr, w_out, ln1_g, ln1_b, w_router, b_router, lay, tm):
    nt = lay.n_tokens

    def seq_map(i):
        return (lay.seq_and_pos(i * tm)[0], 0, 0)

    tok = lambda w: pl.BlockSpec((tm, w), lambda i: (i, 0))
    vec = lambda w: _const_spec((1, w))
    row_specs, tile_ends = lay.trunk_specs(
        tm, lambda local: pl.BlockSpec((tm, D_MODEL), lambda i: (local(i), 0)))
    h_pairs, _ = lay.trunk_specs(
        tm, lambda local: [pl.BlockSpec((1, tm, D_MODEL), lambda i, d=d: (d, local(i), 0)) for d in range(2)])
    h_specs = [spec for pair in h_pairs for spec in pair]
    h_args = [hfb for hfb in hfbs for _ in range(2)]
    return pl.pallas_call(
        functools.partial(_mid_kernel, tile_ends=tile_ends),
        grid=(nt // tm,),
        in_specs=row_specs + h_specs + row_specs + [
                  tok(D_MODEL), tok(2 * D_MODEL),
                  pl.BlockSpec((1, N_MOD, D_MODEL), seq_map),
                  _const_spec((D_MODEL, D_MODEL)), _const_spec((D_MODEL, D_MODEL)),
                  _const_spec((D_MODEL, D_MODEL)),
                  vec(D_MODEL), vec(D_MODEL),
                  _const_spec((D_MODEL, 2 * N_EXPERTS)), vec(N_EXPERTS)],
        out_specs=[tok(D_MODEL), pl.BlockSpec((ROW_SUB * tm, V7X_LANES), lambda i: (i, 0)),
                   tok(N_EXPERTS), tok(N_EXPERTS)],
        out_shape=[jax.ShapeDtypeStruct((nt, D_MODEL), F32),
                   jax.ShapeDtypeStruct((ROW_SUB * nt, V7X_LANES), F32),
                   jax.ShapeDtypeStruct((nt, N_EXPERTS), F32),
                   jax.ShapeDtypeStruct((nt, N_EXPERTS), F32)],
        compiler_params=_cparams(("arbitrary",)),
        name="merge_router",
    )(*o_atts, *h_args, *xs, gg, gs, mod, w_pa, w_pr, w_out, ln1_g.reshape(1, -1), ln1_b.reshape(1, -1),
      w_router, b_router.reshape(1, -1))


def _rank_kernel(ms_ref, rank_ref, cnt_ref, tri_scr, run_scr, *, tp):
    @pl.when(pl.program_id(0) == 0)
    def _():
        r = lax.broadcasted_iota(jnp.int32, (tp, tp), 0)
        c = lax.broadcasted_iota(jnp.int32, (tp, tp), 1)
        tri_scr[...] = jnp.where(c < r, 1.0, 0.0).astype(BF16)
        run_scr[...] = jnp.zeros(run_scr.shape, F32)

    ms = ms_ref[...]
    before = jnp.dot(tri_scr[...], ms.astype(BF16), preferred_element_type=F32)
    rank_ref[...] = before + run_scr[...]
    run_scr[...] = run_scr[...] + jnp.sum(ms, axis=0, keepdims=True)
    cnt_ref[...] = jnp.broadcast_to(run_scr[...], cnt_ref.shape)


def _rank(msel, tp):
    nt = msel.shape[0]
    return pl.pallas_call(
        functools.partial(_rank_kernel, tp=tp),
        grid=(nt // tp,),
        in_specs=[pl.BlockSpec((tp, N_EXPERTS), lambda i: (i, 0))],
        out_specs=[pl.BlockSpec((tp, N_EXPERTS), lambda i: (i, 0)),
                   pl.BlockSpec((V7X_SUBLANES, N_EXPERTS), lambda i: (0, 0))],
        out_shape=[jax.ShapeDtypeStruct((nt, N_EXPERTS), F32),
                   jax.ShapeDtypeStruct((V7X_SUBLANES, N_EXPERTS), F32)],
        scratch_shapes=[pltpu.VMEM((tp, tp), BF16), pltpu.VMEM((1, N_EXPERTS), F32)],
        compiler_params=_cparams(("arbitrary",)),
        name="route_rank",
    )(msel)


def _slots_kernel(rank_ref, ms_ref, wf_ref, start_ref, slot_ref, wt_ref):
    ms = ms_ref[...]
    wf = wf_ref[...]
    slot_full = rank_ref[...] + start_ref[...]
    lane = lax.broadcasted_iota(jnp.int32, ms.shape, 1).astype(F32)
    out_lane = lax.broadcasted_iota(jnp.int32, slot_ref.shape, 1)
    slots = jnp.zeros(slot_ref.shape, F32)
    wts = jnp.zeros(wt_ref.shape, F32)
    rem = ms
    for kk in range(TOP_K):
        idx = jnp.min(jnp.where(rem > 0.0, lane, float(2 * N_EXPERTS)), axis=-1, keepdims=True)
        pick = lane == idx
        s_k = jnp.sum(jnp.where(pick, slot_full, 0.0), axis=-1, keepdims=True)
        w_k = jnp.sum(jnp.where(pick, wf, 0.0), axis=-1, keepdims=True)
        rem = jnp.where(pick, 0.0, rem)
        slots = jnp.where(out_lane == kk, s_k, slots)
        wts = jnp.where(out_lane == kk, w_k, wts)
    slot_ref[...] = slots.astype(jnp.int32)
    wt_ref[...] = wts


def _slots(rank, msel, wfull, pad_start, tp):
    nt = msel.shape[0]
    tok = pl.BlockSpec((tp, N_EXPERTS), lambda i: (i, 0))
    out = pl.BlockSpec((tp, V7X_LANES), lambda i: (i, 0))
    return pl.pallas_call(
        _slots_kernel,
        grid=(nt // tp,),
        in_specs=[tok, tok, tok, pl.BlockSpec((1, N_EXPERTS), lambda i: (0, 0))],
        out_specs=[out, out],
        out_shape=[jax.ShapeDtypeStruct((nt, V7X_LANES), jnp.int32),
                   jax.ShapeDtypeStruct((nt, V7X_LANES), F32)],
        compiler_params=_cparams(("arbitrary",)),
        name="route_slots",
    )(rank, msel, wfull, pad_start)


def _row_view(ref, row):
    return ref.at[pl.ds(pl.multiple_of(row * ROW_SUB, ROW_SUB), ROW_SUB), :]


def _dispatch_kernel(slot_ref, tail_ref, h_ref, xs_ref, zero_scr, sem, zsem, *, tg, tb):
    @pl.when(pl.program_id(0) == 0)
    def _():
        zero_scr[...] = jnp.zeros(zero_scr.shape, F32)
        for e in range(N_EXPERTS):
            tail = pl.multiple_of(tail_ref[e] * ROW_SUB, tb * ROW_SUB)
            pltpu.make_async_copy(zero_scr, xs_ref.at[pl.ds(tail, tb * ROW_SUB), :], zsem).start()
        for e in range(N_EXPERTS):
            pltpu.make_async_copy(zero_scr, xs_ref.at[pl.ds(0, tb * ROW_SUB), :], zsem).wait()

    def issue(t, carry):
        for kk in range(TOP_K):
            pltpu.make_async_copy(_row_view(h_ref, t), _row_view(xs_ref, slot_ref[t * TOP_K + kk]),
                                  sem).start(priority=kk % 2)
        return carry

    lax.fori_loop(0, tg, issue, 0, unroll=2)

    for kk in range(TOP_K):
        pltpu.make_async_copy(h_ref, xs_ref.at[pl.ds(0, tg * ROW_SUB), :], sem).wait()


def _dispatch(slots_flat, tail_start, h2r, cap, tg, tb):
    nt = h2r.shape[0] // ROW_SUB
    return pl.pallas_call(
        functools.partial(_dispatch_kernel, tg=tg, tb=tb),
        grid=(nt // tg,),
        in_specs=[pl.BlockSpec((tg * TOP_K,), lambda i: (i,), memory_space=pltpu.SMEM),
                  pl.BlockSpec(memory_space=pltpu.SMEM),
                  pl.BlockSpec((tg * ROW_SUB, V7X_LANES), lambda i: (i, 0))],
        out_specs=pl.BlockSpec(memory_space=pl.ANY),
        out_shape=jax.ShapeDtypeStruct((cap * ROW_SUB, V7X_LANES), F32),
        scratch_shapes=[pltpu.VMEM((tb * ROW_SUB, V7X_LANES), F32), pltpu.SemaphoreType.DMA(()),
                        pltpu.SemaphoreType.DMA(())],
        compiler_params=_cparams(("arbitrary",)),
        name="moe_dispatch",
    )(slots_flat, tail_start, h2r)


def _expert_kernel(be_ref, bv_ref, xs_ref, w1_ref, b1_ref, w2_ref, b2_ref, ys_ref, w1_scr, w2_scr, *, tb):
    i = pl.program_id(0)
    valid = bv_ref[i]

    @pl.when((i == 0) | (be_ref[i] != be_ref[jnp.maximum(i - 1, 0)]))
    def _():
        w1_scr[...] = w1_ref[0].astype(BF16)
        w2_scr[...] = w2_ref[0].astype(BF16)

    @pl.when(valid > 0)
    def _():
        xb = _load_row_tiles(xs_ref, tb).astype(BF16)
        gu = jnp.dot(xb, w1_scr[...], preferred_element_type=F32) + b1_ref[0]
        glu = jnp.minimum(gu[:, :D_FF], SWIGLU_LIMIT)
        lin = jnp.clip(gu[:, D_FF:], -SWIGLU_LIMIT, SWIGLU_LIMIT)
        act = (lin + 1.0) * glu * _sigmoid(SWIGLU_ALPHA * glu)
        y = jnp.dot(act.astype(BF16), w2_scr[...], preferred_element_type=F32) + b2_ref[0]
        _store_row_tiles(ys_ref, y)

    @pl.when(valid <= 0)
    def _():
        ys_ref[...] = jnp.zeros(ys_ref.shape, F32)


def _experts(block_expert, block_valid, xs, w1, b1, w2, b2, tb):
    cap = xs.shape[0] // ROW_SUB
    rows = pl.BlockSpec((tb * ROW_SUB, V7X_LANES), lambda i, be, bv: (i, 0))
    grid_spec = pltpu.PrefetchScalarGridSpec(
        num_scalar_prefetch=2,
        grid=(cap // tb,),
        in_specs=[rows,
                  pl.BlockSpec((1, D_MODEL, 2 * D_FF), lambda i, be, bv: (be[i], 0, 0)),
                  pl.BlockSpec((1, 1, 2 * D_FF), lambda i, be, bv: (be[i], 0, 0)),
                  pl.BlockSpec((1, D_FF, D_MODEL), lambda i, be, bv: (be[i], 0, 0)),
                  pl.BlockSpec((1, 1, D_MODEL), lambda i, be, bv: (be[i], 0, 0))],
        out_specs=rows,
        scratch_shapes=[pltpu.VMEM((D_MODEL, 2 * D_FF), BF16), pltpu.VMEM((D_FF, D_MODEL), BF16)],
    )
    return pl.pallas_call(
        functools.partial(_expert_kernel, tb=tb),
        grid_spec=grid_spec,
        out_shape=jax.ShapeDtypeStruct((cap * ROW_SUB, V7X_LANES), F32),
        compiler_params=_cparams(("arbitrary",)),
        name="moe_experts",
    )(block_expert, block_valid, xs, w1, b1.reshape(N_EXPERTS, 1, -1), w2, b2.reshape(N_EXPERTS, 1, -1))


def _combine_kernel(slot_ref, next_slot_ref, ys_ref, wt_ref, x1_ref, mod_ref, g_ref, b_ref, *rest,
                    tc, tile_ends):
    y_refs = rest[:len(tile_ends)]
    buf, sem = rest[len(tile_ends):]
    i = pl.program_id(0)
    cur = i % 2

    def row_copy(slots, t, kk, half):
        return pltpu.make_async_copy(_row_view(ys_ref, slots[t * TOP_K + kk]),
                                     _row_view(buf.at[half, kk], t), sem.at[half])

    def gather(slots, half):
        def issue(t, carry):
            for kk in range(TOP_K):
                row_copy(slots, t, kk, half).start(priority=kk % 2)
            return carry
        lax.fori_loop(0, tc, issue, 0, unroll=8)

    @pl.when(i == 0)
    def _():
        gather(slot_ref, 0)

    @pl.when(i + 1 < pl.num_programs(0))
    def _():
        gather(next_slot_ref, 1 - cur)

    for kk in range(TOP_K):
        pltpu.make_async_copy(ys_ref.at[pl.ds(0, tc * ROW_SUB), :], buf.at[cur, kk], sem.at[cur]).wait()
    wt = wt_ref[...]
    ff = sum(wt[:, kk:kk + 1] * _load_row_tiles(buf.at[cur, kk], tc) for kk in range(TOP_K))
    g2 = mod_ref[0, 5:6, :]
    y = _layer_norm(DN_ALPHA * x1_ref[...] + (1.0 + g2) * ff) * g_ref[...] + b_ref[...]
    i = pl.program_id(0)
    start = 0
    for y_ref, end in zip(y_refs, tile_ends):
        @pl.when((i >= start) & (i < end))
        def _(y_ref=y_ref):
            y_ref[...] = y
        start = end


def _combine(slots_flat, ys, wts, x1, mod, ln2_g, ln2_b, lay, tc):
    nt = lay.n_tokens

    def seq_map(i):
        return (lay.seq_and_pos(i * tc)[0], 0, 0)

    out_specs, tile_ends = lay.trunk_specs(
        tc, lambda local: pl.BlockSpec((tc, D_MODEL), lambda i: (local(i), 0)))
    n_steps = nt // tc
    return pl.pallas_call(
        functools.partial(_combine_kernel, tc=tc, tile_ends=tile_ends),
        grid=(n_steps,),
        in_specs=[pl.BlockSpec((tc * TOP_K,), lambda i: (i,), memory_space=pltpu.SMEM),
                  pl.BlockSpec((tc * TOP_K,), lambda i: (jnp.minimum(i + 1, n_steps - 1),),
                               memory_space=pltpu.SMEM),
                  pl.BlockSpec(memory_space=pl.ANY),
                  pl.BlockSpec((tc, V7X_LANES), lambda i: (i, 0)),
                  pl.BlockSpec((tc, D_MODEL), lambda i: (i, 0)),
                  pl.BlockSpec((1, N_MOD, D_MODEL), seq_map),
                  _const_spec((1, D_MODEL)), _const_spec((1, D_MODEL))],
        out_specs=out_specs,
        out_shape=[jax.ShapeDtypeStruct((b * s, D_MODEL), F32) for b, s in lay.trunks],
        scratch_shapes=[pltpu.VMEM((2, TOP_K, tc * ROW_SUB, V7X_LANES), F32), pltpu.SemaphoreType.DMA((2,))],
        compiler_params=_cparams(("arbitrary",)),
        name="moe_combine",
    )(slots_flat, slots_flat, ys, wts, x1, mod, ln2_g.reshape(1, -1), ln2_b.reshape(1, -1))


def _encoder_layer(xs, cs, p):
    trunks = [(x.shape[0], x.shape[1]) for x in xs]
    lay = _Layout(trunks)
    nt = lay.n_tokens
    t = _tiles(nt)

    x2d = [xi.reshape(-1, D_MODEL) for xi in xs]
    c = jnp.concatenate(cs, axis=0)
    bp = -(-lay.n_seqs // 8) * 8
    c = jnp.pad(c, ((0, bp - lay.n_seqs), (0, 0)))
    mod = _ada_mod(c, p["w_ada"], p["b_ada"]).reshape(bp, N_MOD, D_MODEL)

    q, k, vt3, xr, gg, gs = _inproj(x2d, mod, p["w_in"].astype(BF16), p["q_gain"], p["k_gain"],
                                    _rope_tables(lay.max_seq), lay, t["tm_in"])

    w_lru = jnp.concatenate([p["lru_wa"], p["lru_wx"]], axis=-1).astype(BF16)
    b_lru = jnp.stack([p["lru_ba"], p["lru_bx"]], axis=1)
    lam = p["lru_lam"].reshape(2, 1, D_MODEL)
    o_atts, hfbs = [], []
    tok_off = 0
    for b, s in trunks:
        o_atts.append(_attention(q, k, vt3, tok_off, b, s, t["tq"], t["tm_in"]))
        hfbs.append(_scan(xr, p["conv_w"], p["conv_b"].reshape(1, -1), w_lru, b_lru, lam,
                          tok_off, b, s, t["tt"]))
        tok_off += b * s

    wr_hi = p["w_router"].astype(BF16)
    wr_lo = (p["w_router"] - wr_hi.astype(F32)).astype(BF16)
    x1, h2r, wfull, msel = _mid(o_atts, hfbs, gg, gs, x2d, mod, p["w_pa"].astype(BF16),
                               p["w_pr"].astype(BF16), p["w_out"].astype(BF16), p["ln1_g"], p["ln1_b"],
                               jnp.concatenate([wr_hi, wr_lo], axis=1), p["b_router"], lay, t["tm_mid"])

    tb = t["tb"]
    rank, cnt = _rank(msel, t["tp"])
    counts = cnt[0].astype(jnp.int32)
    padded = (counts + tb - 1) // tb * tb
    pad_end = jnp.cumsum(padded)
    pad_start = pad_end - padded
    n_blocks = nt * TOP_K // tb + N_EXPERTS
    blk0 = jnp.arange(n_blocks, dtype=jnp.int32) * tb
    block_expert = jnp.minimum(jnp.sum(pad_end[None, :] <= blk0[:, None], axis=1), N_EXPERTS - 1).astype(jnp.int32)
    block_valid = jnp.clip(pad_start[block_expert] + counts[block_expert] - blk0, 0, tb).astype(jnp.int32)
    slots, wts = _slots(rank, msel, wfull, pad_start.astype(F32).reshape(1, N_EXPERTS), t["tp"])
    slots_flat = slots[:, :TOP_K].reshape(-1)

    tail_start = jnp.where(padded > 0, pad_end - tb, pad_end[-1] - tb).astype(jnp.int32)
    xs_rows = _dispatch(slots_flat, tail_start, h2r, n_blocks * tb, t["tg"], tb)
    ys = _experts(block_expert, block_valid, xs_rows, p["w1"], p["b1"], p["w2"], p["b2"], tb)
    ys_out = _combine(slots_flat, ys, wts, x1, mod, p["ln2_g"], p["ln2_b"], lay, t["tc"])
    return [y.reshape(b, s, D_MODEL) for y, (b, s) in zip(ys_out, trunks)]


_PARAM_NAMES = ("w_ada", "b_ada", "w_in", "q_gain", "k_gain", "conv_w", "conv_b", "lru_wa", "lru_ba",
                "lru_wx", "lru_bx", "lru_lam", "w_pa", "w_pr", "w_out", "ln1_g", "ln1_b", "w_router",
                "b_router", "w1", "b1", "w2", "b2", "ln2_g", "ln2_b")


def kernel(x_prompt, x_sample, c_prompt, c_sample, w_ada, b_ada, w_in, q_gain, k_gain, conv_w, conv_b, lru_wa, lru_ba, lru_wx, lru_bx, lru_lam, w_pa, w_pr, w_out, ln1_g, ln1_b, w_router, b_router, w1, b1, w2, b2, ln2_g, ln2_b):
    stacked = (w_ada, b_ada, w_in, q_gain, k_gain, conv_w, conv_b, lru_wa, lru_ba, lru_wx, lru_bx,
               lru_lam, w_pa, w_pr, w_out, ln1_g, ln1_b, w_router, b_router, w1, b1, w2, b2, ln2_g, ln2_b)
    xs, cs = [x_prompt, x_sample], [c_prompt, c_sample]
    for layer in range(DEPTH):
        p = {name: arr[layer] for name, arr in zip(_PARAM_NAMES, stacked)}
        xs = _encoder_layer(xs, cs, p)
    return (xs[0], xs[1])
```

```python
import functools
import math

import jax
import jax.numpy as jnp
from jax import lax
from jax.experimental import pallas as pl
from jax.experimental.pallas import tpu as pltpu

F32 = jnp.float32
BF16 = jnp.bfloat16

D_MODEL = 1024
GRID_W = 64
N_HEADS = 8
N_KV_HEADS = 2
HEAD_DIM = 128
GROUPS = N_HEADS // N_KV_HEADS
KV_WIDTH = N_KV_HEADS * HEAD_DIM
ROPE_THETA = 10000.0
RNN_BLOCKS = 8
RNN_BLOCK_W = D_MODEL // RNN_BLOCKS
CONV_W = 4
LRU_C = 8.0
N_EXPERTS = 32
TOP_K = 4
D_FF = D_MODEL
SWIGLU_LIMIT = 7.0
SWIGLU_ALPHA = 1.702
DEPTH = 1
DN_ALPHA = (2 * DEPTH) ** 0.25
LN_EPS = 1e-5
RMS_EPS = 1e-6
N_MOD = 6
_Q0 = 0
_K0 = _Q0 + D_MODEL
_V0 = _K0 + KV_WIDTH
_XR0 = _V0 + KV_WIDTH
_GR0 = _XR0 + D_MODEL
_GL0 = _GR0 + D_MODEL
IN_WIDTH = _GL0 + 2 * D_MODEL

V7X_LANES = 128
V7X_SUBLANES = 8
V7X_BF16_SUBLANES = 16
V7X_VMEM_LIMIT_BYTES = 56 * 1024 * 1024
HALO = V7X_BF16_SUBLANES
VT_ROWS = HEAD_DIM + V7X_BF16_SUBLANES


def _tiles(n_tokens):
    big = n_tokens >= 4096
    return dict(
        tm_in=512 if big else 128,
        tq=256 if big else 128,
        tt=1024 if big else 128,
        tm_mid=512 if big else 128,
        tp=512 if big else 128,
        tg=1024 if big else 256,
        tb=512 if big else 128,
        tc=512 if big else 256,
    )


def _cparams(sem):
    return pltpu.CompilerParams(dimension_semantics=sem, vmem_limit_bytes=V7X_VMEM_LIMIT_BYTES)


def _const_spec(shape):
    nd = len(shape)
    return pl.BlockSpec(shape, lambda *_: (0,) * nd, pipeline_mode=pl.Buffered(1))


def _layer_norm(x):
    mu = jnp.mean(x, axis=-1, keepdims=True)
    xc = x - mu
    var = jnp.mean(xc * xc, axis=-1, keepdims=True)
    return xc * lax.rsqrt(var + LN_EPS)


def _sigmoid(x):
    return 1.0 / (1.0 + jnp.exp(-x))


ROW_SUB = D_MODEL // V7X_LANES
assert ROW_SUB == V7X_SUBLANES, "one token row must fill exactly one f32 tile"


def _store_row_tiles(ref, val):
    n = val.shape[0]
    for j in range(ROW_SUB):
        ref[pl.ds(j, n, stride=ROW_SUB), :] = val[:, j * V7X_LANES:(j + 1) * V7X_LANES]


def _load_row_tiles(ref, n):
    return jnp.concatenate([ref[pl.ds(j, n, stride=ROW_SUB), :] for j in range(ROW_SUB)], axis=1)


def _ada_kernel(c_ref, w_ref, b_ref, o_ref):
    c = c_ref[...]
    s = c * _sigmoid(c)
    o_ref[...] = jnp.dot(s, w_ref[...], preferred_element_type=F32,
                         precision=lax.Precision.HIGHEST) + b_ref[...]


def _ada_mod(c_all, w_ada, b_ada):
    bp = c_all.shape[0]
    ncol = w_ada.shape[1]
    return pl.pallas_call(
        _ada_kernel,
        grid=(ncol // D_MODEL,),
        in_specs=[pl.BlockSpec((bp, D_MODEL), lambda j: (0, 0)),
                  pl.BlockSpec((D_MODEL, D_MODEL), lambda j: (0, j)),
                  pl.BlockSpec((1, D_MODEL), lambda j: (0, j))],
        out_specs=pl.BlockSpec((bp, D_MODEL), lambda j: (0, j)),
        out_shape=jax.ShapeDtypeStruct((bp, ncol), F32),
        compiler_params=_cparams(("arbitrary",)),
        name="ada_mod",
    )(c_all, w_ada, b_ada.reshape(1, ncol))


class _Layout:
    def __init__(self, trunks):
        self.trunks = tuple(trunks)
        self.n_tokens = sum(b * s for b, s in trunks)
        self.n_seqs = sum(b for b, _ in trunks)
        self.max_seq = max(s for _, s in trunks)

    def seq_and_pos(self, t0):
        seq = jnp.int32(0)
        pos = jnp.int32(0)
        tok_off, seq_off = 0, 0
        for b, s in self.trunks:
            inside = (t0 >= tok_off) & (t0 < tok_off + b * s)
            rel = jnp.maximum(t0 - tok_off, 0)
            seq = jnp.where(inside, seq_off + rel // s, seq)
            pos = jnp.where(inside, rel % s, pos)
            tok_off += b * s
            seq_off += b
        return seq, pos

    def trunk_specs(self, tm, make_spec):
        specs, ends, t0 = [], [], 0
        for b, s in self.trunks:
            n_t = b * s // tm
            specs.append(make_spec(functools.partial(_clamped_local, t0=t0, n_t=n_t)))
            t0 += n_t
            ends.append(t0)
        return specs, tuple(ends)


def _clamped_local(i, *, t0, n_t):
    return jnp.clip(i - t0, 0, n_t - 1)


def _owner_value(loads, i, tile_ends):
    val = loads[-1]()
    for j in range(len(loads) - 2, -1, -1):
        val = jnp.where(i < tile_ends[j], loads[j](), val)
    return val


def _rope_tables(max_seq):
    n_rows = max_seq // GRID_W
    axis_dim = HEAD_DIM // 2
    inv = ROPE_THETA ** (-jnp.arange(0, axis_dim, 2, dtype=F32) / axis_dim)
    ar = jnp.arange(n_rows, dtype=F32)[:, None] * inv
    ac = jnp.arange(GRID_W, dtype=F32)[:, None] * inv
    cr, sr = (jnp.repeat(f(ar), GRID_W, axis=0) for f in (jnp.cos, jnp.sin))
    cc, sc = (jnp.tile(f(ac), (n_rows, 1)) for f in (jnp.cos, jnp.sin))
    z = jnp.zeros_like(sr)
    cos_t = jnp.concatenate([cr, cr, cc, cc], axis=-1)
    up_t = jnp.concatenate([-sr, z, -sc, z], axis=-1)
    dn_t = jnp.concatenate([z, sr, z, sc], axis=-1)
    return cos_t, up_t, dn_t


def _inproj_kernel(*refs, tile_ends):
    n_tr = len(tile_ends)
    x_refs = refs[:n_tr]
    (mod_ref, w_ref, qg_ref, kg_ref, cos_ref, up_ref, dn_ref,
     q_ref, k_ref, vt_ref, xr_ref, gg_ref, gs_ref) = refs[n_tr:]
    x = _owner_value([lambda r=r: r[...] for r in x_refs], pl.program_id(0), tile_ends)
    sh1 = mod_ref[0, 0:1, :]
    sc1 = mod_ref[0, 1:2, :]
    h = (_layer_norm(x) * (1.0 + sc1) + sh1).astype(BF16)
    cos_t, up_t, dn_t = cos_ref[...], up_ref[...], dn_ref[...]

    def proj(c0, width):
        return jnp.dot(h, w_ref[:, c0:c0 + width], preferred_element_type=F32)

    def norm_rope(z, gain):
        ms = jnp.mean(z * z, axis=-1, keepdims=True)
        y = z * lax.rsqrt(ms + RMS_EPS) * gain
        return (y * cos_t + pltpu.roll(y, HEAD_DIM - 32, 1) * up_t
                + pltpu.roll(y, 32, 1) * dn_t)

    xr_ref[...] = proj(_XR0, D_MODEL).astype(BF16)
    zq = proj(_Q0, D_MODEL)
    qg = qg_ref[...] * (HEAD_DIM ** -0.5 * math.log2(math.e))
    for hd in range(N_HEADS):
        sl = slice(hd * HEAD_DIM, (hd + 1) * HEAD_DIM)
        q_ref[:, sl] = norm_rope(zq[:, sl], qg).astype(BF16)
    zk = proj(_K0, KV_WIDTH)
    kg = kg_ref[...]
    for hd in range(N_KV_HEADS):
        sl = slice(hd * HEAD_DIM, (hd + 1) * HEAD_DIM)
        k_ref[:, sl] = norm_rope(zk[:, sl], kg).astype(BF16)
    zvt = proj(_V0, KV_WIDTH).T.astype(BF16)
    pad_row = lax.broadcasted_iota(jnp.int32, (VT_ROWS - HEAD_DIM, zvt.shape[1]), 0)
    ones_pad = jnp.where(pad_row == 0, 1.0, 0.0).astype(BF16)
    for hd in range(N_KV_HEADS):
        vt_ref[0, hd * VT_ROWS:hd * VT_ROWS + HEAD_DIM, :] = zvt[hd * HEAD_DIM:(hd + 1) * HEAD_DIM]
        vt_ref[0, hd * VT_ROWS + HEAD_DIM:(hd + 1) * VT_ROWS, :] = ones_pad
    gg_ref[...] = jax.nn.gelu(proj(_GR0, D_MODEL), approximate=True).astype(BF16)
    gs_ref[...] = _sigmoid(proj(_GL0, 2 * D_MODEL)).astype(BF16)


def _inproj(xs, mod, w_in, q_gain, k_gain, tables, lay, tm):
    nt = lay.n_tokens
    cos_t, up_t, dn_t = tables

    def seq_map(i):
        return (lay.seq_and_pos(i * tm)[0], 0, 0)

    def pos_map(i):
        return (lay.seq_and_pos(i * tm)[1] // tm, 0)

    tok = lambda w: pl.BlockSpec((tm, w), lambda i: (i, 0))
    rope = pl.BlockSpec((tm, HEAD_DIM), pos_map)
    x_specs, tile_ends = lay.trunk_specs(
        tm, lambda local: pl.BlockSpec((tm, D_MODEL), lambda i: (local(i), 0)))
    return pl.pallas_call(
        functools.partial(_inproj_kernel, tile_ends=tile_ends),
        grid=(nt // tm,),
        in_specs=x_specs + [
                  pl.BlockSpec((1, N_MOD, D_MODEL), seq_map),
                  _const_spec((D_MODEL, IN_WIDTH)),
                  _const_spec((1, HEAD_DIM)), _const_spec((1, HEAD_DIM)),
                  rope, rope, rope],
        out_specs=[tok(D_MODEL), tok(KV_WIDTH),
                   pl.BlockSpec((1, N_KV_HEADS * VT_ROWS, tm), lambda i: (i, 0, 0)),
                   tok(D_MODEL), tok(D_MODEL), tok(2 * D_MODEL)],
        out_shape=[jax.ShapeDtypeStruct((nt, D_MODEL), BF16),
                   jax.ShapeDtypeStruct((nt, KV_WIDTH), BF16),
                   jax.ShapeDtypeStruct((nt // tm, N_KV_HEADS * VT_ROWS, tm), BF16),
                   jax.ShapeDtypeStruct((nt, D_MODEL), BF16),
                   jax.ShapeDtypeStruct((nt, D_MODEL), BF16),
                   jax.ShapeDtypeStruct((nt, 2 * D_MODEL), BF16)],
        compiler_params=_cparams(("arbitrary",)),
        name="in_proj",
    )(*xs, mod, w_in, q_gain.reshape(1, HEAD_DIM), k_gain.reshape(1, HEAD_DIM), cos_t, up_t, dn_t)


def _attn_kernel(q_ref, k_ref, vt_ref, o_ref, qt_all, s_all, m_all, acc_all, *,
                 tq, tk, n_kv, group, n_sub):
    for sub in range(n_sub):
        rows = slice(sub * tq, (sub + 1) * tq)
        _attn_tile(q_ref.at[rows, :], k_ref, vt_ref, o_ref.at[rows, :], qt_all.at[sub], s_all.at[sub],
                   m_all.at[sub], acc_all.at[sub], tq=tq, tk=tk, n_kv=n_kv, group=group)


def _attn_tile(q_ref, k_ref, vt_ref, o_ref, qt_scr, s_scr, m_scr, acc_scr, *, tq, tk, n_kv, group):
    for g in range(GROUPS):
        qg = q_ref[:, g * HEAD_DIM:(g + 1) * HEAD_DIM].astype(F32)
        qt_scr[:, g * tq:(g + 1) * tq] = qg.T.astype(BF16)
    m_scr[...] = jnp.full(m_scr.shape, -jnp.inf, F32)
    acc_scr[...] = jnp.zeros(acc_scr.shape, F32)

    def scores(j, slot):
        kt = k_ref[pl.ds(pl.multiple_of(j * tk, tk), tk), :]
        s_scr[slot] = jnp.dot(kt, qt_scr[...], preferred_element_type=F32)

    def accumulate(j, slot):
        s = s_scr[slot]
        m_old = m_scr[...]
        m_new = jnp.maximum(m_old, jnp.max(s, axis=0, keepdims=True))
        alpha = jnp.exp2(m_old - m_new)
        p = jnp.exp2(s - m_new).astype(BF16)
        pv = jnp.dot(vt_ref[j], p, preferred_element_type=F32)
        acc_scr[...] = alpha * acc_scr[...] + pv
        m_scr[...] = m_new

    scores(0, 0)

    def body(i, carry):
        j = group * i
        for u in range(group):
            scores(jnp.minimum(j + u + 1, n_kv - 1), (u + 1) % 2)
            accumulate(j + u, u % 2)
        return carry

    lax.fori_loop(0, n_kv // group, body, 0)
    out = acc_scr[0:HEAD_DIM, :] / acc_scr[HEAD_DIM:HEAD_DIM + 1, :]
    for g in range(GROUPS):
        o_ref[:, g * HEAD_DIM:(g + 1) * HEAD_DIM] = out[:, g * tq:(g + 1) * tq].T.astype(BF16)


def _attention(q, k, vt3, tok_off, batch, seq, tq, tk):
    n_kv = seq // tk
    group = next((g for g in (8, 4) if n_kv % g == 0 and n_kv // g >= 2), 2)
    assert n_kv % group == 0
    n_sub = next(n for n in (4, 2, 1) if (seq // tq) % n == 0)
    tqs = n_sub * tq
    qrow0 = tok_off // tqs
    srow0 = tok_off // seq
    gw = GROUPS * HEAD_DIM
    n_q = seq // tqs
    return pl.pallas_call(
        functools.partial(_attn_kernel, tq=tq, tk=tk, n_kv=n_kv, group=group, n_sub=n_sub),
        grid=(batch, N_KV_HEADS, n_q),
        in_specs=[pl.BlockSpec((tqs, gw), lambda b, h, i: (qrow0 + b * n_q + i, h)),
                  pl.BlockSpec((seq, HEAD_DIM), lambda b, h, i: (srow0 + b, h)),
                  pl.BlockSpec((n_kv, VT_ROWS, tk), lambda b, h, i: (srow0 + b, h, 0))],
        out_specs=pl.BlockSpec((tqs, gw), lambda b, h, i: (b * n_q + i, h)),
        out_shape=jax.ShapeDtypeStruct((batch * seq, D_MODEL), BF16),
        scratch_shapes=[pltpu.VMEM((n_sub, HEAD_DIM, GROUPS * tq), BF16),
                        pltpu.VMEM((n_sub, 2, tk, GROUPS * tq), F32),
                        pltpu.VMEM((n_sub, 1, GROUPS * tq), F32),
                        pltpu.VMEM((n_sub, VT_ROWS, GROUPS * tq), F32)],
        compiler_params=_cparams(("arbitrary", "arbitrary", "arbitrary")),
        name="attention",
    )(q, k, vt3)


def _scan_kernel(cur_ref, prev_ref, next_ref, cw_ref, cb_ref, w_ref, b_ref, lam_ref, o_ref,
                 xc_scr, a_scr, u_scr, h_scr, carry_scr, *, tt, n_chunks):
    d = pl.program_id(1)
    c = pl.program_id(2)
    chunk = jnp.where(d == 0, c, n_chunks - 1 - c)

    @pl.when(c == 0)
    def _():
        carry_scr[...] = jnp.zeros(carry_scr.shape, F32)

    keep_prev = jnp.where(chunk == 0, 0.0, 1.0)
    keep_next = jnp.where(chunk == n_chunks - 1, 0.0, 1.0)
    cur = cur_ref[...].astype(F32)
    taps = [cw_ref[j:j + 1, :] for j in range(CONV_W)]
    xc_scr[...] = (cb_ref[...] + taps[2] * cur + taps[1] * pltpu.roll(cur, 1, 0)
                   + taps[0] * pltpu.roll(cur, 2, 0) + taps[3] * pltpu.roll(cur, tt - 1, 0))
    sub = V7X_SUBLANES
    head = jnp.concatenate([prev_ref[...].astype(F32)[HALO - sub:HALO] * keep_prev, cur[0:2 * sub]], axis=0)
    tail = jnp.concatenate([cur[tt - 2 * sub:tt], next_ref[...].astype(F32)[0:sub] * keep_next], axis=0)
    first = sum(taps[j] * head[sub - 2 + j:2 * sub - 2 + j] for j in range(CONV_W))
    last = sum(taps[j] * tail[sub - 2 + j:2 * sub - 2 + j] for j in range(CONV_W))
    xc_scr[0:sub, :] = cb_ref[...] + first
    xc_scr[tt - sub:tt, :] = cb_ref[...] + last
    xc = xc_scr[...]
    xcb = xc.astype(BF16)

    lam = lam_ref[0]
    y = jnp.exp(-jnp.abs(lam))
    w1p = 1.0 + y
    log1p_y = jnp.where(w1p == 1.0, y, jnp.log(w1p) * y / jnp.where(w1p == 1.0, 1.0, w1p - 1.0))
    neg_c_sp = (-LRU_C * math.log2(math.e)) * (jnp.maximum(-lam, 0.0) + log1p_y)

    for n in range(RNN_BLOCKS):
        sl = slice(n * RNN_BLOCK_W, (n + 1) * RNN_BLOCK_W)
        pre = jnp.dot(xcb[:, sl], w_ref[0, n], preferred_element_type=F32)
        r = _sigmoid(pre[:, :RNN_BLOCK_W] + b_ref[0, 0:1, sl])
        i = _sigmoid(pre[:, RNN_BLOCK_W:] + b_ref[0, 1:2, sl])
        a = jnp.exp2(r * neg_c_sp[:, sl])
        a_scr[:, sl] = a
        u_scr[:, sl] = jnp.sqrt(1.0 - a * a) * (i * xc[:, sl])

    def step(t, h):
        row = jnp.where(d == 0, t, tt - 1 - t)
        h = a_scr[pl.ds(row, 1), :] * h + u_scr[pl.ds(row, 1), :]
        h_scr[pl.ds(row, 1), :] = h
        return h

    carry_scr[...] = lax.fori_loop(0, tt, step, carry_scr[...], unroll=8)
    o_ref[0] = h_scr[...].astype(BF16)


def _scan(xr, conv_w, conv_b, w_lru, b_lru, lam, tok_off, batch, seq, tt):
    nt = xr.shape[0]
    n_chunks = seq // tt
    row0 = tok_off // tt
    hrow0 = tok_off // HALO
    hpc = tt // HALO
    n_halo = nt // HALO

    def chunk_of(d, c):
        return jnp.where(d == 0, c, n_chunks - 1 - c)

    def prev_map(b, d, c):
        return (jnp.maximum(hrow0 + (b * n_chunks + chunk_of(d, c)) * hpc - 1, 0), 0)

    def next_map(b, d, c):
        return (jnp.minimum(hrow0 + (b * n_chunks + chunk_of(d, c) + 1) * hpc, n_halo - 1), 0)

    return pl.pallas_call(
        functools.partial(_scan_kernel, tt=tt, n_chunks=n_chunks),
        grid=(batch, 2, n_chunks),
        in_specs=[pl.BlockSpec((tt, D_MODEL), lambda b, d, c: (row0 + b * n_chunks + chunk_of(d, c), 0)),
                  pl.BlockSpec((HALO, D_MODEL), prev_map),
                  pl.BlockSpec((HALO, D_MODEL), next_map),
                  pl.BlockSpec((CONV_W, D_MODEL), lambda b, d, c: (0, 0)),
                  pl.BlockSpec((1, D_MODEL), lambda b, d, c: (0, 0)),
                  pl.BlockSpec((1, RNN_BLOCKS, RNN_BLOCK_W, 2 * RNN_BLOCK_W), lambda b, d, c: (d, 0, 0, 0)),
                  pl.BlockSpec((1, 2, D_MODEL), lambda b, d, c: (d, 0, 0)),
                  pl.BlockSpec((1, 1, D_MODEL), lambda b, d, c: (d, 0, 0))],
        out_specs=pl.BlockSpec((1, tt, D_MODEL), lambda b, d, c: (d, b * n_chunks + chunk_of(d, c), 0)),
        out_shape=jax.ShapeDtypeStruct((2, batch * seq, D_MODEL), BF16),
        scratch_shapes=[pltpu.VMEM((tt, D_MODEL), F32),
                        pltpu.VMEM((tt, D_MODEL), F32),
                        pltpu.VMEM((tt, D_MODEL), F32),
                        pltpu.VMEM((tt, D_MODEL), F32),
                        pltpu.VMEM((1, D_MODEL), F32)],
        compiler_params=_cparams(("arbitrary", "arbitrary", "arbitrary")),
        name="lru_scan",
    )(xr, xr, xr, conv_w, conv_b, w_lru, b_lru, lam)


def _mid_kernel(*refs, tile_ends):
    n_tr = len(tile_ends)
    oa_refs = refs[:n_tr]
    h_refs = refs[n_tr:3 * n_tr]
    x_refs = refs[3 * n_tr:4 * n_tr]
    (gg_ref, gs_ref, mod_ref, wpa_ref, wpr_ref, wo_ref, g1_ref, b1_ref, wr_ref, br_ref,
     x1_ref, h2_ref, wf_ref, ms_ref) = refs[4 * n_tr:]
    i = pl.program_id(0)
    oa = _owner_value([lambda r=r: r[...] for r in oa_refs], i, tile_ends)
    h_fwd = _owner_value([lambda r=h_refs[2 * j]: r[0] for j in range(n_tr)], i, tile_ends)
    h_bwd = _owner_value([lambda r=h_refs[2 * j + 1]: r[0] for j in range(n_tr)], i, tile_ends)
    hsum = h_fwd.astype(F32) + h_bwd.astype(F32)
    x_in = _owner_value([lambda r=r: r[...] for r in x_refs], i, tile_ends)
    o_att = jnp.dot(oa, wpa_ref[...], preferred_element_type=F32)
    rec = hsum.astype(BF16) * gg_ref[...]
    o_rec = jnp.dot(rec, wpr_ref[...], preferred_element_type=F32)
    gs = gs_ref[...].astype(F32)
    merged = gs[:, :D_MODEL] * o_att + gs[:, D_MODEL:] * o_rec
    mix = jnp.dot(merged.astype(BF16), wo_ref[...], preferred_element_type=F32)
    g1 = mod_ref[0, 2:3, :]
    sh2 = mod_ref[0, 3:4, :]
    sc2 = mod_ref[0, 4:5, :]
    x1 = _layer_norm(DN_ALPHA * x_in + (1.0 + g1) * mix) * g1_ref[...] + b1_ref[...]
    x1_ref[...] = x1
    h2 = _layer_norm(x1) * (1.0 + sc2) + sh2
    _store_row_tiles(h2_ref, h2)
    tm = h2.shape[0]
    h2_hi = h2.astype(BF16)
    h2_lo = (h2 - h2_hi.astype(F32)).astype(BF16)
    cross = jnp.dot(jnp.concatenate([h2_hi, h2_lo], axis=0), wr_ref[...], preferred_element_type=F32)
    logits = ((cross[:tm, :N_EXPERTS] + cross[:tm, N_EXPERTS:])
              + (cross[tm:, :N_EXPERTS] + cross[tm:, N_EXPERTS:])) + br_ref[...]
    lane = lax.broadcasted_iota(jnp.int32, logits.shape, 1).astype(F32)
    rem = logits
    sel = jnp.zeros(logits.shape, F32)
    top = None
    denom = None
    for kk in range(TOP_K):
        m = jnp.max(rem, axis=-1, keepdims=True)
        idx = jnp.min(jnp.where(rem == m, lane, float(N_EXPERTS)), axis=-1, keepdims=True)
        pick = lane == idx
        sel = jnp.where(pick, 1.0, sel)
        rem = jnp.where(pick, -jnp.inf, rem)
        if kk == 0:
            top = m
            denom = jnp.ones_like(m)
        else:
            denom = denom + jnp.exp(m - top)
    wf_ref[...] = jnp.where(sel > 0.0, jnp.exp(logits - top) / denom, 0.0)
    ms_ref[...] = sel


def _mid(o_atts, hfbs, gg, gs, xs, mod, w_pa, w_pr, w_out, ln1_g, ln1_b, w_router, b_router, lay, tm):
    nt = lay.n_tokens

    def seq_map(i):
        return (lay.seq_and_pos(i * tm)[0], 0, 0)

    tok = lambda w: pl.BlockSpec((tm, w), lambda i: (i, 0))
    vec = lambda w: _const_spec((1, w))
    row_specs, tile_ends = lay.trunk_specs(
        tm, lambda local: pl.BlockSpec((tm, D_MODEL), lambda i: (local(i), 0)))
    h_pairs, _ = lay.trunk_specs(
        tm, lambda local: [pl.BlockSpec((1, tm, D_MODEL), lambda i, d=d: (d, local(i), 0)) for d in range(2)])
    h_specs = [spec for pair in h_pairs for spec in pair]
    h_args = [hfb for hfb in hfbs for _ in range(2)]
    return pl.pallas_call(
        functools.partial(_mid_kernel, tile_ends=tile_ends),
        grid=(nt // tm,),
        in_specs=row_specs + h_specs + row_specs + [
                  tok(D_MODEL), tok(2 * D_MODEL),
                  pl.BlockSpec((1, N_MOD, D_MODEL), seq_map),
                  _const_spec((D_MODEL, D_MODEL)), _const_spec((D_MODEL, D_MODEL)),
                  _const_spec((D_MODEL, D_MODEL)),
                  vec(D_MODEL), vec(D_MODEL),
                  _const_spec((D_MODEL, 2 * N_EXPERTS)), vec(N_EXPERTS)],
        out_specs=[tok(D_MODEL), pl.BlockSpec((ROW_SUB * tm, V7X_LANES), lambda i: (i, 0)),
                   tok(N_EXPERTS), tok(N_EXPERTS)],
        out_shape=[jax.ShapeDtypeStruct((nt, D_MODEL), F32),
                   jax.ShapeDtypeStruct((ROW_SUB * nt, V7X_LANES), F32),
                   jax.ShapeDtypeStruct((nt, N_EXPERTS), F32),
                   jax.ShapeDtypeStruct((nt, N_EXPERTS), F32)],
        compiler_params=_cparams(("arbitrary",)),
        name="merge_router",
    )(*o_atts, *h_args, *xs, gg, gs, mod, w_pa, w_pr, w_out, ln1_g.reshape(1, -1), ln1_b.reshape(1, -1),
      w_router, b_router.reshape(1, -1))


def _rank_kernel(ms_ref, rank_ref, cnt_ref, tri_scr, run_scr, *, tp):
    @pl.when(pl.program_id(0) == 0)
    def _():
        r = lax.broadcasted_iota(jnp.int32, (tp, tp), 0)
        c = lax.broadcasted_iota(jnp.int32, (tp, tp), 1)
        tri_scr[...] = jnp.where(c < r, 1.0, 0.0).astype(BF16)
        run_scr[...] = jnp.zeros(run_scr.shape, F32)

    ms = ms_ref[...]
    before = jnp.dot(tri_scr[...], ms.astype(BF16), preferred_element_type=F32)
    rank_ref[...] = before + run_scr[...]
    run_scr[...] = run_scr[...] + jnp.sum(ms, axis=0, keepdims=True)
    cnt_ref[...] = jnp.broadcast_to(run_scr[...], cnt_ref.shape)


def _rank(msel, tp):
    nt = msel.shape[0]
    return pl.pallas_call(
        functools.partial(_rank_kernel, tp=tp),
        grid=(nt // tp,),
        in_specs=[pl.BlockSpec((tp, N_EXPERTS), lambda i: (i, 0))],
        out_specs=[pl.BlockSpec((tp, N_EXPERTS), lambda i: (i, 0)),
                   pl.BlockSpec((V7X_SUBLANES, N_EXPERTS), lambda i: (0, 0))],
        out_shape=[jax.ShapeDtypeStruct((nt, N_EXPERTS), F32),
                   jax.ShapeDtypeStruct((V7X_SUBLANES, N_EXPERTS), F32)],
        scratch_shapes=[pltpu.VMEM((tp, tp), BF16), pltpu.VMEM((1, N_EXPERTS), F32)],
        compiler_params=_cparams(("arbitrary",)),
        name="route_rank",
    )(msel)


def _slots_kernel(rank_ref, ms_ref, wf_ref, start_ref, slot_ref, wt_ref):
    ms = ms_ref[...]
    wf = wf_ref[...]
    slot_full = rank_ref[...] + start_ref[...]
    lane = lax.broadcasted_iota(jnp.int32, ms.shape, 1).astype(F32)
    out_lane = lax.broadcasted_iota(jnp.int32, slot_ref.shape, 1)
    slots = jnp.zeros(slot_ref.shape, F32)
    wts = jnp.zeros(wt_ref.shape, F32)
    rem = ms
    for kk in range(TOP_K):
        idx = jnp.min(jnp.where(rem > 0.0, lane, float(2 * N_EXPERTS)), axis=-1, keepdims=True)
        pick = lane == idx
        s_k = jnp.sum(jnp.where(pick, slot_full, 0.0), axis=-1, keepdims=True)
        w_k = jnp.sum(jnp.where(pick, wf, 0.0), axis=-1, keepdims=True)
        rem = jnp.where(pick, 0.0, rem)
        slots = jnp.where(out_lane == kk, s_k, slots)
        wts = jnp.where(out_lane == kk, w_k, wts)
    slot_ref[...] = slots.astype(jnp.int32)
    wt_ref[...] = wts


def _slots(rank, msel, wfull, pad_start, tp):
    nt = msel.shape[0]
    tok = pl.BlockSpec((tp, N_EXPERTS), lambda i: (i, 0))
    out = pl.BlockSpec((tp, V7X_LANES), lambda i: (i, 0))
    return pl.pallas_call(
        _slots_kernel,
        grid=(nt // tp,),
        in_specs=[tok, tok, tok, pl.BlockSpec((1, N_EXPERTS), lambda i: (0, 0))],
        out_specs=[out, out],
        out_shape=[jax.ShapeDtypeStruct((nt, V7X_LANES), jnp.int32),
                   jax.ShapeDtypeStruct((nt, V7X_LANES), F32)],
        compiler_params=_cparams(("arbitrary",)),
        name="route_slots",
    )(rank, msel, wfull, pad_start)


def _row_view(ref, row):
    return ref.at[pl.ds(pl.multiple_of(row * ROW_SUB, ROW_SUB), ROW_SUB), :]


def _dispatch_kernel(slot_ref, tail_ref, h_ref, xs_ref, zero_scr, sem, zsem, *, tg, tb):
    @pl.when(pl.program_id(0) == 0)
    def _():
        zero_scr[...] = jnp.zeros(zero_scr.shape, F32)
        for e in range(N_EXPERTS):
            tail = pl.multiple_of(tail_ref[e] * ROW_SUB, tb * ROW_SUB)
            pltpu.make_async_copy(zero_scr, xs_ref.at[pl.ds(tail, tb * ROW_SUB), :], zsem).start()
        for e in range(N_EXPERTS):
            pltpu.make_async_copy(zero_scr, xs_ref.at[pl.ds(0, tb * ROW_SUB), :], zsem).wait()

    def issue(t, carry):
        for kk in range(TOP_K):
            pltpu.make_async_copy(_row_view(h_ref, t), _row_view(xs_ref, slot_ref[t * TOP_K + kk]),
                                  sem).start(priority=kk % 2)
        return carry

    lax.fori_loop(0, tg, issue, 0, unroll=2)

    for kk in range(TOP_K):
        pltpu.make_async_copy(h_ref, xs_ref.at[pl.ds(0, tg * ROW_SUB), :], sem).wait()


def _dispatch(slots_flat, tail_start, h2r, cap, tg, tb):
    nt = h2r.shape[0] // ROW_SUB
    return pl.pallas_call(
        functools.partial(_dispatch_kernel, tg=tg, tb=tb),
        grid=(nt // tg,),
        in_specs=[pl.BlockSpec((tg * TOP_K,), lambda i: (i,), memory_space=pltpu.SMEM),
                  pl.BlockSpec(memory_space=pltpu.SMEM),
                  pl.BlockSpec((tg * ROW_SUB, V7X_LANES), lambda i: (i, 0))],
        out_specs=pl.BlockSpec(memory_space=pl.ANY),
        out_shape=jax.ShapeDtypeStruct((cap * ROW_SUB, V7X_LANES), F32),
        scratch_shapes=[pltpu.VMEM((tb * ROW_SUB, V7X_LANES), F32), pltpu.SemaphoreType.DMA(()),
                        pltpu.SemaphoreType.DMA(())],
        compiler_params=_cparams(("arbitrary",)),
        name="moe_dispatch",
    )(slots_flat, tail_start, h2r)


def _expert_kernel(be_ref, bv_ref, xs_ref, w1_ref, b1_ref, w2_ref, b2_ref, ys_ref, w1_scr, w2_scr, *, tb):
    i = pl.program_id(0)
    valid = bv_ref[i]

    @pl.when((i == 0) | (be_ref[i] != be_ref[jnp.maximum(i - 1, 0)]))
    def _():
        w1_scr[...] = w1_ref[0].astype(BF16)
        w2_scr[...] = w2_ref[0].astype(BF16)

    @pl.when(valid > 0)
    def _():
        xb = _load_row_tiles(xs_ref, tb).astype(BF16)
        gu = jnp.dot(xb, w1_scr[...], preferred_element_type=F32) + b1_ref[0]
        glu = jnp.minimum(gu[:, :D_FF], SWIGLU_LIMIT)
        lin = jnp.clip(gu[:, D_FF:], -SWIGLU_LIMIT, SWIGLU_LIMIT)
        act = (lin + 1.0) * glu * _sigmoid(SWIGLU_ALPHA * glu)
        y = jnp.dot(act.astype(BF16), w2_scr[...], preferred_element_type=F32) + b2_ref[0]
        _store_row_tiles(ys_ref, y)

    @pl.when(valid <= 0)
    def _():
        ys_ref[...] = jnp.zeros(ys_ref.shape, F32)


def _experts(block_expert, block_valid, xs, w1, b1, w2, b2, tb):
    cap = xs.shape[0] // ROW_SUB
    rows = pl.BlockSpec((tb * ROW_SUB, V7X_LANES), lambda i, be, bv: (i, 0))
    grid_spec = pltpu.PrefetchScalarGridSpec(
        num_scalar_prefetch=2,
        grid=(cap // tb,),
        in_specs=[rows,
                  pl.BlockSpec((1, D_MODEL, 2 * D_FF), lambda i, be, bv: (be[i], 0, 0)),
                  pl.BlockSpec((1, 1, 2 * D_FF), lambda i, be, bv: (be[i], 0, 0)),
                  pl.BlockSpec((1, D_FF, D_MODEL), lambda i, be, bv: (be[i], 0, 0)),
                  pl.BlockSpec((1, 1, D_MODEL), lambda i, be, bv: (be[i], 0, 0))],
        out_specs=rows,
        scratch_shapes=[pltpu.VMEM((D_MODEL, 2 * D_FF), BF16), pltpu.VMEM((D_FF, D_MODEL), BF16)],
    )
    return pl.pallas_call(
        functools.partial(_expert_kernel, tb=tb),
        grid_spec=grid_spec,
        out_shape=jax.ShapeDtypeStruct((cap * ROW_SUB, V7X_LANES), F32),
        compiler_params=_cparams(("arbitrary",)),
        name="moe_experts",
    )(block_expert, block_valid, xs, w1, b1.reshape(N_EXPERTS, 1, -1), w2, b2.reshape(N_EXPERTS, 1, -1))


def _combine_kernel(slot_ref, next_slot_ref, ys_ref, wt_ref, x1_ref, mod_ref, g_ref, b_ref, *rest,
                    tc, tile_ends):
    y_refs = rest[:len(tile_ends)]
    buf, sem = rest[len(tile_ends):]
    i = pl.program_id(0)
    cur = i % 2

    def row_copy(slots, t, kk, half):
        return pltpu.make_async_copy(_row_view(ys_ref, slots[t * TOP_K + kk]),
                                     _row_view(buf.at[half, kk], t), sem.at[half])

    def gather(slots, half):
        def issue(t, carry):
            for kk in range(TOP_K):
                row_copy(slots, t, kk, half).start(priority=kk % 2)
            return carry
        lax.fori_loop(0, tc, issue, 0, unroll=8)

    @pl.when(i == 0)
    def _():
        gather(slot_ref, 0)

    @pl.when(i + 1 < pl.num_programs(0))
    def _():
        gather(next_slot_ref, 1 - cur)

    for kk in range(TOP_K):
        pltpu.make_async_copy(ys_ref.at[pl.ds(0, tc * ROW_SUB), :], buf.at[cur, kk], sem.at[cur]).wait()
    wt = wt_ref[...]
    ff = sum(wt[:, kk:kk + 1] * _load_row_tiles(buf.at[cur, kk], tc) for kk in range(TOP_K))
    g2 = mod_ref[0, 5:6, :]
    y = _layer_norm(DN_ALPHA * x1_ref[...] + (1.0 + g2) * ff) * g_ref[...] + b_ref[...]
    i = pl.program_id(0)
    start = 0
    for y_ref, end in zip(y_refs, tile_ends):
        @pl.when((i >= start) & (i < end))
        def _(y_ref=y_ref):
            y_ref[...] = y
        start = end


def _combine(slots_flat, ys, wts, x1, mod, ln2_g, ln2_b, lay, tc):
    nt = lay.n_tokens

    def seq_map(i):
        return (lay.seq_and_pos(i * tc)[0], 0, 0)

    out_specs, tile_ends = lay.trunk_specs(
        tc, lambda local: pl.BlockSpec((tc, D_MODEL), lambda i: (local(i), 0)))
    n_steps = nt // tc
    return pl.pallas_call(
        functools.partial(_combine_kernel, tc=tc, tile_ends=tile_ends),
        grid=(n_steps,),
        in_specs=[pl.BlockSpec((tc * TOP_K,), lambda i: (i,), memory_space=pltpu.SMEM),
                  pl.BlockSpec((tc * TOP_K,), lambda i: (jnp.minimum(i + 1, n_steps - 1),),
                               memory_space=pltpu.SMEM),
                  pl.BlockSpec(memory_space=pl.ANY),
                  pl.BlockSpec((tc, V7X_LANES), lambda i: (i, 0)),
                  pl.BlockSpec((tc, D_MODEL), lambda i: (i, 0)),
                  pl.BlockSpec((1, N_MOD, D_MODEL), seq_map),
                  _const_spec((1, D_MODEL)), _const_spec((1, D_MODEL))],
        out_specs=out_specs,
        out_shape=[jax.ShapeDtypeStruct((b * s, D_MODEL), F32) for b, s in lay.trunks],
        scratch_shapes=[pltpu.VMEM((2, TOP_K, tc * ROW_SUB, V7X_LANES), F32), pltpu.SemaphoreType.DMA((2,))],
        compiler_params=_cparams(("arbitrary",)),
        name="moe_combine",
    )(slots_flat, slots_flat, ys, wts, x1, mod, ln2_g.reshape(1, -1), ln2_b.reshape(1, -1))


def _encoder_layer(xs, cs, p):
    trunks = [(x.shape[0], x.shape[1]) for x in xs]
    lay = _Layout(trunks)
    nt = lay.n_tokens
    t = _tiles(nt)

    x2d = [xi.reshape(-1, D_MODEL) for xi in xs]
    c = jnp.concatenate(cs, axis=0)
    bp = -(-lay.n_seqs // 8) * 8
    c = jnp.pad(c, ((0, bp - lay.n_seqs), (0, 0)))
    mod = _ada_mod(c, p["w_ada"], p["b_ada"]).reshape(bp, N_MOD, D_MODEL)

    q, k, vt3, xr, gg, gs = _inproj(x2d, mod, p["w_in"].astype(BF16), p["q_gain"], p["k_gain"],
                                    _rope_tables(lay.max_seq), lay, t["tm_in"])

    w_lru = jnp.concatenate([p["lru_wa"], p["lru_wx"]], axis=-1).astype(BF16)
    b_lru = jnp.stack([p["lru_ba"], p["lru_bx"]], axis=1)
    lam = p["lru_lam"].reshape(2, 1, D_MODEL)
    o_atts, hfbs = [], []
    tok_off = 0
    for b, s in trunks:
        o_atts.append(_attention(q, k, vt3, tok_off, b, s, t["tq"], t["tm_in"]))
        hfbs.append(_scan(xr, p["conv_w"], p["conv_b"].reshape(1, -1), w_lru, b_lru, lam,
                          tok_off, b, s, t["tt"]))
        tok_off += b * s

    wr_hi = p["w_router"].astype(BF16)
    wr_lo = (p["w_router"] - wr_hi.astype(F32)).astype(BF16)
    x1, h2r, wfull, msel = _mid(o_atts, hfbs, gg, gs, x2d, mod, p["w_pa"].astype(BF16),
                               p["w_pr"].astype(BF16), p["w_out"].astype(BF16), p["ln1_g"], p["ln1_b"],
                               jnp.concatenate([wr_hi, wr_lo], axis=1), p["b_router"], lay, t["tm_mid"])

    tb = t["tb"]
    rank, cnt = _rank(msel, t["tp"])
    counts = cnt[0].astype(jnp.int32)
    padded = (counts + tb - 1) // tb * tb
    pad_end = jnp.cumsum(padded)
    pad_start = pad_end - padded
    n_blocks = nt * TOP_K // tb + N_EXPERTS
    blk0 = jnp.arange(n_blocks, dtype=jnp.int32) * tb
    block_expert = jnp.minimum(jnp.sum(pad_end[None, :] <= blk0[:, None], axis=1), N_EXPERTS - 1).astype(jnp.int32)
    block_valid = jnp.clip(pad_start[block_expert] + counts[block_expert] - blk0, 0, tb).astype(jnp.int32)
    slots, wts = _slots(rank, msel, wfull, pad_start.astype(F32).reshape(1, N_EXPERTS), t["tp"])
    slots_flat = slots[:, :TOP_K].reshape(-1)

    tail_start = jnp.where(padded > 0, pad_end - tb, pad_end[-1] - tb).astype(jnp.int32)
    xs_rows = _dispatch(slots_flat, tail_start, h2r, n_blocks * tb, t["tg"], tb)
    ys = _experts(block_expert, block_valid, xs_rows, p["w1"], p["b1"], p["w2"], p["b2"], tb)
    ys_out = _combine(slots_flat, ys, wts, x1, mod, p["ln2_g"], p["ln2_b"], lay, t["tc"])
    return [y.reshape(b, s, D_MODEL) for y, (b, s) in zip(ys_out, trunks)]


_PARAM_NAMES = ("w_ada", "b_ada", "w_in", "q_gain", "k_gain", "conv_w", "conv_b", "lru_wa", "lru_ba",
                "lru_wx", "lru_bx", "lru_lam", "w_pa", "w_pr", "w_out", "ln1_g", "ln1_b", "w_router",
                "b_router", "w1", "b1", "w2", "b2", "ln2_g", "ln2_b")


def kernel(x_prompt, x_sample, c_prompt, c_sample, w_ada, b_ada, w_in, q_gain, k_gain, conv_w, conv_b, lru_wa, lru_ba, lru_wx, lru_bx, lru_lam, w_pa, w_pr, w_out, ln1_g, ln1_b, w_router, b_router, w1, b1, w2, b2, ln2_g, ln2_b):
    stacked = (w_ada, b_ada, w_in, q_gain, k_gain, conv_w, conv_b, lru_wa, lru_ba, lru_wx, lru_bx,
               lru_lam, w_pa, w_pr, w_out, ln1_g, ln1_b, w_router, b_router, w1, b1, w2, b2, ln2_g, ln2_b)
    xs, cs = [x_prompt, x_sample], [c_prompt, c_sample]
    for layer in range(DEPTH):
        p = {name: arr[layer] for name, arr in zip(_PARAM_NAMES, stacked)}
        xs = _encoder_layer(xs, cs, p)
    return (xs[0], xs[1])
```

```python
import functools
import math

import jax
import jax.numpy as jnp
from jax import lax
from jax.experimental import pallas as pl
from jax.experimental.pallas import tpu as pltpu

F32 = jnp.float32
BF16 = jnp.bfloat16

D_MODEL = 1024
GRID_W = 64
N_HEADS = 8
N_KV_HEADS = 2
HEAD_DIM = 128
GROUPS = N_HEADS // N_KV_HEADS
KV_WIDTH = N_KV_HEADS * HEAD_DIM
ROPE_THETA = 10000.0
RNN_BLOCKS = 8
RNN_BLOCK_W = D_MODEL // RNN_BLOCKS
CONV_W = 4
LRU_C = 8.0
N_EXPERTS = 32
TOP_K = 4
D_FF = D_MODEL
SWIGLU_LIMIT = 7.0
SWIGLU_ALPHA = 1.702
DEPTH = 1
DN_ALPHA = (2 * DEPTH) ** 0.25
LN_EPS = 1e-5
RMS_EPS = 1e-6
N_MOD = 6
_Q0 = 0
_K0 = _Q0 + D_MODEL
_V0 = _K0 + KV_WIDTH
_XR0 = _V0 + KV_WIDTH
_GR0 = _XR0 + D_MODEL
_GL0 = _GR0 + D_MODEL
IN_WIDTH = _GL0 + 2 * D_MODEL

V7X_LANES = 128
V7X_SUBLANES = 8
V7X_BF16_SUBLANES = 16
V7X_VMEM_LIMIT_BYTES = 56 * 1024 * 1024
HALO = V7X_BF16_SUBLANES
VT_ROWS = HEAD_DIM + V7X_BF16_SUBLANES


def _tiles(n_tokens):
    big = n_tokens >= 4096
    return dict(
        tm_in=512 if big else 128,
        tq=256 if big else 128,
        tt=1024 if big else 128,
        tm_mid=512 if big else 128,
        tg=1024 if big else 256,
        tb=512 if big else 128,
        tc=512 if big else 256,
    )


def _cparams(sem):
    return pltpu.CompilerParams(dimension_semantics=sem, vmem_limit_bytes=V7X_VMEM_LIMIT_BYTES)


def _const_spec(shape):
    nd = len(shape)
    return pl.BlockSpec(shape, lambda *_: (0,) * nd, pipeline_mode=pl.Buffered(1))


def _layer_norm(x):
    mu = jnp.mean(x, axis=-1, keepdims=True)
    xc = x - mu
    var = jnp.mean(xc * xc, axis=-1, keepdims=True)
    return xc * lax.rsqrt(var + LN_EPS)


def _sigmoid(x):
    return 1.0 / (1.0 + jnp.exp(-x))


ROW_SUB = D_MODEL // V7X_LANES
assert ROW_SUB == V7X_SUBLANES, "one token row must fill exactly one f32 tile"


def _store_row_tiles(ref, val):
    n = val.shape[0]
    for j in range(ROW_SUB):
        ref[pl.ds(j, n, stride=ROW_SUB), :] = val[:, j * V7X_LANES:(j + 1) * V7X_LANES]


def _load_row_tiles(ref, n):
    return jnp.concatenate([ref[pl.ds(j, n, stride=ROW_SUB), :] for j in range(ROW_SUB)], axis=1)


def _ada_kernel(c_ref, w_ref, b_ref, o_ref):
    c = c_ref[...]
    s = c * _sigmoid(c)
    o_ref[...] = jnp.dot(s, w_ref[...], preferred_element_type=F32,
                         precision=lax.Precision.HIGHEST) + b_ref[...]


def _ada_mod(c_all, w_ada, b_ada):
    bp = c_all.shape[0]
    ncol = w_ada.shape[1]
    return pl.pallas_call(
        _ada_kernel,
        grid=(ncol // D_MODEL,),
        in_specs=[pl.BlockSpec((bp, D_MODEL), lambda j: (0, 0)),
                  pl.BlockSpec((D_MODEL, D_MODEL), lambda j: (0, j)),
                  pl.BlockSpec((1, D_MODEL), lambda j: (0, j))],
        out_specs=pl.BlockSpec((bp, D_MODEL), lambda j: (0, j)),
        out_shape=jax.ShapeDtypeStruct((bp, ncol), F32),
        compiler_params=_cparams(("arbitrary",)),
        name="ada_mod",
    )(c_all, w_ada, b_ada.reshape(1, ncol))


class _Layout:
    def __init__(self, trunks):
        self.trunks = tuple(trunks)
        self.n_tokens = sum(b * s for b, s in trunks)
        self.n_seqs = sum(b for b, _ in trunks)
        self.max_seq = max(s for _, s in trunks)

    def seq_and_pos(self, t0):
        seq = jnp.int32(0)
        pos = jnp.int32(0)
        tok_off, seq_off = 0, 0
        for b, s in self.trunks:
            inside = (t0 >= tok_off) & (t0 < tok_off + b * s)
            rel = jnp.maximum(t0 - tok_off, 0)
            seq = jnp.where(inside, seq_off + rel // s, seq)
            pos = jnp.where(inside, rel % s, pos)
            tok_off += b * s
            seq_off += b
        return seq, pos

    def trunk_specs(self, tm, make_spec):
        specs, ends, t0 = [], [], 0
        for b, s in self.trunks:
            n_t = b * s // tm
            specs.append(make_spec(functools.partial(_clamped_local, t0=t0, n_t=n_t)))
            t0 += n_t
            ends.append(t0)
        return specs, tuple(ends)


def _clamped_local(i, *, t0, n_t):
    return jnp.clip(i - t0, 0, n_t - 1)


def _owner_value(loads, i, tile_ends):
    val = loads[-1]()
    for j in range(len(loads) - 2, -1, -1):
        val = jnp.where(i < tile_ends[j], loads[j](), val)
    return val


def _rope_tables(max_seq):
    n_rows = max_seq // GRID_W
    axis_dim = HEAD_DIM // 2
    inv = ROPE_THETA ** (-jnp.arange(0, axis_dim, 2, dtype=F32) / axis_dim)
    ar = jnp.arange(n_rows, dtype=F32)[:, None] * inv
    ac = jnp.arange(GRID_W, dtype=F32)[:, None] * inv
    cr, sr = (jnp.repeat(f(ar), GRID_W, axis=0) for f in (jnp.cos, jnp.sin))
    cc, sc = (jnp.tile(f(ac), (n_rows, 1)) for f in (jnp.cos, jnp.sin))
    z = jnp.zeros_like(sr)
    cos_t = jnp.concatenate([cr, cr, cc, cc], axis=-1)
    up_t = jnp.concatenate([-sr, z, -sc, z], axis=-1)
    dn_t = jnp.concatenate([z, sr, z, sc], axis=-1)
    return cos_t, up_t, dn_t


def _inproj_kernel(*refs, tile_ends):
    n_tr = len(tile_ends)
    x_refs = refs[:n_tr]
    (mod_ref, w_ref, qg_ref, kg_ref, cos_ref, up_ref, dn_ref,
     q_ref, k_ref, vt_ref, xr_ref, gg_ref, gs_ref) = refs[n_tr:]
    x = _owner_value([lambda r=r: r[...] for r in x_refs], pl.program_id(0), tile_ends)
    sh1 = mod_ref[0, 0:1, :]
    sc1 = mod_ref[0, 1:2, :]
    h = (_layer_norm(x) * (1.0 + sc1) + sh1).astype(BF16)
    cos_t, up_t, dn_t = cos_ref[...], up_ref[...], dn_ref[...]

    def proj(c0, width):
        return jnp.dot(h, w_ref[:, c0:c0 + width], preferred_element_type=F32)

    def norm_rope(z, gain):
        ms = jnp.mean(z * z, axis=-1, keepdims=True)
        y = z * lax.rsqrt(ms + RMS_EPS) * gain
        return (y * cos_t + pltpu.roll(y, HEAD_DIM - 32, 1) * up_t
                + pltpu.roll(y, 32, 1) * dn_t)

    xr_ref[...] = proj(_XR0, D_MODEL).astype(BF16)
    zq = proj(_Q0, D_MODEL)
    qg = qg_ref[...] * (HEAD_DIM ** -0.5 * math.log2(math.e))
    for hd in range(N_HEADS):
        sl = slice(hd * HEAD_DIM, (hd + 1) * HEAD_DIM)
        q_ref[:, sl] = norm_rope(zq[:, sl], qg).astype(BF16)
    zk = proj(_K0, KV_WIDTH)
    kg = kg_ref[...]
    for hd in range(N_KV_HEADS):
        sl = slice(hd * HEAD_DIM, (hd + 1) * HEAD_DIM)
        k_ref[:, sl] = norm_rope(zk[:, sl], kg).astype(BF16)
    zvt = proj(_V0, KV_WIDTH).T.astype(BF16)
    pad_row = lax.broadcasted_iota(jnp.int32, (VT_ROWS - HEAD_DIM, zvt.shape[1]), 0)
    ones_pad = jnp.where(pad_row == 0, 1.0, 0.0).astype(BF16)
    for hd in range(N_KV_HEADS):
        vt_ref[0, hd * VT_ROWS:hd * VT_ROWS + HEAD_DIM, :] = zvt[hd * HEAD_DIM:(hd + 1) * HEAD_DIM]
        vt_ref[0, hd * VT_ROWS + HEAD_DIM:(hd + 1) * VT_ROWS, :] = ones_pad
    gg_ref[...] = jax.nn.gelu(proj(_GR0, D_MODEL), approximate=True).astype(BF16)
    gs_ref[...] = _sigmoid(proj(_GL0, 2 * D_MODEL)).astype(BF16)


def _inproj(xs, mod, w_in, q_gain, k_gain, tables, lay, tm):
    nt = lay.n_tokens
    cos_t, up_t, dn_t = tables

    def seq_map(i):
        return (lay.seq_and_pos(i * tm)[0], 0, 0)

    def pos_map(i):
        return (lay.seq_and_pos(i * tm)[1] // tm, 0)

    tok = lambda w: pl.BlockSpec((tm, w), lambda i: (i, 0))
    rope = pl.BlockSpec((tm, HEAD_DIM), pos_map)
    x_specs, tile_ends = lay.trunk_specs(
        tm, lambda local: pl.BlockSpec((tm, D_MODEL), lambda i: (local(i), 0)))
    return pl.pallas_call(
        functools.partial(_inproj_kernel, tile_ends=tile_ends),
        grid=(nt // tm,),
        in_specs=x_specs + [
                  pl.BlockSpec((1, N_MOD, D_MODEL), seq_map),
                  _const_spec((D_MODEL, IN_WIDTH)),
                  _const_spec((1, HEAD_DIM)), _const_spec((1, HEAD_DIM)),
                  rope, rope, rope],
        out_specs=[tok(D_MODEL), tok(KV_WIDTH),
                   pl.BlockSpec((1, N_KV_HEADS * VT_ROWS, tm), lambda i: (i, 0, 0)),
                   tok(D_MODEL), tok(D_MODEL), tok(2 * D_MODEL)],
        out_shape=[jax.ShapeDtypeStruct((nt, D_MODEL), BF16),
                   jax.ShapeDtypeStruct((nt, KV_WIDTH), BF16),
                   jax.ShapeDtypeStruct((nt // tm, N_KV_HEADS * VT_ROWS, tm), BF16),
                   jax.ShapeDtypeStruct((nt, D_MODEL), BF16),
                   jax.ShapeDtypeStruct((nt, D_MODEL), BF16),
                   jax.ShapeDtypeStruct((nt, 2 * D_MODEL), BF16)],
        compiler_params=_cparams(("arbitrary",)),
        name="in_proj",
    )(*xs, mod, w_in, q_gain.reshape(1, HEAD_DIM), k_gain.reshape(1, HEAD_DIM), cos_t, up_t, dn_t)


def _attn_kernel(q_ref, k_ref, vt_ref, o_ref, qt_all, s_all, m_all, acc_all, *,
                 tq, tk, n_kv, group, n_sub):
    for sub in range(n_sub):
        rows = slice(sub * tq, (sub + 1) * tq)
        _attn_tile(q_ref.at[rows, :], k_ref, vt_ref, o_ref.at[rows, :], qt_all.at[sub], s_all.at[sub],
                   m_all.at[sub], acc_all.at[sub], tq=tq, tk=tk, n_kv=n_kv, group=group)


def _attn_tile(q_ref, k_ref, vt_ref, o_ref, qt_scr, s_scr, m_scr, acc_scr, *, tq, tk, n_kv, group):
    for g in range(GROUPS):
        qg = q_ref[:, g * HEAD_DIM:(g + 1) * HEAD_DIM].astype(F32)
        qt_scr[:, g * tq:(g + 1) * tq] = qg.T.astype(BF16)
    m_scr[...] = jnp.full(m_scr.shape, -jnp.inf, F32)
    acc_scr[...] = jnp.zeros(acc_scr.shape, F32)

    def scores(j, slot):
        kt = k_ref[pl.ds(pl.multiple_of(j * tk, tk), tk), :]
        s_scr[slot] = jnp.dot(kt, qt_scr[...], preferred_element_type=F32)

    def accumulate(j, slot):
        s = s_scr[slot]
        m_old = m_scr[...]
        m_new = jnp.maximum(m_old, jnp.max(s, axis=0, keepdims=True))
        alpha = jnp.exp2(m_old - m_new)
        p = jnp.exp2(s - m_new).astype(BF16)
        pv = jnp.dot(vt_ref[j], p, preferred_element_type=F32)
        acc_scr[...] = alpha * acc_scr[...] + pv
        m_scr[...] = m_new

    scores(0, 0)

    def body(i, carry):
        j = group * i
        for u in range(group):
            scores(jnp.minimum(j + u + 1, n_kv - 1), (u + 1) % 2)
            accumulate(j + u, u % 2)
        return carry

    lax.fori_loop(0, n_kv // group, body, 0)
    out = acc_scr[0:HEAD_DIM, :] / acc_scr[HEAD_DIM:HEAD_DIM + 1, :]
    for g in range(GROUPS):
        o_ref[:, g * HEAD_DIM:(g + 1) * HEAD_DIM] = out[:, g * tq:(g + 1) * tq].T.astype(BF16)


def _attention(q, k, vt3, tok_off, batch, seq, tq, tk):
    n_kv = seq // tk
    group = next((g for g in (8, 4) if n_kv % g == 0 and n_kv // g >= 2), 2)
    assert n_kv % group == 0
    n_sub = 2 if (seq // tq) % 2 == 0 else 1
    tqs = n_sub * tq
    qrow0 = tok_off // tqs
    srow0 = tok_off // seq
    gw = GROUPS * HEAD_DIM
    n_q = seq // tqs
    return pl.pallas_call(
        functools.partial(_attn_kernel, tq=tq, tk=tk, n_kv=n_kv, group=group, n_sub=n_sub),
        grid=(batch, N_KV_HEADS, n_q),
        in_specs=[pl.BlockSpec((tqs, gw), lambda b, h, i: (qrow0 + b * n_q + i, h)),
                  pl.BlockSpec((seq, HEAD_DIM), lambda b, h, i: (srow0 + b, h)),
                  pl.BlockSpec((n_kv, VT_ROWS, tk), lambda b, h, i: (srow0 + b, h, 0))],
        out_specs=pl.BlockSpec((tqs, gw), lambda b, h, i: (b * n_q + i, h)),
        out_shape=jax.ShapeDtypeStruct((batch * seq, D_MODEL), BF16),
        scratch_shapes=[pltpu.VMEM((n_sub, HEAD_DIM, GROUPS * tq), BF16),
                        pltpu.VMEM((n_sub, 2, tk, GROUPS * tq), F32),
                        pltpu.VMEM((n_sub, 1, GROUPS * tq), F32),
                        pltpu.VMEM((n_sub, VT_ROWS, GROUPS * tq), F32)],
        compiler_params=_cparams(("arbitrary", "arbitrary", "arbitrary")),
        name="attention",
    )(q, k, vt3)


def _scan_kernel(cur_ref, prev_ref, next_ref, cw_ref, cb_ref, w_ref, b_ref, lam_ref, o_ref,
                 xc_scr, a_scr, u_scr, h_scr, carry_scr, *, tt, n_chunks):
    d = pl.program_id(1)
    c = pl.program_id(2)
    chunk = jnp.where(d == 0, c, n_chunks - 1 - c)

    @pl.when(c == 0)
    def _():
        carry_scr[...] = jnp.zeros(carry_scr.shape, F32)

    keep_prev = jnp.where(chunk == 0, 0.0, 1.0)
    keep_next = jnp.where(chunk == n_chunks - 1, 0.0, 1.0)
    cur = cur_ref[...].astype(F32)
    taps = [cw_ref[j:j + 1, :] for j in range(CONV_W)]
    xc_scr[...] = (cb_ref[...] + taps[2] * cur + taps[1] * pltpu.roll(cur, 1, 0)
                   + taps[0] * pltpu.roll(cur, 2, 0) + taps[3] * pltpu.roll(cur, tt - 1, 0))
    sub = V7X_SUBLANES
    head = jnp.concatenate([prev_ref[...].astype(F32)[HALO - sub:HALO] * keep_prev, cur[0:2 * sub]], axis=0)
    tail = jnp.concatenate([cur[tt - 2 * sub:tt], next_ref[...].astype(F32)[0:sub] * keep_next], axis=0)
    first = sum(taps[j] * head[sub - 2 + j:2 * sub - 2 + j] for j in range(CONV_W))
    last = sum(taps[j] * tail[sub - 2 + j:2 * sub - 2 + j] for j in range(CONV_W))
    xc_scr[0:sub, :] = cb_ref[...] + first
    xc_scr[tt - sub:tt, :] = cb_ref[...] + last
    xc = xc_scr[...]
    xcb = xc.astype(BF16)

    lam = lam_ref[0]
    y = jnp.exp(-jnp.abs(lam))
    w1p = 1.0 + y
    log1p_y = jnp.where(w1p == 1.0, y, jnp.log(w1p) * y / jnp.where(w1p == 1.0, 1.0, w1p - 1.0))
    neg_c_sp = (-LRU_C * math.log2(math.e)) * (jnp.maximum(-lam, 0.0) + log1p_y)

    for n in range(RNN_BLOCKS):
        sl = slice(n * RNN_BLOCK_W, (n + 1) * RNN_BLOCK_W)
        pre = jnp.dot(xcb[:, sl], w_ref[0, n], preferred_element_type=F32)
        r = _sigmoid(pre[:, :RNN_BLOCK_W] + b_ref[0, 0:1, sl])
        i = _sigmoid(pre[:, RNN_BLOCK_W:] + b_ref[0, 1:2, sl])
        a = jnp.exp2(r * neg_c_sp[:, sl])
        a_scr[:, sl] = a
        u_scr[:, sl] = jnp.sqrt(1.0 - a * a) * (i * xc[:, sl])

    def step(t, h):
        row = jnp.where(d == 0, t, tt - 1 - t)
        h = a_scr[pl.ds(row, 1), :] * h + u_scr[pl.ds(row, 1), :]
        h_scr[pl.ds(row, 1), :] = h
        return h

    carry_scr[...] = lax.fori_loop(0, tt, step, carry_scr[...], unroll=8)
    o_ref[0] = h_scr[...].astype(BF16)


def _scan(xr, conv_w, conv_b, w_lru, b_lru, lam, tok_off, batch, seq, tt):
    nt = xr.shape[0]
    n_chunks = seq // tt
    row0 = tok_off // tt
    hrow0 = tok_off // HALO
    hpc = tt // HALO
    n_halo = nt // HALO

    def chunk_of(d, c):
        return jnp.where(d == 0, c, n_chunks - 1 - c)

    def prev_map(b, d, c):
        return (jnp.maximum(hrow0 + (b * n_chunks + chunk_of(d, c)) * hpc - 1, 0), 0)

    def next_map(b, d, c):
        return (jnp.minimum(hrow0 + (b * n_chunks + chunk_of(d, c) + 1) * hpc, n_halo - 1), 0)

    return pl.pallas_call(
        functools.partial(_scan_kernel, tt=tt, n_chunks=n_chunks),
        grid=(batch, 2, n_chunks),
        in_specs=[pl.BlockSpec((tt, D_MODEL), lambda b, d, c: (row0 + b * n_chunks + chunk_of(d, c), 0)),
                  pl.BlockSpec((HALO, D_MODEL), prev_map),
                  pl.BlockSpec((HALO, D_MODEL), next_map),
                  pl.BlockSpec((CONV_W, D_MODEL), lambda b, d, c: (0, 0)),
                  pl.BlockSpec((1, D_MODEL), lambda b, d, c: (0, 0)),
                  pl.BlockSpec((1, RNN_BLOCKS, RNN_BLOCK_W, 2 * RNN_BLOCK_W), lambda b, d, c: (d, 0, 0, 0)),
                  pl.BlockSpec((1, 2, D_MODEL), lambda b, d, c: (d, 0, 0)),
                  pl.BlockSpec((1, 1, D_MODEL), lambda b, d, c: (d, 0, 0))],
        out_specs=pl.BlockSpec((1, tt, D_MODEL), lambda b, d, c: (d, b * n_chunks + chunk_of(d, c), 0)),
        out_shape=jax.ShapeDtypeStruct((2, batch * seq, D_MODEL), BF16),
        scratch_shapes=[pltpu.VMEM((tt, D_MODEL), F32),
                        pltpu.VMEM((tt, D_MODEL), F32),
                        pltpu.VMEM((tt, D_MODEL), F32),
                        pltpu.VMEM((tt, D_MODEL), F32),
                        pltpu.VMEM((1, D_MODEL), F32)],
        compiler_params=_cparams(("arbitrary", "arbitrary", "arbitrary")),
        name="lru_scan",
    )(xr, xr, xr, conv_w, conv_b, w_lru, b_lru, lam)


def _mid_kernel(*refs, tile_ends):
    n_tr = len(tile_ends)
    oa_refs = refs[:n_tr]
    h_refs = refs[n_tr:3 * n_tr]
    x_refs = refs[3 * n_tr:4 * n_tr]
    (gg_ref, gs_ref, mod_ref, wpa_ref, wpr_ref, wo_ref, g1_ref, b1_ref, wr_ref, br_ref,
     x1_ref, h2_ref, route_ref, wt_ref, cnt_ref, tri_scr, run_scr) = refs[4 * n_tr:]
    i = pl.program_id(0)
    oa = _owner_value([lambda r=r: r[...] for r in oa_refs], i, tile_ends)
    h_fwd = _owner_value([lambda r=h_refs[2 * j]: r[0] for j in range(n_tr)], i, tile_ends)
    h_bwd = _owner_value([lambda r=h_refs[2 * j + 1]: r[0] for j in range(n_tr)], i, tile_ends)
    hsum = h_fwd.astype(F32) + h_bwd.astype(F32)
    x_in = _owner_value([lambda r=r: r[...] for r in x_refs], i, tile_ends)
    o_att = jnp.dot(oa, wpa_ref[...], preferred_element_type=F32)
    rec = hsum.astype(BF16) * gg_ref[...]
    o_rec = jnp.dot(rec, wpr_ref[...], preferred_element_type=F32)
    gs = gs_ref[...].astype(F32)
    merged = gs[:, :D_MODEL] * o_att + gs[:, D_MODEL:] * o_rec
    mix = jnp.dot(merged.astype(BF16), wo_ref[...], preferred_element_type=F32)
    g1 = mod_ref[0, 2:3, :]
    sh2 = mod_ref[0, 3:4, :]
    sc2 = mod_ref[0, 4:5, :]
    x1 = _layer_norm(DN_ALPHA * x_in + (1.0 + g1) * mix) * g1_ref[...] + b1_ref[...]
    x1_ref[...] = x1
    h2 = _layer_norm(x1) * (1.0 + sc2) + sh2
    _store_row_tiles(h2_ref, h2)
    tm = h2.shape[0]
    h2_hi = h2.astype(BF16)
    h2_lo = (h2 - h2_hi.astype(F32)).astype(BF16)
    cross = jnp.dot(jnp.concatenate([h2_hi, h2_lo], axis=0), wr_ref[...], preferred_element_type=F32)
    logits = ((cross[:tm, :N_EXPERTS] + cross[:tm, N_EXPERTS:])
              + (cross[tm:, :N_EXPERTS] + cross[tm:, N_EXPERTS:])) + br_ref[...]
    lane = lax.broadcasted_iota(jnp.int32, logits.shape, 1).astype(F32)
    rem = logits
    sel = jnp.zeros(logits.shape, F32)
    picks, experts, tops = [], [], []
    for kk in range(TOP_K):
        m = jnp.max(rem, axis=-1, keepdims=True)
        idx = jnp.min(jnp.where(rem == m, lane, float(N_EXPERTS)), axis=-1, keepdims=True)
        pick = lane == idx
        sel = jnp.where(pick, 1.0, sel)
        rem = jnp.where(pick, -jnp.inf, rem)
        picks.append(pick)
        experts.append(idx)
        tops.append(m)
    gates = [jnp.ones_like(tops[0])] + [jnp.exp(m - tops[0]) for m in tops[1:]]
    denom = sum(gates)

    @pl.when(pl.program_id(0) == 0)
    def _():
        r = lax.broadcasted_iota(jnp.int32, tri_scr.shape, 0)
        c = lax.broadcasted_iota(jnp.int32, tri_scr.shape, 1)
        tri_scr[...] = jnp.where(c < r, 1.0, 0.0).astype(BF16)
        run_scr[...] = jnp.zeros(run_scr.shape, F32)

    rank = jnp.dot(tri_scr[...], sel.astype(BF16), preferred_element_type=F32) + run_scr[...]
    run_scr[...] = run_scr[...] + jnp.sum(sel, axis=0, keepdims=True)
    cnt_ref[...] = jnp.broadcast_to(run_scr[...], cnt_ref.shape)

    out_lane = lax.broadcasted_iota(jnp.int32, route_ref.shape, 1)
    route = jnp.zeros(route_ref.shape, F32)
    wts = jnp.zeros(wt_ref.shape, F32)
    for kk in range(TOP_K):
        r_k = jnp.sum(jnp.where(picks[kk], rank, 0.0), axis=-1, keepdims=True)
        route = jnp.where(out_lane == kk, experts[kk], route)
        route = jnp.where(out_lane == TOP_K + kk, r_k, route)
        wts = jnp.where(out_lane == kk, gates[kk] / denom, wts)
    route_ref[...] = route.astype(jnp.int32)
    wt_ref[...] = wts


def _mid(o_atts, hfbs, gg, gs, xs, mod, w_pa, w_pr, w_out, ln1_g, ln1_b, w_router, b_router, lay, tm):
    nt = lay.n_tokens

    def seq_map(i):
        return (lay.seq_and_pos(i * tm)[0], 0, 0)

    tok = lambda w: pl.BlockSpec((tm, w), lambda i: (i, 0))
    vec = lambda w: _const_spec((1, w))
    row_specs, tile_ends = lay.trunk_specs(
        tm, lambda local: pl.BlockSpec((tm, D_MODEL), lambda i: (local(i), 0)))
    h_pairs, _ = lay.trunk_specs(
        tm, lambda local: [pl.BlockSpec((1, tm, D_MODEL), lambda i, d=d: (d, local(i), 0)) for d in range(2)])
    h_specs = [spec for pair in h_pairs for spec in pair]
    h_args = [hfb for hfb in hfbs for _ in range(2)]
    return pl.pallas_call(
        functools.partial(_mid_kernel, tile_ends=tile_ends),
        grid=(nt // tm,),
        in_specs=row_specs + h_specs + row_specs + [
                  tok(D_MODEL), tok(2 * D_MODEL),
                  pl.BlockSpec((1, N_MOD, D_MODEL), seq_map),
                  _const_spec((D_MODEL, D_MODEL)), _const_spec((D_MODEL, D_MODEL)),
                  _const_spec((D_MODEL, D_MODEL)),
                  vec(D_MODEL), vec(D_MODEL),
                  _const_spec((D_MODEL, 2 * N_EXPERTS)), vec(N_EXPERTS)],
        out_specs=[tok(D_MODEL), pl.BlockSpec((ROW_SUB * tm, V7X_LANES), lambda i: (i, 0)),
                   tok(V7X_LANES), tok(V7X_LANES),
                   pl.BlockSpec((V7X_SUBLANES, N_EXPERTS), lambda i: (0, 0))],
        out_shape=[jax.ShapeDtypeStruct((nt, D_MODEL), F32),
                   jax.ShapeDtypeStruct((ROW_SUB * nt, V7X_LANES), F32),
                   jax.ShapeDtypeStruct((nt, V7X_LANES), jnp.int32),
                   jax.ShapeDtypeStruct((nt, V7X_LANES), F32),
                   jax.ShapeDtypeStruct((V7X_SUBLANES, N_EXPERTS), F32)],
        scratch_shapes=[pltpu.VMEM((tm, tm), BF16), pltpu.VMEM((1, N_EXPERTS), F32)],
        compiler_params=_cparams(("arbitrary",)),
        name="merge_router",
    )(*o_atts, *h_args, *xs, gg, gs, mod, w_pa, w_pr, w_out, ln1_g.reshape(1, -1), ln1_b.reshape(1, -1),
      w_router, b_router.reshape(1, -1))


def _row_view(ref, row):
    return ref.at[pl.ds(pl.multiple_of(row * ROW_SUB, ROW_SUB), ROW_SUB), :]


def _dispatch_kernel(slot_ref, tail_ref, h_ref, xs_ref, zero_scr, sem, zsem, *, tg, tb):
    @pl.when(pl.program_id(0) == 0)
    def _():
        zero_scr[...] = jnp.zeros(zero_scr.shape, F32)
        for e in range(N_EXPERTS):
            tail = pl.multiple_of(tail_ref[e] * ROW_SUB, tb * ROW_SUB)
            pltpu.make_async_copy(zero_scr, xs_ref.at[pl.ds(tail, tb * ROW_SUB), :], zsem).start()
        for e in range(N_EXPERTS):
            pltpu.make_async_copy(zero_scr, xs_ref.at[pl.ds(0, tb * ROW_SUB), :], zsem).wait()

    def issue(t, carry):
        for kk in range(TOP_K):
            pltpu.make_async_copy(_row_view(h_ref, t), _row_view(xs_ref, slot_ref[t * TOP_K + kk]),
                                  sem).start(priority=kk % 2)
        return carry

    lax.fori_loop(0, tg, issue, 0, unroll=2)

    for kk in range(TOP_K):
        pltpu.make_async_copy(h_ref, xs_ref.at[pl.ds(0, tg * ROW_SUB), :], sem).wait()


def _dispatch(slots_flat, tail_start, h2r, cap, tg, tb):
    nt = h2r.shape[0] // ROW_SUB
    return pl.pallas_call(
        functools.partial(_dispatch_kernel, tg=tg, tb=tb),
        grid=(nt // tg,),
        in_specs=[pl.BlockSpec((tg * TOP_K,), lambda i: (i,), memory_space=pltpu.SMEM),
                  pl.BlockSpec(memory_space=pltpu.SMEM),
                  pl.BlockSpec((tg * ROW_SUB, V7X_LANES), lambda i: (i, 0))],
        out_specs=pl.BlockSpec(memory_space=pl.ANY),
        out_shape=jax.ShapeDtypeStruct((cap * ROW_SUB, V7X_LANES), F32),
        scratch_shapes=[pltpu.VMEM((tb * ROW_SUB, V7X_LANES), F32), pltpu.SemaphoreType.DMA(()),
                        pltpu.SemaphoreType.DMA(())],
        compiler_params=_cparams(("arbitrary",)),
        name="moe_dispatch",
    )(slots_flat, tail_start, h2r)


def _expert_kernel(be_ref, bv_ref, xs_ref, w1_ref, b1_ref, w2_ref, b2_ref, ys_ref, w1_scr, w2_scr, *, tb):
    i = pl.program_id(0)
    valid = bv_ref[i]

    @pl.when((i == 0) | (be_ref[i] != be_ref[jnp.maximum(i - 1, 0)]))
    def _():
        w1_scr[...] = w1_ref[0].astype(BF16)
        w2_scr[...] = w2_ref[0].astype(BF16)

    @pl.when(valid > 0)
    def _():
        xb = _load_row_tiles(xs_ref, tb).astype(BF16)
        gu = jnp.dot(xb, w1_scr[...], preferred_element_type=F32) + b1_ref[0]
        glu = jnp.minimum(gu[:, :D_FF], SWIGLU_LIMIT)
        lin = jnp.clip(gu[:, D_FF:], -SWIGLU_LIMIT, SWIGLU_LIMIT)
        act = (lin + 1.0) * glu * _sigmoid(SWIGLU_ALPHA * glu)
        y = jnp.dot(act.astype(BF16), w2_scr[...], preferred_element_type=F32) + b2_ref[0]
        _store_row_tiles(ys_ref, y)

    @pl.when(valid <= 0)
    def _():
        ys_ref[...] = jnp.zeros(ys_ref.shape, F32)


def _experts(block_expert, block_valid, xs, w1, b1, w2, b2, tb):
    cap = xs.shape[0] // ROW_SUB
    rows = pl.BlockSpec((tb * ROW_SUB, V7X_LANES), lambda i, be, bv: (i, 0))
    grid_spec = pltpu.PrefetchScalarGridSpec(
        num_scalar_prefetch=2,
        grid=(cap // tb,),
        in_specs=[rows,
                  pl.BlockSpec((1, D_MODEL, 2 * D_FF), lambda i, be, bv: (be[i], 0, 0)),
                  pl.BlockSpec((1, 1, 2 * D_FF), lambda i, be, bv: (be[i], 0, 0)),
                  pl.BlockSpec((1, D_FF, D_MODEL), lambda i, be, bv: (be[i], 0, 0)),
                  pl.BlockSpec((1, 1, D_MODEL), lambda i, be, bv: (be[i], 0, 0))],
        out_specs=rows,
        scratch_shapes=[pltpu.VMEM((D_MODEL, 2 * D_FF), BF16), pltpu.VMEM((D_FF, D_MODEL), BF16)],
    )
    return pl.pallas_call(
        functools.partial(_expert_kernel, tb=tb),
        grid_spec=grid_spec,
        out_shape=jax.ShapeDtypeStruct((cap * ROW_SUB, V7X_LANES), F32),
        compiler_params=_cparams(("arbitrary",)),
        name="moe_experts",
    )(block_expert, block_valid, xs, w1, b1.reshape(N_EXPERTS, 1, -1), w2, b2.reshape(N_EXPERTS, 1, -1))


def _combine_kernel(slot_ref, next_slot_ref, ys_ref, wt_ref, x1_ref, mod_ref, g_ref, b_ref, *rest,
                    tc, tile_ends):
    y_refs = rest[:len(tile_ends)]
    buf, sem = rest[len(tile_ends):]
    i = pl.program_id(0)
    cur = i % 2

    def row_copy(slots, t, kk, half):
        return pltpu.make_async_copy(_row_view(ys_ref, slots[t * TOP_K + kk]),
                                     _row_view(buf.at[half, kk], t), sem.at[half])

    def gather(slots, half):
        def issue(t, carry):
            for kk in range(TOP_K):
                row_copy(slots, t, kk, half).start(priority=kk % 2)
            return carry
        lax.fori_loop(0, tc, issue, 0, unroll=8)

    @pl.when(i == 0)
    def _():
        gather(slot_ref, 0)

    @pl.when(i + 1 < pl.num_programs(0))
    def _():
        gather(next_slot_ref, 1 - cur)

    for kk in range(TOP_K):
        pltpu.make_async_copy(ys_ref.at[pl.ds(0, tc * ROW_SUB), :], buf.at[cur, kk], sem.at[cur]).wait()
    wt = wt_ref[...]
    ff = sum(wt[:, kk:kk + 1] * _load_row_tiles(buf.at[cur, kk], tc) for kk in range(TOP_K))
    g2 = mod_ref[0, 5:6, :]
    y = _layer_norm(DN_ALPHA * x1_ref[...] + (1.0 + g2) * ff) * g_ref[...] + b_ref[...]
    i = pl.program_id(0)
    start = 0
    for y_ref, end in zip(y_refs, tile_ends):
        @pl.when((i >= start) & (i < end))
        def _(y_ref=y_ref):
            y_ref[...] = y
        start = end


def _combine(slots_flat, ys, wts, x1, mod, ln2_g, ln2_b, lay, tc):
    nt = lay.n_tokens

    def seq_map(i):
        return (lay.seq_and_pos(i * tc)[0], 0, 0)

    out_specs, tile_ends = lay.trunk_specs(
        tc, lambda local: pl.BlockSpec((tc, D_MODEL), lambda i: (local(i), 0)))
    n_steps = nt // tc
    return pl.pallas_call(
        functools.partial(_combine_kernel, tc=tc, tile_ends=tile_ends),
        grid=(n_steps,),
        in_specs=[pl.BlockSpec((tc * TOP_K,), lambda i: (i,), memory_space=pltpu.SMEM),
                  pl.BlockSpec((tc * TOP_K,), lambda i: (jnp.minimum(i + 1, n_steps - 1),),
                               memory_space=pltpu.SMEM),
                  pl.BlockSpec(memory_space=pl.ANY),
                  pl.BlockSpec((tc, V7X_LANES), lambda i: (i, 0)),
                  pl.BlockSpec((tc, D_MODEL), lambda i: (i, 0)),
                  pl.BlockSpec((1, N_MOD, D_MODEL), seq_map),
                  _const_spec((1, D_MODEL)), _const_spec((1, D_MODEL))],
        out_specs=out_specs,
        out_shape=[jax.ShapeDtypeStruct((b * s, D_MODEL), F32) for b, s in lay.trunks],
        scratch_shapes=[pltpu.VMEM((2, TOP_K, tc * ROW_SUB, V7X_LANES), F32), pltpu.SemaphoreType.DMA((2,))],
        compiler_params=_cparams(("arbitrary",)),
        name="moe_combine",
    )(slots_flat, slots_flat, ys, wts, x1, mod, ln2_g.reshape(1, -1), ln2_b.reshape(1, -1))


def _encoder_layer(xs, cs, p):
    trunks = [(x.shape[0], x.shape[1]) for x in xs]
    lay = _Layout(trunks)
    nt = lay.n_tokens
    t = _tiles(nt)

    x2d = [xi.reshape(-1, D_MODEL) for xi in xs]
    c = jnp.concatenate(cs, axis=0)
    bp = -(-lay.n_seqs // 8) * 8
    c = jnp.pad(c, ((0, bp - lay.n_seqs), (0, 0)))
    mod = _ada_mod(c, p["w_ada"], p["b_ada"]).reshape(bp, N_MOD, D_MODEL)

    q, k, vt3, xr, gg, gs = _inproj(x2d, mod, p["w_in"].astype(BF16), p["q_gain"], p["k_gain"],
                                    _rope_tables(lay.max_seq), lay, t["tm_in"])

    w_lru = jnp.concatenate([p["lru_wa"], p["lru_wx"]], axis=-1).astype(BF16)
    b_lru = jnp.stack([p["lru_ba"], p["lru_bx"]], axis=1)
    lam = p["lru_lam"].reshape(2, 1, D_MODEL)
    o_atts, hfbs = [], []
    tok_off = 0
    for b, s in trunks:
        o_atts.append(_attention(q, k, vt3, tok_off, b, s, t["tq"], t["tm_in"]))
        hfbs.append(_scan(xr, p["conv_w"], p["conv_b"].reshape(1, -1), w_lru, b_lru, lam,
                          tok_off, b, s, t["tt"]))
        tok_off += b * s

    wr_hi = p["w_router"].astype(BF16)
    wr_lo = (p["w_router"] - wr_hi.astype(F32)).astype(BF16)
    x1, h2r, route, wts, cnt = _mid(o_atts, hfbs, gg, gs, x2d, mod, p["w_pa"].astype(BF16),
                                    p["w_pr"].astype(BF16), p["w_out"].astype(BF16), p["ln1_g"], p["ln1_b"],
                                    jnp.concatenate([wr_hi, wr_lo], axis=1), p["b_router"], lay, t["tm_mid"])

    tb = t["tb"]
    counts = cnt[0].astype(jnp.int32)
    padded = (counts + tb - 1) // tb * tb
    pad_end = jnp.cumsum(padded)
    pad_start = pad_end - padded
    n_blocks = nt * TOP_K // tb + N_EXPERTS
    blk0 = jnp.arange(n_blocks, dtype=jnp.int32) * tb
    block_expert = jnp.minimum(jnp.sum(pad_end[None, :] <= blk0[:, None], axis=1), N_EXPERTS - 1).astype(jnp.int32)
    block_valid = jnp.clip(pad_start[block_expert] + counts[block_expert] - blk0, 0, tb).astype(jnp.int32)
    expert = route[:, 0:TOP_K]
    start_of = jnp.sum(jnp.where(expert[..., None] == jnp.arange(N_EXPERTS, dtype=jnp.int32),
                                 pad_start.astype(jnp.int32), 0), axis=-1)
    slots_flat = (start_of + route[:, TOP_K:2 * TOP_K]).reshape(-1)

    tail_start = jnp.where(padded > 0, pad_end - tb, pad_end[-1] - tb).astype(jnp.int32)
    xs_rows = _dispatch(slots_flat, tail_start, h2r, n_blocks * tb, t["tg"], tb)
    ys = _experts(block_expert, block_valid, xs_rows, p["w1"], p["b1"], p["w2"], p["b2"], tb)
    ys_out = _combine(slots_flat, ys, wts, x1, mod, p["ln2_g"], p["ln2_b"], lay, t["tc"])
    return [y.reshape(b, s, D_MODEL) for y, (b, s) in zip(ys_out, trunks)]


_PARAM_NAMES = ("w_ada", "b_ada", "w_in", "q_gain", "k_gain", "conv_w", "conv_b", "lru_wa", "lru_ba",
                "lru_wx", "lru_bx", "lru_lam", "w_pa", "w_pr", "w_out", "ln1_g", "ln1_b", "w_router",
                "b_router", "w1", "b1", "w2", "b2", "ln2_g", "ln2_b")


def kernel(x_prompt, x_sample, c_prompt, c_sample, w_ada, b_ada, w_in, q_gain, k_gain, conv_w, conv_b, lru_wa, lru_ba, lru_wx, lru_bx, lru_lam, w_pa, w_pr, w_out, ln1_g, ln1_b, w_router, b_router, w1, b1, w2, b2, ln2_g, ln2_b):
    stacked = (w_ada, b_ada, w_in, q_gain, k_gain, conv_w, conv_b, lru_wa, lru_ba, lru_wx, lru_bx,
               lru_lam, w_pa, w_pr, w_out, ln1_g, ln1_b, w_router, b_router, w1, b1, w2, b2, ln2_g, ln2_b)
    xs, cs = [x_prompt, x_sample], [c_prompt, c_sample]
    for layer in range(DEPTH):
        p = {name: arr[layer] for name, arr in zip(_PARAM_NAMES, stacked)}
        xs = _encoder_layer(xs, cs, p)
    return (xs[0], xs[1])
```

```python
import functools
import math

import jax
import jax.numpy as jnp
from jax import lax
from jax.experimental import pallas as pl
from jax.experimental.pallas import tpu as pltpu

F32 = jnp.float32
BF16 = jnp.bfloat16

D_MODEL = 1024
GRID_W = 64
N_HEADS = 8
N_KV_HEADS = 2
HEAD_DIM = 128
GROUPS = N_HEADS // N_KV_HEADS
KV_WIDTH = N_KV_HEADS * HEAD_DIM
ROPE_THETA = 10000.0
RNN_BLOCKS = 8
RNN_BLOCK_W = D_MODEL // RNN_BLOCKS
CONV_W = 4
LRU_C = 8.0
N_EXPERTS = 32
TOP_K = 4
D_FF = D_MODEL
SWIGLU_LIMIT = 7.0
SWIGLU_ALPHA = 1.702
DEPTH = 1
DN_ALPHA = (2 * DEPTH) ** 0.25
LN_EPS = 1e-5
RMS_EPS = 1e-6
N_MOD = 6
_Q0 = 0
_K0 = _Q0 + D_MODEL
_V0 = _K0 + KV_WIDTH
_XR0 = _V0 + KV_WIDTH
_GR0 = _XR0 + D_MODEL
_GL0 = _GR0 + D_MODEL
IN_WIDTH = _GL0 + 2 * D_MODEL

V7X_LANES = 128
V7X_SUBLANES = 8
V7X_BF16_SUBLANES = 16
V7X_VMEM_LIMIT_BYTES = 56 * 1024 * 1024
HALO = V7X_BF16_SUBLANES
VT_ROWS = HEAD_DIM + V7X_BF16_SUBLANES


def _tiles(n_tokens):
    big = n_tokens >= 4096
    return dict(
        tm_in=512 if big else 128,
        tq=256 if big else 128,
        tt=1024 if big else 128,
        tm_mid=512 if big else 128,
        tg=1024 if big else 256,
        tb=512 if big else 128,
        tc=512 if big else 256,
    )


def _cparams(sem):
    return pltpu.CompilerParams(dimension_semantics=sem, vmem_limit_bytes=V7X_VMEM_LIMIT_BYTES)


def _const_spec(shape):
    nd = len(shape)
    return pl.BlockSpec(shape, lambda *_: (0,) * nd, pipeline_mode=pl.Buffered(1))


def _layer_norm(x):
    mu = jnp.mean(x, axis=-1, keepdims=True)
    xc = x - mu
    var = jnp.mean(xc * xc, axis=-1, keepdims=True)
    return xc * lax.rsqrt(var + LN_EPS)


def _sigmoid(x):
    return 1.0 / (1.0 + jnp.exp(-x))


ROW_SUB = D_MODEL // V7X_LANES
assert ROW_SUB == V7X_SUBLANES, "one token row must fill exactly one f32 tile"


def _store_row_tiles(ref, val):
    n = val.shape[0]
    for j in range(ROW_SUB):
        ref[pl.ds(j, n, stride=ROW_SUB), :] = val[:, j * V7X_LANES:(j + 1) * V7X_LANES]


def _load_row_tiles(ref, n):
    return jnp.concatenate([ref[pl.ds(j, n, stride=ROW_SUB), :] for j in range(ROW_SUB)], axis=1)


def _ada_kernel(c_ref, w_ref, b_ref, o_ref):
    c = c_ref[...]
    s = c * _sigmoid(c)
    o_ref[...] = jnp.dot(s, w_ref[...], preferred_element_type=F32,
                         precision=lax.Precision.HIGHEST) + b_ref[...]


def _ada_mod(c_all, w_ada, b_ada):
    bp = c_all.shape[0]
    ncol = w_ada.shape[1]
    return pl.pallas_call(
        _ada_kernel,
        grid=(ncol // D_MODEL,),
        in_specs=[pl.BlockSpec((bp, D_MODEL), lambda j: (0, 0)),
                  pl.BlockSpec((D_MODEL, D_MODEL), lambda j: (0, j)),
                  pl.BlockSpec((1, D_MODEL), lambda j: (0, j))],
        out_specs=pl.BlockSpec((bp, D_MODEL), lambda j: (0, j)),
        out_shape=jax.ShapeDtypeStruct((bp, ncol), F32),
        compiler_params=_cparams(("arbitrary",)),
        name="ada_mod",
    )(c_all, w_ada, b_ada.reshape(1, ncol))


class _Layout:
    def __init__(self, trunks):
        self.trunks = tuple(trunks)
        self.n_tokens = sum(b * s for b, s in trunks)
        self.n_seqs = sum(b for b, _ in trunks)
        self.max_seq = max(s for _, s in trunks)

    def seq_and_pos(self, t0):
        seq = jnp.int32(0)
        pos = jnp.int32(0)
        tok_off, seq_off = 0, 0
        for b, s in self.trunks:
            inside = (t0 >= tok_off) & (t0 < tok_off + b * s)
            rel = jnp.maximum(t0 - tok_off, 0)
            seq = jnp.where(inside, seq_off + rel // s, seq)
            pos = jnp.where(inside, rel % s, pos)
            tok_off += b * s
            seq_off += b
        return seq, pos

    def trunk_specs(self, tm, make_spec):
        specs, ends, t0 = [], [], 0
        for b, s in self.trunks:
            n_t = b * s // tm
            specs.append(make_spec(functools.partial(_clamped_local, t0=t0, n_t=n_t)))
            t0 += n_t
            ends.append(t0)
        return specs, tuple(ends)


def _clamped_local(i, *, t0, n_t):
    return jnp.clip(i - t0, 0, n_t - 1)


def _owner_value(loads, i, tile_ends):
    val = loads[-1]()
    for j in range(len(loads) - 2, -1, -1):
        val = jnp.where(i < tile_ends[j], loads[j](), val)
    return val


def _rope_tables(max_seq):
    n_rows = max_seq // GRID_W
    axis_dim = HEAD_DIM // 2
    inv = ROPE_THETA ** (-jnp.arange(0, axis_dim, 2, dtype=F32) / axis_dim)
    ar = jnp.arange(n_rows, dtype=F32)[:, None] * inv
    ac = jnp.arange(GRID_W, dtype=F32)[:, None] * inv
    cr, sr = (jnp.repeat(f(ar), GRID_W, axis=0) for f in (jnp.cos, jnp.sin))
    cc, sc = (jnp.tile(f(ac), (n_rows, 1)) for f in (jnp.cos, jnp.sin))
    z = jnp.zeros_like(sr)
    cos_t = jnp.concatenate([cr, cr, cc, cc], axis=-1)
    up_t = jnp.concatenate([-sr, z, -sc, z], axis=-1)
    dn_t = jnp.concatenate([z, sr, z, sc], axis=-1)
    return cos_t, up_t, dn_t


def _inproj_kernel(*refs, tile_ends):
    n_tr = len(tile_ends)
    x_refs = refs[:n_tr]
    (mod_ref, w_ref, qg_ref, kg_ref, cos_ref, up_ref, dn_ref,
     q_ref, k_ref, vt_ref, xr_ref, gg_ref, gs_ref) = refs[n_tr:]
    x = _owner_value([lambda r=r: r[...] for r in x_refs], pl.program_id(0), tile_ends)
    sh1 = mod_ref[0, 0:1, :]
    sc1 = mod_ref[0, 1:2, :]
    h = (_layer_norm(x) * (1.0 + sc1) + sh1).astype(BF16)
    cos_t, up_t, dn_t = cos_ref[...], up_ref[...], dn_ref[...]

    def proj(c0, width):
        return jnp.dot(h, w_ref[:, c0:c0 + width], preferred_element_type=F32)

    def norm_rope(z, gain):
        ms = jnp.mean(z * z, axis=-1, keepdims=True)
        y = z * lax.rsqrt(ms + RMS_EPS) * gain
        return (y * cos_t + pltpu.roll(y, HEAD_DIM - 32, 1) * up_t
                + pltpu.roll(y, 32, 1) * dn_t)

    xr_ref[...] = proj(_XR0, D_MODEL).astype(BF16)
    zq = proj(_Q0, D_MODEL)
    qg = qg_ref[...] * (HEAD_DIM ** -0.5 * math.log2(math.e))
    for hd in range(N_HEADS):
        sl = slice(hd * HEAD_DIM, (hd + 1) * HEAD_DIM)
        q_ref[:, sl] = norm_rope(zq[:, sl], qg).astype(BF16)
    zk = proj(_K0, KV_WIDTH)
    kg = kg_ref[...]
    for hd in range(N_KV_HEADS):
        sl = slice(hd * HEAD_DIM, (hd + 1) * HEAD_DIM)
        k_ref[:, sl] = norm_rope(zk[:, sl], kg).astype(BF16)
    zvt = proj(_V0, KV_WIDTH).T.astype(BF16)
    pad_row = lax.broadcasted_iota(jnp.int32, (VT_ROWS - HEAD_DIM, zvt.shape[1]), 0)
    ones_pad = jnp.where(pad_row == 0, 1.0, 0.0).astype(BF16)
    for hd in range(N_KV_HEADS):
        vt_ref[0, hd * VT_ROWS:hd * VT_ROWS + HEAD_DIM, :] = zvt[hd * HEAD_DIM:(hd + 1) * HEAD_DIM]
        vt_ref[0, hd * VT_ROWS + HEAD_DIM:(hd + 1) * VT_ROWS, :] = ones_pad
    gg_ref[...] = jax.nn.gelu(proj(_GR0, D_MODEL), approximate=True).astype(BF16)
    gs_ref[...] = _sigmoid(proj(_GL0, 2 * D_MODEL)).astype(BF16)


def _inproj(xs, mod, w_in, q_gain, k_gain, tables, lay, tm):
    nt = lay.n_tokens
    cos_t, up_t, dn_t = tables

    def seq_map(i):
        return (lay.seq_and_pos(i * tm)[0], 0, 0)

    def pos_map(i):
        return (lay.seq_and_pos(i * tm)[1] // tm, 0)

    tok = lambda w: pl.BlockSpec((tm, w), lambda i: (i, 0))
    rope = pl.BlockSpec((tm, HEAD_DIM), pos_map)
    x_specs, tile_ends = lay.trunk_specs(
        tm, lambda local: pl.BlockSpec((tm, D_MODEL), lambda i: (local(i), 0)))
    return pl.pallas_call(
        functools.partial(_inproj_kernel, tile_ends=tile_ends),
        grid=(nt // tm,),
        in_specs=x_specs + [
                  pl.BlockSpec((1, N_MOD, D_MODEL), seq_map),
                  _const_spec((D_MODEL, IN_WIDTH)),
                  _const_spec((1, HEAD_DIM)), _const_spec((1, HEAD_DIM)),
                  rope, rope, rope],
        out_specs=[tok(D_MODEL), tok(KV_WIDTH),
                   pl.BlockSpec((1, N_KV_HEADS * VT_ROWS, tm), lambda i: (i, 0, 0)),
                   tok(D_MODEL), tok(D_MODEL), tok(2 * D_MODEL)],
        out_shape=[jax.ShapeDtypeStruct((nt, D_MODEL), BF16),
                   jax.ShapeDtypeStruct((nt, KV_WIDTH), BF16),
                   jax.ShapeDtypeStruct((nt // tm, N_KV_HEADS * VT_ROWS, tm), BF16),
                   jax.ShapeDtypeStruct((nt, D_MODEL), BF16),
                   jax.ShapeDtypeStruct((nt, D_MODEL), BF16),
                   jax.ShapeDtypeStruct((nt, 2 * D_MODEL), BF16)],
        compiler_params=_cparams(("arbitrary",)),
        name="in_proj",
    )(*xs, mod, w_in, q_gain.reshape(1, HEAD_DIM), k_gain.reshape(1, HEAD_DIM), cos_t, up_t, dn_t)


def _attn_kernel(q_ref, k_ref, vt_ref, o_ref, qt_all, s_all, m_all, acc_all, *,
                 tq, tk, n_kv, group, n_sub):
    for sub in range(n_sub):
        rows = slice(sub * tq, (sub + 1) * tq)
        _attn_tile(q_ref.at[rows, :], k_ref, vt_ref, o_ref.at[rows, :], qt_all.at[sub], s_all.at[sub],
                   m_all.at[sub], acc_all.at[sub], tq=tq, tk=tk, n_kv=n_kv, group=group)


def _attn_tile(q_ref, k_ref, vt_ref, o_ref, qt_scr, s_scr, m_scr, acc_scr, *, tq, tk, n_kv, group):
    for g in range(GROUPS):
        qg = q_ref[:, g * HEAD_DIM:(g + 1) * HEAD_DIM].astype(F32)
        qt_scr[:, g * tq:(g + 1) * tq] = qg.T.astype(BF16)
    m_scr[...] = jnp.full(m_scr.shape, -jnp.inf, F32)
    acc_scr[...] = jnp.zeros(acc_scr.shape, F32)

    def scores(j, slot):
        kt = k_ref[pl.ds(pl.multiple_of(j * tk, tk), tk), :]
        s_scr[slot] = jnp.dot(kt, qt_scr[...], preferred_element_type=F32)

    def accumulate(j, slot):
        s = s_scr[slot]
        m_old = m_scr[...]
        m_new = jnp.maximum(m_old, jnp.max(s, axis=0, keepdims=True))
        alpha = jnp.exp2(m_old - m_new)
        p = jnp.exp2(s - m_new).astype(BF16)
        pv = jnp.dot(vt_ref[j], p, preferred_element_type=F32)
        acc_scr[...] = alpha * acc_scr[...] + pv
        m_scr[...] = m_new

    scores(0, 0)

    def body(i, carry):
        j = group * i
        for u in range(group):
            scores(jnp.minimum(j + u + 1, n_kv - 1), (u + 1) % 2)
            accumulate(j + u, u % 2)
        return carry

    lax.fori_loop(0, n_kv // group, body, 0)
    out = acc_scr[0:HEAD_DIM, :] / acc_scr[HEAD_DIM:HEAD_DIM + 1, :]
    for g in range(GROUPS):
        o_ref[:, g * HEAD_DIM:(g + 1) * HEAD_DIM] = out[:, g * tq:(g + 1) * tq].T.astype(BF16)


def _attention(q, k, vt3, tok_off, batch, seq, tq, tk):
    n_kv = seq // tk
    group = next((g for g in (8, 4) if n_kv % g == 0 and n_kv // g >= 2), 2)
    assert n_kv % group == 0
    n_sub = 2 if (seq // tq) % 2 == 0 else 1
    tqs = n_sub * tq
    qrow0 = tok_off // tqs
    srow0 = tok_off // seq
    gw = GROUPS * HEAD_DIM
    n_q = seq // tqs
    return pl.pallas_call(
        functools.partial(_attn_kernel, tq=tq, tk=tk, n_kv=n_kv, group=group, n_sub=n_sub),
        grid=(batch, N_KV_HEADS, n_q),
        in_specs=[pl.BlockSpec((tqs, gw), lambda b, h, i: (qrow0 + b * n_q + i, h)),
                  pl.BlockSpec((seq, HEAD_DIM), lambda b, h, i: (srow0 + b, h)),
                  pl.BlockSpec((n_kv, VT_ROWS, tk), lambda b, h, i: (srow0 + b, h, 0))],
        out_specs=pl.BlockSpec((tqs, gw), lambda b, h, i: (b * n_q + i, h)),
        out_shape=jax.ShapeDtypeStruct((batch * seq, D_MODEL), BF16),
        scratch_shapes=[pltpu.VMEM((n_sub, HEAD_DIM, GROUPS * tq), BF16),
                        pltpu.VMEM((n_sub, 2, tk, GROUPS * tq), F32),
                        pltpu.VMEM((n_sub, 1, GROUPS * tq), F32),
                        pltpu.VMEM((n_sub, VT_ROWS, GROUPS * tq), F32)],
        compiler_params=_cparams(("arbitrary", "arbitrary", "arbitrary")),
        name="attention",
    )(q, k, vt3)


def _scan_kernel(cur_ref, prev_ref, next_ref, cw_ref, cb_ref, w_ref, b_ref, lam_ref, o_ref,
                 xc_scr, a_scr, u_scr, h_scr, carry_scr, *, tt, n_chunks):
    d = pl.program_id(1)
    c = pl.program_id(2)
    chunk = jnp.where(d == 0, c, n_chunks - 1 - c)

    @pl.when(c == 0)
    def _():
        carry_scr[...] = jnp.zeros(carry_scr.shape, F32)

    keep_prev = jnp.where(chunk == 0, 0.0, 1.0)
    keep_next = jnp.where(chunk == n_chunks - 1, 0.0, 1.0)
    cur = cur_ref[...].astype(F32)
    taps = [cw_ref[j:j + 1, :] for j in range(CONV_W)]
    xc_scr[...] = (cb_ref[...] + taps[2] * cur + taps[1] * pltpu.roll(cur, 1, 0)
                   + taps[0] * pltpu.roll(cur, 2, 0) + taps[3] * pltpu.roll(cur, tt - 1, 0))
    sub = V7X_SUBLANES
    head = jnp.concatenate([prev_ref[...].astype(F32)[HALO - sub:HALO] * keep_prev, cur[0:2 * sub]], axis=0)
    tail = jnp.concatenate([cur[tt - 2 * sub:tt], next_ref[...].astype(F32)[0:sub] * keep_next], axis=0)
    first = sum(taps[j] * head[sub - 2 + j:2 * sub - 2 + j] for j in range(CONV_W))
    last = sum(taps[j] * tail[sub - 2 + j:2 * sub - 2 + j] for j in range(CONV_W))
    xc_scr[0:sub, :] = cb_ref[...] + first
    xc_scr[tt - sub:tt, :] = cb_ref[...] + last
    xc = xc_scr[...]
    xcb = xc.astype(BF16)

    lam = lam_ref[0]
    y = jnp.exp(-jnp.abs(lam))
    w1p = 1.0 + y
    log1p_y = jnp.where(w1p == 1.0, y, jnp.log(w1p) * y / jnp.where(w1p == 1.0, 1.0, w1p - 1.0))
    neg_c_sp = (-LRU_C * math.log2(math.e)) * (jnp.maximum(-lam, 0.0) + log1p_y)

    for n in range(RNN_BLOCKS):
        sl = slice(n * RNN_BLOCK_W, (n + 1) * RNN_BLOCK_W)
        pre = jnp.dot(xcb[:, sl], w_ref[0, n], preferred_element_type=F32)
        r = 1.0 / (1.0 + jnp.exp2(pre[:, :RNN_BLOCK_W] + b_ref[0, 0:1, sl]))
        i = 1.0 / (1.0 + jnp.exp2(pre[:, RNN_BLOCK_W:] + b_ref[0, 1:2, sl]))
        a = jnp.exp2(r * neg_c_sp[:, sl])
        a_scr[:, sl] = a
        u_scr[:, sl] = jnp.sqrt(1.0 - a * a) * (i * xc[:, sl])

    def step(t, h):
        row = jnp.where(d == 0, t, tt - 1 - t)
        h = a_scr[pl.ds(row, 1), :] * h + u_scr[pl.ds(row, 1), :]
        h_scr[pl.ds(row, 1), :] = h
        return h

    carry_scr[...] = lax.fori_loop(0, tt, step, carry_scr[...], unroll=8)
    o_ref[0] = h_scr[...].astype(BF16)


def _scan(xr, conv_w, conv_b, w_lru, b_lru, lam, tok_off, batch, seq, tt):
    nt = xr.shape[0]
    n_chunks = seq // tt
    row0 = tok_off // tt
    hrow0 = tok_off // HALO
    hpc = tt // HALO
    n_halo = nt // HALO

    def chunk_of(d, c):
        return jnp.where(d == 0, c, n_chunks - 1 - c)

    def prev_map(b, d, c):
        return (jnp.maximum(hrow0 + (b * n_chunks + chunk_of(d, c)) * hpc - 1, 0), 0)

    def next_map(b, d, c):
        return (jnp.minimum(hrow0 + (b * n_chunks + chunk_of(d, c) + 1) * hpc, n_halo - 1), 0)

    return pl.pallas_call(
        functools.partial(_scan_kernel, tt=tt, n_chunks=n_chunks),
        grid=(batch, 2, n_chunks),
        in_specs=[pl.BlockSpec((tt, D_MODEL), lambda b, d, c: (row0 + b * n_chunks + chunk_of(d, c), 0)),
                  pl.BlockSpec((HALO, D_MODEL), prev_map),
                  pl.BlockSpec((HALO, D_MODEL), next_map),
                  pl.BlockSpec((CONV_W, D_MODEL), lambda b, d, c: (0, 0)),
                  pl.BlockSpec((1, D_MODEL), lambda b, d, c: (0, 0)),
                  pl.BlockSpec((1, RNN_BLOCKS, RNN_BLOCK_W, 2 * RNN_BLOCK_W), lambda b, d, c: (d, 0, 0, 0)),
                  pl.BlockSpec((1, 2, D_MODEL), lambda b, d, c: (d, 0, 0)),
                  pl.BlockSpec((1, 1, D_MODEL), lambda b, d, c: (d, 0, 0))],
        out_specs=pl.BlockSpec((1, tt, D_MODEL), lambda b, d, c: (d, b * n_chunks + chunk_of(d, c), 0)),
        out_shape=jax.ShapeDtypeStruct((2, batch * seq, D_MODEL), BF16),
        scratch_shapes=[pltpu.VMEM((tt, D_MODEL), F32),
                        pltpu.VMEM((tt, D_MODEL), F32),
                        pltpu.VMEM((tt, D_MODEL), F32),
                        pltpu.VMEM((tt, D_MODEL), F32),
                        pltpu.VMEM((1, D_MODEL), F32)],
        compiler_params=_cparams(("arbitrary", "arbitrary", "arbitrary")),
        name="lru_scan",
    )(xr, xr, xr, conv_w, conv_b, w_lru, b_lru, lam)


def _mid_kernel(*refs, tile_ends):
    n_tr = len(tile_ends)
    oa_refs = refs[:n_tr]
    h_refs = refs[n_tr:3 * n_tr]
    x_refs = refs[3 * n_tr:4 * n_tr]
    (gg_ref, gs_ref, mod_ref, wpa_ref, wpr_ref, wo_ref, g1_ref, b1_ref, wr_ref, br_ref,
     x1_ref, h2_ref, route_ref, wt_ref, cnt_ref, tri_scr, run_scr) = refs[4 * n_tr:]
    i = pl.program_id(0)
    oa = _owner_value([lambda r=r: r[...] for r in oa_refs], i, tile_ends)
    h_fwd = _owner_value([lambda r=h_refs[2 * j]: r[0] for j in range(n_tr)], i, tile_ends)
    h_bwd = _owner_value([lambda r=h_refs[2 * j + 1]: r[0] for j in range(n_tr)], i, tile_ends)
    hsum = h_fwd.astype(F32) + h_bwd.astype(F32)
    x_in = _owner_value([lambda r=r: r[...] for r in x_refs], i, tile_ends)
    o_att = jnp.dot(oa, wpa_ref[...], preferred_element_type=F32)
    rec = hsum.astype(BF16) * gg_ref[...]
    o_rec = jnp.dot(rec, wpr_ref[...], preferred_element_type=F32)
    gs = gs_ref[...].astype(F32)
    merged = gs[:, :D_MODEL] * o_att + gs[:, D_MODEL:] * o_rec
    mix = jnp.dot(merged.astype(BF16), wo_ref[...], preferred_element_type=F32)
    g1 = mod_ref[0, 2:3, :]
    sh2 = mod_ref[0, 3:4, :]
    sc2 = mod_ref[0, 4:5, :]
    x1 = _layer_norm(DN_ALPHA * x_in + (1.0 + g1) * mix) * g1_ref[...] + b1_ref[...]
    x1_ref[...] = x1
    h2 = _layer_norm(x1) * (1.0 + sc2) + sh2
    _store_row_tiles(h2_ref, h2)
    tm = h2.shape[0]
    h2_hi = h2.astype(BF16)
    h2_lo = (h2 - h2_hi.astype(F32)).astype(BF16)
    cross = jnp.dot(jnp.concatenate([h2_hi, h2_lo], axis=0), wr_ref[...], preferred_element_type=F32)
    logits = ((cross[:tm, :N_EXPERTS] + cross[:tm, N_EXPERTS:])
              + (cross[tm:, :N_EXPERTS] + cross[tm:, N_EXPERTS:])) + br_ref[...]
    lane = lax.broadcasted_iota(jnp.int32, logits.shape, 1).astype(F32)
    rem = logits
    sel = jnp.zeros(logits.shape, F32)
    picks, experts, tops = [], [], []
    for kk in range(TOP_K):
        m = jnp.max(rem, axis=-1, keepdims=True)
        idx = jnp.min(jnp.where(rem == m, lane, float(N_EXPERTS)), axis=-1, keepdims=True)
        pick = lane == idx
        sel = jnp.where(pick, 1.0, sel)
        rem = jnp.where(pick, -jnp.inf, rem)
        picks.append(pick)
        experts.append(idx)
        tops.append(m)
    gates = [jnp.ones_like(tops[0])] + [jnp.exp(m - tops[0]) for m in tops[1:]]
    denom = sum(gates)

    @pl.when(pl.program_id(0) == 0)
    def _():
        r = lax.broadcasted_iota(jnp.int32, tri_scr.shape, 0)
        c = lax.broadcasted_iota(jnp.int32, tri_scr.shape, 1)
        tri_scr[...] = jnp.where(c < r, 1.0, 0.0).astype(BF16)
        run_scr[...] = jnp.zeros(run_scr.shape, F32)

    rank = jnp.dot(tri_scr[...], sel.astype(BF16), preferred_element_type=F32) + run_scr[...]
    run_scr[...] = run_scr[...] + jnp.sum(sel, axis=0, keepdims=True)
    cnt_ref[...] = jnp.broadcast_to(run_scr[...], cnt_ref.shape)

    out_lane = lax.broadcasted_iota(jnp.int32, route_ref.shape, 1)
    route = jnp.zeros(route_ref.shape, F32)
    wts = jnp.zeros(wt_ref.shape, F32)
    for kk in range(TOP_K):
        r_k = jnp.sum(jnp.where(picks[kk], rank, 0.0), axis=-1, keepdims=True)
        route = jnp.where(out_lane == kk, experts[kk], route)
        route = jnp.where(out_lane == TOP_K + kk, r_k, route)
        wts = jnp.where(out_lane == kk, gates[kk] / denom, wts)
    route_ref[...] = route.astype(jnp.int32)
    wt_ref[...] = wts


def _mid(o_atts, hfbs, gg, gs, xs, mod, w_pa, w_pr, w_out, ln1_g, ln1_b, w_router, b_router, lay, tm):
    nt = lay.n_tokens

    def seq_map(i):
        return (lay.seq_and_pos(i * tm)[0], 0, 0)

    tok = lambda w: pl.BlockSpec((tm, w), lambda i: (i, 0))
    vec = lambda w: _const_spec((1, w))
    row_specs, tile_ends = lay.trunk_specs(
        tm, lambda local: pl.BlockSpec((tm, D_MODEL), lambda i: (local(i), 0)))
    h_pairs, _ = lay.trunk_specs(
        tm, lambda local: [pl.BlockSpec((1, tm, D_MODEL), lambda i, d=d: (d, local(i), 0)) for d in range(2)])
    h_specs = [spec for pair in h_pairs for spec in pair]
    h_args = [hfb for hfb in hfbs for _ in range(2)]
    return pl.pallas_call(
        functools.partial(_mid_kernel, tile_ends=tile_ends),
        grid=(nt // tm,),
        in_specs=row_specs + h_specs + row_specs + [
                  tok(D_MODEL), tok(2 * D_MODEL),
                  pl.BlockSpec((1, N_MOD, D_MODEL), seq_map),
                  _const_spec((D_MODEL, D_MODEL)), _const_spec((D_MODEL, D_MODEL)),
                  _const_spec((D_MODEL, D_MODEL)),
                  vec(D_MODEL), vec(D_MODEL),
                  _const_spec((D_MODEL, 2 * N_EXPERTS)), vec(N_EXPERTS)],
        out_specs=[tok(D_MODEL), pl.BlockSpec((ROW_SUB * tm, V7X_LANES), lambda i: (i, 0)),
                   tok(V7X_LANES), tok(V7X_LANES),
                   pl.BlockSpec((V7X_SUBLANES, N_EXPERTS), lambda i: (0, 0))],
        out_shape=[jax.ShapeDtypeStruct((nt, D_MODEL), F32),
                   jax.ShapeDtypeStruct((ROW_SUB * nt, V7X_LANES), F32),
                   jax.ShapeDtypeStruct((nt, V7X_LANES), jnp.int32),
                   jax.ShapeDtypeStruct((nt, V7X_LANES), F32),
                   jax.ShapeDtypeStruct((V7X_SUBLANES, N_EXPERTS), F32)],
        scratch_shapes=[pltpu.VMEM((tm, tm), BF16), pltpu.VMEM((1, N_EXPERTS), F32)],
        compiler_params=_cparams(("arbitrary",)),
        name="merge_router",
    )(*o_atts, *h_args, *xs, gg, gs, mod, w_pa, w_pr, w_out, ln1_g.reshape(1, -1), ln1_b.reshape(1, -1),
      w_router, b_router.reshape(1, -1))


def _row_view(ref, row):
    return ref.at[pl.ds(pl.multiple_of(row * ROW_SUB, ROW_SUB), ROW_SUB), :]


def _dispatch_kernel(slot_ref, tail_ref, h_ref, xs_ref, zero_scr, sem, zsem, *, tg, tb):
    @pl.when(pl.program_id(0) == 0)
    def _():
        zero_scr[...] = jnp.zeros(zero_scr.shape, F32)
        for e in range(N_EXPERTS):
            tail = pl.multiple_of(tail_ref[e] * ROW_SUB, tb * ROW_SUB)
            pltpu.make_async_copy(zero_scr, xs_ref.at[pl.ds(tail, tb * ROW_SUB), :], zsem).start()
        for e in range(N_EXPERTS):
            pltpu.make_async_copy(zero_scr, xs_ref.at[pl.ds(0, tb * ROW_SUB), :], zsem).wait()

    def issue(t, carry):
        for kk in range(TOP_K):
            pltpu.make_async_copy(_row_view(h_ref, t), _row_view(xs_ref, slot_ref[t * TOP_K + kk]),
                                  sem).start(priority=kk % 2)
        return carry

    lax.fori_loop(0, tg, issue, 0, unroll=2)

    for kk in range(TOP_K):
        pltpu.make_async_copy(h_ref, xs_ref.at[pl.ds(0, tg * ROW_SUB), :], sem).wait()


def _dispatch(slots_flat, tail_start, h2r, cap, tg, tb):
    nt = h2r.shape[0] // ROW_SUB
    return pl.pallas_call(
        functools.partial(_dispatch_kernel, tg=tg, tb=tb),
        grid=(nt // tg,),
        in_specs=[pl.BlockSpec((tg * TOP_K,), lambda i: (i,), memory_space=pltpu.SMEM),
                  pl.BlockSpec(memory_space=pltpu.SMEM),
                  pl.BlockSpec((tg * ROW_SUB, V7X_LANES), lambda i: (i, 0))],
        out_specs=pl.BlockSpec(memory_space=pl.ANY),
        out_shape=jax.ShapeDtypeStruct((cap * ROW_SUB, V7X_LANES), F32),
        scratch_shapes=[pltpu.VMEM((tb * ROW_SUB, V7X_LANES), F32), pltpu.SemaphoreType.DMA(()),
                        pltpu.SemaphoreType.DMA(())],
        compiler_params=_cparams(("arbitrary",)),
        name="moe_dispatch",
    )(slots_flat, tail_start, h2r)


def _expert_kernel(be_ref, bv_ref, xs_ref, w1_ref, b1_ref, w2_ref, b2_ref, ys_ref, w1_scr, w2_scr, *, tb):
    i = pl.program_id(0)
    valid = bv_ref[i]

    @pl.when((i == 0) | (be_ref[i] != be_ref[jnp.maximum(i - 1, 0)]))
    def _():
        w1_scr[...] = w1_ref[0].astype(BF16)
        w2_scr[...] = w2_ref[0].astype(BF16)

    @pl.when(valid > 0)
    def _():
        xb = _load_row_tiles(xs_ref, tb).astype(BF16)
        gu = jnp.dot(xb, w1_scr[...], preferred_element_type=F32) + b1_ref[0]
        glu = jnp.minimum(gu[:, :D_FF], SWIGLU_LIMIT)
        lin = jnp.clip(gu[:, D_FF:], -SWIGLU_LIMIT, SWIGLU_LIMIT)
        act = (lin + 1.0) * glu * _sigmoid(SWIGLU_ALPHA * glu)
        y = jnp.dot(act.astype(BF16), w2_scr[...], preferred_element_type=F32) + b2_ref[0]
        _store_row_tiles(ys_ref, y)

    @pl.when(valid <= 0)
    def _():
        ys_ref[...] = jnp.zeros(ys_ref.shape, F32)


def _experts(block_expert, block_valid, xs, w1, b1, w2, b2, tb):
    cap = xs.shape[0] // ROW_SUB
    rows = pl.BlockSpec((tb * ROW_SUB, V7X_LANES), lambda i, be, bv: (i, 0))
    grid_spec = pltpu.PrefetchScalarGridSpec(
        num_scalar_prefetch=2,
        grid=(cap // tb,),
        in_specs=[rows,
                  pl.BlockSpec((1, D_MODEL, 2 * D_FF), lambda i, be, bv: (be[i], 0, 0)),
                  pl.BlockSpec((1, 1, 2 * D_FF), lambda i, be, bv: (be[i], 0, 0)),
                  pl.BlockSpec((1, D_FF, D_MODEL), lambda i, be, bv: (be[i], 0, 0)),
                  pl.BlockSpec((1, 1, D_MODEL), lambda i, be, bv: (be[i], 0, 0))],
        out_specs=rows,
        scratch_shapes=[pltpu.VMEM((D_MODEL, 2 * D_FF), BF16), pltpu.VMEM((D_FF, D_MODEL), BF16)],
    )
    return pl.pallas_call(
        functools.partial(_expert_kernel, tb=tb),
        grid_spec=grid_spec,
        out_shape=jax.ShapeDtypeStruct((cap * ROW_SUB, V7X_LANES), F32),
        compiler_params=_cparams(("arbitrary",)),
        name="moe_experts",
    )(block_expert, block_valid, xs, w1, b1.reshape(N_EXPERTS, 1, -1), w2, b2.reshape(N_EXPERTS, 1, -1))


def _combine_kernel(slot_ref, next_slot_ref, ys_ref, wt_ref, x1_ref, mod_ref, g_ref, b_ref, *rest,
                    tc, tile_ends):
    y_refs = rest[:len(tile_ends)]
    buf, sem = rest[len(tile_ends):]
    i = pl.program_id(0)
    cur = i % 2

    def row_copy(slots, t, kk, half):
        return pltpu.make_async_copy(_row_view(ys_ref, slots[t * TOP_K + kk]),
                                     _row_view(buf.at[half, kk], t), sem.at[half])

    def gather(slots, half):
        def issue(t, carry):
            for kk in range(TOP_K):
                row_copy(slots, t, kk, half).start(priority=kk % 2)
            return carry
        lax.fori_loop(0, tc, issue, 0, unroll=8)

    @pl.when(i == 0)
    def _():
        gather(slot_ref, 0)

    @pl.when(i + 1 < pl.num_programs(0))
    def _():
        gather(next_slot_ref, 1 - cur)

    for kk in range(TOP_K):
        pltpu.make_async_copy(ys_ref.at[pl.ds(0, tc * ROW_SUB), :], buf.at[cur, kk], sem.at[cur]).wait()
    wt = wt_ref[...]
    ff = sum(wt[:, kk:kk + 1] * _load_row_tiles(buf.at[cur, kk], tc) for kk in range(TOP_K))
    g2 = mod_ref[0, 5:6, :]
    y = _layer_norm(DN_ALPHA * x1_ref[...] + (1.0 + g2) * ff) * g_ref[...] + b_ref[...]
    i = pl.program_id(0)
    start = 0
    for y_ref, end in zip(y_refs, tile_ends):
        @pl.when((i >= start) & (i < end))
        def _(y_ref=y_ref):
            y_ref[...] = y
        start = end


def _combine(slots_flat, ys, wts, x1, mod, ln2_g, ln2_b, lay, tc):
    nt = lay.n_tokens

    def seq_map(i):
        return (lay.seq_and_pos(i * tc)[0], 0, 0)

    out_specs, tile_ends = lay.trunk_specs(
        tc, lambda local: pl.BlockSpec((tc, D_MODEL), lambda i: (local(i), 0)))
    n_steps = nt // tc
    return pl.pallas_call(
        functools.partial(_combine_kernel, tc=tc, tile_ends=tile_ends),
        grid=(n_steps,),
        in_specs=[pl.BlockSpec((tc * TOP_K,), lambda i: (i,), memory_space=pltpu.SMEM),
                  pl.BlockSpec((tc * TOP_K,), lambda i: (jnp.minimum(i + 1, n_steps - 1),),
                               memory_space=pltpu.SMEM),
                  pl.BlockSpec(memory_space=pl.ANY),
                  pl.BlockSpec((tc, V7X_LANES), lambda i: (i, 0)),
                  pl.BlockSpec((tc, D_MODEL), lambda i: (i, 0)),
                  pl.BlockSpec((1, N_MOD, D_MODEL), seq_map),
                  _const_spec((1, D_MODEL)), _const_spec((1, D_MODEL))],
        out_specs=out_specs,
        out_shape=[jax.ShapeDtypeStruct((b * s, D_MODEL), F32) for b, s in lay.trunks],
        scratch_shapes=[pltpu.VMEM((2, TOP_K, tc * ROW_SUB, V7X_LANES), F32), pltpu.SemaphoreType.DMA((2,))],
        compiler_params=_cparams(("arbitrary",)),
        name="moe_combine",
    )(slots_flat, slots_flat, ys, wts, x1, mod, ln2_g.reshape(1, -1), ln2_b.reshape(1, -1))


def _encoder_layer(xs, cs, p):
    trunks = [(x.shape[0], x.shape[1]) for x in xs]
    lay = _Layout(trunks)
    nt = lay.n_tokens
    t = _tiles(nt)

    x2d = [xi.reshape(-1, D_MODEL) for xi in xs]
    c = jnp.concatenate(cs, axis=0)
    bp = -(-lay.n_seqs // 8) * 8
    c = jnp.pad(c, ((0, bp - lay.n_seqs), (0, 0)))
    mod = _ada_mod(c, p["w_ada"], p["b_ada"]).reshape(bp, N_MOD, D_MODEL)

    q, k, vt3, xr, gg, gs = _inproj(x2d, mod, p["w_in"].astype(BF16), p["q_gain"], p["k_gain"],
                                    _rope_tables(lay.max_seq), lay, t["tm_in"])

    gate_scale = -math.log2(math.e)
    w_lru = (gate_scale * jnp.concatenate([p["lru_wa"], p["lru_wx"]], axis=-1)).astype(BF16)
    b_lru = gate_scale * jnp.stack([p["lru_ba"], p["lru_bx"]], axis=1)
    lam = p["lru_lam"].reshape(2, 1, D_MODEL)
    o_atts, hfbs = [], []
    tok_off = 0
    for b, s in trunks:
        o_atts.append(_attention(q, k, vt3, tok_off, b, s, t["tq"], t["tm_in"]))
        hfbs.append(_scan(xr, p["conv_w"], p["conv_b"].reshape(1, -1), w_lru, b_lru, lam,
                          tok_off, b, s, t["tt"]))
        tok_off += b * s

    wr_hi = p["w_router"].astype(BF16)
    wr_lo = (p["w_router"] - wr_hi.astype(F32)).astype(BF16)
    x1, h2r, route, wts, cnt = _mid(o_atts, hfbs, gg, gs, x2d, mod, p["w_pa"].astype(BF16),
                                    p["w_pr"].astype(BF16), p["w_out"].astype(BF16), p["ln1_g"], p["ln1_b"],
                                    jnp.concatenate([wr_hi, wr_lo], axis=1), p["b_router"], lay, t["tm_mid"])

    tb = t["tb"]
    counts = cnt[0].astype(jnp.int32)
    padded = (counts + tb - 1) // tb * tb
    pad_end = jnp.cumsum(padded)
    pad_start = pad_end - padded
    n_blocks = nt * TOP_K // tb + N_EXPERTS
    blk0 = jnp.arange(n_blocks, dtype=jnp.int32) * tb
    block_expert = jnp.minimum(jnp.sum(pad_end[None, :] <= blk0[:, None], axis=1), N_EXPERTS - 1).astype(jnp.int32)
    block_valid = jnp.clip(pad_start[block_expert] + counts[block_expert] - blk0, 0, tb).astype(jnp.int32)
    expert = route[:, 0:TOP_K]
    start_of = jnp.sum(jnp.where(expert[..., None] == jnp.arange(N_EXPERTS, dtype=jnp.int32),
                                 pad_start.astype(jnp.int32), 0), axis=-1)
    slots_flat = (start_of + route[:, TOP_K:2 * TOP_K]).reshape(-1)

    tail_start = jnp.where(padded > 0, pad_end - tb, pad_end[-1] - tb).astype(jnp.int32)
    xs_rows = _dispatch(slots_flat, tail_start, h2r, n_blocks * tb, t["tg"], tb)
    ys = _experts(block_expert, block_valid, xs_rows, p["w1"], p["b1"], p["w2"], p["b2"], tb)
    ys_out = _combine(slots_flat, ys, wts, x1, mod, p["ln2_g"], p["ln2_b"], lay, t["tc"])
    return [y.reshape(b, s, D_MODEL) for y, (b, s) in zip(ys_out, trunks)]


_PARAM_NAMES = ("w_ada", "b_ada", "w_in", "q_gain", "k_gain", "conv_w", "conv_b", "lru_wa", "lru_ba",
                "lru_wx", "lru_bx", "lru_lam", "w_pa", "w_pr", "w_out", "ln1_g", "ln1_b", "w_router",
                "b_router", "w1", "b1", "w2", "b2", "ln2_g", "ln2_b")


def kernel(x_prompt, x_sample, c_prompt, c_sample, w_ada, b_ada, w_in, q_gain, k_gain, conv_w, conv_b, lru_wa, lru_ba, lru_wx, lru_bx, lru_lam, w_pa, w_pr, w_out, ln1_g, ln1_b, w_router, b_router, w1, b1, w2, b2, ln2_g, ln2_b):
    stacked = (w_ada, b_ada, w_in, q_gain, k_gain, conv_w, conv_b, lru_wa, lru_ba, lru_wx, lru_bx,
               lru_lam, w_pa, w_pr, w_out, ln1_g, ln1_b, w_router, b_router, w1, b1, w2, b2, ln2_g, ln2_b)
    xs, cs = [x_prompt, x_sample], [c_prompt, c_sample]
    for layer in range(DEPTH):
        p = {name: arr[layer] for name, arr in zip(_PARAM_NAMES, stacked)}
        xs = _encoder_layer(xs, cs, p)
    return (xs[0], xs[1])
```

```python
import functools
import math

import jax
import jax.numpy as jnp
from jax import lax
from jax.experimental import pallas as pl
from jax.experimental.pallas import tpu as pltpu

F32 = jnp.float32
BF16 = jnp.bfloat16

D_MODEL = 1024
GRID_W = 64
N_HEADS = 8
N_KV_HEADS = 2
HEAD_DIM = 128
GROUPS = N_HEADS // N_KV_HEADS
KV_WIDTH = N_KV_HEADS * HEAD_DIM
ROPE_THETA = 10000.0
RNN_BLOCKS = 8
RNN_BLOCK_W = D_MODEL // RNN_BLOCKS
CONV_W = 4
LRU_C = 8.0
N_EXPERTS = 32
TOP_K = 4
D_FF = D_MODEL
SWIGLU_LIMIT = 7.0
SWIGLU_ALPHA = 1.702
DEPTH = 1
DN_ALPHA = (2 * DEPTH) ** 0.25
LN_EPS = 1e-5
RMS_EPS = 1e-6
N_MOD = 6
_Q0 = 0
_K0 = _Q0 + D_MODEL
_V0 = _K0 + KV_WIDTH
_XR0 = _V0 + KV_WIDTH
_GR0 = _XR0 + D_MODEL
_GL0 = _GR0 + D_MODEL
IN_WIDTH = _GL0 + 2 * D_MODEL

V7X_LANES = 128
V7X_SUBLANES = 8
V7X_BF16_SUBLANES = 16
V7X_VMEM_LIMIT_BYTES = 56 * 1024 * 1024
HALO = V7X_BF16_SUBLANES
VT_ROWS = HEAD_DIM + V7X_BF16_SUBLANES


def _tiles(n_tokens):
    big = n_tokens >= 4096
    return dict(
        tm_in=512 if big else 128,
        tq=256 if big else 128,
        tt=1024 if big else 128,
        tm_mid=512 if big else 128,
        tg=1024 if big else 256,
        tb=512 if big else 128,
        tc=512 if big else 256,
    )


def _cparams(sem):
    return pltpu.CompilerParams(dimension_semantics=sem, vmem_limit_bytes=V7X_VMEM_LIMIT_BYTES)


def _const_spec(shape):
    nd = len(shape)
    return pl.BlockSpec(shape, lambda *_: (0,) * nd, pipeline_mode=pl.Buffered(1))


def _layer_norm(x):
    mu = jnp.mean(x, axis=-1, keepdims=True)
    xc = x - mu
    var = jnp.mean(xc * xc, axis=-1, keepdims=True)
    return xc * lax.rsqrt(var + LN_EPS)


def _sigmoid(x):
    return 1.0 / (1.0 + jnp.exp(-x))


ROW_SUB = D_MODEL // V7X_LANES
assert ROW_SUB == V7X_SUBLANES, "one token row must fill exactly one f32 tile"


def _store_row_tiles(ref, val):
    n = val.shape[0]
    for j in range(ROW_SUB):
        ref[pl.ds(j, n, stride=ROW_SUB), :] = val[:, j * V7X_LANES:(j + 1) * V7X_LANES]


def _load_row_tiles(ref, n):
    return jnp.concatenate([ref[pl.ds(j, n, stride=ROW_SUB), :] for j in range(ROW_SUB)], axis=1)


def _ada_kernel(c_ref, w_ref, b_ref, o_ref):
    c = c_ref[...]
    s = c * _sigmoid(c)
    o_ref[...] = jnp.dot(s, w_ref[...], preferred_element_type=F32,
                         precision=lax.Precision.HIGHEST) + b_ref[...]


def _ada_mod(c_all, w_ada, b_ada):
    bp = c_all.shape[0]
    ncol = w_ada.shape[1]
    return pl.pallas_call(
        _ada_kernel,
        grid=(ncol // D_MODEL,),
        in_specs=[pl.BlockSpec((bp, D_MODEL), lambda j: (0, 0)),
                  pl.BlockSpec((D_MODEL, D_MODEL), lambda j: (0, j)),
                  pl.BlockSpec((1, D_MODEL), lambda j: (0, j))],
        out_specs=pl.BlockSpec((bp, D_MODEL), lambda j: (0, j)),
        out_shape=jax.ShapeDtypeStruct((bp, ncol), F32),
        compiler_params=_cparams(("arbitrary",)),
        name="ada_mod",
    )(c_all, w_ada, b_ada.reshape(1, ncol))


class _Layout:
    def __init__(self, trunks):
        self.trunks = tuple(trunks)
        self.n_tokens = sum(b * s for b, s in trunks)
        self.n_seqs = sum(b for b, _ in trunks)
        self.max_seq = max(s for _, s in trunks)

    def seq_and_pos(self, t0):
        seq = jnp.int32(0)
        pos = jnp.int32(0)
        tok_off, seq_off = 0, 0
        for b, s in self.trunks:
            inside = (t0 >= tok_off) & (t0 < tok_off + b * s)
            rel = jnp.maximum(t0 - tok_off, 0)
            seq = jnp.where(inside, seq_off + rel // s, seq)
            pos = jnp.where(inside, rel % s, pos)
            tok_off += b * s
            seq_off += b
        return seq, pos

    def trunk_specs(self, tm, make_spec):
        specs, ends, t0 = [], [], 0
        for b, s in self.trunks:
            n_t = b * s // tm
            specs.append(make_spec(functools.partial(_clamped_local, t0=t0, n_t=n_t)))
            t0 += n_t
            ends.append(t0)
        return specs, tuple(ends)


def _clamped_local(i, *, t0, n_t):
    return jnp.clip(i - t0, 0, n_t - 1)


def _owner_value(loads, i, tile_ends):
    val = loads[-1]()
    for j in range(len(loads) - 2, -1, -1):
        val = jnp.where(i < tile_ends[j], loads[j](), val)
    return val


def _rope_tables(max_seq):
    n_rows = max_seq // GRID_W
    axis_dim = HEAD_DIM // 2
    inv = ROPE_THETA ** (-jnp.arange(0, axis_dim, 2, dtype=F32) / axis_dim)
    ar = jnp.arange(n_rows, dtype=F32)[:, None] * inv
    ac = jnp.arange(GRID_W, dtype=F32)[:, None] * inv
    cr, sr = (jnp.repeat(f(ar), GRID_W, axis=0) for f in (jnp.cos, jnp.sin))
    cc, sc = (jnp.tile(f(ac), (n_rows, 1)) for f in (jnp.cos, jnp.sin))
    z = jnp.zeros_like(sr)
    cos_t = jnp.concatenate([cr, cr, cc, cc], axis=-1)
    up_t = jnp.concatenate([-sr, z, -sc, z], axis=-1)
    dn_t = jnp.concatenate([z, sr, z, sc], axis=-1)
    return cos_t, up_t, dn_t


def _inproj_kernel(*refs, tile_ends):
    n_tr = len(tile_ends)
    x_refs = refs[:n_tr]
    (mod_ref, w_ref, qg_ref, kg_ref, cos_ref, up_ref, dn_ref,
     q_ref, k_ref, vt_ref, xr_ref, gg_ref, gs_ref) = refs[n_tr:]
    x = _owner_value([lambda r=r: r[...] for r in x_refs], pl.program_id(0), tile_ends)
    sh1 = mod_ref[0, 0:1, :]
    sc1 = mod_ref[0, 1:2, :]
    h = (_layer_norm(x) * (1.0 + sc1) + sh1).astype(BF16)
    cos_t, up_t, dn_t = cos_ref[...], up_ref[...], dn_ref[...]

    def proj(c0, width):
        return jnp.dot(h, w_ref[:, c0:c0 + width], preferred_element_type=F32)

    def norm_rope(z, gain):
        ms = jnp.mean(z * z, axis=-1, keepdims=True)
        y = z * lax.rsqrt(ms + RMS_EPS) * gain
        return (y * cos_t + pltpu.roll(y, HEAD_DIM - 32, 1) * up_t
                + pltpu.roll(y, 32, 1) * dn_t)

    xr_ref[...] = proj(_XR0, D_MODEL).astype(BF16)
    zq = proj(_Q0, D_MODEL)
    qg = qg_ref[...] * (HEAD_DIM ** -0.5 * math.log2(math.e))
    for hd in range(N_HEADS):
        sl = slice(hd * HEAD_DIM, (hd + 1) * HEAD_DIM)
        q_ref[:, sl] = norm_rope(zq[:, sl], qg).astype(BF16)
    zk = proj(_K0, KV_WIDTH)
    kg = kg_ref[...]
    for hd in range(N_KV_HEADS):
        sl = slice(hd * HEAD_DIM, (hd + 1) * HEAD_DIM)
        k_ref[:, sl] = norm_rope(zk[:, sl], kg).astype(BF16)
    zvt = proj(_V0, KV_WIDTH).T.astype(BF16)
    pad_row = lax.broadcasted_iota(jnp.int32, (VT_ROWS - HEAD_DIM, zvt.shape[1]), 0)
    ones_pad = jnp.where(pad_row == 0, 1.0, 0.0).astype(BF16)
    for hd in range(N_KV_HEADS):
        vt_ref[0, hd * VT_ROWS:hd * VT_ROWS + HEAD_DIM, :] = zvt[hd * HEAD_DIM:(hd + 1) * HEAD_DIM]
        vt_ref[0, hd * VT_ROWS + HEAD_DIM:(hd + 1) * VT_ROWS, :] = ones_pad
    gg_ref[...] = jax.nn.gelu(proj(_GR0, D_MODEL), approximate=True).astype(BF16)
    gs_ref[...] = _sigmoid(proj(_GL0, 2 * D_MODEL)).astype(BF16)


def _inproj(xs, mod, w_in, q_gain, k_gain, tables, lay, tm):
    nt = lay.n_tokens
    cos_t, up_t, dn_t = tables

    def seq_map(i):
        return (lay.seq_and_pos(i * tm)[0], 0, 0)

    def pos_map(i):
        return (lay.seq_and_pos(i * tm)[1] // tm, 0)

    tok = lambda w: pl.BlockSpec((tm, w), lambda i: (i, 0))
    rope = pl.BlockSpec((tm, HEAD_DIM), pos_map)
    x_specs, tile_ends = lay.trunk_specs(
        tm, lambda local: pl.BlockSpec((tm, D_MODEL), lambda i: (local(i), 0)))
    return pl.pallas_call(
        functools.partial(_inproj_kernel, tile_ends=tile_ends),
        grid=(nt // tm,),
        in_specs=x_specs + [
                  pl.BlockSpec((1, N_MOD, D_MODEL), seq_map),
                  _const_spec((D_MODEL, IN_WIDTH)),
                  _const_spec((1, HEAD_DIM)), _const_spec((1, HEAD_DIM)),
                  rope, rope, rope],
        out_specs=[tok(D_MODEL), tok(KV_WIDTH),
                   pl.BlockSpec((1, N_KV_HEADS * VT_ROWS, tm), lambda i: (i, 0, 0)),
                   tok(D_MODEL), tok(D_MODEL), tok(2 * D_MODEL)],
        out_shape=[jax.ShapeDtypeStruct((nt, D_MODEL), BF16),
                   jax.ShapeDtypeStruct((nt, KV_WIDTH), BF16),
                   jax.ShapeDtypeStruct((nt // tm, N_KV_HEADS * VT_ROWS, tm), BF16),
                   jax.ShapeDtypeStruct((nt, D_MODEL), BF16),
                   jax.ShapeDtypeStruct((nt, D_MODEL), BF16),
                   jax.ShapeDtypeStruct((nt, 2 * D_MODEL), BF16)],
        compiler_params=_cparams(("arbitrary",)),
        name="in_proj",
    )(*xs, mod, w_in, q_gain.reshape(1, HEAD_DIM), k_gain.reshape(1, HEAD_DIM), cos_t, up_t, dn_t)


def _attn_kernel(q_ref, k_ref, vt_ref, o_ref, qt_all, s_all, m_all, acc_all, *,
                 tq, tk, n_kv, group, n_sub):
    for sub in range(n_sub):
        rows = slice(sub * tq, (sub + 1) * tq)
        _attn_tile(q_ref.at[rows, :], k_ref, vt_ref, o_ref.at[rows, :], qt_all.at[sub], s_all.at[sub],
                   m_all.at[sub], acc_all.at[sub], tq=tq, tk=tk, n_kv=n_kv, group=group)


def _attn_tile(q_ref, k_ref, vt_ref, o_ref, qt_scr, s_scr, m_scr, acc_scr, *, tq, tk, n_kv, group):
    for g in range(GROUPS):
        qg = q_ref[:, g * HEAD_DIM:(g + 1) * HEAD_DIM].astype(F32)
        qt_scr[:, g * tq:(g + 1) * tq] = qg.T.astype(BF16)
    m_scr[...] = jnp.full(m_scr.shape, -jnp.inf, F32)
    acc_scr[...] = jnp.zeros(acc_scr.shape, F32)

    def scores(j, slot):
        kt = k_ref[pl.ds(pl.multiple_of(j * tk, tk), tk), :]
        s_scr[slot] = jnp.dot(kt, qt_scr[...], preferred_element_type=F32)

    def accumulate(j, slot):
        s = s_scr[slot]
        m_old = m_scr[...]
        m_new = jnp.maximum(m_old, jnp.max(s, axis=0, keepdims=True))
        alpha = jnp.exp2(m_old - m_new)
        p = jnp.exp2(s - m_new).astype(BF16)
        pv = jnp.dot(vt_ref[j], p, preferred_element_type=F32)
        acc_scr[...] = alpha * acc_scr[...] + pv
        m_scr[...] = m_new

    scores(0, 0)

    def body(i, carry):
        j = group * i
        for u in range(group):
            scores(jnp.minimum(j + u + 1, n_kv - 1), (u + 1) % 2)
            accumulate(j + u, u % 2)
        return carry

    lax.fori_loop(0, n_kv // group, body, 0)
    out = acc_scr[0:HEAD_DIM, :] / acc_scr[HEAD_DIM:HEAD_DIM + 1, :]
    for g in range(GROUPS):
        o_ref[:, g * HEAD_DIM:(g + 1) * HEAD_DIM] = out[:, g * tq:(g + 1) * tq].T.astype(BF16)


def _attention(q, k, vt3, tok_off, batch, seq, tq, tk):
    n_kv = seq // tk
    group = next((g for g in (8, 4) if n_kv % g == 0 and n_kv // g >= 2), 2)
    assert n_kv % group == 0
    n_sub = 2 if (seq // tq) % 2 == 0 else 1
    tqs = n_sub * tq
    qrow0 = tok_off // tqs
    srow0 = tok_off // seq
    gw = GROUPS * HEAD_DIM
    n_q = seq // tqs
    return pl.pallas_call(
        functools.partial(_attn_kernel, tq=tq, tk=tk, n_kv=n_kv, group=group, n_sub=n_sub),
        grid=(batch, N_KV_HEADS, n_q),
        in_specs=[pl.BlockSpec((tqs, gw), lambda b, h, i: (qrow0 + b * n_q + i, h)),
                  pl.BlockSpec((seq, HEAD_DIM), lambda b, h, i: (srow0 + b, h)),
                  pl.BlockSpec((n_kv, VT_ROWS, tk), lambda b, h, i: (srow0 + b, h, 0))],
        out_specs=pl.BlockSpec((tqs, gw), lambda b, h, i: (b * n_q + i, h)),
        out_shape=jax.ShapeDtypeStruct((batch * seq, D_MODEL), BF16),
        scratch_shapes=[pltpu.VMEM((n_sub, HEAD_DIM, GROUPS * tq), BF16),
                        pltpu.VMEM((n_sub, 2, tk, GROUPS * tq), F32),
                        pltpu.VMEM((n_sub, 1, GROUPS * tq), F32),
                        pltpu.VMEM((n_sub, VT_ROWS, GROUPS * tq), F32)],
        compiler_params=_cparams(("arbitrary", "arbitrary", "arbitrary")),
        name="attention",
    )(q, k, vt3)


def _scan_kernel(cur_ref, prev_ref, next_ref, cw_ref, cb_ref, w_ref, b_ref, lam_ref, o_ref,
                 xc_scr, a_scr, u_scr, h_scr, carry_scr, *, tt, n_chunks):
    d = pl.program_id(1)
    c = pl.program_id(2)
    chunk = jnp.where(d == 0, c, n_chunks - 1 - c)

    @pl.when(c == 0)
    def _():
        carry_scr[...] = jnp.zeros(carry_scr.shape, F32)

    keep_prev = jnp.where(chunk == 0, 0.0, 1.0)
    keep_next = jnp.where(chunk == n_chunks - 1, 0.0, 1.0)
    cur = cur_ref[...].astype(F32)
    taps = [cw_ref[j:j + 1, :] for j in range(CONV_W)]
    xc_scr[...] = (cb_ref[...] + taps[2] * cur + taps[1] * pltpu.roll(cur, 1, 0)
                   + taps[0] * pltpu.roll(cur, 2, 0) + taps[3] * pltpu.roll(cur, tt - 1, 0))
    sub = V7X_SUBLANES
    head = jnp.concatenate([prev_ref[...].astype(F32)[HALO - sub:HALO] * keep_prev, cur[0:2 * sub]], axis=0)
    tail = jnp.concatenate([cur[tt - 2 * sub:tt], next_ref[...].astype(F32)[0:sub] * keep_next], axis=0)
    first = sum(taps[j] * head[sub - 2 + j:2 * sub - 2 + j] for j in range(CONV_W))
    last = sum(taps[j] * tail[sub - 2 + j:2 * sub - 2 + j] for j in range(CONV_W))
    xc_scr[0:sub, :] = cb_ref[...] + first
    xc_scr[tt - sub:tt, :] = cb_ref[...] + last
    xc = xc_scr[...]
    xcb = xc.astype(BF16)

    lam = lam_ref[0]
    y = jnp.exp(-jnp.abs(lam))
    w1p = 1.0 + y
    log1p_y = jnp.where(w1p == 1.0, y, jnp.log(w1p) * y / jnp.where(w1p == 1.0, 1.0, w1p - 1.0))
    neg_c_sp = (-LRU_C * math.log2(math.e)) * (jnp.maximum(-lam, 0.0) + log1p_y)

    for n in range(RNN_BLOCKS):
        sl = slice(n * RNN_BLOCK_W, (n + 1) * RNN_BLOCK_W)
        pre = jnp.dot(xcb[:, sl], w_ref[0, n], preferred_element_type=F32)
        r = 1.0 / (1.0 + jnp.exp2(pre[:, :RNN_BLOCK_W] + b_ref[0, 0:1, sl]))
        i = 1.0 / (1.0 + jnp.exp2(pre[:, RNN_BLOCK_W:] + b_ref[0, 1:2, sl]))
        a = jnp.exp2(r * neg_c_sp[:, sl])
        a_scr[:, sl] = a
        u_scr[:, sl] = jnp.sqrt(1.0 - a * a) * (i * xc[:, sl])

    def step(t, h):
        row = jnp.where(d == 0, t, tt - 1 - t)
        h = a_scr[pl.ds(row, 1), :] * h + u_scr[pl.ds(row, 1), :]
        h_scr[pl.ds(row, 1), :] = h
        return h

    carry_scr[...] = lax.fori_loop(0, tt, step, carry_scr[...], unroll=8)
    o_ref[0] = h_scr[...].astype(BF16)


def _scan(xr, conv_w, conv_b, w_lru, b_lru, lam, tok_off, batch, seq, tt):
    nt = xr.shape[0]
    n_chunks = seq // tt
    row0 = tok_off // tt
    hrow0 = tok_off // HALO
    hpc = tt // HALO
    n_halo = nt // HALO

    def chunk_of(d, c):
        return jnp.where(d == 0, c, n_chunks - 1 - c)

    def prev_map(b, d, c):
        return (jnp.maximum(hrow0 + (b * n_chunks + chunk_of(d, c)) * hpc - 1, 0), 0)

    def next_map(b, d, c):
        return (jnp.minimum(hrow0 + (b * n_chunks + chunk_of(d, c) + 1) * hpc, n_halo - 1), 0)

    return pl.pallas_call(
        functools.partial(_scan_kernel, tt=tt, n_chunks=n_chunks),
        grid=(batch, 2, n_chunks),
        in_specs=[pl.BlockSpec((tt, D_MODEL), lambda b, d, c: (row0 + b * n_chunks + chunk_of(d, c), 0)),
                  pl.BlockSpec((HALO, D_MODEL), prev_map),
                  pl.BlockSpec((HALO, D_MODEL), next_map),
                  pl.BlockSpec((CONV_W, D_MODEL), lambda b, d, c: (0, 0)),
                  pl.BlockSpec((1, D_MODEL), lambda b, d, c: (0, 0)),
                  pl.BlockSpec((1, RNN_BLOCKS, RNN_BLOCK_W, 2 * RNN_BLOCK_W), lambda b, d, c: (d, 0, 0, 0)),
                  pl.BlockSpec((1, 2, D_MODEL), lambda b, d, c: (d, 0, 0)),
                  pl.BlockSpec((1, 1, D_MODEL), lambda b, d, c: (d, 0, 0))],
        out_specs=pl.BlockSpec((1, tt, D_MODEL), lambda b, d, c: (d, b * n_chunks + chunk_of(d, c), 0)),
        out_shape=jax.ShapeDtypeStruct((2, batch * seq, D_MODEL), BF16),
        scratch_shapes=[pltpu.VMEM((tt, D_MODEL), F32),
                        pltpu.VMEM((tt, D_MODEL), F32),
                        pltpu.VMEM((tt, D_MODEL), F32),
                        pltpu.VMEM((tt, D_MODEL), F32),
                        pltpu.VMEM((1, D_MODEL), F32)],
        compiler_params=_cparams(("arbitrary", "arbitrary", "arbitrary")),
        name="lru_scan",
    )(xr, xr, xr, conv_w, conv_b, w_lru, b_lru, lam)


def _mid_kernel(*refs, tile_ends):
    n_tr = len(tile_ends)
    oa_refs = refs[:n_tr]
    h_refs = refs[n_tr:3 * n_tr]
    x_refs = refs[3 * n_tr:4 * n_tr]
    (gg_ref, gs_ref, mod_ref, wpa_ref, wpr_ref, wo_ref, g1_ref, b1_ref, wr_ref, br_ref,
     x1_ref, h2_ref, route_ref, wt_ref, cnt_ref, tri_scr, run_scr) = refs[4 * n_tr:]
    i = pl.program_id(0)
    oa = _owner_value([lambda r=r: r[...] for r in oa_refs], i, tile_ends)
    h_fwd = _owner_value([lambda r=h_refs[2 * j]: r[0] for j in range(n_tr)], i, tile_ends)
    h_bwd = _owner_value([lambda r=h_refs[2 * j + 1]: r[0] for j in range(n_tr)], i, tile_ends)
    hsum = h_fwd.astype(F32) + h_bwd.astype(F32)
    x_in = _owner_value([lambda r=r: r[...] for r in x_refs], i, tile_ends)
    o_att = jnp.dot(oa, wpa_ref[...], preferred_element_type=F32)
    rec = hsum.astype(BF16) * gg_ref[...]
    o_rec = jnp.dot(rec, wpr_ref[...], preferred_element_type=F32)
    gs = gs_ref[...].astype(F32)
    merged = gs[:, :D_MODEL] * o_att + gs[:, D_MODEL:] * o_rec
    mix = jnp.dot(merged.astype(BF16), wo_ref[...], preferred_element_type=F32)
    g1 = mod_ref[0, 2:3, :]
    sh2 = mod_ref[0, 3:4, :]
    sc2 = mod_ref[0, 4:5, :]
    x1 = _layer_norm(DN_ALPHA * x_in + (1.0 + g1) * mix) * g1_ref[...] + b1_ref[...]
    x1_ref[...] = x1
    h2 = _layer_norm(x1) * (1.0 + sc2) + sh2
    _store_row_tiles(h2_ref, h2)
    tm = h2.shape[0]
    h2_hi = h2.astype(BF16)
    h2_lo = (h2 - h2_hi.astype(F32)).astype(BF16)
    cross = jnp.dot(jnp.concatenate([h2_hi, h2_lo], axis=0), wr_ref[...], preferred_element_type=F32)
    logits = ((cross[:tm, :N_EXPERTS] + cross[:tm, N_EXPERTS:])
              + (cross[tm:, :N_EXPERTS] + cross[tm:, N_EXPERTS:])) + br_ref[...]
    lane = lax.broadcasted_iota(jnp.int32, logits.shape, 1).astype(F32)
    rem = logits
    sel = jnp.zeros(logits.shape, F32)
    picks, experts, tops = [], [], []
    for kk in range(TOP_K):
        m = jnp.max(rem, axis=-1, keepdims=True)
        idx = jnp.min(jnp.where(rem == m, lane, float(N_EXPERTS)), axis=-1, keepdims=True)
        pick = lane == idx
        sel = jnp.where(pick, 1.0, sel)
        rem = jnp.where(pick, -jnp.inf, rem)
        picks.append(pick)
        experts.append(idx)
        tops.append(m)
    gates = [jnp.ones_like(tops[0])] + [jnp.exp(m - tops[0]) for m in tops[1:]]
    denom = sum(gates)

    @pl.when(pl.program_id(0) == 0)
    def _():
        r = lax.broadcasted_iota(jnp.int32, tri_scr.shape, 0)
        c = lax.broadcasted_iota(jnp.int32, tri_scr.shape, 1)
        tri_scr[...] = jnp.where(c < r, 1.0, 0.0).astype(BF16)
        run_scr[...] = jnp.zeros(run_scr.shape, F32)

    rank = jnp.dot(tri_scr[...], sel.astype(BF16), preferred_element_type=F32) + run_scr[...]
    run_scr[...] = run_scr[...] + jnp.sum(sel, axis=0, keepdims=True)
    cnt_ref[...] = jnp.broadcast_to(run_scr[...], cnt_ref.shape)

    out_lane = lax.broadcasted_iota(jnp.int32, route_ref.shape, 1)
    route = jnp.zeros(route_ref.shape, F32)
    wts = jnp.zeros(wt_ref.shape, F32)
    for kk in range(TOP_K):
        r_k = jnp.sum(jnp.where(picks[kk], rank, 0.0), axis=-1, keepdims=True)
        route = jnp.where(out_lane == kk, experts[kk], route)
        route = jnp.where(out_lane == TOP_K + kk, r_k, route)
        wts = jnp.where(out_lane == kk, gates[kk] / denom, wts)
    route_ref[...] = route.astype(jnp.int32)
    wt_ref[...] = wts


def _mid(o_atts, hfbs, gg, gs, xs, mod, w_pa, w_pr, w_out, ln1_g, ln1_b, w_router, b_router, lay, tm):
    nt = lay.n_tokens

    def seq_map(i):
        return (lay.seq_and_pos(i * tm)[0], 0, 0)

    tok = lambda w: pl.BlockSpec((tm, w), lambda i: (i, 0))
    vec = lambda w: _const_spec((1, w))
    row_specs, tile_ends = lay.trunk_specs(
        tm, lambda local: pl.BlockSpec((tm, D_MODEL), lambda i: (local(i), 0)))
    h_pairs, _ = lay.trunk_specs(
        tm, lambda local: [pl.BlockSpec((1, tm, D_MODEL), lambda i, d=d: (d, local(i), 0)) for d in range(2)])
    h_specs = [spec for pair in h_pairs for spec in pair]
    h_args = [hfb for hfb in hfbs for _ in range(2)]
    return pl.pallas_call(
        functools.partial(_mid_kernel, tile_ends=tile_ends),
        grid=(nt // tm,),
        in_specs=row_specs + h_specs + row_specs + [
                  tok(D_MODEL), tok(2 * D_MODEL),
                  pl.BlockSpec((1, N_MOD, D_MODEL), seq_map),
                  _const_spec((D_MODEL, D_MODEL)), _const_spec((D_MODEL, D_MODEL)),
                  _const_spec((D_MODEL, D_MODEL)),
                  vec(D_MODEL), vec(D_MODEL),
                  _const_spec((D_MODEL, 2 * N_EXPERTS)), vec(N_EXPERTS)],
        out_specs=[tok(D_MODEL), pl.BlockSpec((ROW_SUB * tm, V7X_LANES), lambda i: (i, 0)),
                   tok(V7X_LANES), tok(V7X_LANES),
                   pl.BlockSpec((V7X_SUBLANES, N_EXPERTS), lambda i: (0, 0))],
        out_shape=[jax.ShapeDtypeStruct((nt, D_MODEL), F32),
                   jax.ShapeDtypeStruct((ROW_SUB * nt, V7X_LANES), F32),
                   jax.ShapeDtypeStruct((nt, V7X_LANES), jnp.int32),
                   jax.ShapeDtypeStruct((nt, V7X_LANES), F32),
                   jax.ShapeDtypeStruct((V7X_SUBLANES, N_EXPERTS), F32)],
        scratch_shapes=[pltpu.VMEM((tm, tm), BF16), pltpu.VMEM((1, N_EXPERTS), F32)],
        compiler_params=_cparams(("arbitrary",)),
        name="merge_router",
    )(*o_atts, *h_args, *xs, gg, gs, mod, w_pa, w_pr, w_out, ln1_g.reshape(1, -1), ln1_b.reshape(1, -1),
      w_router, b_router.reshape(1, -1))


def _slot_view(ref, first_row):
    return ref.at[pl.ds(pl.multiple_of(first_row, ROW_SUB), ROW_SUB), :]


def _row_view(ref, row):
    return ref.at[pl.ds(pl.multiple_of(row * ROW_SUB, ROW_SUB), ROW_SUB), :]


def _dispatch_kernel(slot_ref, tail_ref, h_ref, xs_ref, zero_scr, sem, zsem, *, tg, tb):
    @pl.when(pl.program_id(0) == 0)
    def _():
        zero_scr[...] = jnp.zeros(zero_scr.shape, F32)
        for e in range(N_EXPERTS):
            tail = pl.multiple_of(tail_ref[e] * ROW_SUB, tb * ROW_SUB)
            pltpu.make_async_copy(zero_scr, xs_ref.at[pl.ds(tail, tb * ROW_SUB), :], zsem).start()
        for e in range(N_EXPERTS):
            pltpu.make_async_copy(zero_scr, xs_ref.at[pl.ds(0, tb * ROW_SUB), :], zsem).wait()

    def issue(t, carry):
        for kk in range(TOP_K):
            pltpu.make_async_copy(_row_view(h_ref, t), _slot_view(xs_ref, slot_ref[t * TOP_K + kk]),
                                  sem).start(priority=kk % 2)
        return carry

    lax.fori_loop(0, tg, issue, 0, unroll=2)

    for kk in range(TOP_K):
        pltpu.make_async_copy(h_ref, xs_ref.at[pl.ds(0, tg * ROW_SUB), :], sem).wait()


def _dispatch(slots_flat, tail_start, h2r, cap, tg, tb):
    nt = h2r.shape[0] // ROW_SUB
    return pl.pallas_call(
        functools.partial(_dispatch_kernel, tg=tg, tb=tb),
        grid=(nt // tg,),
        in_specs=[pl.BlockSpec((tg * TOP_K,), lambda i: (i,), memory_space=pltpu.SMEM),
                  pl.BlockSpec(memory_space=pltpu.SMEM),
                  pl.BlockSpec((tg * ROW_SUB, V7X_LANES), lambda i: (i, 0))],
        out_specs=pl.BlockSpec(memory_space=pl.ANY),
        out_shape=jax.ShapeDtypeStruct((cap * ROW_SUB, V7X_LANES), F32),
        scratch_shapes=[pltpu.VMEM((tb * ROW_SUB, V7X_LANES), F32), pltpu.SemaphoreType.DMA(()),
                        pltpu.SemaphoreType.DMA(())],
        compiler_params=_cparams(("arbitrary",)),
        name="moe_dispatch",
    )(slots_flat, tail_start, h2r)


def _expert_kernel(be_ref, bv_ref, xs_ref, w1_ref, b1_ref, w2_ref, b2_ref, ys_ref, w1_scr, w2_scr, *, tb):
    i = pl.program_id(0)
    valid = bv_ref[i]

    @pl.when((i == 0) | (be_ref[i] != be_ref[jnp.maximum(i - 1, 0)]))
    def _():
        w1_scr[...] = w1_ref[0].astype(BF16)
        w2_scr[...] = w2_ref[0].astype(BF16)

    @pl.when(valid > 0)
    def _():
        xb = _load_row_tiles(xs_ref, tb).astype(BF16)
        gu = jnp.dot(xb, w1_scr[...], preferred_element_type=F32) + b1_ref[0]
        glu = jnp.minimum(gu[:, :D_FF], SWIGLU_LIMIT)
        lin = jnp.clip(gu[:, D_FF:], -SWIGLU_LIMIT, SWIGLU_LIMIT)
        act = (lin + 1.0) * glu * _sigmoid(SWIGLU_ALPHA * glu)
        y = jnp.dot(act.astype(BF16), w2_scr[...], preferred_element_type=F32) + b2_ref[0]
        _store_row_tiles(ys_ref, y)

    @pl.when(valid <= 0)
    def _():
        ys_ref[...] = jnp.zeros(ys_ref.shape, F32)


def _experts(block_expert, block_valid, xs, w1, b1, w2, b2, tb):
    cap = xs.shape[0] // ROW_SUB
    rows = pl.BlockSpec((tb * ROW_SUB, V7X_LANES), lambda i, be, bv: (i, 0))
    grid_spec = pltpu.PrefetchScalarGridSpec(
        num_scalar_prefetch=2,
        grid=(cap // tb,),
        in_specs=[rows,
                  pl.BlockSpec((1, D_MODEL, 2 * D_FF), lambda i, be, bv: (be[i], 0, 0)),
                  pl.BlockSpec((1, 1, 2 * D_FF), lambda i, be, bv: (be[i], 0, 0)),
                  pl.BlockSpec((1, D_FF, D_MODEL), lambda i, be, bv: (be[i], 0, 0)),
                  pl.BlockSpec((1, 1, D_MODEL), lambda i, be, bv: (be[i], 0, 0))],
        out_specs=rows,
        scratch_shapes=[pltpu.VMEM((D_MODEL, 2 * D_FF), BF16), pltpu.VMEM((D_FF, D_MODEL), BF16)],
    )
    return pl.pallas_call(
        functools.partial(_expert_kernel, tb=tb),
        grid_spec=grid_spec,
        out_shape=jax.ShapeDtypeStruct((cap * ROW_SUB, V7X_LANES), F32),
        compiler_params=_cparams(("arbitrary",)),
        name="moe_experts",
    )(block_expert, block_valid, xs, w1, b1.reshape(N_EXPERTS, 1, -1), w2, b2.reshape(N_EXPERTS, 1, -1))


def _combine_kernel(slot_ref, next_slot_ref, ys_ref, wt_ref, x1_ref, mod_ref, g_ref, b_ref, *rest,
                    tc, tile_ends):
    y_refs = rest[:len(tile_ends)]
    buf, sem = rest[len(tile_ends):]
    i = pl.program_id(0)
    cur = i % 2

    def row_copy(slots, t, kk, half):
        return pltpu.make_async_copy(_slot_view(ys_ref, slots[t * TOP_K + kk]),
                                     _row_view(buf.at[half, kk], t), sem.at[half])

    def gather(slots, half):
        def issue(t, carry):
            for kk in range(TOP_K):
                row_copy(slots, t, kk, half).start(priority=kk % 2)
            return carry
        lax.fori_loop(0, tc, issue, 0, unroll=8)

    @pl.when(i == 0)
    def _():
        gather(slot_ref, 0)

    @pl.when(i + 1 < pl.num_programs(0))
    def _():
        gather(next_slot_ref, 1 - cur)

    for kk in range(TOP_K):
        pltpu.make_async_copy(ys_ref.at[pl.ds(0, tc * ROW_SUB), :], buf.at[cur, kk], sem.at[cur]).wait()
    wt = wt_ref[...]
    ff = sum(wt[:, kk:kk + 1] * _load_row_tiles(buf.at[cur, kk], tc) for kk in range(TOP_K))
    g2 = mod_ref[0, 5:6, :]
    y = _layer_norm(DN_ALPHA * x1_ref[...] + (1.0 + g2) * ff) * g_ref[...] + b_ref[...]
    i = pl.program_id(0)
    start = 0
    for y_ref, end in zip(y_refs, tile_ends):
        @pl.when((i >= start) & (i < end))
        def _(y_ref=y_ref):
            y_ref[...] = y
        start = end


def _combine(slots_flat, ys, wts, x1, mod, ln2_g, ln2_b, lay, tc):
    nt = lay.n_tokens

    def seq_map(i):
        return (lay.seq_and_pos(i * tc)[0], 0, 0)

    out_specs, tile_ends = lay.trunk_specs(
        tc, lambda local: pl.BlockSpec((tc, D_MODEL), lambda i: (local(i), 0)))
    n_steps = nt // tc
    return pl.pallas_call(
        functools.partial(_combine_kernel, tc=tc, tile_ends=tile_ends),
        grid=(n_steps,),
        in_specs=[pl.BlockSpec((tc * TOP_K,), lambda i: (i,), memory_space=pltpu.SMEM),
                  pl.BlockSpec((tc * TOP_K,), lambda i: (jnp.minimum(i + 1, n_steps - 1),),
                               memory_space=pltpu.SMEM),
                  pl.BlockSpec(memory_space=pl.ANY),
                  pl.BlockSpec((tc, V7X_LANES), lambda i: (i, 0)),
                  pl.BlockSpec((tc, D_MODEL), lambda i: (i, 0)),
                  pl.BlockSpec((1, N_MOD, D_MODEL), seq_map),
                  _const_spec((1, D_MODEL)), _const_spec((1, D_MODEL))],
        out_specs=out_specs,
        out_shape=[jax.ShapeDtypeStruct((b * s, D_MODEL), F32) for b, s in lay.trunks],
        scratch_shapes=[pltpu.VMEM((2, TOP_K, tc * ROW_SUB, V7X_LANES), F32), pltpu.SemaphoreType.DMA((2,))],
        compiler_params=_cparams(("arbitrary",)),
        name="moe_combine",
    )(slots_flat, slots_flat, ys, wts, x1, mod, ln2_g.reshape(1, -1), ln2_b.reshape(1, -1))


def _encoder_layer(xs, cs, p):
    trunks = [(x.shape[0], x.shape[1]) for x in xs]
    lay = _Layout(trunks)
    nt = lay.n_tokens
    t = _tiles(nt)

    x2d = [xi.reshape(-1, D_MODEL) for xi in xs]
    c = jnp.concatenate(cs, axis=0)
    bp = -(-lay.n_seqs // 8) * 8
    c = jnp.pad(c, ((0, bp - lay.n_seqs), (0, 0)))
    mod = _ada_mod(c, p["w_ada"], p["b_ada"]).reshape(bp, N_MOD, D_MODEL)

    q, k, vt3, xr, gg, gs = _inproj(x2d, mod, p["w_in"].astype(BF16), p["q_gain"], p["k_gain"],
                                    _rope_tables(lay.max_seq), lay, t["tm_in"])

    gate_scale = -math.log2(math.e)
    w_lru = (gate_scale * jnp.concatenate([p["lru_wa"], p["lru_wx"]], axis=-1)).astype(BF16)
    b_lru = gate_scale * jnp.stack([p["lru_ba"], p["lru_bx"]], axis=1)
    lam = p["lru_lam"].reshape(2, 1, D_MODEL)
    o_atts, hfbs = [], []
    tok_off = 0
    for b, s in trunks:
        o_atts.append(_attention(q, k, vt3, tok_off, b, s, t["tq"], t["tm_in"]))
        hfbs.append(_scan(xr, p["conv_w"], p["conv_b"].reshape(1, -1), w_lru, b_lru, lam,
                          tok_off, b, s, t["tt"]))
        tok_off += b * s

    wr_hi = p["w_router"].astype(BF16)
    wr_lo = (p["w_router"] - wr_hi.astype(F32)).astype(BF16)
    x1, h2r, route, wts, cnt = _mid(o_atts, hfbs, gg, gs, x2d, mod, p["w_pa"].astype(BF16),
                                    p["w_pr"].astype(BF16), p["w_out"].astype(BF16), p["ln1_g"], p["ln1_b"],
                                    jnp.concatenate([wr_hi, wr_lo], axis=1), p["b_router"], lay, t["tm_mid"])

    tb = t["tb"]
    counts = cnt[0].astype(jnp.int32)
    padded = (counts + tb - 1) // tb * tb
    pad_end = jnp.cumsum(padded)
    pad_start = pad_end - padded
    n_blocks = nt * TOP_K // tb + N_EXPERTS
    blk0 = jnp.arange(n_blocks, dtype=jnp.int32) * tb
    block_expert = jnp.minimum(jnp.sum(pad_end[None, :] <= blk0[:, None], axis=1), N_EXPERTS - 1).astype(jnp.int32)
    block_valid = jnp.clip(pad_start[block_expert] + counts[block_expert] - blk0, 0, tb).astype(jnp.int32)
    expert = route[:, 0:TOP_K]
    start_of = jnp.sum(jnp.where(expert[..., None] == jnp.arange(N_EXPERTS, dtype=jnp.int32),
                                 pad_start.astype(jnp.int32), 0), axis=-1)
    slots_flat = ((start_of + route[:, TOP_K:2 * TOP_K]) * ROW_SUB).reshape(-1)

    tail_start = jnp.where(padded > 0, pad_end - tb, pad_end[-1] - tb).astype(jnp.int32)
    xs_rows = _dispatch(slots_flat, tail_start, h2r, n_blocks * tb, t["tg"], tb)
    ys = _experts(block_expert, block_valid, xs_rows, p["w1"], p["b1"], p["w2"], p["b2"], tb)
    ys_out = _combine(slots_flat, ys, wts, x1, mod, p["ln2_g"], p["ln2_b"], lay, t["tc"])
    return [y.reshape(b, s, D_MODEL) for y, (b, s) in zip(ys_out, trunks)]


_PARAM_NAMES = ("w_ada", "b_ada", "w_in", "q_gain", "k_gain", "conv_w", "conv_b", "lru_wa", "lru_ba",
                "lru_wx", "lru_bx", "lru_lam", "w_pa", "w_pr", "w_out", "ln1_g", "ln1_b", "w_router",
                "b_router", "w1", "b1", "w2", "b2", "ln2_g", "ln2_b")


def kernel(x_prompt, x_sample, c_prompt, c_sample, w_ada, b_ada, w_in, q_gain, k_gain, conv_w, conv_b, lru_wa, lru_ba, lru_wx, lru_bx, lru_lam, w_pa, w_pr, w_out, ln1_g, ln1_b, w_router, b_router, w1, b1, w2, b2, ln2_g, ln2_b):
    stacked = (w_ada, b_ada, w_in, q_gain, k_gain, conv_w, conv_b, lru_wa, lru_ba, lru_wx, lru_bx,
               lru_lam, w_pa, w_pr, w_out, ln1_g, ln1_b, w_router, b_router, w1, b1, w2, b2, ln2_g, ln2_b)
    xs, cs = [x_prompt, x_sample], [c_prompt, c_sample]
    for layer in range(DEPTH):
        p = {name: arr[layer] for name, arr in zip(_PARAM_NAMES, stacked)}
        xs = _encoder_layer(xs, cs, p)
    return (xs[0], xs[1])
```
